```python
import jax, jax.numpy as jnp
from jax import lax
import numpy as np

D_MODEL = 1024
BATCH = 4
SEQ = 4096
DEPTH = 2

GRID_W = 64
CTX_LEN = 256
D_MIX = D_MODEL
POOL_WINDOWS = (2, 4, 8, 16)
POOL_GROUPS = len(POOL_WINDOWS)
POOL_WIDTH = D_MIX // 4
POOL_GC = POOL_WIDTH // POOL_GROUPS
HEAD_DIM = 64
N_HEADS = (D_MIX // 2) // HEAD_DIM
N_KV_HEADS = 2
GQA_GROUP = N_HEADS // N_KV_HEADS
ATTN_WIDTH = N_HEADS * HEAD_DIM
KV_WIDTH = N_KV_HEADS * HEAD_DIM
WINDOW = 128
BLOCK = 128
ROPE_BASE = 10000.0
FOUR_WIDTH = D_MIX - POOL_WIDTH - ATTN_WIDTH
FOUR_GROUPS = 4
FOUR_GC = FOUR_WIDTH // FOUR_GROUPS
POOL_OFF = 0
Q_OFF = POOL_OFF + POOL_WIDTH
K_OFF = Q_OFF + ATTN_WIDTH
V_OFF = K_OFF + KV_WIDTH
FOUR_OFF = V_OFF + KV_WIDTH
IN_WIDTH = FOUR_OFF + FOUR_WIDTH
N_EXPERT_GROUPS = 4
EXPERTS_PER_GROUP = 4
N_EXPERTS = N_EXPERT_GROUPS * EXPERTS_PER_GROUP
TOP_K_IN_GROUP = 2
D_EXPERT = D_MODEL // 2
EPS = 1e-6
NEG_INF = -1e30

kernel_name = "hybrid_pool_swa_fourier_hmoe_dit"


def rms_norm(x, g):
    xf = x.astype(jnp.float32)
    y = xf * lax.rsqrt(jnp.mean(xf * xf, axis=-1, keepdims=True) + EPS)
    return (y * g.astype(jnp.float32)).astype(x.dtype)


def axial_rope_tables(n_tokens, dtype):
    rows = n_tokens // GRID_W
    r = jnp.broadcast_to(jnp.arange(rows)[:, None], (rows, GRID_W)).reshape(-1).astype(jnp.float32)
    col = jnp.broadcast_to(jnp.arange(GRID_W)[None, :], (rows, GRID_W)).reshape(-1).astype(jnp.float32)
    half = HEAD_DIM // 2
    inv = 1.0 / (ROPE_BASE ** (jnp.arange(0, half, 2, dtype=jnp.float32) / half))
    ar = r[:, None] * inv
    ac = col[:, None] * inv
    ang = jnp.concatenate([ar, ar, ac, ac], axis=-1)
    return jnp.cos(ang).astype(dtype), jnp.sin(ang).astype(dtype)


def apply_axial_rope(x, cos, sin):
    a, b, c, d = jnp.split(x, 4, axis=-1)
    rot = jnp.concatenate([-b, a, -d, c], axis=-1)
    return x * cos[None, :, None, :] + rot * sin[None, :, None, :]


def pool_mixer(u, w, scale):
    B, N, _ = u.shape
    uf = u.astype(jnp.float32).reshape(B, N, POOL_GROUPS, POOL_GC)
    cs = jnp.concatenate([jnp.zeros_like(uf[:, :1]), jnp.cumsum(uf, axis=1)], axis=1)
    t = jnp.arange(N)
    outs = []
    for g, win in enumerate(POOL_WINDOWS):
        lo = jnp.clip(t - win // 2, 0, N - 1)
        hi = jnp.clip(t + win // 2 - 1, 0, N - 1)
        cnt = (hi - lo + 1).astype(jnp.float32)
        s = cs[:, hi + 1, g] - cs[:, lo, g]
        outs.append(s / cnt[None, :, None] - uf[:, :, g])
    y = jnp.stack(outs, axis=2).astype(u.dtype)
    y = jnp.einsum('bngc,gcd->bngd', y, w).reshape(B, N, POOL_WIDTH)
    return y * scale


def fourier_mixer(u, w):
    B, N, _ = u.shape
    uf = u.astype(jnp.float32).reshape(B, N, FOUR_GROUPS, FOUR_GC)
    y = jnp.fft.fftn(uf, axes=(1, 3), norm="ortho").real.astype(u.dtype)
    return jnp.einsum('bngc,gcd->bngd', y, w).reshape(B, N, FOUR_WIDTH)


def split_qkv(p, q_g, k_g):
    B, N, _ = p.shape
    q = rms_norm(p[..., Q_OFF:K_OFF].reshape(B, N, N_HEADS, HEAD_DIM), q_g)
    k = rms_norm(p[..., K_OFF:V_OFF].reshape(B, N, N_KV_HEADS, HEAD_DIM), k_g)
    v = p[..., V_OFF:FOUR_OFF].reshape(B, N, N_KV_HEADS, HEAD_DIM)
    return q, k, v


def latent_window_attention(q, k, v, kc, vc, sink):
    B, N = q.shape[:2]
    L = kc.shape[1]
    nb = N // BLOCK
    scale = HEAD_DIM ** -0.5
    qb = q.reshape(B, nb, BLOCK, N_KV_HEADS, GQA_GROUP, HEAD_DIM)

    def band(t):
        tp = jnp.pad(t, ((0, 0), (BLOCK, BLOCK), (0, 0), (0, 0))).reshape(B, nb + 2, BLOCK, N_KV_HEADS, HEAD_DIM)
        return jnp.concatenate([tp[:, :-2], tp[:, 1:-1], tp[:, 2:]], axis=2)

    kw, vw = band(k), band(v)
    s_loc = jnp.einsum('bnqhgd,bnkhd->bnhgqk', qb, kw).astype(jnp.float32) * scale
    qpos = jnp.arange(nb)[:, None, None] * BLOCK + jnp.arange(BLOCK)[None, :, None]
    kpos = (jnp.arange(nb)[:, None, None] - 1) * BLOCK + jnp.arange(3 * BLOCK)[None, None, :]
    valid = (jnp.abs(qpos - kpos) <= WINDOW) & (kpos >= 0) & (kpos < N)
    s_loc = jnp.where(valid[None, :, None, None], s_loc, NEG_INF)
    s_ctx = jnp.einsum('bnqhgd,blhd->bnhgql', qb, kc).astype(jnp.float32) * scale
    s_sink = jnp.broadcast_to(sink.astype(jnp.float32).reshape(1, 1, N_KV_HEADS, GQA_GROUP, 1, 1),
                              s_loc.shape[:-1] + (1,))
    p = jax.nn.softmax(jnp.concatenate([s_loc, s_ctx, s_sink], axis=-1), axis=-1).astype(v.dtype)
    o = (jnp.einsum('bnhgqk,bnkhd->bnqhgd', p[..., :3 * BLOCK], vw)
         + jnp.einsum('bnhgql,blhd->bnqhgd', p[..., 3 * BLOCK:3 * BLOCK + L], vc))
    return o.reshape(B, N, ATTN_WIDTH)


def context_attention(qc, kc, vc, sink):
    B, L = qc.shape[:2]
    scale = HEAD_DIM ** -0.5
    qg = qc.reshape(B, L, N_KV_HEADS, GQA_GROUP, HEAD_DIM)
    s = jnp.einsum('blhgd,bmhd->bhglm', qg, kc).astype(jnp.float32) * scale
    s_sink = jnp.broadcast_to(sink.astype(jnp.float32).reshape(1, N_KV_HEADS, GQA_GROUP, 1, 1), s.shape[:-1] + (1,))
    p = jax.nn.softmax(jnp.concatenate([s, s_sink], axis=-1), axis=-1)[..., :L].astype(vc.dtype)
    return jnp.einsum('bhglm,bmhd->blhgd', p, vc).reshape(B, L, ATTN_WIDTH)


def merge_heads(p, attn_out, pool_w, pool_scale, four_w, w_out):
    pool_out = pool_mixer(p[..., POOL_OFF:Q_OFF], pool_w, pool_scale)
    four_out = fourier_mixer(p[..., FOUR_OFF:IN_WIDTH], four_w)
    return jnp.concatenate([pool_out, attn_out, four_out], axis=-1) @ w_out


def hier_moe(h, w_grp, b_grp, w_rtr, b_rtr, w_gate, w_up, w_down):
    T = h.shape[0]
    p_grp = jax.nn.softmax((h @ w_grp + b_grp).astype(jnp.float32), axis=-1)
    pg, g = lax.top_k(p_grp, 1)
    logits_e = (h @ w_rtr + b_rtr).astype(jnp.float32).reshape(T, N_EXPERT_GROUPS, EXPERTS_PER_GROUP)
    logits_sel = jnp.einsum('tge,tg->te', logits_e, jax.nn.one_hot(g[:, 0], N_EXPERT_GROUPS, dtype=jnp.float32))
    pe, ie = lax.top_k(jax.nn.softmax(logits_sel, axis=-1), TOP_K_IN_GROUP)
    wts = pg * pe / jnp.sum(pe, axis=-1, keepdims=True)
    eid = g * EXPERTS_PER_GROUP + ie
    gates = jnp.sum(jax.nn.one_hot(eid, N_EXPERTS, dtype=jnp.float32) * wts[..., None], axis=1).astype(h.dtype)
    y = jnp.zeros_like(h)
    for e in range(N_EXPERTS):
        a = jax.nn.silu(h @ w_gate[e]) * (h @ w_up[e])
        y = y + gates[:, e:e + 1] * (a @ w_down[e])
    return y


def setup_inputs(seed: int = 0) -> dict:
    key = jax.random.key(seed)
    ks = jax.random.split(key, 24)
    f32 = jnp.float32
    D = D_MODEL
    nrm = lambda k, shape, s: jax.random.normal(k, shape, f32) * s
    return {
        "x": nrm(ks[0], (BATCH, SEQ, D), 1.0),
        "c": nrm(ks[1], (BATCH, D), 1.0),
        "ctx": nrm(ks[2], (BATCH, CTX_LEN, D), 1.0),
        "c_ctx": nrm(ks[3], (D,), 1.0),
        "w_mod": nrm(ks[4], (DEPTH, D, 6 * D), 0.5 * D ** -0.5),
        "b_mod": nrm(ks[5], (DEPTH, 6 * D), 0.02),
        "norm1_g": 1.0 + nrm(ks[6], (DEPTH, D), 0.05),
        "w_in": nrm(ks[7], (DEPTH, D, IN_WIDTH), D ** -0.5),
        "q_norm_g": 1.0 + nrm(ks[8], (DEPTH, HEAD_DIM), 0.05),
        "k_norm_g": 1.0 + nrm(ks[9], (DEPTH, HEAD_DIM), 0.05),
        "attn_sink": nrm(ks[10], (DEPTH, N_HEADS), 0.5),
        "pool_w": nrm(ks[11], (DEPTH, POOL_GROUPS, POOL_GC, POOL_GC), POOL_GC ** -0.5),
        "pool_scale": 1.0 + nrm(ks[12], (DEPTH, POOL_WIDTH), 0.05),
        "four_w": nrm(ks[13], (DEPTH, FOUR_GROUPS, FOUR_GC, FOUR_GC), FOUR_GC ** -0.5),
        "w_out": nrm(ks[14], (DEPTH, D_MIX, D), D_MIX ** -0.5),
        "norm2_g": 1.0 + nrm(ks[15], (DEPTH, D), 0.05),
        "w_grp": nrm(ks[16], (DEPTH, D, N_EXPERT_GROUPS), D ** -0.5),
        "b_grp": nrm(ks[17], (DEPTH, N_EXPERT_GROUPS), 0.01),
        "w_rtr": nrm(ks[18], (DEPTH, D, N_EXPERTS), D ** -0.5),
        "b_rtr": nrm(ks[19], (DEPTH, N_EXPERTS), 0.01),
        "w_gate": nrm(ks[20], (DEPTH, N_EXPERTS, D, D_EXPERT), D ** -0.5),
        "w_up": nrm(ks[21], (DEPTH, N_EXPERTS, D, D_EXPERT), D ** -0.5),
        "w_down": nrm(ks[22], (DEPTH, N_EXPERTS, D_EXPERT, D), D_EXPERT ** -0.5),
    }


def reference(x, c, ctx, c_ctx, w_mod, b_mod, norm1_g, w_in, q_norm_g, k_norm_g, attn_sink,
              pool_w, pool_scale, four_w, w_out, norm2_g, w_grp, b_grp, w_rtr, b_rtr,
              w_gate, w_up, w_down):
    B, N, D = x.shape
    L = ctx.shape[1]
    cos, sin = axial_rope_tables(N, x.dtype)
    s_lat = jax.nn.silu(c)
    s_ctx = jax.nn.silu(c_ctx)
    xc = ctx
    for l in range(DEPTH):
        last = l == DEPTH - 1
        mod = s_lat @ w_mod[l] + b_mod[l]
        sh1, sc1, g1, sh2, sc2, g2 = jnp.split(mod[:, None, :], 6, axis=-1)
        modc = s_ctx @ w_mod[l] + b_mod[l]
        ch1, cs1, cg1, ch2, cs2, cg2 = jnp.split(modc, 6)

        hc = rms_norm(xc, norm1_g[l]) * (1.0 + cs1) + ch1
        if last:
            pkv = hc @ w_in[l][:, K_OFF:FOUR_OFF]
            kc = rms_norm(pkv[..., :KV_WIDTH].reshape(B, L, N_KV_HEADS, HEAD_DIM), k_norm_g[l])
            vc = pkv[..., KV_WIDTH:].reshape(B, L, N_KV_HEADS, HEAD_DIM)
        else:
            pc = hc @ w_in[l]
            qc, kc, vc = split_qkv(pc, q_norm_g[l], k_norm_g[l])
            yc = merge_heads(pc, context_attention(qc, kc, vc, attn_sink[l]),
                             pool_w[l], pool_scale[l], four_w[l], w_out[l])
            xc = xc + cg1 * yc

        h = rms_norm(x, norm1_g[l]) * (1.0 + sc1) + sh1
        p = h @ w_in[l]
        q, k, v = split_qkv(p, q_norm_g[l], k_norm_g[l])
        q = apply_axial_rope(q, cos, sin)
        k = apply_axial_rope(k, cos, sin)
        y = merge_heads(p, latent_window_attention(q, k, v, kc, vc, attn_sink[l]),
                        pool_w[l], pool_scale[l], four_w[l], w_out[l])
        x = x + g1 * y

        h2 = (rms_norm(x, norm2_g[l]) * (1.0 + sc2) + sh2).reshape(B * N, D)
        moe_args = (w_grp[l], b_grp[l], w_rtr[l], b_rtr[l], w_gate[l], w_up[l], w_down[l])
        if last:
            x = x + g2 * hier_moe(h2, *moe_args).reshape(B, N, D)
        else:
            h2c = (rms_norm(xc, norm2_g[l]) * (1.0 + cs2) + ch2).reshape(B * L, D)
            out = hier_moe(jnp.concatenate([h2, h2c], axis=0), *moe_args)
            x = x + g2 * out[:B * N].reshape(B, N, D)
            xc = xc + cg2 * out[B * N:].reshape(B, L, D)
    return x
```

```python
import functools

import numpy as np
import jax
import jax.numpy as jnp
from jax import lax
from jax.experimental import pallas as pl
from jax.experimental.pallas import tpu as pltpu

F32 = jnp.float32
BF16 = jnp.bfloat16
I32 = jnp.int32
HI = lax.Precision.HIGHEST

D = 1024
HEAD_DIM = 64
N_HEADS = 8
GRID_W = 64
POOL_WINDOWS = (2, 4, 8, 16)
POOL_W = 256
ATTN_W = 512
KV_W = 128
FOUR_W = 256
IN_W = 1280
N_EXPERTS = 16
D_EXPERT = 512
WINDOW = 128
ROPE_BASE = 10000.0
EPS = 1e-6
NEG_INF = -1e30
CTX_ROW = 4
N_BINS = 24
PAIR_SLOT_A = (0, 2, 2, 3, 3, 3)
PAIR_SLOT_B = (1, 1, 0, 0, 1, 2)
META_W = 128
ROW_W = D + META_W

VMEM_LIMIT = 56 * 1024 * 1024
TM_IN = 512
TQ = 512
TM_OUT = 256
TM_FOUR = 256
TM_MOE = 256
TM_ROW = 256
MOD_TN = 1024


def _cparams(n_axes):
    return pltpu.CompilerParams(dimension_semantics=("arbitrary",) * n_axes,
                                vmem_limit_bytes=VMEM_LIMIT)


def _silu(v):
    return v / (1.0 + jnp.exp(-v))


def _mod_body(cs_ref, w_ref, b_ref, o_ref):
    s = _silu(cs_ref[...])
    o_ref[0] = jnp.dot(s, w_ref[0], preferred_element_type=F32, precision=HI) + b_ref[0]


def _mod_call(cs, w_mod, b_mod):
    depth = w_mod.shape[0]
    return pl.pallas_call(
        _mod_body,
        grid=(depth, 6 * D // MOD_TN),
        in_specs=[pl.BlockSpec((8, D), lambda l, j: (0, 0)),
                  pl.BlockSpec((1, D, MOD_TN), lambda l, j: (l, 0, j)),
                  pl.BlockSpec((1, 1, MOD_TN), lambda l, j: (l, 0, j))],
        out_specs=pl.BlockSpec((1, 8, MOD_TN), lambda l, j: (l, 0, j)),
        out_shape=jax.ShapeDtypeStruct((depth, 8, 6 * D), F32),
        compiler_params=_cparams(2),
        name="modulation",
    )(cs, w_mod, b_mod.reshape(depth, 1, 6 * D))


def _ab_body(c_ref, s_ref, w_ref, o_ref):
    w = w_ref[0]
    ca = jnp.dot(c_ref[...], w, preferred_element_type=F32, precision=HI)
    sa = jnp.dot(s_ref[...], w, preferred_element_type=F32, precision=HI)
    o_ref[0] = (jnp.concatenate([ca, sa], axis=1) * (HEAD_DIM ** -0.5)).astype(BF16)


def _ab_call(c64bd, s64bd, wbd):
    depth = wbd.shape[0]
    return pl.pallas_call(
        _ab_body,
        grid=(depth,),
        in_specs=[pl.BlockSpec((FOUR_W, FOUR_W), lambda l: (0, 0)),
                  pl.BlockSpec((FOUR_W, FOUR_W), lambda l: (0, 0)),
                  pl.BlockSpec((1, FOUR_W, FOUR_W), lambda l: (l, 0, 0))],
        out_specs=pl.BlockSpec((1, FOUR_W, 2 * FOUR_W), lambda l: (l, 0, 0)),
        out_shape=jax.ShapeDtypeStruct((depth, FOUR_W, 2 * FOUR_W), BF16),
        compiler_params=_cparams(1),
        name="fourier_weights",
    )(c64bd, s64bd, wbd)


def _head_rms(t, m, g):
    tt = t * t
    hi = tt.astype(BF16)
    lo = (tt - hi.astype(F32)).astype(BF16)
    ms = (jnp.dot(hi, m, preferred_element_type=F32) + jnp.dot(lo, m, preferred_element_type=F32))
    return t * lax.rsqrt(ms + EPS) * g


def _rope(t, cos, sin_signed):
    w = t.shape[1]
    lane = lax.broadcasted_iota(I32, t.shape, 1)
    fwd = pltpu.roll(t, w - 16, 1)
    bwd = pltpu.roll(t, 16, 1)
    rot = jnp.where((lane & 16) == 0, fwd, bwd)
    return t * cos + rot * sin_signed


def _in_body(*refs, rope, per_batch):
    if rope:
        (x_ref, mod_ref, g1_ref, w_ref, m_ref, qg_ref, kg_ref, ab_ref, cos_ref, sin_ref,
         pu_ref, q_ref, kv_ref, ua_ref) = refs
    else:
        (x_ref, mod_ref, g1_ref, w_ref, m_ref, qg_ref, kg_ref, ab_ref,
         pu_ref, q_ref, kv_ref, ua_ref) = refs
    row = pl.program_id(0) if per_batch else CTX_ROW
    sh1 = mod_ref[pl.ds(row, 1), pl.ds(0, D)]
    sc1 = mod_ref[pl.ds(row, 1), pl.ds(D, D)]
    x = x_ref[0]
    ms = jnp.mean(x * x, axis=-1, keepdims=True)
    h = (x * lax.rsqrt(ms + EPS) * g1_ref[...]) * (1.0 + sc1) + sh1
    p = jnp.dot(h.astype(BF16), w_ref[...], preferred_element_type=F32)
    pu = p[:, 0:256]
    q = p[:, 256:768]
    k = p[:, 768:896]
    v = p[:, 896:1024]
    fu = p[:, 1024:1280]
    m = m_ref[...]
    q = _head_rms(q, m, qg_ref[...])
    k = _head_rms(k, m[0:KV_W, 0:KV_W], kg_ref[...])
    if rope:
        cos = cos_ref[...]
        sin = sin_ref[...]
        q = _rope(q, jnp.concatenate([cos] * 4, axis=1), jnp.concatenate([sin] * 4, axis=1))
        k = _rope(k, cos, sin)
    q = q * (HEAD_DIM ** -0.5)
    pu_ref[0] = pu.astype(BF16)
    q_ref[0] = q.astype(BF16)
    kv_ref[0] = jnp.concatenate([k, pltpu.roll(k, 64, 1), v, pltpu.roll(v, 64, 1)], axis=1).astype(BF16)
    ua_ref[0] = jnp.dot(fu.astype(BF16), ab_ref[...], preferred_element_type=F32).astype(BF16)


def _in_call(x3, mod, g1, w_in, m512, qg, kg, ab, cos2, sin2, *, rope, per_batch):
    nb, seq, _ = x3.shape
    tm = min(TM_IN, seq)
    full = lambda shape: pl.BlockSpec(shape, lambda b, i: (0,) * len(shape))
    in_specs = [pl.BlockSpec((1, tm, D), lambda b, i: (b, i, 0)),
                full((8, 6 * D)), full((1, D)), full((D, IN_W)), full((ATTN_W, ATTN_W)),
                full((1, ATTN_W)), full((1, KV_W)), full((FOUR_W, 2 * FOUR_W))]
    args = [x3, mod, g1, w_in, m512, qg, kg, ab]
    if rope:
        in_specs += [pl.BlockSpec((tm, 128), lambda b, i: (i, 0)),
                     pl.BlockSpec((tm, 128), lambda b, i: (i, 0))]
        args += [cos2, sin2]
    widths = (POOL_W, ATTN_W, 4 * KV_W, 2 * FOUR_W)
    return pl.pallas_call(
        functools.partial(_in_body, rope=rope, per_batch=per_batch),
        grid=(nb, seq // tm),
        in_specs=in_specs,
        out_specs=[pl.BlockSpec((1, tm, w), lambda b, i: (b, i, 0)) for w in widths],
        out_shape=[jax.ShapeDtypeStruct((nb, seq, w), BF16) for w in widths],
        compiler_params=_cparams(2),
        name="in_proj_rope" if rope else "in_proj_ctx",
    )(*args)


def _fill_masked(kv, kz_ref, vz_ref):
    lane = lax.broadcasted_iota(I32, (kv.shape[0], 128), 1)
    lo = lane < 64
    for src, dst in ((0, kz_ref), (256, vz_ref)):
        part = kv[:, src:src + 128]
        swp = kv[:, src + 128:src + 256]
        z = jnp.zeros_like(part)
        dst[0] = jnp.where(lo, part, z)
        dst[1] = jnp.where(lo, z, swp)
        dst[2] = jnp.where(lo, swp, z)
        dst[3] = jnp.where(lo, z, part)


def _head_out(qp, k_parts, v_parts, masks, sink):
    s_parts = []
    for kz, mk in zip(k_parts, masks):
        s = lax.dot_general(qp, kz, (((1,), (1,)), ((), ())), preferred_element_type=F32)
        if mk is not None:
            s = jnp.where(mk, s, NEG_INF)
        s_parts.append(s)
    m = functools.reduce(jnp.maximum, [jnp.max(s, axis=-1, keepdims=True) for s in s_parts])
    m = jnp.maximum(m, sink)
    den = jnp.exp(sink - m)
    acc = None
    for s, vz in zip(s_parts, v_parts):
        e = jnp.exp(s - m)
        den = den + jnp.sum(e, axis=-1, keepdims=True)
        pv = jnp.dot(e.astype(BF16), vz, preferred_element_type=F32)
        acc = pv if acc is None else acc + pv
    return acc / den


def _attn_lat_body(sink_ref, q_ref, kvp_ref, kvm_ref, kvn_ref, kvc_ref, o_ref,
                   kz_ref, vz_ref, kzc_ref, vzc_ref, *, tq, seq):
    i = pl.program_id(1)
    _fill_masked(jnp.concatenate([kvp_ref[0], kvm_ref[0], kvn_ref[0]], axis=0), kz_ref, vz_ref)
    _fill_masked(kvc_ref[0], kzc_ref, vzc_ref)
    sinks = [sink_ref[h] for h in range(N_HEADS)]

    def sub(j, carry):
        r0 = pl.multiple_of(j * 128, 128)
        qblk = q_ref[0, pl.ds(r0, 128), :]
        ii = lax.broadcasted_iota(I32, (128, 3 * 128), 0)
        cc = lax.broadcasted_iota(I32, (128, 3 * 128), 1)
        base = i * tq + j * 128 - 128
        valid = (ii <= cc) & (cc <= ii + 2 * WINDOW) & (cc >= -base) & (cc < seq - base)
        cols = []
        for pr in range(4):
            kvh = pr // 2
            qp = qblk[:, pr * 128:(pr + 1) * 128]
            outs = []
            for half in range(2):
                idx = 2 * kvh + half
                outs.append(_head_out(
                    qp,
                    [kz_ref[idx, pl.ds(r0, 3 * 128), :], kzc_ref[idx]],
                    [vz_ref[idx, pl.ds(r0, 3 * 128), :], vzc_ref[idx]],
                    [valid, None], sinks[2 * pr + half]))
            cols.append((outs[0] + outs[1]).astype(BF16))
        o_ref[0, pl.ds(r0, 128), :] = jnp.concatenate(cols, axis=1)
        return carry

    lax.fori_loop(0, tq // 128, sub, 0)


def _attn_lat_call(sink, q, kv, kvc):
    nb, seq, _ = q.shape
    lc = kvc.shape[1]
    tq = TQ
    nblk = seq // 128
    r = tq // 128
    return pl.pallas_call(
        functools.partial(_attn_lat_body, tq=tq, seq=seq),
        grid=(nb, seq // tq),
        in_specs=[pl.BlockSpec(memory_space=pltpu.SMEM),
                  pl.BlockSpec((1, tq, ATTN_W), lambda b, i: (b, i, 0)),
                  pl.BlockSpec((1, 128, 4 * KV_W), lambda b, i: (b, jnp.maximum(i * r - 1, 0), 0)),
                  pl.BlockSpec((1, tq, 4 * KV_W), lambda b, i: (b, i, 0)),
                  pl.BlockSpec((1, 128, 4 * KV_W), lambda b, i: (b, jnp.minimum((i + 1) * r, nblk - 1), 0)),
                  pl.BlockSpec((1, lc, 4 * KV_W), lambda b, i: (b, 0, 0))],
        out_specs=pl.BlockSpec((1, tq, ATTN_W), lambda b, i: (b, i, 0)),
        out_shape=jax.ShapeDtypeStruct((nb, seq, ATTN_W), BF16),
        scratch_shapes=[pltpu.VMEM((4, tq + 256, 128), BF16), pltpu.VMEM((4, tq + 256, 128), BF16),
                        pltpu.VMEM((4, lc, 128), BF16), pltpu.VMEM((4, lc, 128), BF16)],
        compiler_params=_cparams(2),
        name="attention_window",
    )(sink, q, kv, kv, kv, kvc)


def _attn_ctx_body(sink_ref, q_ref, kvc_ref, o_ref, kzc_ref, vzc_ref, *, lc):
    _fill_masked(kvc_ref[0], kzc_ref, vzc_ref)
    sinks = [sink_ref[h] for h in range(N_HEADS)]
    for j in range(lc // 128):
        qblk = q_ref[0, j * 128:(j + 1) * 128, :]
        cols = []
        for pr in range(4):
            kvh = pr // 2
            qp = qblk[:, pr * 128:(pr + 1) * 128]
            outs = [_head_out(qp, [kzc_ref[2 * kvh + half]], [vzc_ref[2 * kvh + half]], [None],
                              sinks[2 * pr + half]) for half in range(2)]
            cols.append((outs[0] + outs[1]).astype(BF16))
        o_ref[0, j * 128:(j + 1) * 128, :] = jnp.concatenate(cols, axis=1)


def _attn_ctx_call(sink, qc, kvc):
    nb, lc, _ = qc.shape
    return pl.pallas_call(
        functools.partial(_attn_ctx_body, lc=lc),
        grid=(nb,),
        in_specs=[pl.BlockSpec(memory_space=pltpu.SMEM),
                  pl.BlockSpec((1, lc, ATTN_W), lambda b: (b, 0, 0)),
                  pl.BlockSpec((1, lc, 4 * KV_W), lambda b: (b, 0, 0))],
        out_specs=pl.BlockSpec((1, lc, ATTN_W), lambda b: (b, 0, 0)),
        out_shape=jax.ShapeDtypeStruct((nb, lc, ATTN_W), BF16),
        scratch_shapes=[pltpu.VMEM((4, lc, 128), BF16), pltpu.VMEM((4, lc, 128), BF16)],
        compiler_params=_cparams(1),
        name="attention_ctx",
    )(sink, qc, kvc)


def _four_body(ua_ref, cb_ref, sb_ref, ca_ref, sa_ref, o_ref, *, nb, scale):
    i = pl.program_id(0)
    ca = ca_ref[pl.ds(i, 1), :]
    sa = sa_ref[pl.ds(i, 1), :]
    cb = cb_ref[...]
    sb = sb_ref[...]
    ct = (ca * cb - sa * sb).astype(BF16)
    st = (sa * cb + ca * sb).astype(BF16)
    for b in range(nb):
        ua = ua_ref[b, :, 0:FOUR_W]
        ub = ua_ref[b, :, FOUR_W:2 * FOUR_W]
        r = (jnp.dot(ct, ua, preferred_element_type=F32) - jnp.dot(st, ub, preferred_element_type=F32))
        o_ref[b] = (r * scale).astype(BF16)


def _four_call(uaub, cb, sb, ca, sa):
    nb, seq, _ = uaub.shape
    tm = cb.shape[0]
    one = pl.Buffered(1)
    return pl.pallas_call(
        functools.partial(_four_body, nb=nb, scale=float(seq) ** -0.5),
        grid=(seq // tm,),
        in_specs=[pl.BlockSpec((nb, seq, 2 * FOUR_W), lambda i: (0, 0, 0), pipeline_mode=one),
                  pl.BlockSpec((tm, seq), lambda i: (0, 0), pipeline_mode=one),
                  pl.BlockSpec((tm, seq), lambda i: (0, 0), pipeline_mode=one),
                  pl.BlockSpec((seq // tm, seq), lambda i: (0, 0), pipeline_mode=one),
                  pl.BlockSpec((seq // tm, seq), lambda i: (0, 0), pipeline_mode=one)],
        out_specs=pl.BlockSpec((nb, tm, FOUR_W), lambda i: (0, i, 0)),
        out_shape=jax.ShapeDtypeStruct((nb, seq, FOUR_W), BF16),
        compiler_params=_cparams(1),
        name="fourier_dft",
    )(uaub, cb, sb, ca, sa)


def _dft_tables(seq, tm):
    n = jnp.arange(seq, dtype=I32)[None, :]

    def tab(rows):
        ang = ((rows[:, None] * n) % seq).astype(F32) * (2.0 * np.pi / seq)
        return jnp.cos(ang), jnp.sin(ang)

    cb, sb = tab(jnp.arange(tm, dtype=I32))
    ca, sa = tab(jnp.arange(seq // tm, dtype=I32) * tm)
    return cb, sb, ca, sa


def _out_body(x_ref, pup_ref, pum_ref, pun_ref, at_ref, fo_ref, wo_ref, pw_ref, ps_ref, mod_ref,
              g2_ref, wr_ref, br_ref, tri_ref, cin_ref,
              xo_ref, rows_ref, meta_ref, cnt_ref, *, tm, seq, per_batch):
    b = pl.program_id(0)
    i = pl.program_id(1)
    row = b if per_batch else CTX_ROW
    g1 = mod_ref[pl.ds(row, 1), pl.ds(2 * D, D)]
    sh2 = mod_ref[pl.ds(row, 1), pl.ds(3 * D, D)]
    sc2 = mod_ref[pl.ds(row, 1), pl.ds(4 * D, D)]

    @pl.when((b == 0) & (i == 0))
    def _():
        cnt_ref[...] = cin_ref[...]

    um = pum_ref[0]
    uext = jnp.concatenate([pup_ref[0], um, pun_ref[0]], axis=0)
    tg = i * tm + lax.broadcasted_iota(I32, (tm, tm + 32), 0)
    sg = i * tm - 16 + lax.broadcasted_iota(I32, (tm, tm + 32), 1)
    tcol = i * tm + lax.broadcasted_iota(I32, (tm, 1), 0)
    grp = lax.broadcasted_iota(I32, (tm, POOL_W), 1) >> 6
    pooled = jnp.zeros((tm, POOL_W), F32)
    for g, w in enumerate(POOL_WINDOWS):
        ind = ((sg >= jnp.maximum(tg - w // 2, 0)) & (sg <= jnp.minimum(tg + w // 2 - 1, seq - 1)))
        r = jnp.dot(ind.astype(BF16), uext, preferred_element_type=F32)
        cnt = jnp.minimum(tcol + w // 2 - 1, seq - 1) - jnp.maximum(tcol - w // 2, 0) + 1
        pooled = jnp.where(grp == g, r * (1.0 / cnt.astype(F32)), pooled)
    y = pooled - um.astype(F32)
    pool_out = jnp.dot(y.astype(BF16), pw_ref[...], preferred_element_type=F32) * ps_ref[...]

    cat = jnp.concatenate([pool_out.astype(BF16), at_ref[0], fo_ref[0]], axis=1)
    xm = x_ref[0] + g1 * jnp.dot(cat, wo_ref[...], preferred_element_type=F32)
    xo_ref[0] = xm

    ms = jnp.mean(xm * xm, axis=-1, keepdims=True)
    h2 = (xm * lax.rsqrt(ms + EPS) * g2_ref[...]) * (1.0 + sc2) + sh2

    logits = jnp.dot(h2, wr_ref[...], preferred_element_type=F32, precision=HI) + br_ref[...]
    lane = lax.broadcasted_iota(I32, (tm, 128), 1)
    big = jnp.int32(999)

    def first_argmax(vals):
        mx = jnp.max(vals, axis=-1, keepdims=True)
        return mx, jnp.min(jnp.where(vals == mx, lane, big), axis=-1, keepdims=True)

    is_grp = lane < 4
    mg, gi = first_argmax(jnp.where(is_grp, logits, NEG_INF))
    pg = 1.0 / jnp.sum(jnp.where(is_grp, jnp.exp(jnp.where(is_grp, logits, mg) - mg), 0.0),
                       axis=-1, keepdims=True)
    est = 4 + 4 * gi
    le = jnp.where((lane >= est) & (lane < est + 4), logits, NEG_INF)
    m1, i1 = first_argmax(le)
    m2, i2 = first_argmax(jnp.where(lane == i1, NEG_INF, le))
    e2 = jnp.exp(m2 - m1)
    w1 = pg / (1.0 + e2)
    w2 = pg * e2 / (1.0 + e2)
    a1 = i1 - est
    a2 = i2 - est
    code = jnp.minimum(a1, a2) * 4 + jnp.maximum(a1, a2)
    pidx = jnp.where(code == 1, 0, jnp.where(code == 6, 1, jnp.where(code == 2, 2,
           jnp.where(code == 3, 3, jnp.where(code == 7, 4, 5)))))
    slot_a = jnp.where(pidx == 0, 0, jnp.where(pidx <= 2, 2, 3))
    slot_b = jnp.where(pidx <= 1, 1, jnp.where(pidx <= 3, 0, jnp.where(pidx == 4, 1, 2)))
    wa = jnp.where(a1 == slot_a, w1, w2)
    wb = jnp.where(a1 == slot_b, w1, w2)
    bin_ = gi * 6 + pidx

    onehot = lane == bin_
    pref = jnp.dot(tri_ref[...], onehot.astype(BF16), preferred_element_type=F32)
    carry = cnt_ref[0:1, :]
    rank = jnp.sum(jnp.where(onehot, pref - 1.0 + carry, 0.0), axis=-1, keepdims=True)
    cnt_ref[...] = jnp.broadcast_to(carry + pref[tm - 1:tm, :], cnt_ref.shape)

    rows_ref[:, 0:D] = h2
    rows_ref[:, D:ROW_W] = jnp.where(lane == 0, wa, jnp.where(lane == 1, wb, 0.0))
    meta_ref[...] = jnp.where(lane == 0, bin_.astype(F32), jnp.where(lane == 1, rank, 0.0))


def _out_call(x3, pu, attn, four, w_out, pool_wbd, pool_scale, mod, g2, wr, br, tri, cnt_in, *, per_batch):
    nb, seq, _ = x3.shape
    tm = TM_OUT
    nt = seq // tm
    hb = tm // 16
    n_rows = nb * seq
    full = lambda shape: pl.BlockSpec(shape, lambda b, i: (0,) * len(shape))
    in_specs = [pl.BlockSpec((1, tm, D), lambda b, i: (b, i, 0)),
                pl.BlockSpec((1, 16, POOL_W), lambda b, i: (b, jnp.maximum(i * hb - 1, 0), 0)),
                pl.BlockSpec((1, tm, POOL_W), lambda b, i: (b, i, 0)),
                pl.BlockSpec((1, 16, POOL_W), lambda b, i: (b, jnp.minimum((i + 1) * hb, seq // 16 - 1), 0)),
                pl.BlockSpec((1, tm, ATTN_W), lambda b, i: (b, i, 0)),
                pl.BlockSpec((1, tm, FOUR_W), lambda b, i: (b, i, 0)),
                full((D, D)), full((POOL_W, POOL_W)), full((1, POOL_W)), full((8, 6 * D)), full((1, D)),
                full((D, 128)), full((1, 128)), full((tm, tm)), full((8, 128))]
    args = [x3, pu, pu, pu, attn, four, w_out, pool_wbd, pool_scale, mod, g2, wr, br, tri, cnt_in]
    return pl.pallas_call(
        functools.partial(_out_body, tm=tm, seq=seq, per_batch=per_batch),
        grid=(nb, nt),
        in_specs=in_specs,
        out_specs=[pl.BlockSpec((1, tm, D), lambda b, i: (b, i, 0)),
                   pl.BlockSpec((tm, ROW_W), lambda b, i: (b * nt + i, 0)),
                   pl.BlockSpec((tm, 128), lambda b, i: (b * nt + i, 0)),
                   pl.BlockSpec((8, 128), lambda b, i: (0, 0))],
        out_shape=[jax.ShapeDtypeStruct((nb, seq, D), F32),
                   jax.ShapeDtypeStruct((n_rows, ROW_W), F32),
                   jax.ShapeDtypeStruct((n_rows, 128), F32),
                   jax.ShapeDtypeStruct((8, 128), F32)],
        compiler_params=_cparams(2),
        name="out_proj_router" if per_batch else "out_proj_router_ctx",
    )(*args)


def _scatter_body(dest_ref, h_ref, xs_in_ref, xs_ref, sem, *, tm):
    del xs_in_ref
    base = pl.program_id(0) * tm

    def issue(r, c):
        d = dest_ref[base + r]
        pltpu.make_async_copy(h_ref.at[pl.ds(r, 1)], xs_ref.at[pl.ds(d, 1)], sem).start()
        return c

    lax.fori_loop(0, tm, issue, 0, unroll=8)

    def drain(r, c):
        pltpu.make_async_copy(h_ref.at[pl.ds(0, 1)], xs_ref.at[pl.ds(0, 1)], sem).wait()
        return c

    lax.fori_loop(0, tm, drain, 0, unroll=8)


def _scatter_call(dest, rows, xs_init):
    n_rows = rows.shape[0]
    tm = TM_ROW
    return pl.pallas_call(
        functools.partial(_scatter_body, tm=tm),
        grid_spec=pltpu.PrefetchScalarGridSpec(
            num_scalar_prefetch=1,
            grid=(n_rows // tm,),
            in_specs=[pl.BlockSpec((tm, ROW_W), lambda i, d: (i, 0)),
                      pl.BlockSpec(memory_space=pl.ANY)],
            out_specs=pl.BlockSpec(memory_space=pl.ANY),
            scratch_shapes=[pltpu.SemaphoreType.DMA(())]),
        out_shape=jax.ShapeDtypeStruct(xs_init.shape, F32),
        input_output_aliases={2: 0},
        compiler_params=_cparams(1),
        name="moe_scatter_rows",
    )(dest, rows, xs_init)


def _moe_body(ea_ref, eb_ref, nv_ref, xs_ref, wga, wua, wda, wgb, wub, wdb, ys_ref):
    del ea_ref, eb_ref
    live = pl.program_id(0) < nv_ref[0]

    @pl.when(jnp.logical_not(live))
    def _():
        ys_ref[...] = jnp.zeros_like(ys_ref)

    @pl.when(live)
    def _():
        xs = xs_ref[...]
        h = xs[:, 0:D].astype(BF16)

        def expert(wg, wu, wd, w):
            g = jnp.dot(h, wg[0].astype(BF16), preferred_element_type=F32)
            u = jnp.dot(h, wu[0].astype(BF16), preferred_element_type=F32)
            a = _silu(g) * u * w
            return jnp.dot(a.astype(BF16), wd[0].astype(BF16), preferred_element_type=F32)

        ys_ref[...] = (expert(wga, wua, wda, xs[:, D:D + 1]) + expert(wgb, wub, wdb, xs[:, D + 1:D + 2]))


def _moe_call(tile_ea, tile_eb, n_valid, xs, w_gate, w_up, w_down):
    tm = TM_MOE
    n_tiles = xs.shape[0] // tm
    row = lambda s, ea, eb, nv: (jnp.minimum(s, nv[0] - 1), 0)
    wa = lambda s, ea, eb, nv: (ea[s], 0, 0)
    wb = lambda s, ea, eb, nv: (eb[s], 0, 0)
    up_spec = lambda f: pl.BlockSpec((1, D, D_EXPERT), f)
    dn_spec = lambda f: pl.BlockSpec((1, D_EXPERT, D), f)
    return pl.pallas_call(
        _moe_body,
        grid_spec=pltpu.PrefetchScalarGridSpec(
            num_scalar_prefetch=3,
            grid=(n_tiles,),
            in_specs=[pl.BlockSpec((tm, ROW_W), row),
                      up_spec(wa), up_spec(wa), dn_spec(wa), up_spec(wb), up_spec(wb), dn_spec(wb)],
            out_specs=pl.BlockSpec((tm, D), lambda s, ea, eb, nv: (s, 0))),
        out_shape=jax.ShapeDtypeStruct((xs.shape[0], D), F32),
        compiler_params=_cparams(1),
        name="moe_experts",
    )(tile_ea, tile_eb, n_valid, xs, w_gate, w_up, w_down, w_gate, w_up, w_down)


def _gather_body(dest_ref, x_ref, mod_ref, ys_ref, o_ref, ybuf, sem, *, tm, seq, per_batch):
    i = pl.program_id(0)
    base = i * tm

    def issue(r, c):
        d = dest_ref[base + r]
        pltpu.make_async_copy(ys_ref.at[pl.ds(d, 1)], ybuf.at[pl.ds(r, 1)], sem).start()
        return c

    lax.fori_loop(0, tm, issue, 0, unroll=8)

    def drain(r, c):
        pltpu.make_async_copy(ys_ref.at[pl.ds(0, 1)], ybuf.at[pl.ds(0, 1)], sem).wait()
        return c

    lax.fori_loop(0, tm, drain, 0, unroll=8)
    row = (i * tm) // seq if per_batch else CTX_ROW
    g2 = mod_ref[pl.ds(row, 1), pl.ds(5 * D, D)]
    o_ref[...] = x_ref[...] + g2 * ybuf[...]


def _gather_call(dest, x2, mod, ys, *, seq, per_batch):
    n = x2.shape[0]
    tm = TM_ROW
    return pl.pallas_call(
        functools.partial(_gather_body, tm=tm, seq=seq, per_batch=per_batch),
        grid_spec=pltpu.PrefetchScalarGridSpec(
            num_scalar_prefetch=1,
            grid=(n // tm,),
            in_specs=[pl.BlockSpec((tm, D), lambda i, d: (i, 0)),
                      pl.BlockSpec((8, 6 * D), lambda i, d: (0, 0)),
                      pl.BlockSpec(memory_space=pl.ANY)],
            out_specs=pl.BlockSpec((tm, D), lambda i, d: (i, 0)),
            scratch_shapes=[pltpu.VMEM((tm, D), F32), pltpu.SemaphoreType.DMA(())]),
        out_shape=jax.ShapeDtypeStruct((n, D), F32),
        compiler_params=_cparams(1),
        name="moe_gather_rows" if per_batch else "moe_gather_rows_ctx",
    )(dest, x2, mod, ys)


def _routing_tables(metas, cnt, n_tiles):
    tm = TM_MOE
    counts = cnt[0, :N_BINS].astype(I32)
    tiles = (counts + tm - 1) // tm
    tile_end = jnp.cumsum(tiles)
    starts = (tile_end - tiles) * tm
    ids = jnp.arange(N_BINS, dtype=I32)
    dest = [meta[:, 1].astype(I32)
            + jnp.sum(jnp.where(meta[:, 0].astype(I32)[:, None] == ids[None, :], starts[None, :], 0), axis=1)
            for meta in metas]
    n_valid = tile_end[-1]
    s = jnp.minimum(jnp.arange(n_tiles, dtype=I32), n_valid - 1)
    tile_bin = jnp.sum((s[:, None] >= tile_end[None, :]).astype(I32), axis=1)
    grp = tile_bin // 6
    pidx = tile_bin % 6
    slot_a = jnp.asarray(PAIR_SLOT_A, I32)
    slot_b = jnp.asarray(PAIR_SLOT_B, I32)
    sel = lambda tab: jnp.sum(jnp.where(pidx[:, None] == jnp.arange(6, dtype=I32)[None, :], tab[None, :], 0), axis=1)
    return dest, 4 * grp + sel(slot_a), 4 * grp + sel(slot_b), n_valid.reshape(1)


def _block_diag(w):
    g, c, d = w.shape
    out = jnp.zeros((g * c, g * d), w.dtype)
    for j in range(g):
        out = out.at[j * c:(j + 1) * c, j * d:(j + 1) * d].set(w[j])
    return out


def _rope_tables(n_tokens):
    rows = n_tokens // GRID_W
    r = jnp.broadcast_to(jnp.arange(rows)[:, None], (rows, GRID_W)).reshape(-1).astype(F32)
    col = jnp.broadcast_to(jnp.arange(GRID_W)[None, :], (rows, GRID_W)).reshape(-1).astype(F32)
    half = HEAD_DIM // 2
    inv = 1.0 / (ROPE_BASE ** (jnp.arange(0, half, 2, dtype=F32) / half))
    ar = r[:, None] * inv
    ac = col[:, None] * inv
    ang = jnp.concatenate([ar, ar, ac, ac], axis=-1)
    cos, sin = jnp.cos(ang), jnp.sin(ang)
    sign = jnp.where((jnp.arange(HEAD_DIM) & 16) == 0, -1.0, 1.0).astype(F32)
    return jnp.concatenate([cos, cos], axis=1), jnp.concatenate([sin * sign, sin * sign], axis=1)


def kernel(x, c, ctx, c_ctx, w_mod, b_mod, norm1_g, w_in, q_norm_g, k_norm_g, attn_sink, pool_w, pool_scale,
           four_w, w_out, norm2_g, w_grp, b_grp, w_rtr, b_rtr, w_gate, w_up, w_down):
    nb, seq, _ = x.shape
    lc = ctx.shape[1]
    depth = w_mod.shape[0]
    t_lat = nb * seq
    t_ctx = nb * lc

    cs = jnp.concatenate([c, c_ctx[None, :], jnp.zeros((8 - nb - 1, D), F32)], axis=0)
    m512 = jnp.asarray(np.kron(np.eye(N_HEADS), np.full((HEAD_DIM, HEAD_DIM), 1.0 / HEAD_DIM)), BF16)
    kk = np.arange(HEAD_DIM)
    ang64 = 2.0 * np.pi * ((kk[:, None] * kk[None, :]) % HEAD_DIM) / HEAD_DIM
    c64bd = jnp.asarray(np.kron(np.eye(4), np.cos(ang64)), F32)
    s64bd = jnp.asarray(np.kron(np.eye(4), np.sin(ang64)), F32)
    tri = jnp.asarray(np.tril(np.ones((TM_OUT, TM_OUT))), BF16)
    cos2, sin2 = _rope_tables(seq)
    tabs_lat = _dft_tables(seq, TM_FOUR)
    tabs_ctx = _dft_tables(lc, min(TM_FOUR, lc))
    w_in_b = w_in.astype(BF16)
    w_out_b = w_out.astype(BF16)
    wbd = jnp.stack([_block_diag(four_w[l]) for l in range(depth)])
    pool_wbd = jnp.stack([_block_diag(pool_w[l]) for l in range(depth)]).astype(BF16)
    wr = jnp.concatenate([w_grp, w_rtr, jnp.zeros((depth, D, 128 - 4 - N_EXPERTS), F32)], axis=2)
    br = jnp.concatenate([b_grp, b_rtr, jnp.zeros((depth, 128 - 4 - N_EXPERTS), F32)], axis=1)

    mod_all = _mod_call(cs, w_mod, b_mod)
    ab_all = _ab_call(c64bd, s64bd, wbd)

    xc = ctx
    for l in range(depth):
        last = l == depth - 1
        mod = mod_all[l]
        g1 = norm1_g[l][None, :]
        g2 = norm2_g[l][None, :]
        qg = jnp.tile(q_norm_g[l], N_HEADS)[None, :]
        kg = jnp.tile(k_norm_g[l], KV_W // HEAD_DIM)[None, :]
        in_args = (mod, g1, w_in_b[l], m512, qg, kg, ab_all[l])

        puc, qc, kvc, uac = _in_call(xc, *in_args, None, None, rope=False, per_batch=False)
        pul, ql, kvl, ual = _in_call(x, *in_args, cos2, sin2, rope=True, per_batch=True)
        attn_l = _attn_lat_call(attn_sink[l], ql, kvl, kvc)
        four_l = _four_call(ual, *tabs_lat)

        n_rows = t_lat if last else t_lat + t_ctx
        out_args = (w_out_b[l], pool_wbd[l], pool_scale[l][None, :], mod, g2, wr[l], br[l][None, :], tri)
        cnt0 = jnp.zeros((8, 128), F32)
        x_mid, rows_l, meta_l, cnt = _out_call(x, pul, attn_l, four_l, *out_args, cnt0, per_batch=True)
        metas = [meta_l]
        if not last:
            attn_c = _attn_ctx_call(attn_sink[l], qc, kvc)
            four_c = _four_call(uac, *tabs_ctx)
            xc_mid, rows_c, meta_c, cnt = _out_call(xc, puc, attn_c, four_c, *out_args, cnt, per_batch=False)
            metas.append(meta_c)

        n_tiles = n_rows // TM_MOE + N_BINS
        dests, tile_ea, tile_eb, n_valid = _routing_tables(metas, cnt, n_tiles)
        xs = _scatter_call(dests[0], rows_l, jnp.zeros((n_tiles * TM_MOE, ROW_W), F32))
        if not last:
            xs = _scatter_call(dests[1], rows_c, xs)
        ys = _moe_call(tile_ea, tile_eb, n_valid, xs, w_gate[l], w_up[l], w_down[l])
        x = _gather_call(dests[0], x_mid.reshape(t_lat, D), mod, ys, seq=seq, per_batch=True).reshape(nb, seq, D)
        if not last:
            xc = _gather_call(dests[1], xc_mid.reshape(t_ctx, D), mod, ys, seq=lc,
                              per_batch=False).reshape(nb, lc, D)
    return x
```

```python
import functools

import numpy as np
import jax
import jax.numpy as jnp
from jax import lax
from jax.experimental import pallas as pl
from jax.experimental.pallas import tpu as pltpu

F32 = jnp.float32
BF16 = jnp.bfloat16
I32 = jnp.int32
HI = lax.Precision.HIGHEST

D = 1024
HEAD_DIM = 64
N_HEADS = 8
GRID_W = 64
POOL_WINDOWS = (2, 4, 8, 16)
POOL_W = 256
ATTN_W = 512
KV_W = 128
FOUR_W = 256
IN_W = 1280
N_EXPERTS = 16
D_EXPERT = 512
WINDOW = 128
ROPE_BASE = 10000.0
EPS = 1e-6
NEG_INF = -1e30
CTX_ROW = 4
N_BINS = 24
PAIR_SLOT_A = (0, 2, 2, 3, 3, 3)
PAIR_SLOT_B = (1, 1, 0, 0, 1, 2)
META_W = 128
ROW_W = D + META_W
HALO = 16

VMEM_LIMIT = 56 * 1024 * 1024
TM_IN = 512
TQ = 512
TM_OUT = 256
TM_FOUR = 256
TM_MOE = 256
TM_ROW = 256
MOD_TN = 1024


def _cparams(n_axes):
    return pltpu.CompilerParams(dimension_semantics=("arbitrary",) * n_axes,
                                vmem_limit_bytes=VMEM_LIMIT)


def _silu(v):
    return v / (1.0 + jnp.exp(-v))


def _mod_body(cs_ref, w_ref, b_ref, o_ref):
    s = _silu(cs_ref[...])
    o_ref[0] = jnp.dot(s, w_ref[0], preferred_element_type=F32, precision=HI) + b_ref[0]


def _mod_call(cs, w_mod, b_mod):
    depth = w_mod.shape[0]
    return pl.pallas_call(
        _mod_body,
        grid=(depth, 6 * D // MOD_TN),
        in_specs=[pl.BlockSpec((8, D), lambda l, j: (0, 0)),
                  pl.BlockSpec((1, D, MOD_TN), lambda l, j: (l, 0, j)),
                  pl.BlockSpec((1, 1, MOD_TN), lambda l, j: (l, 0, j))],
        out_specs=pl.BlockSpec((1, 8, MOD_TN), lambda l, j: (l, 0, j)),
        out_shape=jax.ShapeDtypeStruct((depth, 8, 6 * D), F32),
        compiler_params=_cparams(2),
        name="modulation",
    )(cs, w_mod, b_mod.reshape(depth, 1, 6 * D))


def _ab_body(c_ref, s_ref, w_ref, o_ref):
    w = w_ref[0]
    ca = jnp.dot(c_ref[...], w, preferred_element_type=F32, precision=HI)
    sa = jnp.dot(s_ref[...], w, preferred_element_type=F32, precision=HI)
    o_ref[0] = (jnp.concatenate([ca, sa], axis=1) * (HEAD_DIM ** -0.5)).astype(BF16)


def _ab_call(c64bd, s64bd, wbd):
    depth = wbd.shape[0]
    return pl.pallas_call(
        _ab_body,
        grid=(depth,),
        in_specs=[pl.BlockSpec((FOUR_W, FOUR_W), lambda l: (0, 0)),
                  pl.BlockSpec((FOUR_W, FOUR_W), lambda l: (0, 0)),
                  pl.BlockSpec((1, FOUR_W, FOUR_W), lambda l: (l, 0, 0))],
        out_specs=pl.BlockSpec((1, FOUR_W, 2 * FOUR_W), lambda l: (l, 0, 0)),
        out_shape=jax.ShapeDtypeStruct((depth, FOUR_W, 2 * FOUR_W), BF16),
        compiler_params=_cparams(1),
        name="fourier_weights",
    )(c64bd, s64bd, wbd)


def _head_rms(t, m, g):
    tt = t * t
    hi = tt.astype(BF16)
    lo = (tt - hi.astype(F32)).astype(BF16)
    ms = (jnp.dot(hi, m, preferred_element_type=F32) + jnp.dot(lo, m, preferred_element_type=F32))
    return t * lax.rsqrt(ms + EPS) * g


def _rope(t, cos, sin_signed):
    w = t.shape[1]
    lane = lax.broadcasted_iota(I32, t.shape, 1)
    fwd = pltpu.roll(t, w - 16, 1)
    bwd = pltpu.roll(t, 16, 1)
    rot = jnp.where((lane & 16) == 0, fwd, bwd)
    return t * cos + rot * sin_signed


def _in_body(*refs, rope, per_batch):
    if rope:
        (x_ref, mod_ref, g1_ref, w_ref, m_ref, qg_ref, kg_ref, ab_ref, cos_ref, sin_ref,
         pu_ref, q_ref, kv_ref, ua_ref) = refs
    else:
        (x_ref, mod_ref, g1_ref, w_ref, m_ref, qg_ref, kg_ref, ab_ref,
         pu_ref, q_ref, kv_ref, ua_ref) = refs
    row = pl.program_id(0) if per_batch else CTX_ROW
    sh1 = mod_ref[pl.ds(row, 1), pl.ds(0, D)]
    sc1 = mod_ref[pl.ds(row, 1), pl.ds(D, D)]
    x = x_ref[0]
    ms = jnp.mean(x * x, axis=-1, keepdims=True)
    h = (x * lax.rsqrt(ms + EPS) * g1_ref[...]) * (1.0 + sc1) + sh1
    p = jnp.dot(h.astype(BF16), w_ref[...], preferred_element_type=F32)
    pu = p[:, 0:256]
    q = p[:, 256:768]
    k = p[:, 768:896]
    v = p[:, 896:1024]
    fu = p[:, 1024:1280]
    m = m_ref[...]
    q = _head_rms(q, m, qg_ref[...])
    k = _head_rms(k, m[0:KV_W, 0:KV_W], kg_ref[...])
    if rope:
        cos = cos_ref[...]
        sin = sin_ref[...]
        q = _rope(q, jnp.concatenate([cos] * 4, axis=1), jnp.concatenate([sin] * 4, axis=1))
        k = _rope(k, cos, sin)
    q = q * (HEAD_DIM ** -0.5)
    pu_ref[0] = pu.astype(BF16)
    q_ref[0] = q.astype(BF16)
    kv_ref[0] = jnp.concatenate([k, pltpu.roll(k, 64, 1), v, pltpu.roll(v, 64, 1)], axis=1).astype(BF16)
    ua_ref[0] = jnp.dot(fu.astype(BF16), ab_ref[...], preferred_element_type=F32).astype(BF16)


def _in_call(x3, mod, g1, w_in, m512, qg, kg, ab, cos2, sin2, *, rope, per_batch):
    nb, seq, _ = x3.shape
    tm = min(TM_IN, seq)
    full = lambda shape: pl.BlockSpec(shape, lambda b, i: (0,) * len(shape))
    in_specs = [pl.BlockSpec((1, tm, D), lambda b, i: (b, i, 0)),
                full((8, 6 * D)), full((1, D)), full((D, IN_W)), full((ATTN_W, ATTN_W)),
                full((1, ATTN_W)), full((1, KV_W)), full((FOUR_W, 2 * FOUR_W))]
    args = [x3, mod, g1, w_in, m512, qg, kg, ab]
    if rope:
        in_specs += [pl.BlockSpec((tm, 128), lambda b, i: (i, 0)),
                     pl.BlockSpec((tm, 128), lambda b, i: (i, 0))]
        args += [cos2, sin2]
    widths = (POOL_W, ATTN_W, 4 * KV_W, 2 * FOUR_W)
    return pl.pallas_call(
        functools.partial(_in_body, rope=rope, per_batch=per_batch),
        grid=(nb, seq // tm),
        in_specs=in_specs,
        out_specs=[pl.BlockSpec((1, tm, w), lambda b, i: (b, i, 0)) for w in widths],
        out_shape=[jax.ShapeDtypeStruct((nb, seq, w), BF16) for w in widths],
        compiler_params=_cparams(2),
        name="in_proj_rope" if rope else "in_proj_ctx",
    )(*args)


_NT = (((1,), (1,)), ((), ()))


def _stack_heads(qpair0, qpair1, lo):
    z = jnp.zeros_like(qpair0)
    parts = [jnp.where(lo, qpair0, z), jnp.where(lo, qpair1, z),
             jnp.where(lo, pltpu.roll(qpair0, 64, 1), z), jnp.where(lo, pltpu.roll(qpair1, 64, 1), z)]
    return jnp.concatenate(parts, axis=0).astype(BF16)


def _group_attention(q4, k_parts, va_parts, vb_parts, masks, sink_col):
    s_parts = []
    for kz, mk in zip(k_parts, masks):
        s = lax.dot_general(q4, kz, _NT, preferred_element_type=F32)
        if mk is not None:
            nk = s.shape[1]
            s = jnp.where(mk[None], s.reshape(4, 128, nk), NEG_INF).reshape(512, nk)
        s_parts.append(s)
    m = functools.reduce(jnp.maximum, [jnp.max(s, axis=-1, keepdims=True) for s in s_parts])
    m = jnp.maximum(m, sink_col)
    den = jnp.exp(sink_col - m)
    oe = oo = None
    for s, va, vb in zip(s_parts, va_parts, vb_parts):
        e = jnp.exp(s - m)
        den = den + jnp.sum(e, axis=-1, keepdims=True)
        eb = e.astype(BF16)
        pe = jnp.dot(eb[0:256], va, preferred_element_type=F32)
        po = jnp.dot(eb[256:512], vb, preferred_element_type=F32)
        oe = pe if oe is None else oe + pe
        oo = po if oo is None else oo + po
    inv = 1.0 / den
    return oe * inv[0:256], oo * inv[256:512]


def _sink_cols(sink_ref):
    rb = lax.broadcasted_iota(I32, (512, 1), 0) >> 7
    cols = []
    for kvh in range(2):
        s = [sink_ref[4 * kvh + j] for j in (0, 2, 1, 3)]
        cols.append(jnp.where(rb == 0, s[0], jnp.where(rb == 1, s[1], jnp.where(rb == 2, s[2], s[3]))))
    return cols


def _attend_block(qblk, kv_parts, masks, sink_cols):
    lo = lax.broadcasted_iota(I32, (128, 128), 1) < 64
    cols = []
    for kvh in range(2):
        q4 = _stack_heads(qblk[:, 256 * kvh:256 * kvh + 128], qblk[:, 256 * kvh + 128:256 * kvh + 256], lo)
        ko = 128 * kvh
        vao = 256 + 128 * kvh
        vbo = 384 - 128 * kvh
        oe, oo = _group_attention(q4, [kv[:, ko:ko + 128] for kv in kv_parts],
                                  [kv[:, vao:vao + 128] for kv in kv_parts],
                                  [kv[:, vbo:vbo + 128] for kv in kv_parts], masks, sink_cols[kvh])
        cols.append(jnp.where(lo, oe[0:128], oo[0:128]))
        cols.append(jnp.where(lo, oe[128:256], oo[128:256]))
    return jnp.concatenate(cols, axis=1).astype(BF16)


def _attn_lat_body(sink_ref, q_ref, kvp_ref, kvm_ref, kvn_ref, kvc_ref, o_ref, kvw_ref, *, tq, seq):
    i = pl.program_id(1)
    kvw_ref[0:128] = kvp_ref[0]
    kvw_ref[128:128 + tq] = kvm_ref[0]
    kvw_ref[128 + tq:256 + tq] = kvn_ref[0]
    kvc = kvc_ref[0]
    sink_cols = _sink_cols(sink_ref)

    def sub(j, carry):
        r0 = pl.multiple_of(j * 128, 128)
        win = kvw_ref[pl.ds(r0, 3 * 128), :]
        ii = lax.broadcasted_iota(I32, (128, 3 * 128), 0)
        cc = lax.broadcasted_iota(I32, (128, 3 * 128), 1)
        base = i * tq + j * 128 - 128
        valid = (ii <= cc) & (cc <= ii + 2 * WINDOW) & (cc >= -base) & (cc < seq - base)
        qblk = q_ref[0, pl.ds(r0, 128), :].astype(F32)
        o_ref[0, pl.ds(r0, 128), :] = _attend_block(qblk, [win, kvc], [valid, None], sink_cols)
        return carry

    lax.fori_loop(0, tq // 128, sub, 0)


def _attn_lat_call(sink, q, kv, kvc):
    nb, seq, _ = q.shape
    lc = kvc.shape[1]
    tq = TQ
    nblk = seq // 128
    r = tq // 128
    return pl.pallas_call(
        functools.partial(_attn_lat_body, tq=tq, seq=seq),
        grid=(nb, seq // tq),
        in_specs=[pl.BlockSpec(memory_space=pltpu.SMEM),
                  pl.BlockSpec((1, tq, ATTN_W), lambda b, i: (b, i, 0)),
                  pl.BlockSpec((1, 128, 4 * KV_W), lambda b, i: (b, jnp.maximum(i * r - 1, 0), 0)),
                  pl.BlockSpec((1, tq, 4 * KV_W), lambda b, i: (b, i, 0)),
                  pl.BlockSpec((1, 128, 4 * KV_W), lambda b, i: (b, jnp.minimum((i + 1) * r, nblk - 1), 0)),
                  pl.BlockSpec((1, lc, 4 * KV_W), lambda b, i: (b, 0, 0))],
        out_specs=pl.BlockSpec((1, tq, ATTN_W), lambda b, i: (b, i, 0)),
        out_shape=jax.ShapeDtypeStruct((nb, seq, ATTN_W), BF16),
        scratch_shapes=[pltpu.VMEM((tq + 256, 4 * KV_W), BF16)],
        compiler_params=_cparams(2),
        name="attention_window",
    )(sink, q, kv, kv, kv, kvc)


def _attn_ctx_body(sink_ref, q_ref, kvc_ref, o_ref, *, lc):
    kvc = kvc_ref[0]
    sink_cols = _sink_cols(sink_ref)
    for j in range(lc // 128):
        qblk = q_ref[0, j * 128:(j + 1) * 128, :].astype(F32)
        o_ref[0, j * 128:(j + 1) * 128, :] = _attend_block(qblk, [kvc], [None], sink_cols)


def _attn_ctx_call(sink, qc, kvc):
    nb, lc, _ = qc.shape
    return pl.pallas_call(
        functools.partial(_attn_ctx_body, lc=lc),
        grid=(nb,),
        in_specs=[pl.BlockSpec(memory_space=pltpu.SMEM),
                  pl.BlockSpec((1, lc, ATTN_W), lambda b: (b, 0, 0)),
                  pl.BlockSpec((1, lc, 4 * KV_W), lambda b: (b, 0, 0))],
        out_specs=pl.BlockSpec((1, lc, ATTN_W), lambda b: (b, 0, 0)),
        out_shape=jax.ShapeDtypeStruct((nb, lc, ATTN_W), BF16),
        compiler_params=_cparams(1),
        name="attention_ctx",
    )(sink, qc, kvc)


def _four_body(ua_ref, cb_ref, sb_ref, ca_ref, sa_ref, o_ref, *, nb, scale):
    i = pl.program_id(0)
    ca = ca_ref[pl.ds(i, 1), :]
    sa = sa_ref[pl.ds(i, 1), :]
    cb = cb_ref[...]
    sb = sb_ref[...]
    ct = (ca * cb - sa * sb).astype(BF16)
    st = (sa * cb + ca * sb).astype(BF16)
    for b in range(nb):
        ua = ua_ref[b, :, 0:FOUR_W]
        ub = ua_ref[b, :, FOUR_W:2 * FOUR_W]
        r = (jnp.dot(ct, ua, preferred_element_type=F32) - jnp.dot(st, ub, preferred_element_type=F32))
        o_ref[b] = (r * scale).astype(BF16)


def _four_call(uaub, cb, sb, ca, sa):
    nb, seq, _ = uaub.shape
    tm = cb.shape[0]
    one = pl.Buffered(1)
    return pl.pallas_call(
        functools.partial(_four_body, nb=nb, scale=float(seq) ** -0.5),
        grid=(seq // tm,),
        in_specs=[pl.BlockSpec((nb, seq, 2 * FOUR_W), lambda i: (0, 0, 0), pipeline_mode=one),
                  pl.BlockSpec((tm, seq), lambda i: (0, 0), pipeline_mode=one),
                  pl.BlockSpec((tm, seq), lambda i: (0, 0), pipeline_mode=one),
                  pl.BlockSpec((seq // tm, seq), lambda i: (0, 0), pipeline_mode=one),
                  pl.BlockSpec((seq // tm, seq), lambda i: (0, 0), pipeline_mode=one)],
        out_specs=pl.BlockSpec((nb, tm, FOUR_W), lambda i: (0, i, 0)),
        out_shape=jax.ShapeDtypeStruct((nb, seq, FOUR_W), BF16),
        compiler_params=_cparams(1),
        name="fourier_dft",
    )(uaub, cb, sb, ca, sa)


def _dft_tables(seq, tm):
    n = jnp.arange(seq, dtype=I32)[None, :]

    def tab(rows):
        ang = ((rows[:, None] * n) % seq).astype(F32) * (2.0 * np.pi / seq)
        return jnp.cos(ang), jnp.sin(ang)

    cb, sb = tab(jnp.arange(tm, dtype=I32))
    ca, sa = tab(jnp.arange(seq // tm, dtype=I32) * tm)
    return cb, sb, ca, sa


def _route(lt, tri_ref, cnt_ref, tm):
    rowi = lax.broadcasted_iota(I32, (32, tm), 0)
    big = jnp.int32(999)

    def first_argmax(vals):
        mx = jnp.max(vals, axis=0, keepdims=True)
        return mx, jnp.min(jnp.where(vals == mx, rowi, big), axis=0, keepdims=True)

    is_grp = rowi < 4
    mg, gi = first_argmax(jnp.where(is_grp, lt, NEG_INF))
    pg = 1.0 / jnp.sum(jnp.where(is_grp, jnp.exp(jnp.where(is_grp, lt, mg) - mg), 0.0), axis=0, keepdims=True)
    est = 4 + 4 * gi
    le = jnp.where((rowi >= est) & (rowi < est + 4), lt, NEG_INF)
    m1, i1 = first_argmax(le)
    m2, i2 = first_argmax(jnp.where(rowi == i1, NEG_INF, le))
    e2 = jnp.exp(m2 - m1)
    w1 = pg / (1.0 + e2)
    w2 = pg * e2 / (1.0 + e2)
    a1 = i1 - est
    a2 = i2 - est
    code = jnp.minimum(a1, a2) * 4 + jnp.maximum(a1, a2)
    pidx = jnp.where(code == 1, 0, jnp.where(code == 6, 1, jnp.where(code == 2, 2,
           jnp.where(code == 3, 3, jnp.where(code == 7, 4, 5)))))
    slot_a = jnp.where(pidx == 0, 0, jnp.where(pidx <= 2, 2, 3))
    slot_b = jnp.where(pidx <= 1, 1, jnp.where(pidx <= 3, 0, jnp.where(pidx == 4, 1, 2)))
    wa = jnp.where(a1 == slot_a, w1, w2)
    wb = jnp.where(a1 == slot_b, w1, w2)
    bin_ = gi * 6 + pidx

    onehot = rowi == bin_
    pref = jnp.dot(onehot.astype(BF16), tri_ref[...], preferred_element_type=F32)
    carry = cnt_ref[:, 0:1]
    rank = jnp.sum(jnp.where(onehot, pref - 1.0 + carry, 0.0), axis=0, keepdims=True)
    cnt_ref[...] = jnp.broadcast_to(carry + pref[:, tm - 1:tm], cnt_ref.shape)
    return jnp.concatenate([bin_.astype(F32), rank, wa, wb, jnp.zeros((128 - 4, tm), F32)], axis=0)


def _out_body(x_ref, pup_ref, pum_ref, pun_ref, at_ref, fo_ref, wo_ref, pw_ref, ps_ref, band_ref, icnt_ref,
              mod_ref, g2_ref, w2_ref, br_ref, tri_ref, cin_ref,
              xo_ref, rows_ref, cnt_ref, *, tm, per_batch):
    b = pl.program_id(0)
    i = pl.program_id(1)
    nt = pl.num_programs(1)
    row = b if per_batch else CTX_ROW
    g1 = mod_ref[pl.ds(row, 1), pl.ds(2 * D, D)]
    sh2 = mod_ref[pl.ds(row, 1), pl.ds(3 * D, D)]
    sc2 = mod_ref[pl.ds(row, 1), pl.ds(4 * D, D)]

    @pl.when((b == 0) & (i == 0))
    def _():
        cnt_ref[...] = cin_ref[...]

    um = pum_ref[0]
    zh = jnp.zeros((HALO, POOL_W), BF16)
    uext = jnp.concatenate([jnp.where(i > 0, pup_ref[0], zh), um, jnp.where(i < nt - 1, pun_ref[0], zh)], axis=0)
    grp = lax.broadcasted_iota(I32, (tm, POOL_W), 1) >> 6
    pooled = jnp.zeros((tm, POOL_W), F32)
    for g in range(len(POOL_WINDOWS)):
        pooled = jnp.where(grp == g, jnp.dot(band_ref[g], uext, preferred_element_type=F32), pooled)
    y = pooled * icnt_ref[...] - um.astype(F32)
    pool_out = jnp.dot(y.astype(BF16), pw_ref[...], preferred_element_type=F32) * ps_ref[...]

    cat = jnp.concatenate([pool_out.astype(BF16), at_ref[0], fo_ref[0]], axis=1)
    xm = x_ref[0] + g1 * jnp.dot(cat, wo_ref[...], preferred_element_type=F32)
    xo_ref[0] = xm

    ms = jnp.mean(xm * xm, axis=-1, keepdims=True)
    h2 = (xm * lax.rsqrt(ms + EPS) * g2_ref[...]) * (1.0 + sc2) + sh2

    hh = h2.astype(BF16)
    hl = (h2 - hh.astype(F32)).astype(BF16)
    w2 = w2_ref[...]
    p2 = jnp.dot(hh, w2, preferred_element_type=F32)
    logits = (p2[:, 0:128] + p2[:, 128:256] + jnp.dot(hl, w2[:, 0:128], preferred_element_type=F32) + br_ref[...])
    meta = _route(logits.T[0:32, :], tri_ref, cnt_ref, tm).T

    rows_ref[:, 0:D] = h2
    rows_ref[:, D:ROW_W] = meta


def _out_call(x3, pu, attn, four, w_out, pool_wbd, pool_scale, bands, icnt, mod, g2, w2, br, tri, cnt_in, *,
              per_batch):
    nb, seq, _ = x3.shape
    tm = TM_OUT
    nt = seq // tm
    hb = tm // HALO
    full = lambda shape: pl.BlockSpec(shape, lambda b, i: (0,) * len(shape))
    in_specs = [pl.BlockSpec((1, tm, D), lambda b, i: (b, i, 0)),
                pl.BlockSpec((1, HALO, POOL_W), lambda b, i: (b, jnp.maximum(i * hb - 1, 0), 0)),
                pl.BlockSpec((1, tm, POOL_W), lambda b, i: (b, i, 0)),
                pl.BlockSpec((1, HALO, POOL_W), lambda b, i: (b, jnp.minimum((i + 1) * hb, seq // HALO - 1), 0)),
                pl.BlockSpec((1, tm, ATTN_W), lambda b, i: (b, i, 0)),
                pl.BlockSpec((1, tm, FOUR_W), lambda b, i: (b, i, 0)),
                full((D, D)), full((POOL_W, POOL_W)), full((1, POOL_W)),
                full((len(POOL_WINDOWS), tm, tm + 2 * HALO)),
                pl.BlockSpec((tm, POOL_W), lambda b, i: (i, 0)),
                full((8, 6 * D)), full((1, D)), full((D, 256)), full((1, 128)), full((tm, tm)), full((32, 128))]
    args = [x3, pu, pu, pu, attn, four, w_out, pool_wbd, pool_scale, bands, icnt, mod, g2, w2, br, tri, cnt_in]
    return pl.pallas_call(
        functools.partial(_out_body, tm=tm, per_batch=per_batch),
        grid=(nb, nt),
        in_specs=in_specs,
        out_specs=[pl.BlockSpec((1, tm, D), lambda b, i: (b, i, 0)),
                   pl.BlockSpec((tm, ROW_W), lambda b, i: (b * nt + i, 0)),
                   pl.BlockSpec((32, 128), lambda b, i: (0, 0))],
        out_shape=[jax.ShapeDtypeStruct((nb, seq, D), F32),
                   jax.ShapeDtypeStruct((nb * seq, ROW_W), F32),
                   jax.ShapeDtypeStruct((32, 128), F32)],
        compiler_params=_cparams(2),
        name="out_proj_router" if per_batch else "out_proj_router_ctx",
    )(*args)


def _pool_tables(seq, tm):
    t = np.arange(tm)[:, None]
    s = np.arange(tm + 2 * HALO)[None, :] - HALO
    bands = np.stack([(s >= t - w // 2) & (s <= t + w // 2 - 1) for w in POOL_WINDOWS]).astype(np.float32)
    pos = np.arange(seq)
    icnt = np.stack([1.0 / (np.minimum(pos + w // 2 - 1, seq - 1) - np.maximum(pos - w // 2, 0) + 1)
                     for w in POOL_WINDOWS], axis=1)
    return jnp.asarray(bands, BF16), jnp.asarray(np.repeat(icnt, POOL_W // len(POOL_WINDOWS), axis=1), F32)


def _scatter_rows(dest_ref, h_ref, xs_ref, sem, base, tm):
    def issue(r, c):
        pltpu.make_async_copy(h_ref.at[pl.ds(r, 1)], xs_ref.at[pl.ds(dest_ref[base + r], 1)], sem).start()
        return c

    lax.fori_loop(0, tm, issue, 0, unroll=8)

    def drain(r, c):
        pltpu.make_async_copy(h_ref.at[pl.ds(0, 1)], xs_ref.at[pl.ds(0, 1)], sem).wait()
        return c

    lax.fori_loop(0, tm, drain, 0, unroll=8)


def _scatter_body(dest_ref, *refs, tm, n_first):
    *h_refs, xs_ref, sem = refs
    i = pl.program_id(0)
    if len(h_refs) == 1:
        _scatter_rows(dest_ref, h_refs[0], xs_ref, sem, i * tm, tm)
    else:
        @pl.when(i < n_first)
        def _():
            _scatter_rows(dest_ref, h_refs[0], xs_ref, sem, i * tm, tm)

        @pl.when(i >= n_first)
        def _():
            _scatter_rows(dest_ref, h_refs[1], xs_ref, sem, i * tm, tm)


def _scatter_call(dest, row_sets):
    tm = TM_ROW
    n_first = row_sets[0].shape[0] // tm
    n_rows = sum(r.shape[0] for r in row_sets)
    in_specs = [pl.BlockSpec((tm, ROW_W), lambda i, d: (jnp.minimum(i, n_first - 1), 0))]
    if len(row_sets) == 2:
        in_specs.append(pl.BlockSpec((tm, ROW_W), lambda i, d: (jnp.maximum(i - n_first, 0), 0)))
    return pl.pallas_call(
        functools.partial(_scatter_body, tm=tm, n_first=n_first),
        grid_spec=pltpu.PrefetchScalarGridSpec(
            num_scalar_prefetch=1,
            grid=(n_rows // tm,),
            in_specs=in_specs,
            out_specs=pl.BlockSpec(memory_space=pl.ANY),
            scratch_shapes=[pltpu.SemaphoreType.DMA(())]),
        out_shape=jax.ShapeDtypeStruct((n_rows, ROW_W), F32),
        compiler_params=_cparams(1),
        name="moe_scatter_rows",
    )(dest, *row_sets)


def _moe_body(tile_ref, ea_ref, eb_ref, lo_ref, hi_ref, first_ref, nv_ref,
              xs_ref, wga, wua, wda, wgb, wub, wdb, ys_ref):
    del tile_ref, ea_ref, eb_ref
    w = pl.program_id(0)

    @pl.when(w < nv_ref[0])
    def _():
        xs = xs_ref[...]
        h = xs[:, 0:D].astype(BF16)
        rowi = lax.broadcasted_iota(I32, (xs.shape[0], 1), 0)
        mine = (rowi >= lo_ref[w]) & (rowi < hi_ref[w])

        def expert(wg, wu, wd, gate):
            g = jnp.dot(h, wg[0, 0].astype(BF16), preferred_element_type=F32)
            u = jnp.dot(h, wu[0, 0].astype(BF16), preferred_element_type=F32)
            a = _silu(g) * u * jnp.where(mine, gate, 0.0)
            return jnp.dot(a.astype(BF16), wd[0, 0].astype(BF16), preferred_element_type=F32)

        y = expert(wga, wua, wda, xs[:, D + 2:D + 3]) + expert(wgb, wub, wdb, xs[:, D + 3:D + 4])

        @pl.when(first_ref[w] == 1)
        def _():
            ys_ref[...] = y

        @pl.when(first_ref[w] == 0)
        def _():
            ys_ref[...] += y


def _moe_call(items, xs, w_gate, w_up, w_down, layer):
    tm = TM_MOE
    n_items = items[0].shape[0]
    row = lambda w, tile, *_: (tile[w], 0)
    wa = lambda w, tile, ea, *_: (layer, ea[w], 0, 0)
    wb = lambda w, tile, ea, eb, *_: (layer, eb[w], 0, 0)
    up_spec = lambda f: pl.BlockSpec((1, 1, D, D_EXPERT), f)
    dn_spec = lambda f: pl.BlockSpec((1, 1, D_EXPERT, D), f)
    return pl.pallas_call(
        _moe_body,
        grid_spec=pltpu.PrefetchScalarGridSpec(
            num_scalar_prefetch=len(items),
            grid=(n_items,),
            in_specs=[pl.BlockSpec((tm, ROW_W), row),
                      up_spec(wa), up_spec(wa), dn_spec(wa), up_spec(wb), up_spec(wb), dn_spec(wb)],
            out_specs=pl.BlockSpec((tm, D), row)),
        out_shape=jax.ShapeDtypeStruct((xs.shape[0], D), F32),
        compiler_params=_cparams(1),
        name="moe_experts",
    )(*items, xs, w_gate, w_up, w_down, w_gate, w_up, w_down)


def _gather_body(dest_ref, x_ref, mod_ref, ys_ref, o_ref, ybuf, sem, *, tm, seq, per_batch, dest_off):
    i = pl.program_id(0)
    base = dest_off + i * tm

    def issue(r, c):
        d = dest_ref[base + r]
        pltpu.make_async_copy(ys_ref.at[pl.ds(d, 1)], ybuf.at[pl.ds(r, 1)], sem).start()
        return c

    lax.fori_loop(0, tm, issue, 0, unroll=8)

    def drain(r, c):
        pltpu.make_async_copy(ys_ref.at[pl.ds(0, 1)], ybuf.at[pl.ds(0, 1)], sem).wait()
        return c

    lax.fori_loop(0, tm, drain, 0, unroll=8)
    row = (i * tm) // seq if per_batch else CTX_ROW
    g2 = mod_ref[pl.ds(row, 1), pl.ds(5 * D, D)]
    o_ref[...] = x_ref[...] + g2 * ybuf[...]


def _gather_call(dest, x2, mod, ys, *, seq, per_batch, dest_off):
    n = x2.shape[0]
    tm = TM_ROW
    return pl.pallas_call(
        functools.partial(_gather_body, tm=tm, seq=seq, per_batch=per_batch, dest_off=dest_off),
        grid_spec=pltpu.PrefetchScalarGridSpec(
            num_scalar_prefetch=1,
            grid=(n // tm,),
            in_specs=[pl.BlockSpec((tm, D), lambda i, d: (i, 0)),
                      pl.BlockSpec((8, 6 * D), lambda i, d: (0, 0)),
                      pl.BlockSpec(memory_space=pl.ANY)],
            out_specs=pl.BlockSpec((tm, D), lambda i, d: (i, 0)),
            scratch_shapes=[pltpu.VMEM((tm, D), F32), pltpu.SemaphoreType.DMA(())]),
        out_shape=jax.ShapeDtypeStruct((n, D), F32),
        compiler_params=_cparams(1),
        name="moe_gather_rows" if per_batch else "moe_gather_rows_ctx",
    )(dest, x2, mod, ys)


def _routing_tables(bins, rank, cnt, n_rows):
    tm = TM_MOE
    counts = cnt[:N_BINS, 0].astype(I32)
    ends = jnp.cumsum(counts)
    starts = ends - counts
    ids = jnp.arange(N_BINS, dtype=I32)
    pick = lambda key, tab: jnp.sum(jnp.where(key[:, None] == ids[None, :], tab[None, :], 0), axis=1)
    dest = rank + pick(bins, starts)

    tile_first = starts // tm
    tile_last = jnp.maximum(ends - 1, 0) // tm
    n_it = jnp.where(counts > 0, tile_last - tile_first + 1, 0)
    it_end = jnp.cumsum(n_it)
    it_start = it_end - n_it
    n_valid = it_end[-1]
    n_items = n_rows // tm + N_BINS
    w = jnp.minimum(jnp.arange(n_items, dtype=I32), n_valid - 1)
    wbin = jnp.sum((w[:, None] >= it_end[None, :]).astype(I32), axis=1)
    tile = pick(wbin, tile_first) + w - pick(wbin, it_start)
    lo = jnp.clip(pick(wbin, starts) - tile * tm, 0, tm)
    hi = jnp.clip(pick(wbin, ends) - tile * tm, 0, tm)
    first = jnp.concatenate([jnp.ones((1,), I32), (tile[1:] != tile[:-1]).astype(I32)])
    pidx = wbin % 6
    six = jnp.arange(6, dtype=I32)
    slot = lambda tab: jnp.sum(jnp.where(pidx[:, None] == six[None, :], jnp.asarray(tab, I32)[None, :], 0), axis=1)
    ea = 4 * (wbin // 6) + slot(PAIR_SLOT_A)
    eb = 4 * (wbin // 6) + slot(PAIR_SLOT_B)
    return dest, (tile, ea, eb, lo, hi, first, n_valid.reshape(1))


def _block_diag(w):
    g, c, d = w.shape
    out = jnp.zeros((g * c, g * d), w.dtype)
    for j in range(g):
        out = out.at[j * c:(j + 1) * c, j * d:(j + 1) * d].set(w[j])
    return out


def _rope_tables(n_tokens):
    rows = n_tokens // GRID_W
    r = jnp.broadcast_to(jnp.arange(rows)[:, None], (rows, GRID_W)).reshape(-1).astype(F32)
    col = jnp.broadcast_to(jnp.arange(GRID_W)[None, :], (rows, GRID_W)).reshape(-1).astype(F32)
    half = HEAD_DIM // 2
    inv = 1.0 / (ROPE_BASE ** (jnp.arange(0, half, 2, dtype=F32) / half))
    ar = r[:, None] * inv
    ac = col[:, None] * inv
    ang = jnp.concatenate([ar, ar, ac, ac], axis=-1)
    cos, sin = jnp.cos(ang), jnp.sin(ang)
    sign = jnp.where((jnp.arange(HEAD_DIM) & 16) == 0, -1.0, 1.0).astype(F32)
    return jnp.concatenate([cos, cos], axis=1), jnp.concatenate([sin * sign, sin * sign], axis=1)


def kernel(x, c, ctx, c_ctx, w_mod, b_mod, norm1_g, w_in, q_norm_g, k_norm_g, attn_sink, pool_w, pool_scale,
           four_w, w_out, norm2_g, w_grp, b_grp, w_rtr, b_rtr, w_gate, w_up, w_down):
    nb, seq, _ = x.shape
    lc = ctx.shape[1]
    depth = w_mod.shape[0]
    t_lat = nb * seq
    t_ctx = nb * lc

    cs = jnp.concatenate([c, c_ctx[None, :], jnp.zeros((8 - nb - 1, D), F32)], axis=0)
    m512 = jnp.asarray(np.kron(np.eye(N_HEADS), np.full((HEAD_DIM, HEAD_DIM), 1.0 / HEAD_DIM)), BF16)
    kk = np.arange(HEAD_DIM)
    ang64 = 2.0 * np.pi * ((kk[:, None] * kk[None, :]) % HEAD_DIM) / HEAD_DIM
    c64bd = jnp.asarray(np.kron(np.eye(4), np.cos(ang64)), F32)
    s64bd = jnp.asarray(np.kron(np.eye(4), np.sin(ang64)), F32)
    tri = jnp.asarray(np.triu(np.ones((TM_OUT, TM_OUT))), BF16)
    cos2, sin2 = _rope_tables(seq)
    tabs_lat = _dft_tables(seq, TM_FOUR)
    tabs_ctx = _dft_tables(lc, min(TM_FOUR, lc))
    pool_lat = _pool_tables(seq, TM_OUT)
    pool_ctx = _pool_tables(lc, TM_OUT)
    w_in_b = w_in.astype(BF16)
    w_out_b = w_out.astype(BF16)
    wbd = jnp.stack([_block_diag(four_w[l]) for l in range(depth)])
    pool_wbd = jnp.stack([_block_diag(pool_w[l]) for l in range(depth)]).astype(BF16)
    wr = jnp.concatenate([w_grp, w_rtr, jnp.zeros((depth, D, 128 - 4 - N_EXPERTS), F32)], axis=2)
    wr_hi = wr.astype(BF16)
    w2 = jnp.concatenate([wr_hi, (wr - wr_hi.astype(F32)).astype(BF16)], axis=2)
    br = jnp.concatenate([b_grp, b_rtr, jnp.zeros((depth, 128 - 4 - N_EXPERTS), F32)], axis=1)

    mod_all = _mod_call(cs, w_mod, b_mod)
    ab_all = _ab_call(c64bd, s64bd, wbd)

    xc = ctx
    for l in range(depth):
        last = l == depth - 1
        mod = mod_all[l]
        g1 = norm1_g[l][None, :]
        g2 = norm2_g[l][None, :]
        qg = jnp.tile(q_norm_g[l], N_HEADS)[None, :]
        kg = jnp.tile(k_norm_g[l], KV_W // HEAD_DIM)[None, :]
        in_args = (mod, g1, w_in_b[l], m512, qg, kg, ab_all[l])

        puc, qc, kvc, uac = _in_call(xc, *in_args, None, None, rope=False, per_batch=False)
        pul, ql, kvl, ual = _in_call(x, *in_args, cos2, sin2, rope=True, per_batch=True)
        attn_l = _attn_lat_call(attn_sink[l], ql, kvl, kvc)
        four_l = _four_call(ual, *tabs_lat)

        proj = (w_out_b[l], pool_wbd[l], pool_scale[l][None, :])
        rout = (mod, g2, w2[l], br[l][None, :], tri)
        cnt0 = jnp.zeros((32, 128), F32)
        x_mid, rows_l, cnt = _out_call(x, pul, attn_l, four_l, *proj, *pool_lat, *rout, cnt0, per_batch=True)
        row_sets = [rows_l]
        if not last:
            attn_c = _attn_ctx_call(attn_sink[l], qc, kvc)
            four_c = _four_call(uac, *tabs_ctx)
            xc_mid, rows_c, cnt = _out_call(xc, puc, attn_c, four_c, *proj, *pool_ctx, *rout, cnt, per_batch=False)
            row_sets.append(rows_c)

        bins = jnp.concatenate([r[:, D].astype(I32) for r in row_sets])
        rank = jnp.concatenate([r[:, D + 1].astype(I32) for r in row_sets])
        dest, items = _routing_tables(bins, rank, cnt, bins.shape[0])
        xs = _scatter_call(dest, row_sets)
        ys = _moe_call(items, xs, w_gate, w_up, w_down, l)
        x = _gather_call(dest, x_mid.reshape(t_lat, D), mod, ys, seq=seq, per_batch=True,
                         dest_off=0).reshape(nb, seq, D)
        if not last:
            xc = _gather_call(dest, xc_mid.reshape(t_ctx, D), mod, ys, seq=lc, per_batch=False,
                              dest_off=t_lat).reshape(nb, lc, D)
    return x
```

```python
import functools

import numpy as np
import jax
import jax.numpy as jnp
from jax import lax
from jax.experimental import pallas as pl
from jax.experimental.pallas import tpu as pltpu

F32 = jnp.float32
BF16 = jnp.bfloat16
I32 = jnp.int32
HI = lax.Precision.HIGHEST

D = 1024
HEAD_DIM = 64
N_HEADS = 8
GRID_W = 64
POOL_WINDOWS = (2, 4, 8, 16)
POOL_W = 256
ATTN_W = 512
KV_W = 128
FOUR_W = 256
IN_W = 1280
N_EXPERTS = 16
D_EXPERT = 512
WINDOW = 128
ROPE_BASE = 10000.0
EPS = 1e-6
NEG_INF = -1e30
CTX_ROW = 4
N_BINS = 24
PAIR_SLOT_A = (0, 2, 2, 3, 3, 3)
PAIR_SLOT_B = (1, 1, 0, 0, 1, 2)
META_W = 128
ROW_W = D + META_W
HALO = 16

VMEM_LIMIT = 56 * 1024 * 1024
TM_IN = 512
TQ = 512
TM_OUT = 256
TM_FOUR = 256
TM_MOE = 256
TM_ROW = 256
SCATTER_SLOTS = 4
MOD_TN = 1024


def _cparams(n_axes):
    return pltpu.CompilerParams(dimension_semantics=("arbitrary",) * n_axes,
                                vmem_limit_bytes=VMEM_LIMIT)


def _silu(v):
    return v / (1.0 + jnp.exp(-v))


def _mod_body(cs_ref, w_ref, b_ref, o_ref):
    s = _silu(cs_ref[...])
    o_ref[0] = jnp.dot(s, w_ref[0], preferred_element_type=F32, precision=HI) + b_ref[0]


def _mod_call(cs, w_mod, b_mod):
    depth = w_mod.shape[0]
    return pl.pallas_call(
        _mod_body,
        grid=(depth, 6 * D // MOD_TN),
        in_specs=[pl.BlockSpec((8, D), lambda l, j: (0, 0)),
                  pl.BlockSpec((1, D, MOD_TN), lambda l, j: (l, 0, j)),
                  pl.BlockSpec((1, 1, MOD_TN), lambda l, j: (l, 0, j))],
        out_specs=pl.BlockSpec((1, 8, MOD_TN), lambda l, j: (l, 0, j)),
        out_shape=jax.ShapeDtypeStruct((depth, 8, 6 * D), F32),
        compiler_params=_cparams(2),
        name="modulation",
    )(cs, w_mod, b_mod.reshape(depth, 1, 6 * D))


def _ab_body(c_ref, s_ref, w_ref, o_ref):
    w = w_ref[0]
    ca = jnp.dot(c_ref[...], w, preferred_element_type=F32, precision=HI)
    sa = jnp.dot(s_ref[...], w, preferred_element_type=F32, precision=HI)
    o_ref[0] = (jnp.concatenate([ca, sa], axis=1) * (HEAD_DIM ** -0.5)).astype(BF16)


def _ab_call(c64bd, s64bd, wbd):
    depth = wbd.shape[0]
    return pl.pallas_call(
        _ab_body,
        grid=(depth,),
        in_specs=[pl.BlockSpec((FOUR_W, FOUR_W), lambda l: (0, 0)),
                  pl.BlockSpec((FOUR_W, FOUR_W), lambda l: (0, 0)),
                  pl.BlockSpec((1, FOUR_W, FOUR_W), lambda l: (l, 0, 0))],
        out_specs=pl.BlockSpec((1, FOUR_W, 2 * FOUR_W), lambda l: (l, 0, 0)),
        out_shape=jax.ShapeDtypeStruct((depth, FOUR_W, 2 * FOUR_W), BF16),
        compiler_params=_cparams(1),
        name="fourier_weights",
    )(c64bd, s64bd, wbd)


def _head_rms(t, m, g):
    tt = t * t
    hi = tt.astype(BF16)
    lo = (tt - hi.astype(F32)).astype(BF16)
    ms = (jnp.dot(hi, m, preferred_element_type=F32) + jnp.dot(lo, m, preferred_element_type=F32))
    return t * lax.rsqrt(ms + EPS) * g


def _rope(t, cos, sin_signed):
    w = t.shape[1]
    lane = lax.broadcasted_iota(I32, t.shape, 1)
    fwd = pltpu.roll(t, w - 16, 1)
    bwd = pltpu.roll(t, 16, 1)
    rot = jnp.where((lane & 16) == 0, fwd, bwd)
    return t * cos + rot * sin_signed


def _in_body(*refs, rope, per_batch):
    if rope:
        (x_ref, mod_ref, g1_ref, w_ref, m_ref, qg_ref, kg_ref, ab_ref, cos_ref, sin_ref,
         pu_ref, q_ref, kv_ref, ua_ref) = refs
    else:
        (x_ref, mod_ref, g1_ref, w_ref, m_ref, qg_ref, kg_ref, ab_ref,
         pu_ref, q_ref, kv_ref, ua_ref) = refs
    row = pl.program_id(0) if per_batch else CTX_ROW
    sh1 = mod_ref[pl.ds(row, 1), pl.ds(0, D)]
    sc1 = mod_ref[pl.ds(row, 1), pl.ds(D, D)]
    x = x_ref[0]
    ms = jnp.mean(x * x, axis=-1, keepdims=True)
    h = (x * lax.rsqrt(ms + EPS) * g1_ref[...]) * (1.0 + sc1) + sh1
    p = jnp.dot(h.astype(BF16), w_ref[...], preferred_element_type=F32)
    pu = p[:, 0:256]
    q = p[:, 256:768]
    k = p[:, 768:896]
    v = p[:, 896:1024]
    fu = p[:, 1024:1280]
    m = m_ref[...]
    q = _head_rms(q, m, qg_ref[...])
    k = _head_rms(k, m[0:KV_W, 0:KV_W], kg_ref[...])
    if rope:
        cos = cos_ref[...]
        sin = sin_ref[...]
        q = _rope(q, jnp.concatenate([cos] * 4, axis=1), jnp.concatenate([sin] * 4, axis=1))
        k = _rope(k, cos, sin)
    q = q * (HEAD_DIM ** -0.5)
    pu_ref[0] = pu.astype(BF16)
    q_ref[0] = q.astype(BF16)
    kv_ref[0] = jnp.concatenate([k, pltpu.roll(k, 64, 1), v, pltpu.roll(v, 64, 1)], axis=1).astype(BF16)
    ua_ref[0] = jnp.dot(fu.astype(BF16), ab_ref[...], preferred_element_type=F32).astype(BF16)


def _in_call(x3, mod, g1, w_in, m512, qg, kg, ab, cos2, sin2, *, rope, per_batch):
    nb, seq, _ = x3.shape
    tm = min(TM_IN, seq)
    full = lambda shape: pl.BlockSpec(shape, lambda b, i: (0,) * len(shape))
    in_specs = [pl.BlockSpec((1, tm, D), lambda b, i: (b, i, 0)),
                full((8, 6 * D)), full((1, D)), full((D, IN_W)), full((ATTN_W, ATTN_W)),
                full((1, ATTN_W)), full((1, KV_W)), full((FOUR_W, 2 * FOUR_W))]
    args = [x3, mod, g1, w_in, m512, qg, kg, ab]
    if rope:
        in_specs += [pl.BlockSpec((tm, 128), lambda b, i: (i, 0)),
                     pl.BlockSpec((tm, 128), lambda b, i: (i, 0))]
        args += [cos2, sin2]
    widths = (POOL_W, ATTN_W, 4 * KV_W, 2 * FOUR_W)
    return pl.pallas_call(
        functools.partial(_in_body, rope=rope, per_batch=per_batch),
        grid=(nb, seq // tm),
        in_specs=in_specs,
        out_specs=[pl.BlockSpec((1, tm, w), lambda b, i: (b, i, 0)) for w in widths],
        out_shape=[jax.ShapeDtypeStruct((nb, seq, w), BF16) for w in widths],
        compiler_params=_cparams(2),
        name="in_proj_rope" if rope else "in_proj_ctx",
    )(*args)


_NT = (((1,), (1,)), ((), ()))


def _stack_heads(qpair0, qpair1, lo):
    z = jnp.zeros_like(qpair0)
    parts = [jnp.where(lo, qpair0, z), jnp.where(lo, qpair1, z),
             jnp.where(lo, pltpu.roll(qpair0, 64, 1), z), jnp.where(lo, pltpu.roll(qpair1, 64, 1), z)]
    return jnp.concatenate(parts, axis=0).astype(BF16)


def _group_attention(q4, k_parts, va_parts, vb_parts, masks, sink_col):
    s_parts = []
    for kz, mk in zip(k_parts, masks):
        s = lax.dot_general(q4, kz, _NT, preferred_element_type=F32)
        if mk is not None:
            nk = s.shape[1]
            s = jnp.where(mk[None], s.reshape(4, 128, nk), NEG_INF).reshape(512, nk)
        s_parts.append(s)
    m = functools.reduce(jnp.maximum, [jnp.max(s, axis=-1, keepdims=True) for s in s_parts])
    m = jnp.maximum(m, sink_col)
    den = jnp.exp(sink_col - m)
    oe = oo = None
    for s, va, vb in zip(s_parts, va_parts, vb_parts):
        e = jnp.exp(s - m)
        den = den + jnp.sum(e, axis=-1, keepdims=True)
        eb = e.astype(BF16)
        pe = jnp.dot(eb[0:256], va, preferred_element_type=F32)
        po = jnp.dot(eb[256:512], vb, preferred_element_type=F32)
        oe = pe if oe is None else oe + pe
        oo = po if oo is None else oo + po
    inv = 1.0 / den
    return oe * inv[0:256], oo * inv[256:512]


def _sink_cols(sink_ref):
    rb = lax.broadcasted_iota(I32, (512, 1), 0) >> 7
    cols = []
    for kvh in range(2):
        s = [sink_ref[4 * kvh + j] for j in (0, 2, 1, 3)]
        cols.append(jnp.where(rb == 0, s[0], jnp.where(rb == 1, s[1], jnp.where(rb == 2, s[2], s[3]))))
    return cols


def _attend_block(qblk, kv_parts, masks, sink_cols):
    lo = lax.broadcasted_iota(I32, (128, 128), 1) < 64
    cols = []
    for kvh in range(2):
        q4 = _stack_heads(qblk[:, 256 * kvh:256 * kvh + 128], qblk[:, 256 * kvh + 128:256 * kvh + 256], lo)
        ko = 128 * kvh
        vao = 256 + 128 * kvh
        vbo = 384 - 128 * kvh
        oe, oo = _group_attention(q4, [kv[:, ko:ko + 128] for kv in kv_parts],
                                  [kv[:, vao:vao + 128] for kv in kv_parts],
                                  [kv[:, vbo:vbo + 128] for kv in kv_parts], masks, sink_cols[kvh])
        cols.append(jnp.where(lo, oe[0:128], oo[0:128]))
        cols.append(jnp.where(lo, oe[128:256], oo[128:256]))
    return jnp.concatenate(cols, axis=1).astype(BF16)


def _attn_lat_body(sink_ref, q_ref, kvp_ref, kvm_ref, kvn_ref, kvc_ref, o_ref, kvw_ref, *, tq, seq):
    i = pl.program_id(1)
    kvw_ref[0:128] = kvp_ref[0]
    kvw_ref[128:128 + tq] = kvm_ref[0]
    kvw_ref[128 + tq:256 + tq] = kvn_ref[0]
    kvc = kvc_ref[0]
    sink_cols = _sink_cols(sink_ref)

    def sub(j, carry):
        r0 = pl.multiple_of(j * 128, 128)
        win = kvw_ref[pl.ds(r0, 3 * 128), :]
        ii = lax.broadcasted_iota(I32, (128, 3 * 128), 0)
        cc = lax.broadcasted_iota(I32, (128, 3 * 128), 1)
        base = i * tq + j * 128 - 128
        valid = (ii <= cc) & (cc <= ii + 2 * WINDOW) & (cc >= -base) & (cc < seq - base)
        qblk = q_ref[0, pl.ds(r0, 128), :].astype(F32)
        o_ref[0, pl.ds(r0, 128), :] = _attend_block(qblk, [win, kvc], [valid, None], sink_cols)
        return carry

    lax.fori_loop(0, tq // 128, sub, 0)


def _attn_lat_call(sink, q, kv, kvc):
    nb, seq, _ = q.shape
    lc = kvc.shape[1]
    tq = TQ
    nblk = seq // 128
    r = tq // 128
    return pl.pallas_call(
        functools.partial(_attn_lat_body, tq=tq, seq=seq),
        grid=(nb, seq // tq),
        in_specs=[pl.BlockSpec(memory_space=pltpu.SMEM),
                  pl.BlockSpec((1, tq, ATTN_W), lambda b, i: (b, i, 0)),
                  pl.BlockSpec((1, 128, 4 * KV_W), lambda b, i: (b, jnp.maximum(i * r - 1, 0), 0)),
                  pl.BlockSpec((1, tq, 4 * KV_W), lambda b, i: (b, i, 0)),
                  pl.BlockSpec((1, 128, 4 * KV_W), lambda b, i: (b, jnp.minimum((i + 1) * r, nblk - 1), 0)),
                  pl.BlockSpec((1, lc, 4 * KV_W), lambda b, i: (b, 0, 0))],
        out_specs=pl.BlockSpec((1, tq, ATTN_W), lambda b, i: (b, i, 0)),
        out_shape=jax.ShapeDtypeStruct((nb, seq, ATTN_W), BF16),
        scratch_shapes=[pltpu.VMEM((tq + 256, 4 * KV_W), BF16)],
        compiler_params=_cparams(2),
        name="attention_window",
    )(sink, q, kv, kv, kv, kvc)


def _attn_ctx_body(sink_ref, q_ref, kvc_ref, o_ref, *, lc):
    kvc = kvc_ref[0]
    sink_cols = _sink_cols(sink_ref)
    for j in range(lc // 128):
        qblk = q_ref[0, j * 128:(j + 1) * 128, :].astype(F32)
        o_ref[0, j * 128:(j + 1) * 128, :] = _attend_block(qblk, [kvc], [None], sink_cols)


def _attn_ctx_call(sink, qc, kvc):
    nb, lc, _ = qc.shape
    return pl.pallas_call(
        functools.partial(_attn_ctx_body, lc=lc),
        grid=(nb,),
        in_specs=[pl.BlockSpec(memory_space=pltpu.SMEM),
                  pl.BlockSpec((1, lc, ATTN_W), lambda b: (b, 0, 0)),
                  pl.BlockSpec((1, lc, 4 * KV_W), lambda b: (b, 0, 0))],
        out_specs=pl.BlockSpec((1, lc, ATTN_W), lambda b: (b, 0, 0)),
        out_shape=jax.ShapeDtypeStruct((nb, lc, ATTN_W), BF16),
        compiler_params=_cparams(1),
        name="attention_ctx",
    )(sink, qc, kvc)


def _four_body(ua_ref, cb_ref, sb_ref, ca_ref, sa_ref, o_ref, *, nb, scale):
    i = pl.program_id(0)
    ca = ca_ref[pl.ds(i, 1), :]
    sa = sa_ref[pl.ds(i, 1), :]
    cb = cb_ref[...]
    sb = sb_ref[...]
    ct = (ca * cb - sa * sb).astype(BF16)
    st = (sa * cb + ca * sb).astype(BF16)
    for b in range(nb):
        ua = ua_ref[b, :, 0:FOUR_W]
        ub = ua_ref[b, :, FOUR_W:2 * FOUR_W]
        r = (jnp.dot(ct, ua, preferred_element_type=F32) - jnp.dot(st, ub, preferred_element_type=F32))
        o_ref[b] = (r * scale).astype(BF16)


def _four_call(uaub, cb, sb, ca, sa):
    nb, seq, _ = uaub.shape
    tm = cb.shape[0]
    one = pl.Buffered(1)
    return pl.pallas_call(
        functools.partial(_four_body, nb=nb, scale=float(seq) ** -0.5),
        grid=(seq // tm,),
        in_specs=[pl.BlockSpec((nb, seq, 2 * FOUR_W), lambda i: (0, 0, 0), pipeline_mode=one),
                  pl.BlockSpec((tm, seq), lambda i: (0, 0), pipeline_mode=one),
                  pl.BlockSpec((tm, seq), lambda i: (0, 0), pipeline_mode=one),
                  pl.BlockSpec((seq // tm, seq), lambda i: (0, 0), pipeline_mode=one),
                  pl.BlockSpec((seq // tm, seq), lambda i: (0, 0), pipeline_mode=one)],
        out_specs=pl.BlockSpec((nb, tm, FOUR_W), lambda i: (0, i, 0)),
        out_shape=jax.ShapeDtypeStruct((nb, seq, FOUR_W), BF16),
        compiler_params=_cparams(1),
        name="fourier_dft",
    )(uaub, cb, sb, ca, sa)


def _dft_tables(seq, tm):
    n = jnp.arange(seq, dtype=I32)[None, :]

    def tab(rows):
        ang = ((rows[:, None] * n) % seq).astype(F32) * (2.0 * np.pi / seq)
        return jnp.cos(ang), jnp.sin(ang)

    cb, sb = tab(jnp.arange(tm, dtype=I32))
    ca, sa = tab(jnp.arange(seq // tm, dtype=I32) * tm)
    return cb, sb, ca, sa


def _route(lt, tri_ref, cnt_ref, tm):
    rowi = lax.broadcasted_iota(I32, (32, tm), 0)
    big = jnp.int32(999)

    def first_argmax(vals):
        mx = jnp.max(vals, axis=0, keepdims=True)
        return mx, jnp.min(jnp.where(vals == mx, rowi, big), axis=0, keepdims=True)

    is_grp = rowi < 4
    mg, gi = first_argmax(jnp.where(is_grp, lt, NEG_INF))
    pg = 1.0 / jnp.sum(jnp.where(is_grp, jnp.exp(jnp.where(is_grp, lt, mg) - mg), 0.0), axis=0, keepdims=True)
    est = 4 + 4 * gi
    le = jnp.where((rowi >= est) & (rowi < est + 4), lt, NEG_INF)
    m1, i1 = first_argmax(le)
    m2, i2 = first_argmax(jnp.where(rowi == i1, NEG_INF, le))
    e2 = jnp.exp(m2 - m1)
    w1 = pg / (1.0 + e2)
    w2 = pg * e2 / (1.0 + e2)
    a1 = i1 - est
    a2 = i2 - est
    code = jnp.minimum(a1, a2) * 4 + jnp.maximum(a1, a2)
    pidx = jnp.where(code == 1, 0, jnp.where(code == 6, 1, jnp.where(code == 2, 2,
           jnp.where(code == 3, 3, jnp.where(code == 7, 4, 5)))))
    slot_a = jnp.where(pidx == 0, 0, jnp.where(pidx <= 2, 2, 3))
    slot_b = jnp.where(pidx <= 1, 1, jnp.where(pidx <= 3, 0, jnp.where(pidx == 4, 1, 2)))
    wa = jnp.where(a1 == slot_a, w1, w2)
    wb = jnp.where(a1 == slot_b, w1, w2)
    bin_ = gi * 6 + pidx

    onehot = rowi == bin_
    pref = jnp.dot(onehot.astype(BF16), tri_ref[...], preferred_element_type=F32)
    carry = cnt_ref[:, 0:1]
    rank = jnp.sum(jnp.where(onehot, pref - 1.0 + carry, 0.0), axis=0, keepdims=True)
    cnt_ref[...] = jnp.broadcast_to(carry + pref[:, tm - 1:tm], cnt_ref.shape)
    return jnp.concatenate([bin_.astype(F32), rank, wa, wb, jnp.zeros((128 - 4, tm), F32)], axis=0)


def _out_body(x_ref, pup_ref, pum_ref, pun_ref, at_ref, fo_ref, wo_ref, pw_ref, ps_ref, band_ref, icnt_ref,
              mod_ref, g2_ref, w2_ref, br_ref, tri_ref, cin_ref,
              xo_ref, rows_ref, cnt_ref, *, tm, per_batch):
    b = pl.program_id(0)
    i = pl.program_id(1)
    nt = pl.num_programs(1)
    row = b if per_batch else CTX_ROW
    g1 = mod_ref[pl.ds(row, 1), pl.ds(2 * D, D)]
    sh2 = mod_ref[pl.ds(row, 1), pl.ds(3 * D, D)]
    sc2 = mod_ref[pl.ds(row, 1), pl.ds(4 * D, D)]

    @pl.when((b == 0) & (i == 0))
    def _():
        cnt_ref[...] = cin_ref[...]

    um = pum_ref[0]
    zh = jnp.zeros((HALO, POOL_W), BF16)
    uext = jnp.concatenate([jnp.where(i > 0, pup_ref[0], zh), um, jnp.where(i < nt - 1, pun_ref[0], zh)], axis=0)
    grp = lax.broadcasted_iota(I32, (tm, POOL_W), 1) >> 6
    pooled = jnp.zeros((tm, POOL_W), F32)
    for g in range(len(POOL_WINDOWS)):
        pooled = jnp.where(grp == g, jnp.dot(band_ref[g], uext, preferred_element_type=F32), pooled)
    y = pooled * icnt_ref[...] - um.astype(F32)
    pool_out = jnp.dot(y.astype(BF16), pw_ref[...], preferred_element_type=F32) * ps_ref[...]

    cat = jnp.concatenate([pool_out.astype(BF16), at_ref[0], fo_ref[0]], axis=1)
    xm = x_ref[0] + g1 * jnp.dot(cat, wo_ref[...], preferred_element_type=F32)
    xo_ref[0] = xm

    ms = jnp.mean(xm * xm, axis=-1, keepdims=True)
    h2 = (xm * lax.rsqrt(ms + EPS) * g2_ref[...]) * (1.0 + sc2) + sh2

    hh = h2.astype(BF16)
    hl = (h2 - hh.astype(F32)).astype(BF16)
    w2 = w2_ref[...]
    p2 = jnp.dot(hh, w2, preferred_element_type=F32)
    logits = (p2[:, 0:128] + p2[:, 128:256] + jnp.dot(hl, w2[:, 0:128], preferred_element_type=F32) + br_ref[...])
    meta = _route(logits.T[0:32, :], tri_ref, cnt_ref, tm).T

    rows_ref[:, 0:D] = h2
    rows_ref[:, D:ROW_W] = meta


def _out_call(x3, pu, attn, four, w_out, pool_wbd, pool_scale, bands, icnt, mod, g2, w2, br, tri, cnt_in, *,
              per_batch):
    nb, seq, _ = x3.shape
    tm = TM_OUT
    nt = seq // tm
    hb = tm // HALO
    full = lambda shape: pl.BlockSpec(shape, lambda b, i: (0,) * len(shape))
    in_specs = [pl.BlockSpec((1, tm, D), lambda b, i: (b, i, 0)),
                pl.BlockSpec((1, HALO, POOL_W), lambda b, i: (b, jnp.maximum(i * hb - 1, 0), 0)),
                pl.BlockSpec((1, tm, POOL_W), lambda b, i: (b, i, 0)),
                pl.BlockSpec((1, HALO, POOL_W), lambda b, i: (b, jnp.minimum((i + 1) * hb, seq // HALO - 1), 0)),
                pl.BlockSpec((1, tm, ATTN_W), lambda b, i: (b, i, 0)),
                pl.BlockSpec((1, tm, FOUR_W), lambda b, i: (b, i, 0)),
                full((D, D)), full((POOL_W, POOL_W)), full((1, POOL_W)),
                full((len(POOL_WINDOWS), tm, tm + 2 * HALO)),
                pl.BlockSpec((tm, POOL_W), lambda b, i: (i, 0)),
                full((8, 6 * D)), full((1, D)), full((D, 256)), full((1, 128)), full((tm, tm)), full((32, 128))]
    args = [x3, pu, pu, pu, attn, four, w_out, pool_wbd, pool_scale, bands, icnt, mod, g2, w2, br, tri, cnt_in]
    return pl.pallas_call(
        functools.partial(_out_body, tm=tm, per_batch=per_batch),
        grid=(nb, nt),
        in_specs=in_specs,
        out_specs=[pl.BlockSpec((1, tm, D), lambda b, i: (b, i, 0)),
                   pl.BlockSpec((tm, ROW_W), lambda b, i: (b * nt + i, 0)),
                   pl.BlockSpec((32, 128), lambda b, i: (0, 0))],
        out_shape=[jax.ShapeDtypeStruct((nb, seq, D), F32),
                   jax.ShapeDtypeStruct((nb * seq, ROW_W), F32),
                   jax.ShapeDtypeStruct((32, 128), F32)],
        compiler_params=_cparams(2),
        name="out_proj_router" if per_batch else "out_proj_router_ctx",
    )(*args)


def _pool_tables(seq, tm):
    t = np.arange(tm)[:, None]
    s = np.arange(tm + 2 * HALO)[None, :] - HALO
    bands = np.stack([(s >= t - w // 2) & (s <= t + w // 2 - 1) for w in POOL_WINDOWS]).astype(np.float32)
    pos = np.arange(seq)
    icnt = np.stack([1.0 / (np.minimum(pos + w // 2 - 1, seq - 1) - np.maximum(pos - w // 2, 0) + 1)
                     for w in POOL_WINDOWS], axis=1)
    return jnp.asarray(bands, BF16), jnp.asarray(np.repeat(icnt, POOL_W // len(POOL_WINDOWS), axis=1), F32)


def _row_copies(tm, make_copy):
    for r in range(tm):
        make_copy(r).start(priority=r % 2)


def _row_waits(tm, make_copy):
    def drain(r, c):
        make_copy(0).wait()
        return c

    lax.fori_loop(0, tm, drain, 0, unroll=8)


def _scatter_body(dest_ref, *refs, tm, n_tiles, n_first):
    n_h = len(refs) - 1 - 3 * SCATTER_SLOTS
    h_refs, xs_ref = refs[:n_h], refs[n_h]
    bufs = refs[n_h + 1:n_h + 1 + SCATTER_SLOTS]
    lsems = refs[n_h + 1 + SCATTER_SLOTS:n_h + 1 + 2 * SCATTER_SLOTS]
    rsems = refs[n_h + 1 + 2 * SCATTER_SLOTS:]

    def load(t, slot):
        def start(h_ref, tt):
            pltpu.make_async_copy(h_ref.at[pl.ds(tt * tm, tm)], bufs[slot], lsems[slot]).start()

        if n_h == 1:
            start(h_refs[0], t)
        else:
            @pl.when(t < n_first)
            def _():
                start(h_refs[0], t)

            @pl.when(t >= n_first)
            def _():
                start(h_refs[1], t - n_first)

    def row_copy(slot, r, d):
        return pltpu.make_async_copy(bufs[slot].at[pl.ds(r, 1)], xs_ref.at[pl.ds(d, 1)], rsems[slot])

    load(0, 0)
    load(1, 1)

    def group(g, c):
        for slot in range(SCATTER_SLOTS):
            t = g * SCATTER_SLOTS + slot
            ahead = (slot + 2) % SCATTER_SLOTS
            pltpu.make_async_copy(h_refs[0].at[pl.ds(0, tm)], bufs[slot], lsems[slot]).wait()

            @pl.when(t >= 2)
            def _():
                _row_waits(tm, lambda r: row_copy(ahead, r, 0))

            @pl.when(t + 2 < n_tiles)
            def _():
                load(t + 2, ahead)

            _row_copies(tm, lambda r: row_copy(slot, r, dest_ref[t * tm + r]))
        return c

    lax.fori_loop(0, n_tiles // SCATTER_SLOTS, group, 0)
    for t in (n_tiles - 2, n_tiles - 1):
        _row_waits(tm, lambda r: row_copy(t % SCATTER_SLOTS, r, 0))


def _scatter_call(dest, row_sets):
    tm = TM_ROW
    n_first = row_sets[0].shape[0] // tm
    n_rows = sum(r.shape[0] for r in row_sets)
    n_tiles = n_rows // tm
    assert n_tiles % SCATTER_SLOTS == 0 and n_tiles >= SCATTER_SLOTS
    return pl.pallas_call(
        functools.partial(_scatter_body, tm=tm, n_tiles=n_tiles, n_first=n_first),
        grid_spec=pltpu.PrefetchScalarGridSpec(
            num_scalar_prefetch=1,
            grid=(1,),
            in_specs=[pl.BlockSpec(memory_space=pl.ANY)] * len(row_sets),
            out_specs=pl.BlockSpec(memory_space=pl.ANY),
            scratch_shapes=([pltpu.VMEM((tm, ROW_W), F32)] * SCATTER_SLOTS
                            + [pltpu.SemaphoreType.DMA(())] * (2 * SCATTER_SLOTS))),
        out_shape=jax.ShapeDtypeStruct((n_rows, ROW_W), F32),
        compiler_params=_cparams(1),
        name="moe_scatter_rows",
    )(dest, *row_sets)


def _moe_body(tile_ref, ea_ref, eb_ref, lo_ref, hi_ref, first_ref, nv_ref,
              xs_ref, wga, wua, wda, wgb, wub, wdb, ys_ref):
    del tile_ref, ea_ref, eb_ref
    w = pl.program_id(0)

    @pl.when(w < nv_ref[0])
    def _():
        h = xs_ref[:, 0:D].astype(BF16)
        meta = xs_ref[:, D:ROW_W]
        rowi = lax.broadcasted_iota(I32, (h.shape[0], 1), 0)
        mine = (rowi >= lo_ref[w]) & (rowi < hi_ref[w])

        def expert(wg, wu, wd, gate):
            g = jnp.dot(h, wg[0, 0].astype(BF16), preferred_element_type=F32)
            u = jnp.dot(h, wu[0, 0].astype(BF16), preferred_element_type=F32)
            a = _silu(g) * u * jnp.where(mine, gate, 0.0)
            return jnp.dot(a.astype(BF16), wd[0, 0].astype(BF16), preferred_element_type=F32)

        y = expert(wga, wua, wda, meta[:, 2:3]) + expert(wgb, wub, wdb, meta[:, 3:4])

        @pl.when(first_ref[w] == 1)
        def _():
            ys_ref[...] = y

        @pl.when(first_ref[w] == 0)
        def _():
            ys_ref[...] += y


def _moe_call(items, xs, w_gate, w_up, w_down, layer):
    tm = TM_MOE
    n_items = items[0].shape[0]
    row = lambda w, tile, *_: (tile[w], 0)
    wa = lambda w, tile, ea, *_: (layer, ea[w], 0, 0)
    wb = lambda w, tile, ea, eb, *_: (layer, eb[w], 0, 0)
    up_spec = lambda f: pl.BlockSpec((1, 1, D, D_EXPERT), f)
    dn_spec = lambda f: pl.BlockSpec((1, 1, D_EXPERT, D), f)
    return pl.pallas_call(
        _moe_body,
        grid_spec=pltpu.PrefetchScalarGridSpec(
            num_scalar_prefetch=len(items),
            grid=(n_items,),
            in_specs=[pl.BlockSpec((tm, ROW_W), row),
                      up_spec(wa), up_spec(wa), dn_spec(wa), up_spec(wb), up_spec(wb), dn_spec(wb)],
            out_specs=pl.BlockSpec((tm, D), row)),
        out_shape=jax.ShapeDtypeStruct((xs.shape[0], D), F32),
        compiler_params=_cparams(1),
        name="moe_experts",
    )(*items, xs, w_gate, w_up, w_down, w_gate, w_up, w_down)


def _gather_body(dest_ref, x_ref, mod_ref, ys_ref, o_ref, ybuf0, ybuf1, sem0, sem1, *,
                 tm, seq, per_batch, dest_off):
    i = pl.program_id(0)
    n_tiles = pl.num_programs(0)
    ybufs = (ybuf0, ybuf1)
    sems = (sem0, sem1)
    row = (i * tm) // seq if per_batch else CTX_ROW
    g2 = mod_ref[pl.ds(row, 1), pl.ds(5 * D, D)]

    def row_copy(slot, r, d):
        return pltpu.make_async_copy(ys_ref.at[pl.ds(d, 1)], ybufs[slot].at[pl.ds(r, 1)], sems[slot])

    def fetch(t, slot):
        base = dest_off + t * tm
        _row_copies(tm, lambda r: row_copy(slot, r, dest_ref[base + r]))

    @pl.when(i == 0)
    def _():
        fetch(0, 0)

    for slot in range(2):
        @pl.when(i % 2 == slot)
        def _():
            @pl.when(i + 1 < n_tiles)
            def _():
                fetch(i + 1, 1 - slot)

            _row_waits(tm, lambda r: row_copy(slot, r, 0))
            o_ref[...] = x_ref[...] + g2 * ybufs[slot][...]


def _gather_call(dest, x2, mod, ys, *, seq, per_batch, dest_off):
    n = x2.shape[0]
    tm = TM_ROW
    return pl.pallas_call(
        functools.partial(_gather_body, tm=tm, seq=seq, per_batch=per_batch, dest_off=dest_off),
        grid_spec=pltpu.PrefetchScalarGridSpec(
            num_scalar_prefetch=1,
            grid=(n // tm,),
            in_specs=[pl.BlockSpec((tm, D), lambda i, *_: (i, 0)),
                      pl.BlockSpec((8, 6 * D), lambda i, *_: (0, 0)),
                      pl.BlockSpec(memory_space=pl.ANY)],
            out_specs=pl.BlockSpec((tm, D), lambda i, *_: (i, 0)),
            scratch_shapes=[pltpu.VMEM((tm, D), F32), pltpu.VMEM((tm, D), F32),
                            pltpu.SemaphoreType.DMA(()), pltpu.SemaphoreType.DMA(())]),
        out_shape=jax.ShapeDtypeStruct((n, D), F32),
        compiler_params=_cparams(1),
        name="moe_gather_rows" if per_batch else "moe_gather_rows_ctx",
    )(dest, x2, mod, ys)


def _routing_tables(bins, rank, cnt, n_rows):
    tm = TM_MOE
    counts = cnt[:N_BINS, 0].astype(I32)
    ends = jnp.cumsum(counts)
    starts = ends - counts
    ids = jnp.arange(N_BINS, dtype=I32)
    pick = lambda key, tab: jnp.sum(jnp.where(key[:, None] == ids[None, :], tab[None, :], 0), axis=1)
    dest = rank + pick(bins, starts)

    tile_first = starts // tm
    tile_last = jnp.maximum(ends - 1, 0) // tm
    n_it = jnp.where(counts > 0, tile_last - tile_first + 1, 0)
    it_end = jnp.cumsum(n_it)
    it_start = it_end - n_it
    n_valid = it_end[-1]
    n_items = n_rows // tm + N_BINS
    w = jnp.minimum(jnp.arange(n_items, dtype=I32), n_valid - 1)
    wbin = jnp.sum((w[:, None] >= it_end[None, :]).astype(I32), axis=1)
    tile = pick(wbin, tile_first) + w - pick(wbin, it_start)
    lo = jnp.clip(pick(wbin, starts) - tile * tm, 0, tm)
    hi = jnp.clip(pick(wbin, ends) - tile * tm, 0, tm)
    first = jnp.concatenate([jnp.ones((1,), I32), (tile[1:] != tile[:-1]).astype(I32)])
    pidx = wbin % 6
    six = jnp.arange(6, dtype=I32)
    slot = lambda tab: jnp.sum(jnp.where(pidx[:, None] == six[None, :], jnp.asarray(tab, I32)[None, :], 0), axis=1)
    ea = 4 * (wbin // 6) + slot(PAIR_SLOT_A)
    eb = 4 * (wbin // 6) + slot(PAIR_SLOT_B)
    return dest, (tile, ea, eb, lo, hi, first, n_valid.reshape(1))


def _block_diag(w):
    g, c, d = w.shape
    out = jnp.zeros((g * c, g * d), w.dtype)
    for j in range(g):
        out = out.at[j * c:(j + 1) * c, j * d:(j + 1) * d].set(w[j])
    return out


def _rope_tables(n_tokens):
    rows = n_tokens // GRID_W
    r = jnp.broadcast_to(jnp.arange(rows)[:, None], (rows, GRID_W)).reshape(-1).astype(F32)
    col = jnp.broadcast_to(jnp.arange(GRID_W)[None, :], (rows, GRID_W)).reshape(-1).astype(F32)
    half = HEAD_DIM // 2
    inv = 1.0 / (ROPE_BASE ** (jnp.arange(0, half, 2, dtype=F32) / half))
    ar = r[:, None] * inv
    ac = col[:, None] * inv
    ang = jnp.concatenate([ar, ar, ac, ac], axis=-1)
    cos, sin = jnp.cos(ang), jnp.sin(ang)
    sign = jnp.where((jnp.arange(HEAD_DIM) & 16) == 0, -1.0, 1.0).astype(F32)
    return jnp.concatenate([cos, cos], axis=1), jnp.concatenate([sin * sign, sin * sign], axis=1)


def kernel(x, c, ctx, c_ctx, w_mod, b_mod, norm1_g, w_in, q_norm_g, k_norm_g, attn_sink, pool_w, pool_scale,
           four_w, w_out, norm2_g, w_grp, b_grp, w_rtr, b_rtr, w_gate, w_up, w_down):
    nb, seq, _ = x.shape
    lc = ctx.shape[1]
    depth = w_mod.shape[0]
    t_lat = nb * seq
    t_ctx = nb * lc

    cs = jnp.concatenate([c, c_ctx[None, :], jnp.zeros((8 - nb - 1, D), F32)], axis=0)
    m512 = jnp.asarray(np.kron(np.eye(N_HEADS), np.full((HEAD_DIM, HEAD_DIM), 1.0 / HEAD_DIM)), BF16)
    kk = np.arange(HEAD_DIM)
    ang64 = 2.0 * np.pi * ((kk[:, None] * kk[None, :]) % HEAD_DIM) / HEAD_DIM
    c64bd = jnp.asarray(np.kron(np.eye(4), np.cos(ang64)), F32)
    s64bd = jnp.asarray(np.kron(np.eye(4), np.sin(ang64)), F32)
    tri = jnp.asarray(np.triu(np.ones((TM_OUT, TM_OUT))), BF16)
    cos2, sin2 = _rope_tables(seq)
    tabs_lat = _dft_tables(seq, TM_FOUR)
    tabs_ctx = _dft_tables(lc, min(TM_FOUR, lc))
    pool_lat = _pool_tables(seq, TM_OUT)
    pool_ctx = _pool_tables(lc, TM_OUT)
    w_in_b = w_in.astype(BF16)
    w_out_b = w_out.astype(BF16)
    wbd = jnp.stack([_block_diag(four_w[l]) for l in range(depth)])
    pool_wbd = jnp.stack([_block_diag(pool_w[l]) for l in range(depth)]).astype(BF16)
    wr = jnp.concatenate([w_grp, w_rtr, jnp.zeros((depth, D, 128 - 4 - N_EXPERTS), F32)], axis=2)
    wr_hi = wr.astype(BF16)
    w2 = jnp.concatenate([wr_hi, (wr - wr_hi.astype(F32)).astype(BF16)], axis=2)
    br = jnp.concatenate([b_grp, b_rtr, jnp.zeros((depth, 128 - 4 - N_EXPERTS), F32)], axis=1)

    mod_all = _mod_call(cs, w_mod, b_mod)
    ab_all = _ab_call(c64bd, s64bd, wbd)

    xc = ctx
    for l in range(depth):
        last = l == depth - 1
        mod = mod_all[l]
        g1 = norm1_g[l][None, :]
        g2 = norm2_g[l][None, :]
        qg = jnp.tile(q_norm_g[l], N_HEADS)[None, :]
        kg = jnp.tile(k_norm_g[l], KV_W // HEAD_DIM)[None, :]
        in_args = (mod, g1, w_in_b[l], m512, qg, kg, ab_all[l])

        puc, qc, kvc, uac = _in_call(xc, *in_args, None, None, rope=False, per_batch=False)
        pul, ql, kvl, ual = _in_call(x, *in_args, cos2, sin2, rope=True, per_batch=True)
        attn_l = _attn_lat_call(attn_sink[l], ql, kvl, kvc)
        four_l = _four_call(ual, *tabs_lat)

        proj = (w_out_b[l], pool_wbd[l], pool_scale[l][None, :])
        rout = (mod, g2, w2[l], br[l][None, :], tri)
        cnt0 = jnp.zeros((32, 128), F32)
        x_mid, rows_l, cnt = _out_call(x, pul, attn_l, four_l, *proj, *pool_lat, *rout, cnt0, per_batch=True)
        row_sets = [rows_l]
        if not last:
            attn_c = _attn_ctx_call(attn_sink[l], qc, kvc)
            four_c = _four_call(uac, *tabs_ctx)
            xc_mid, rows_c, cnt = _out_call(xc, puc, attn_c, four_c, *proj, *pool_ctx, *rout, cnt, per_batch=False)
            row_sets.append(rows_c)

        route = [r[:, D:D + 2].astype(I32) for r in row_sets]
        bins = jnp.concatenate([r[:, 0] for r in route])
        rank = jnp.concatenate([r[:, 1] for r in route])
        dest, items = _routing_tables(bins, rank, cnt, bins.shape[0])
        xs = _scatter_call(dest, row_sets)
        ys = _moe_call(items, xs, w_gate, w_up, w_down, l)
        x = _gather_call(dest, x_mid.reshape(t_lat, D), mod, ys, seq=seq, per_batch=True,
                         dest_off=0).reshape(nb, seq, D)
        if not last:
            xc = _gather_call(dest, xc_mid.reshape(t_ctx, D), mod, ys, seq=lc, per_batch=False,
                              dest_off=t_lat).reshape(nb, lc, D)
    return x
```

```python
import functools

import numpy as np
import jax
import jax.numpy as jnp
from jax import lax
from jax.experimental import pallas as pl
from jax.experimental.pallas import tpu as pltpu

F32 = jnp.float32
BF16 = jnp.bfloat16
I32 = jnp.int32
HI = lax.Precision.HIGHEST

D = 1024
HEAD_DIM = 64
N_HEADS = 8
GRID_W = 64
POOL_WINDOWS = (2, 4, 8, 16)
POOL_W = 256
ATTN_W = 512
KV_W = 128
FOUR_W = 256
IN_W = 1280
N_EXPERTS = 16
D_EXPERT = 512
WINDOW = 128
ROPE_BASE = 10000.0
EPS = 1e-6
NEG_INF = -1e30
LOG2_E = 1.4426950408889634
CTX_ROW = 4
N_BINS = 24
PAIR_SLOT_A = (0, 2, 2, 3, 3, 3)
PAIR_SLOT_B = (1, 1, 0, 0, 1, 2)
META_W = 128
ROW_W = D + META_W
HALO = 16
POOL_CHUNK = 128

VMEM_LIMIT = 56 * 1024 * 1024
TM_IN = 512
TQ = 512
TM_OUT = 512
TM_FOUR = 256
TM_MOE = 256
TM_ROW = 256
SCATTER_SLOTS = 4
MOD_TN = 1024


def _cparams(n_axes):
    return pltpu.CompilerParams(dimension_semantics=("arbitrary",) * n_axes,
                                vmem_limit_bytes=VMEM_LIMIT)


def _silu(v):
    return v / (1.0 + jnp.exp(-v))


def _mod_body(cs_ref, w_ref, b_ref, o_ref):
    s = _silu(cs_ref[...])
    o_ref[0] = jnp.dot(s, w_ref[0], preferred_element_type=F32, precision=HI) + b_ref[0]


def _mod_call(cs, w_mod, b_mod):
    depth = w_mod.shape[0]
    return pl.pallas_call(
        _mod_body,
        grid=(depth, 6 * D // MOD_TN),
        in_specs=[pl.BlockSpec((8, D), lambda l, j: (0, 0)),
                  pl.BlockSpec((1, D, MOD_TN), lambda l, j: (l, 0, j)),
                  pl.BlockSpec((1, 1, MOD_TN), lambda l, j: (l, 0, j))],
        out_specs=pl.BlockSpec((1, 8, MOD_TN), lambda l, j: (l, 0, j)),
        out_shape=jax.ShapeDtypeStruct((depth, 8, 6 * D), F32),
        compiler_params=_cparams(2),
        name="modulation",
    )(cs, w_mod, b_mod.reshape(depth, 1, 6 * D))


def _ab_body(c_ref, s_ref, w_ref, o_ref):
    w = w_ref[0]
    ca = jnp.dot(c_ref[...], w, preferred_element_type=F32, precision=HI)
    sa = jnp.dot(s_ref[...], w, preferred_element_type=F32, precision=HI)
    o_ref[0] = (jnp.concatenate([ca, sa], axis=1) * (HEAD_DIM ** -0.5)).astype(BF16)


def _ab_call(c64bd, s64bd, wbd):
    depth = wbd.shape[0]
    return pl.pallas_call(
        _ab_body,
        grid=(depth,),
        in_specs=[pl.BlockSpec((FOUR_W, FOUR_W), lambda l: (0, 0)),
                  pl.BlockSpec((FOUR_W, FOUR_W), lambda l: (0, 0)),
                  pl.BlockSpec((1, FOUR_W, FOUR_W), lambda l: (l, 0, 0))],
        out_specs=pl.BlockSpec((1, FOUR_W, 2 * FOUR_W), lambda l: (l, 0, 0)),
        out_shape=jax.ShapeDtypeStruct((depth, FOUR_W, 2 * FOUR_W), BF16),
        compiler_params=_cparams(1),
        name="fourier_weights",
    )(c64bd, s64bd, wbd)


def _head_rms(t, m, g):
    ms = jnp.dot((t * t).astype(BF16), m, preferred_element_type=F32)
    return t * lax.rsqrt(ms + EPS) * g


def _rope(t, cos, sin_signed):
    w = t.shape[1]
    lane = lax.broadcasted_iota(I32, t.shape, 1)
    fwd = pltpu.roll(t, w - 16, 1)
    bwd = pltpu.roll(t, 16, 1)
    rot = jnp.where((lane & 16) == 0, fwd, bwd)
    return t * cos + rot * sin_signed


def _in_body(*refs, rope, per_batch):
    if rope:
        (x_ref, mod_ref, g1_ref, w_ref, m_ref, qg_ref, kg_ref, ab_ref, cos_ref, sin_ref,
         pu_ref, q_ref, kv_ref, ua_ref) = refs
    else:
        (x_ref, mod_ref, g1_ref, w_ref, m_ref, qg_ref, kg_ref, ab_ref,
         pu_ref, q_ref, kv_ref, ua_ref) = refs
    row = pl.program_id(0) if per_batch else CTX_ROW
    sh1 = mod_ref[pl.ds(row, 1), pl.ds(0, D)]
    sc1 = mod_ref[pl.ds(row, 1), pl.ds(D, D)]
    x = x_ref[0]
    ms = jnp.mean(x * x, axis=-1, keepdims=True)
    h = (x * lax.rsqrt(ms + EPS) * g1_ref[...]) * (1.0 + sc1) + sh1
    p = jnp.dot(h.astype(BF16), w_ref[...], preferred_element_type=F32)
    pu = p[:, 0:256]
    q = p[:, 256:768]
    k = p[:, 768:896]
    v = p[:, 896:1024]
    fu = p[:, 1024:1280]
    m = m_ref[...]
    q = _head_rms(q, m, qg_ref[...])
    k = _head_rms(k, m[0:KV_W, 0:KV_W], kg_ref[...])
    if rope:
        cos = cos_ref[...]
        sin = sin_ref[...]
        q = _rope(q, jnp.concatenate([cos] * 4, axis=1), jnp.concatenate([sin] * 4, axis=1))
        k = _rope(k, cos, sin)
    q = q * (HEAD_DIM ** -0.5 * LOG2_E)
    pu_ref[0] = pu.astype(BF16)
    q_ref[0] = q.astype(BF16)
    kv_ref[0] = jnp.concatenate([k, pltpu.roll(k, 64, 1), v, pltpu.roll(v, 64, 1)], axis=1).astype(BF16)
    ua_ref[0] = jnp.dot(fu.astype(BF16), ab_ref[...], preferred_element_type=F32).astype(BF16)


def _in_call(x3, mod, g1, w_in, m512, qg, kg, ab, cos2, sin2, *, rope, per_batch):
    nb, seq, _ = x3.shape
    tm = min(TM_IN, seq)
    full = lambda shape: pl.BlockSpec(shape, lambda b, i: (0,) * len(shape))
    in_specs = [pl.BlockSpec((1, tm, D), lambda b, i: (b, i, 0)),
                full((8, 6 * D)), full((1, D)), full((D, IN_W)), full((ATTN_W, ATTN_W)),
                full((1, ATTN_W)), full((1, KV_W)), full((FOUR_W, 2 * FOUR_W))]
    args = [x3, mod, g1, w_in, m512, qg, kg, ab]
    if rope:
        in_specs += [pl.BlockSpec((tm, 128), lambda b, i: (i, 0)),
                     pl.BlockSpec((tm, 128), lambda b, i: (i, 0))]
        args += [cos2, sin2]
    widths = (POOL_W, ATTN_W, 4 * KV_W, 2 * FOUR_W)
    return pl.pallas_call(
        functools.partial(_in_body, rope=rope, per_batch=per_batch),
        grid=(nb, seq // tm),
        in_specs=in_specs,
        out_specs=[pl.BlockSpec((1, tm, w), lambda b, i: (b, i, 0)) for w in widths],
        out_shape=[jax.ShapeDtypeStruct((nb, seq, w), BF16) for w in widths],
        compiler_params=_cparams(2),
        name="in_proj_rope" if rope else "in_proj_ctx",
    )(*args)


_NT = (((1,), (1,)), ((), ()))


def _stack_heads(qpair0, qpair1, lo):
    z = jnp.zeros_like(qpair0)
    parts = [jnp.where(lo, qpair0, z), jnp.where(lo, qpair1, z),
             jnp.where(lo, pltpu.roll(qpair0, 64, 1), z), jnp.where(lo, pltpu.roll(qpair1, 64, 1), z)]
    return jnp.concatenate(parts, axis=0).astype(BF16)


def _group_attention(q4, k_parts, va_parts, vb_parts, masks, sink_col):
    s_parts = []
    for kz, mk in zip(k_parts, masks):
        s = lax.dot_general(q4, kz, _NT, preferred_element_type=F32)
        if mk is not None:
            nk = s.shape[1]
            s = jnp.where(mk[None], s.reshape(4, 128, nk), NEG_INF).reshape(512, nk)
        s_parts.append(s)
    m = functools.reduce(jnp.maximum, [jnp.max(s, axis=-1, keepdims=True) for s in s_parts])
    m = jnp.maximum(m, sink_col)
    den = jnp.exp2(sink_col - m)
    oe = oo = None
    for s, va, vb in zip(s_parts, va_parts, vb_parts):
        e = jnp.exp2(s - m)
        den = den + jnp.sum(e, axis=-1, keepdims=True)
        eb = e.astype(BF16)
        pe = jnp.dot(eb[0:256], va, preferred_element_type=F32)
        po = jnp.dot(eb[256:512], vb, preferred_element_type=F32)
        oe = pe if oe is None else oe + pe
        oo = po if oo is None else oo + po
    inv = 1.0 / den
    return oe * inv[0:256], oo * inv[256:512]


def _sink_cols(sink_ref):
    rb = lax.broadcasted_iota(I32, (512, 1), 0) >> 7
    cols = []
    for kvh in range(2):
        s = [sink_ref[4 * kvh + j] * LOG2_E for j in (0, 2, 1, 3)]
        cols.append(jnp.where(rb == 0, s[0], jnp.where(rb == 1, s[1], jnp.where(rb == 2, s[2], s[3]))))
    return cols


def _attend_block(qblk, kv_parts, masks, sink_cols):
    lo = lax.broadcasted_iota(I32, (128, 128), 1) < 64
    cols = []
    for kvh in range(2):
        q4 = _stack_heads(qblk[:, 256 * kvh:256 * kvh + 128], qblk[:, 256 * kvh + 128:256 * kvh + 256], lo)
        ko = 128 * kvh
        vao = 256 + 128 * kvh
        vbo = 384 - 128 * kvh
        oe, oo = _group_attention(q4, [kv[:, ko:ko + 128] for kv in kv_parts],
                                  [kv[:, vao:vao + 128] for kv in kv_parts],
                                  [kv[:, vbo:vbo + 128] for kv in kv_parts], masks, sink_cols[kvh])
        cols.append(jnp.where(lo, oe[0:128], oo[0:128]))
        cols.append(jnp.where(lo, oe[128:256], oo[128:256]))
    return jnp.concatenate(cols, axis=1).astype(BF16)


def _attn_lat_body(sink_ref, q_ref, kvp_ref, kvm_ref, kvn_ref, kvc_ref, o_ref, kvw_ref, *, tq, seq):
    i = pl.program_id(1)
    kvw_ref[0:128] = kvp_ref[0]
    kvw_ref[128:128 + tq] = kvm_ref[0]
    kvw_ref[128 + tq:256 + tq] = kvn_ref[0]
    kvc = kvc_ref[0]
    sink_cols = _sink_cols(sink_ref)

    def sub(j, carry):
        r0 = pl.multiple_of(j * 128, 128)
        win = kvw_ref[pl.ds(r0, 3 * 128), :]
        ii = lax.broadcasted_iota(I32, (128, 3 * 128), 0)
        cc = lax.broadcasted_iota(I32, (128, 3 * 128), 1)
        base = i * tq + j * 128 - 128
        valid = (ii <= cc) & (cc <= ii + 2 * WINDOW) & (cc >= -base) & (cc < seq - base)
        qblk = q_ref[0, pl.ds(r0, 128), :].astype(F32)
        o_ref[0, pl.ds(r0, 128), :] = _attend_block(qblk, [win, kvc], [valid, None], sink_cols)
        return carry

    for j in range(tq // 128):
        sub(j, 0)


def _attn_lat_call(sink, q, kv, kvc):
    nb, seq, _ = q.shape
    lc = kvc.shape[1]
    tq = TQ
    nblk = seq // 128
    r = tq // 128
    return pl.pallas_call(
        functools.partial(_attn_lat_body, tq=tq, seq=seq),
        grid=(nb, seq // tq),
        in_specs=[pl.BlockSpec(memory_space=pltpu.SMEM),
                  pl.BlockSpec((1, tq, ATTN_W), lambda b, i: (b, i, 0)),
                  pl.BlockSpec((1, 128, 4 * KV_W), lambda b, i: (b, jnp.maximum(i * r - 1, 0), 0)),
                  pl.BlockSpec((1, tq, 4 * KV_W), lambda b, i: (b, i, 0)),
                  pl.BlockSpec((1, 128, 4 * KV_W), lambda b, i: (b, jnp.minimum((i + 1) * r, nblk - 1), 0)),
                  pl.BlockSpec((1, lc, 4 * KV_W), lambda b, i: (b, 0, 0))],
        out_specs=pl.BlockSpec((1, tq, ATTN_W), lambda b, i: (b, i, 0)),
        out_shape=jax.ShapeDtypeStruct((nb, seq, ATTN_W), BF16),
        scratch_shapes=[pltpu.VMEM((tq + 256, 4 * KV_W), BF16)],
        compiler_params=_cparams(2),
        name="attention_window",
    )(sink, q, kv, kv, kv, kvc)


def _attn_ctx_body(sink_ref, q_ref, kvc_ref, o_ref, *, lc):
    kvc = kvc_ref[0]
    sink_cols = _sink_cols(sink_ref)
    for j in range(lc // 128):
        qblk = q_ref[0, j * 128:(j + 1) * 128, :].astype(F32)
        o_ref[0, j * 128:(j + 1) * 128, :] = _attend_block(qblk, [kvc], [None], sink_cols)


def _attn_ctx_call(sink, qc, kvc):
    nb, lc, _ = qc.shape
    return pl.pallas_call(
        functools.partial(_attn_ctx_body, lc=lc),
        grid=(nb,),
        in_specs=[pl.BlockSpec(memory_space=pltpu.SMEM),
                  pl.BlockSpec((1, lc, ATTN_W), lambda b: (b, 0, 0)),
                  pl.BlockSpec((1, lc, 4 * KV_W), lambda b: (b, 0, 0))],
        out_specs=pl.BlockSpec((1, lc, ATTN_W), lambda b: (b, 0, 0)),
        out_shape=jax.ShapeDtypeStruct((nb, lc, ATTN_W), BF16),
        compiler_params=_cparams(1),
        name="attention_ctx",
    )(sink, qc, kvc)


def _four_body(ua_ref, cb_ref, sb_ref, ca_ref, sa_ref, o_ref, *, nb, scale):
    i = pl.program_id(0)
    ca = ca_ref[pl.ds(i, 1), :]
    sa = sa_ref[pl.ds(i, 1), :]
    cb = cb_ref[...]
    sb = sb_ref[...]
    ct = (ca * cb - sa * sb).astype(BF16)
    st = (sa * cb + ca * sb).astype(BF16)
    for b in range(nb):
        ua = ua_ref[b, :, 0:FOUR_W]
        ub = ua_ref[b, :, FOUR_W:2 * FOUR_W]
        r = (jnp.dot(ct, ua, preferred_element_type=F32) - jnp.dot(st, ub, preferred_element_type=F32))
        o_ref[b] = (r * scale).astype(BF16)


def _four_call(uaub, cb, sb, ca, sa):
    nb, seq, _ = uaub.shape
    tm = cb.shape[0]
    one = pl.Buffered(1)
    return pl.pallas_call(
        functools.partial(_four_body, nb=nb, scale=float(seq) ** -0.5),
        grid=(seq // tm,),
        in_specs=[pl.BlockSpec((nb, seq, 2 * FOUR_W), lambda i: (0, 0, 0), pipeline_mode=one),
                  pl.BlockSpec((tm, seq), lambda i: (0, 0), pipeline_mode=one),
                  pl.BlockSpec((tm, seq), lambda i: (0, 0), pipeline_mode=one),
                  pl.BlockSpec((seq // tm, seq), lambda i: (0, 0), pipeline_mode=one),
                  pl.BlockSpec((seq // tm, seq), lambda i: (0, 0), pipeline_mode=one)],
        out_specs=pl.BlockSpec((nb, tm, FOUR_W), lambda i: (0, i, 0)),
        out_shape=jax.ShapeDtypeStruct((nb, seq, FOUR_W), BF16),
        compiler_params=_cparams(1),
        name="fourier_dft",
    )(uaub, cb, sb, ca, sa)


def _dft_tables(seq, tm):
    n = jnp.arange(seq, dtype=I32)[None, :]

    def tab(rows):
        ang = ((rows[:, None] * n) % seq).astype(F32) * (2.0 * np.pi / seq)
        return jnp.cos(ang), jnp.sin(ang)

    cb, sb = tab(jnp.arange(tm, dtype=I32))
    ca, sa = tab(jnp.arange(seq // tm, dtype=I32) * tm)
    return cb, sb, ca, sa


def _route(lt, tri_ref, cnt_ref, tm):
    rowi = lax.broadcasted_iota(I32, (32, tm), 0)
    big = jnp.int32(999)

    def first_argmax(vals):
        mx = jnp.max(vals, axis=0, keepdims=True)
        return mx, jnp.min(jnp.where(vals == mx, rowi, big), axis=0, keepdims=True)

    is_grp = rowi < 4
    mg, gi = first_argmax(jnp.where(is_grp, lt, NEG_INF))
    pg = 1.0 / jnp.sum(jnp.where(is_grp, jnp.exp(jnp.where(is_grp, lt, mg) - mg), 0.0), axis=0, keepdims=True)
    est = 4 + 4 * gi
    le = jnp.where((rowi >= est) & (rowi < est + 4), lt, NEG_INF)
    m1, i1 = first_argmax(le)
    m2, i2 = first_argmax(jnp.where(rowi == i1, NEG_INF, le))
    e2 = jnp.exp(m2 - m1)
    w1 = pg / (1.0 + e2)
    w2 = pg * e2 / (1.0 + e2)
    a1 = i1 - est
    a2 = i2 - est
    code = jnp.minimum(a1, a2) * 4 + jnp.maximum(a1, a2)
    pidx = jnp.where(code == 1, 0, jnp.where(code == 6, 1, jnp.where(code == 2, 2,
           jnp.where(code == 3, 3, jnp.where(code == 7, 4, 5)))))
    slot_a = jnp.where(pidx == 0, 0, jnp.where(pidx <= 2, 2, 3))
    slot_b = jnp.where(pidx <= 1, 1, jnp.where(pidx <= 3, 0, jnp.where(pidx == 4, 1, 2)))
    wa = jnp.where(a1 == slot_a, w1, w2)
    wb = jnp.where(a1 == slot_b, w1, w2)
    bin_ = gi * 6 + pidx

    onehot = rowi == bin_
    pref = jnp.dot(onehot.astype(BF16), tri_ref[...], preferred_element_type=F32)
    carry = cnt_ref[:, 0:1]
    rank = jnp.sum(jnp.where(onehot, pref - 1.0 + carry, 0.0), axis=0, keepdims=True)
    cnt_ref[...] = jnp.broadcast_to(carry + pref[:, tm - 1:tm], cnt_ref.shape)
    return jnp.concatenate([bin_.astype(F32), rank, wa, wb, jnp.zeros((128 - 4, tm), F32)], axis=0)


def _out_body(x_ref, pup_ref, pum_ref, pun_ref, at_ref, fo_ref, wo_ref, pw_ref, ps_ref, band_ref, icnt_ref,
              mod_ref, g2_ref, w2_ref, br_ref, tri_ref, cin_ref,
              xo_ref, rows_ref, cnt_ref, *, tm, per_batch):
    b = pl.program_id(0)
    i = pl.program_id(1)
    nt = pl.num_programs(1)
    row = b if per_batch else CTX_ROW
    g1 = mod_ref[pl.ds(row, 1), pl.ds(2 * D, D)]
    sh2 = mod_ref[pl.ds(row, 1), pl.ds(3 * D, D)]
    sc2 = mod_ref[pl.ds(row, 1), pl.ds(4 * D, D)]

    @pl.when((b == 0) & (i == 0))
    def _():
        cnt_ref[...] = cin_ref[...]

    um = pum_ref[0]
    zh = jnp.zeros((HALO, POOL_W), BF16)
    uext = jnp.concatenate([jnp.where(i > 0, pup_ref[0], zh), um, jnp.where(i < nt - 1, pun_ref[0], zh)], axis=0)
    grp = lax.broadcasted_iota(I32, (POOL_CHUNK, POOL_W), 1) >> 6
    chunks = []
    for c in range(tm // POOL_CHUNK):
        uc = uext[POOL_CHUNK * c:POOL_CHUNK * (c + 1) + 2 * HALO]
        pc = jnp.zeros((POOL_CHUNK, POOL_W), F32)
        for g in range(len(POOL_WINDOWS)):
            pc = jnp.where(grp == g, jnp.dot(band_ref[g], uc, preferred_element_type=F32), pc)
        chunks.append(pc)
    pooled = jnp.concatenate(chunks, axis=0)
    y = pooled * icnt_ref[...] - um.astype(F32)
    pool_out = jnp.dot(y.astype(BF16), pw_ref[...], preferred_element_type=F32) * ps_ref[...]

    cat = jnp.concatenate([pool_out.astype(BF16), at_ref[0], fo_ref[0]], axis=1)
    xm = x_ref[0] + g1 * jnp.dot(cat, wo_ref[...], preferred_element_type=F32)
    xo_ref[0] = xm

    ms = jnp.mean(xm * xm, axis=-1, keepdims=True)
    h2 = (xm * lax.rsqrt(ms + EPS) * g2_ref[...]) * (1.0 + sc2) + sh2

    hh = h2.astype(BF16)
    hl = (h2 - hh.astype(F32)).astype(BF16)
    w2 = w2_ref[...]
    p2 = jnp.dot(hh, w2, preferred_element_type=F32)
    logits = (p2[:, 0:128] + p2[:, 128:256] + jnp.dot(hl, w2[:, 0:128], preferred_element_type=F32) + br_ref[...])
    meta = _route(logits.T[0:32, :], tri_ref, cnt_ref, tm).T

    rows_ref[:, 0:D] = h2
    rows_ref[:, D:ROW_W] = meta


def _out_call(x3, pu, attn, four, w_out, pool_wbd, pool_scale, bands, icnt, mod, g2, w2, br, tri, cnt_in, *,
              per_batch):
    nb, seq, _ = x3.shape
    tm = min(TM_OUT, seq)
    nt = seq // tm
    hb = tm // HALO
    full = lambda shape: pl.BlockSpec(shape, lambda b, i: (0,) * len(shape))
    in_specs = [pl.BlockSpec((1, tm, D), lambda b, i: (b, i, 0)),
                pl.BlockSpec((1, HALO, POOL_W), lambda b, i: (b, jnp.maximum(i * hb - 1, 0), 0)),
                pl.BlockSpec((1, tm, POOL_W), lambda b, i: (b, i, 0)),
                pl.BlockSpec((1, HALO, POOL_W), lambda b, i: (b, jnp.minimum((i + 1) * hb, seq // HALO - 1), 0)),
                pl.BlockSpec((1, tm, ATTN_W), lambda b, i: (b, i, 0)),
                pl.BlockSpec((1, tm, FOUR_W), lambda b, i: (b, i, 0)),
                full((D, D)), full((POOL_W, POOL_W)), full((1, POOL_W)),
                full((len(POOL_WINDOWS), POOL_CHUNK, POOL_CHUNK + 2 * HALO)),
                pl.BlockSpec((tm, POOL_W), lambda b, i: (i, 0)),
                full((8, 6 * D)), full((1, D)), full((D, 256)), full((1, 128)), full((tm, tm)), full((32, 128))]
    args = [x3, pu, pu, pu, attn, four, w_out, pool_wbd, pool_scale, bands, icnt, mod, g2, w2, br, tri, cnt_in]
    return pl.pallas_call(
        functools.partial(_out_body, tm=tm, per_batch=per_batch),
        grid=(nb, nt),
        in_specs=in_specs,
        out_specs=[pl.BlockSpec((1, tm, D), lambda b, i: (b, i, 0)),
                   pl.BlockSpec((tm, ROW_W), lambda b, i: (b * nt + i, 0)),
                   pl.BlockSpec((32, 128), lambda b, i: (0, 0))],
        out_shape=[jax.ShapeDtypeStruct((nb, seq, D), F32),
                   jax.ShapeDtypeStruct((nb * seq, ROW_W), F32),
                   jax.ShapeDtypeStruct((32, 128), F32)],
        compiler_params=_cparams(2),
        name="out_proj_router" if per_batch else "out_proj_router_ctx",
    )(*args)


def _pool_tables(seq):
    t = np.arange(POOL_CHUNK)[:, None]
    s = np.arange(POOL_CHUNK + 2 * HALO)[None, :] - HALO
    bands = np.stack([(s >= t - w // 2) & (s <= t + w // 2 - 1) for w in POOL_WINDOWS]).astype(np.float32)
    pos = np.arange(seq)
    icnt = np.stack([1.0 / (np.minimum(pos + w // 2 - 1, seq - 1) - np.maximum(pos - w // 2, 0) + 1)
                     for w in POOL_WINDOWS], axis=1)
    return jnp.asarray(bands, BF16), jnp.asarray(np.repeat(icnt, POOL_W // len(POOL_WINDOWS), axis=1), F32)


def _row_copies(tm, make_copy):
    for r in range(tm):
        make_copy(r).start(priority=r % 2)


def _row_waits(tm, make_copy):
    def drain(r, c):
        make_copy(0).wait()
        return c

    lax.fori_loop(0, tm, drain, 0, unroll=8)


def _scatter_body(dest_ref, *refs, tm, n_tiles, n_first):
    n_h = len(refs) - 1 - 3 * SCATTER_SLOTS
    h_refs, xs_ref = refs[:n_h], refs[n_h]
    bufs = refs[n_h + 1:n_h + 1 + SCATTER_SLOTS]
    lsems = refs[n_h + 1 + SCATTER_SLOTS:n_h + 1 + 2 * SCATTER_SLOTS]
    rsems = refs[n_h + 1 + 2 * SCATTER_SLOTS:]

    def load(t, slot):
        def start(h_ref, tt):
            pltpu.make_async_copy(h_ref.at[pl.ds(tt * tm, tm)], bufs[slot], lsems[slot]).start()

        if n_h == 1:
            start(h_refs[0], t)
        else:
            @pl.when(t < n_first)
            def _():
                start(h_refs[0], t)

            @pl.when(t >= n_first)
            def _():
                start(h_refs[1], t - n_first)

    def row_copy(slot, r, d):
        return pltpu.make_async_copy(bufs[slot].at[pl.ds(r, 1)], xs_ref.at[pl.ds(d, 1)], rsems[slot])

    load(0, 0)
    load(1, 1)

    def group(g, c):
        for slot in range(SCATTER_SLOTS):
            t = g * SCATTER_SLOTS + slot
            ahead = (slot + 2) % SCATTER_SLOTS
            pltpu.make_async_copy(h_refs[0].at[pl.ds(0, tm)], bufs[slot], lsems[slot]).wait()

            @pl.when(t >= 2)
            def _():
                _row_waits(tm, lambda r: row_copy(ahead, r, 0))

            @pl.when(t + 2 < n_tiles)
            def _():
                load(t + 2, ahead)

            _row_copies(tm, lambda r: row_copy(slot, r, dest_ref[t * tm + r]))
        return c

    lax.fori_loop(0, n_tiles // SCATTER_SLOTS, group, 0)
    for t in (n_tiles - 2, n_tiles - 1):
        _row_waits(tm, lambda r: row_copy(t % SCATTER_SLOTS, r, 0))


def _scatter_call(dest, row_sets):
    tm = TM_ROW
    n_first = row_sets[0].shape[0] // tm
    n_rows = sum(r.shape[0] for r in row_sets)
    n_tiles = n_rows // tm
    assert n_tiles % SCATTER_SLOTS == 0 and n_tiles >= SCATTER_SLOTS
    return pl.pallas_call(
        functools.partial(_scatter_body, tm=tm, n_tiles=n_tiles, n_first=n_first),
        grid_spec=pltpu.PrefetchScalarGridSpec(
            num_scalar_prefetch=1,
            grid=(1,),
            in_specs=[pl.BlockSpec(memory_space=pl.ANY)] * len(row_sets),
            out_specs=pl.BlockSpec(memory_space=pl.ANY),
            scratch_shapes=([pltpu.VMEM((tm, ROW_W), F32)] * SCATTER_SLOTS
                            + [pltpu.SemaphoreType.DMA(())] * (2 * SCATTER_SLOTS))),
        out_shape=jax.ShapeDtypeStruct((n_rows, ROW_W), F32),
        compiler_params=_cparams(1),
        name="moe_scatter_rows",
    )(dest, *row_sets)


def _moe_body(tile_ref, ea_ref, eb_ref, lo_ref, hi_ref, first_ref, nv_ref,
              xs_ref, wga, wua, wda, wgb, wub, wdb, ys_ref):
    del tile_ref, ea_ref, eb_ref
    w = pl.program_id(0)

    @pl.when(w < nv_ref[0])
    def _():
        h = xs_ref[:, 0:D].astype(BF16)
        meta = xs_ref[:, D:ROW_W]
        rowi = lax.broadcasted_iota(I32, (h.shape[0], 1), 0)
        mine = (rowi >= lo_ref[w]) & (rowi < hi_ref[w])

        def expert(wg, wu, wd, gate):
            g = jnp.dot(h, wg[0, 0].astype(BF16), preferred_element_type=F32)
            u = jnp.dot(h, wu[0, 0].astype(BF16), preferred_element_type=F32)
            a = _silu(g) * u * jnp.where(mine, gate, 0.0)
            return jnp.dot(a.astype(BF16), wd[0, 0].astype(BF16), preferred_element_type=F32)

        y = expert(wga, wua, wda, meta[:, 2:3]) + expert(wgb, wub, wdb, meta[:, 3:4])

        @pl.when(first_ref[w] == 1)
        def _():
            ys_ref[...] = y

        @pl.when(first_ref[w] == 0)
        def _():
            ys_ref[...] += y


def _moe_call(items, xs, w_gate, w_up, w_down, layer):
    tm = TM_MOE
    n_items = items[0].shape[0]
    row = lambda w, tile, *_: (tile[w], 0)
    wa = lambda w, tile, ea, *_: (layer, ea[w], 0, 0)
    wb = lambda w, tile, ea, eb, *_: (layer, eb[w], 0, 0)
    up_spec = lambda f: pl.BlockSpec((1, 1, D, D_EXPERT), f)
    dn_spec = lambda f: pl.BlockSpec((1, 1, D_EXPERT, D), f)
    return pl.pallas_call(
        _moe_body,
        grid_spec=pltpu.PrefetchScalarGridSpec(
            num_scalar_prefetch=len(items),
            grid=(n_items,),
            in_specs=[pl.BlockSpec((tm, ROW_W), row),
                      up_spec(wa), up_spec(wa), dn_spec(wa), up_spec(wb), up_spec(wb), dn_spec(wb)],
            out_specs=pl.BlockSpec((tm, D), row)),
        out_shape=jax.ShapeDtypeStruct((xs.shape[0], D), F32),
        compiler_params=_cparams(1),
        name="moe_experts",
    )(*items, xs, w_gate, w_up, w_down, w_gate, w_up, w_down)


def _gather_body(dest_ref, x_ref, mod_ref, ys_ref, o_ref, ybuf0, ybuf1, sem0, sem1, *,
                 tm, seq, per_batch, dest_off):
    i = pl.program_id(0)
    n_tiles = pl.num_programs(0)
    ybufs = (ybuf0, ybuf1)
    sems = (sem0, sem1)
    row = (i * tm) // seq if per_batch else CTX_ROW
    g2 = mod_ref[pl.ds(row, 1), pl.ds(5 * D, D)]

    def row_copy(slot, r, d):
        return pltpu.make_async_copy(ys_ref.at[pl.ds(d, 1)], ybufs[slot].at[pl.ds(r, 1)], sems[slot])

    def fetch(t, slot):
        base = dest_off + t * tm
        _row_copies(tm, lambda r: row_copy(slot, r, dest_ref[base + r]))

    @pl.when(i == 0)
    def _():
        fetch(0, 0)

    for slot in range(2):
        @pl.when(i % 2 == slot)
        def _():
            @pl.when(i + 1 < n_tiles)
            def _():
                fetch(i + 1, 1 - slot)

            _row_waits(tm, lambda r: row_copy(slot, r, 0))
            o_ref[...] = x_ref[...] + g2 * ybufs[slot][...]


def _gather_call(dest, x2, mod, ys, *, seq, per_batch, dest_off):
    n = x2.shape[0]
    tm = TM_ROW
    return pl.pallas_call(
        functools.partial(_gather_body, tm=tm, seq=seq, per_batch=per_batch, dest_off=dest_off),
        grid_spec=pltpu.PrefetchScalarGridSpec(
            num_scalar_prefetch=1,
            grid=(n // tm,),
            in_specs=[pl.BlockSpec((tm, D), lambda i, *_: (i, 0)),
                      pl.BlockSpec((8, 6 * D), lambda i, *_: (0, 0)),
                      pl.BlockSpec(memory_space=pl.ANY)],
            out_specs=pl.BlockSpec((tm, D), lambda i, *_: (i, 0)),
            scratch_shapes=[pltpu.VMEM((tm, D), F32), pltpu.VMEM((tm, D), F32),
                            pltpu.SemaphoreType.DMA(()), pltpu.SemaphoreType.DMA(())]),
        out_shape=jax.ShapeDtypeStruct((n, D), F32),
        compiler_params=_cparams(1),
        name="moe_gather_rows" if per_batch else "moe_gather_rows_ctx",
    )(dest, x2, mod, ys)


def _routing_tables(bins, rank, cnt, n_rows):
    tm = TM_MOE
    counts = cnt[:N_BINS, 0].astype(I32)
    ends = jnp.cumsum(counts)
    starts = ends - counts
    ids = jnp.arange(N_BINS, dtype=I32)
    pick = lambda key, tab: jnp.sum(jnp.where(key[:, None] == ids[None, :], tab[None, :], 0), axis=1)
    dest = rank + pick(bins, starts)

    tile_first = starts // tm
    tile_last = jnp.maximum(ends - 1, 0) // tm
    n_it = jnp.where(counts > 0, tile_last - tile_first + 1, 0)
    it_end = jnp.cumsum(n_it)
    it_start = it_end - n_it
    n_valid = it_end[-1]
    n_items = n_rows // tm + N_BINS
    w = jnp.minimum(jnp.arange(n_items, dtype=I32), n_valid - 1)
    wbin = jnp.sum((w[:, None] >= it_end[None, :]).astype(I32), axis=1)
    tile = pick(wbin, tile_first) + w - pick(wbin, it_start)
    lo = jnp.clip(pick(wbin, starts) - tile * tm, 0, tm)
    hi = jnp.clip(pick(wbin, ends) - tile * tm, 0, tm)
    first = jnp.concatenate([jnp.ones((1,), I32), (tile[1:] != tile[:-1]).astype(I32)])
    pidx = wbin % 6
    six = jnp.arange(6, dtype=I32)
    slot = lambda tab: jnp.sum(jnp.where(pidx[:, None] == six[None, :], jnp.asarray(tab, I32)[None, :], 0), axis=1)
    ea = 4 * (wbin // 6) + slot(PAIR_SLOT_A)
    eb = 4 * (wbin // 6) + slot(PAIR_SLOT_B)
    return dest, (tile, ea, eb, lo, hi, first, n_valid.reshape(1))


def _block_diag(w):
    g, c, d = w.shape
    out = jnp.zeros((g * c, g * d), w.dtype)
    for j in range(g):
        out = out.at[j * c:(j + 1) * c, j * d:(j + 1) * d].set(w[j])
    return out


def _rope_tables(n_tokens):
    rows = n_tokens // GRID_W
    r = jnp.broadcast_to(jnp.arange(rows)[:, None], (rows, GRID_W)).reshape(-1).astype(F32)
    col = jnp.broadcast_to(jnp.arange(GRID_W)[None, :], (rows, GRID_W)).reshape(-1).astype(F32)
    half = HEAD_DIM // 2
    inv = 1.0 / (ROPE_BASE ** (jnp.arange(0, half, 2, dtype=F32) / half))
    ar = r[:, None] * inv
    ac = col[:, None] * inv
    ang = jnp.concatenate([ar, ar, ac, ac], axis=-1)
    cos, sin = jnp.cos(ang), jnp.sin(ang)
    sign = jnp.where((jnp.arange(HEAD_DIM) & 16) == 0, -1.0, 1.0).astype(F32)
    return jnp.concatenate([cos, cos], axis=1), jnp.concatenate([sin * sign, sin * sign], axis=1)


def kernel(x, c, ctx, c_ctx, w_mod, b_mod, norm1_g, w_in, q_norm_g, k_norm_g, attn_sink, pool_w, pool_scale,
           four_w, w_out, norm2_g, w_grp, b_grp, w_rtr, b_rtr, w_gate, w_up, w_down):
    nb, seq, _ = x.shape
    lc = ctx.shape[1]
    depth = w_mod.shape[0]
    t_lat = nb * seq
    t_ctx = nb * lc

    cs = jnp.concatenate([c, c_ctx[None, :], jnp.zeros((8 - nb - 1, D), F32)], axis=0)
    m512 = jnp.asarray(np.kron(np.eye(N_HEADS), np.full((HEAD_DIM, HEAD_DIM), 1.0 / HEAD_DIM)), BF16)
    kk = np.arange(HEAD_DIM)
    ang64 = 2.0 * np.pi * ((kk[:, None] * kk[None, :]) % HEAD_DIM) / HEAD_DIM
    c64bd = jnp.asarray(np.kron(np.eye(4), np.cos(ang64)), F32)
    s64bd = jnp.asarray(np.kron(np.eye(4), np.sin(ang64)), F32)
    tri_of = lambda n: jnp.asarray(np.triu(np.ones((min(TM_OUT, n),) * 2)), BF16)
    cos2, sin2 = _rope_tables(seq)
    tabs_lat = _dft_tables(seq, TM_FOUR)
    tabs_ctx = _dft_tables(lc, min(TM_FOUR, lc))
    pool_lat = _pool_tables(seq)
    pool_ctx = _pool_tables(lc)
    w_in_b = w_in.astype(BF16)
    w_out_b = w_out.astype(BF16)
    wbd = jnp.stack([_block_diag(four_w[l]) for l in range(depth)])
    pool_wbd = jnp.stack([_block_diag(pool_w[l]) for l in range(depth)]).astype(BF16)
    wr = jnp.concatenate([w_grp, w_rtr, jnp.zeros((depth, D, 128 - 4 - N_EXPERTS), F32)], axis=2)
    wr_hi = wr.astype(BF16)
    w2 = jnp.concatenate([wr_hi, (wr - wr_hi.astype(F32)).astype(BF16)], axis=2)
    br = jnp.concatenate([b_grp, b_rtr, jnp.zeros((depth, 128 - 4 - N_EXPERTS), F32)], axis=1)

    mod_all = _mod_call(cs, w_mod, b_mod)
    ab_all = _ab_call(c64bd, s64bd, wbd)

    xc = ctx
    for l in range(depth):
        last = l == depth - 1
        mod = mod_all[l]
        g1 = norm1_g[l][None, :]
        g2 = norm2_g[l][None, :]
        qg = jnp.tile(q_norm_g[l], N_HEADS)[None, :]
        kg = jnp.tile(k_norm_g[l], KV_W // HEAD_DIM)[None, :]
        in_args = (mod, g1, w_in_b[l], m512, qg, kg, ab_all[l])

        puc, qc, kvc, uac = _in_call(xc, *in_args, None, None, rope=False, per_batch=False)
        pul, ql, kvl, ual = _in_call(x, *in_args, cos2, sin2, rope=True, per_batch=True)
        attn_l = _attn_lat_call(attn_sink[l], ql, kvl, kvc)
        four_l = _four_call(ual, *tabs_lat)

        proj = (w_out_b[l], pool_wbd[l], pool_scale[l][None, :])
        rout = (mod, g2, w2[l], br[l][None, :])
        cnt0 = jnp.zeros((32, 128), F32)
        x_mid, rows_l, cnt = _out_call(x, pul, attn_l, four_l, *proj, *pool_lat, *rout, tri_of(seq), cnt0,
                                       per_batch=True)
        row_sets = [rows_l]
        if not last:
            attn_c = _attn_ctx_call(attn_sink[l], qc, kvc)
            four_c = _four_call(uac, *tabs_ctx)
            xc_mid, rows_c, cnt = _out_call(xc, puc, attn_c, four_c, *proj, *pool_ctx, *rout, tri_of(lc), cnt,
                                            per_batch=False)
            row_sets.append(rows_c)

        route = [r[:, D:D + 2].astype(I32) for r in row_sets]
        bins = jnp.concatenate([r[:, 0] for r in route])
        rank = jnp.concatenate([r[:, 1] for r in route])
        dest, items = _routing_tables(bins, rank, cnt, bins.shape[0])
        xs = _scatter_call(dest, row_sets)
        ys = _moe_call(items, xs, w_gate, w_up, w_down, l)
        x = _gather_call(dest, x_mid.reshape(t_lat, D), mod, ys, seq=seq, per_batch=True,
                         dest_off=0).reshape(nb, seq, D)
        if not last:
            xc = _gather_call(dest, xc_mid.reshape(t_ctx, D), mod, ys, seq=lc, per_batch=False,
                              dest_off=t_lat).reshape(nb, lc, D)
    return x
```

```python
import functools

import numpy as np
import jax
import jax.numpy as jnp
from jax import lax
from jax.experimental import pallas as pl
from jax.experimental.pallas import tpu as pltpu

F32 = jnp.float32
BF16 = jnp.bfloat16
I32 = jnp.int32
HI = lax.Precision.HIGHEST

D = 1024
HEAD_DIM = 64
N_HEADS = 8
GRID_W = 64
POOL_WINDOWS = (2, 4, 8, 16)
POOL_W = 256
ATTN_W = 512
KV_W = 128
FOUR_W = 256
IN_W = 1280
N_EXPERTS = 16
D_EXPERT = 512
WINDOW = 128
ROPE_BASE = 10000.0
EPS = 1e-6
NEG_INF = -1e30
LOG2_E = 1.4426950408889634
CTX_ROW = 4
N_BINS = 24
PAIR_SLOT_A = (0, 2, 2, 3, 3, 3)
PAIR_SLOT_B = (1, 1, 0, 0, 1, 2)
META_W = 128
ROW_W = D + META_W
HALO = 16
POOL_CHUNK = 128

VMEM_LIMIT = 56 * 1024 * 1024
TM_IN = 512
TQ = 512
TM_OUT = 512
TM_FOUR = 256
FOUR_R1 = 16
FOUR1_A_CHUNK = 4
TM_MOE = 256
TM_ROW = 256
SCATTER_SLOTS = 4
MOD_TN = 1024


def _cparams(n_axes):
    return pltpu.CompilerParams(dimension_semantics=("arbitrary",) * n_axes,
                                vmem_limit_bytes=VMEM_LIMIT)


def _silu(v):
    return v / (1.0 + jnp.exp(-v))


def _mod_body(cs_ref, w_ref, b_ref, o_ref):
    s = _silu(cs_ref[...])
    o_ref[0] = jnp.dot(s, w_ref[0], preferred_element_type=F32, precision=HI) + b_ref[0]


def _mod_call(cs, w_mod, b_mod):
    depth = w_mod.shape[0]
    return pl.pallas_call(
        _mod_body,
        grid=(depth, 6 * D // MOD_TN),
        in_specs=[pl.BlockSpec((8, D), lambda l, j: (0, 0)),
                  pl.BlockSpec((1, D, MOD_TN), lambda l, j: (l, 0, j)),
                  pl.BlockSpec((1, 1, MOD_TN), lambda l, j: (l, 0, j))],
        out_specs=pl.BlockSpec((1, 8, MOD_TN), lambda l, j: (l, 0, j)),
        out_shape=jax.ShapeDtypeStruct((depth, 8, 6 * D), F32),
        compiler_params=_cparams(2),
        name="modulation",
    )(cs, w_mod, b_mod.reshape(depth, 1, 6 * D))


def _ab_body(c_ref, s_ref, w_ref, o_ref):
    w = w_ref[0]
    ca = jnp.dot(c_ref[...], w, preferred_element_type=F32, precision=HI)
    sa = jnp.dot(s_ref[...], w, preferred_element_type=F32, precision=HI)
    o_ref[0] = (jnp.concatenate([ca, sa], axis=1) * (HEAD_DIM ** -0.5)).astype(BF16)


def _ab_call(c64bd, s64bd, wbd):
    depth = wbd.shape[0]
    return pl.pallas_call(
        _ab_body,
        grid=(depth,),
        in_specs=[pl.BlockSpec((FOUR_W, FOUR_W), lambda l: (0, 0)),
                  pl.BlockSpec((FOUR_W, FOUR_W), lambda l: (0, 0)),
                  pl.BlockSpec((1, FOUR_W, FOUR_W), lambda l: (l, 0, 0))],
        out_specs=pl.BlockSpec((1, FOUR_W, 2 * FOUR_W), lambda l: (l, 0, 0)),
        out_shape=jax.ShapeDtypeStruct((depth, FOUR_W, 2 * FOUR_W), BF16),
        compiler_params=_cparams(1),
        name="fourier_weights",
    )(c64bd, s64bd, wbd)


def _head_rms(t, m, g):
    ms = jnp.dot((t * t).astype(BF16), m, preferred_element_type=F32)
    return t * lax.rsqrt(ms + EPS) * g


def _rope(t, cos, sin_signed):
    w = t.shape[1]
    lane = lax.broadcasted_iota(I32, t.shape, 1)
    fwd = pltpu.roll(t, w - 16, 1)
    bwd = pltpu.roll(t, 16, 1)
    rot = jnp.where((lane & 16) == 0, fwd, bwd)
    return t * cos + rot * sin_signed


def _in_body(*refs, rope, per_batch):
    if rope:
        (x_ref, mod_ref, g1_ref, w_ref, m_ref, qg_ref, kg_ref, ab_ref, cos_ref, sin_ref,
         pu_ref, q_ref, kv_ref, ua_ref) = refs
    else:
        (x_ref, mod_ref, g1_ref, w_ref, m_ref, qg_ref, kg_ref, ab_ref,
         pu_ref, q_ref, kv_ref, ua_ref) = refs
    row = pl.program_id(0) if per_batch else CTX_ROW
    sh1 = mod_ref[pl.ds(row, 1), pl.ds(0, D)]
    sc1 = mod_ref[pl.ds(row, 1), pl.ds(D, D)]
    x = x_ref[0]
    ms = jnp.mean(x * x, axis=-1, keepdims=True)
    h = (x * lax.rsqrt(ms + EPS) * g1_ref[...]) * (1.0 + sc1) + sh1
    p = jnp.dot(h.astype(BF16), w_ref[...], preferred_element_type=F32)
    pu = p[:, 0:256]
    q = p[:, 256:768]
    k = p[:, 768:896]
    v = p[:, 896:1024]
    fu = p[:, 1024:1280]
    m = m_ref[...]
    q = _head_rms(q, m, qg_ref[...])
    k = _head_rms(k, m[0:KV_W, 0:KV_W], kg_ref[...])
    if rope:
        cos = cos_ref[...]
        sin = sin_ref[...]
        q = _rope(q, jnp.concatenate([cos] * 4, axis=1), jnp.concatenate([sin] * 4, axis=1))
        k = _rope(k, cos, sin)
    q = q * (HEAD_DIM ** -0.5 * LOG2_E)
    pu_ref[0] = pu.astype(BF16)
    q_ref[0] = q.astype(BF16)
    kv_ref[0] = jnp.concatenate([k, pltpu.roll(k, 64, 1), v, pltpu.roll(v, 64, 1)], axis=1).astype(BF16)
    ua_ref[0] = jnp.dot(fu.astype(BF16), ab_ref[...], preferred_element_type=F32).astype(BF16)


def _in_call(x3, mod, g1, w_in, m512, qg, kg, ab, cos2, sin2, *, rope, per_batch):
    nb, seq, _ = x3.shape
    tm = min(TM_IN, seq)
    full = lambda shape: pl.BlockSpec(shape, lambda b, i: (0,) * len(shape))
    in_specs = [pl.BlockSpec((1, tm, D), lambda b, i: (b, i, 0)),
                full((8, 6 * D)), full((1, D)), full((D, IN_W)), full((ATTN_W, ATTN_W)),
                full((1, ATTN_W)), full((1, KV_W)), full((FOUR_W, 2 * FOUR_W))]
    args = [x3, mod, g1, w_in, m512, qg, kg, ab]
    if rope:
        in_specs += [pl.BlockSpec((tm, 128), lambda b, i: (i, 0)),
                     pl.BlockSpec((tm, 128), lambda b, i: (i, 0))]
        args += [cos2, sin2]
    widths = (POOL_W, ATTN_W, 4 * KV_W, 2 * FOUR_W)
    return pl.pallas_call(
        functools.partial(_in_body, rope=rope, per_batch=per_batch),
        grid=(nb, seq // tm),
        in_specs=in_specs,
        out_specs=[pl.BlockSpec((1, tm, w), lambda b, i: (b, i, 0)) for w in widths],
        out_shape=[jax.ShapeDtypeStruct((nb, seq, w), BF16) for w in widths],
        compiler_params=_cparams(2),
        name="in_proj_rope" if rope else "in_proj_ctx",
    )(*args)


_NT = (((1,), (1,)), ((), ()))


def _stack_heads(qpair0, qpair1, lo):
    z = jnp.zeros_like(qpair0)
    parts = [jnp.where(lo, qpair0, z), jnp.where(lo, qpair1, z),
             jnp.where(lo, pltpu.roll(qpair0, 64, 1), z), jnp.where(lo, pltpu.roll(qpair1, 64, 1), z)]
    return jnp.concatenate(parts, axis=0).astype(BF16)


def _group_attention(q4, k_parts, va_parts, vb_parts, masks, sink_col):
    s_parts = []
    for kz, mk in zip(k_parts, masks):
        s = lax.dot_general(q4, kz, _NT, preferred_element_type=F32)
        if mk is not None:
            nk = s.shape[1]
            s = jnp.where(mk[None], s.reshape(4, 128, nk), NEG_INF).reshape(512, nk)
        s_parts.append(s)
    m = functools.reduce(jnp.maximum, [jnp.max(s, axis=-1, keepdims=True) for s in s_parts])
    m = jnp.maximum(m, sink_col)
    den = jnp.exp2(sink_col - m)
    oe = oo = None
    for s, va, vb in zip(s_parts, va_parts, vb_parts):
        e = jnp.exp2(s - m)
        den = den + jnp.sum(e, axis=-1, keepdims=True)
        eb = e.astype(BF16)
        pe = jnp.dot(eb[0:256], va, preferred_element_type=F32)
        po = jnp.dot(eb[256:512], vb, preferred_element_type=F32)
        oe = pe if oe is None else oe + pe
        oo = po if oo is None else oo + po
    inv = 1.0 / den
    return oe * inv[0:256], oo * inv[256:512]


def _sink_cols(sink_ref):
    rb = lax.broadcasted_iota(I32, (512, 1), 0) >> 7
    cols = []
    for kvh in range(2):
        s = [sink_ref[4 * kvh + j] * LOG2_E for j in (0, 2, 1, 3)]
        cols.append(jnp.where(rb == 0, s[0], jnp.where(rb == 1, s[1], jnp.where(rb == 2, s[2], s[3]))))
    return cols


def _attend_block(qblk, kv_parts, masks, sink_cols):
    lo = lax.broadcasted_iota(I32, (128, 128), 1) < 64
    cols = []
    for kvh in range(2):
        q4 = _stack_heads(qblk[:, 256 * kvh:256 * kvh + 128], qblk[:, 256 * kvh + 128:256 * kvh + 256], lo)
        ko = 128 * kvh
        vao = 256 + 128 * kvh
        vbo = 384 - 128 * kvh
        oe, oo = _group_attention(q4, [kv[:, ko:ko + 128] for kv in kv_parts],
                                  [kv[:, vao:vao + 128] for kv in kv_parts],
                                  [kv[:, vbo:vbo + 128] for kv in kv_parts], masks, sink_cols[kvh])
        cols.append(jnp.where(lo, oe[0:128], oo[0:128]))
        cols.append(jnp.where(lo, oe[128:256], oo[128:256]))
    return jnp.concatenate(cols, axis=1).astype(BF16)


def _attn_lat_body(sink_ref, q_ref, kvp_ref, kvm_ref, kvn_ref, kvc_ref, o_ref, kvw_ref, *, tq, seq):
    i = pl.program_id(1)
    kvw_ref[0:128] = kvp_ref[0]
    kvw_ref[128:128 + tq] = kvm_ref[0]
    kvw_ref[128 + tq:256 + tq] = kvn_ref[0]
    kvc = kvc_ref[0]
    sink_cols = _sink_cols(sink_ref)

    def sub(j, carry):
        r0 = pl.multiple_of(j * 128, 128)
        win = kvw_ref[pl.ds(r0, 3 * 128), :]
        ii = lax.broadcasted_iota(I32, (128, 3 * 128), 0)
        cc = lax.broadcasted_iota(I32, (128, 3 * 128), 1)
        base = i * tq + j * 128 - 128
        valid = (ii <= cc) & (cc <= ii + 2 * WINDOW) & (cc >= -base) & (cc < seq - base)
        qblk = q_ref[0, pl.ds(r0, 128), :].astype(F32)
        o_ref[0, pl.ds(r0, 128), :] = _attend_block(qblk, [win, kvc], [valid, None], sink_cols)
        return carry

    for j in range(tq // 128):
        sub(j, 0)


def _attn_lat_call(sink, q, kv, kvc):
    nb, seq, _ = q.shape
    lc = kvc.shape[1]
    tq = TQ
    nblk = seq // 128
    r = tq // 128
    return pl.pallas_call(
        functools.partial(_attn_lat_body, tq=tq, seq=seq),
        grid=(nb, seq // tq),
        in_specs=[pl.BlockSpec(memory_space=pltpu.SMEM),
                  pl.BlockSpec((1, tq, ATTN_W), lambda b, i: (b, i, 0)),
                  pl.BlockSpec((1, 128, 4 * KV_W), lambda b, i: (b, jnp.maximum(i * r - 1, 0), 0)),
                  pl.BlockSpec((1, tq, 4 * KV_W), lambda b, i: (b, i, 0)),
                  pl.BlockSpec((1, 128, 4 * KV_W), lambda b, i: (b, jnp.minimum((i + 1) * r, nblk - 1), 0)),
                  pl.BlockSpec((1, lc, 4 * KV_W), lambda b, i: (b, 0, 0))],
        out_specs=pl.BlockSpec((1, tq, ATTN_W), lambda b, i: (b, i, 0)),
        out_shape=jax.ShapeDtypeStruct((nb, seq, ATTN_W), BF16),
        scratch_shapes=[pltpu.VMEM((tq + 256, 4 * KV_W), BF16)],
        compiler_params=_cparams(2),
        name="attention_window",
    )(sink, q, kv, kv, kv, kvc)


def _attn_ctx_body(sink_ref, q_ref, kvc_ref, o_ref, *, lc):
    kvc = kvc_ref[0]
    sink_cols = _sink_cols(sink_ref)
    for j in range(lc // 128):
        qblk = q_ref[0, j * 128:(j + 1) * 128, :].astype(F32)
        o_ref[0, j * 128:(j + 1) * 128, :] = _attend_block(qblk, [kvc], [None], sink_cols)


def _attn_ctx_call(sink, qc, kvc):
    nb, lc, _ = qc.shape
    return pl.pallas_call(
        functools.partial(_attn_ctx_body, lc=lc),
        grid=(nb,),
        in_specs=[pl.BlockSpec(memory_space=pltpu.SMEM),
                  pl.BlockSpec((1, lc, ATTN_W), lambda b: (b, 0, 0)),
                  pl.BlockSpec((1, lc, 4 * KV_W), lambda b: (b, 0, 0))],
        out_specs=pl.BlockSpec((1, lc, ATTN_W), lambda b: (b, 0, 0)),
        out_shape=jax.ShapeDtypeStruct((nb, lc, ATTN_W), BF16),
        compiler_params=_cparams(1),
        name="attention_ctx",
    )(sink, qc, kvc)


def _four_body(ua_ref, cb_ref, sb_ref, ca_ref, sa_ref, o_ref, *, nb, scale):
    i = pl.program_id(0)
    ca = ca_ref[pl.ds(i, 1), :]
    sa = sa_ref[pl.ds(i, 1), :]
    cb = cb_ref[...]
    sb = sb_ref[...]
    ct = (ca * cb - sa * sb).astype(BF16)
    st = (sa * cb + ca * sb).astype(BF16)
    for b in range(nb):
        ua = ua_ref[b, :, 0:FOUR_W]
        ub = ua_ref[b, :, FOUR_W:2 * FOUR_W]
        r = (jnp.dot(ct, ua, preferred_element_type=F32) - jnp.dot(st, ub, preferred_element_type=F32))
        o_ref[b] = (r * scale).astype(BF16)


def _four_call(uaub, cb, sb, ca, sa):
    nb, seq, _ = uaub.shape
    tm = cb.shape[0]
    one = pl.Buffered(1)
    return pl.pallas_call(
        functools.partial(_four_body, nb=nb, scale=float(seq) ** -0.5),
        grid=(seq // tm,),
        in_specs=[pl.BlockSpec((nb, seq, 2 * FOUR_W), lambda i: (0, 0, 0), pipeline_mode=one),
                  pl.BlockSpec((tm, seq), lambda i: (0, 0), pipeline_mode=one),
                  pl.BlockSpec((tm, seq), lambda i: (0, 0), pipeline_mode=one),
                  pl.BlockSpec((seq // tm, seq), lambda i: (0, 0), pipeline_mode=one),
                  pl.BlockSpec((seq // tm, seq), lambda i: (0, 0), pipeline_mode=one)],
        out_specs=pl.BlockSpec((nb, tm, FOUR_W), lambda i: (0, i, 0)),
        out_shape=jax.ShapeDtypeStruct((nb, seq, FOUR_W), BF16),
        compiler_params=_cparams(1),
        name="fourier_dft",
    )(uaub, cb, sb, ca, sa)


def _dft_tables(seq, tm):
    n = np.arange(seq)[None, :]

    def tab(rows):
        ang = ((rows[:, None] * n) % seq) * (2.0 * np.pi / seq)
        return jnp.asarray(np.cos(ang), F32), jnp.asarray(np.sin(ang), F32)

    cb, sb = tab(np.arange(tm))
    ca, sa = tab(np.arange(seq // tm) * tm)
    return cb, sb, ca, sa


def _four1_body(x_ref, g_ref, yr_ref, yi_ref):
    g = g_ref[...].astype(BF16)
    n = FOUR_R1 * 16
    w = FOUR_W
    for aa in range(x_ref.shape[2]):
        x = x_ref[0, :, aa].reshape(n, 2 * w)
        p = jnp.dot(g, x, preferred_element_type=F32)
        yr = p[0:n, 0:w] - p[n:2 * n, w:2 * w]
        yi = -(p[0:n, w:2 * w] + p[n:2 * n, 0:w])
        yr_ref[0, :, aa] = yr.astype(BF16).reshape(FOUR_R1, 16, w)
        yi_ref[0, :, aa] = yi.astype(BF16).reshape(FOUR_R1, 16, w)


def _four1_call(uaub, g):
    nb, seq, _ = uaub.shape
    r2 = seq // FOUR_R1
    na = r2 // 16
    x = uaub.reshape(nb, FOUR_R1, na, 16, 2 * FOUR_W)
    out = jax.ShapeDtypeStruct((nb, FOUR_R1, na, 16, FOUR_W), BF16)
    ac = FOUR1_A_CHUNK
    return pl.pallas_call(
        _four1_body,
        grid=(nb, na // ac),
        in_specs=[pl.BlockSpec((1, FOUR_R1, ac, 16, 2 * FOUR_W), lambda b, j: (b, 0, j, 0, 0)),
                  pl.BlockSpec(g.shape, lambda b, j: (0, 0))],
        out_specs=[pl.BlockSpec((1, FOUR_R1, ac, 16, FOUR_W), lambda b, j: (b, 0, j, 0, 0))] * 2,
        out_shape=[out, out],
        compiler_params=_cparams(2),
        name="fourier_stage1",
    )(x, g)


def _four2_body(yr_ref, yi_ref, c_ref, s_ref, ca_ref, sa_ref, o_ref, obuf, sem, *, nb, scale):
    k1 = pl.program_id(0)
    nk = pl.num_programs(0)
    ca = ca_ref[pl.ds(k1, 1), :]
    sa = sa_ref[pl.ds(k1, 1), :]
    c = c_ref[...]
    s = s_ref[...]
    gc = (c * ca - s * sa).astype(BF16)
    gs = (s * ca + c * sa).astype(BF16)

    def out_copy(slot, b, kk):
        return pltpu.make_async_copy(obuf.at[slot, b], o_ref.at[b, :, kk, :], sem.at[slot])

    for slot in range(2):
        @pl.when(k1 % 2 == slot)
        def _():
            @pl.when(k1 >= 2)
            def _():
                for b in range(nb):
                    out_copy(slot, b, 0).wait()

            for b in range(nb):
                acc = (jnp.dot(gc, yr_ref[b, 0], preferred_element_type=F32)
                       + jnp.dot(gs, yi_ref[b, 0], preferred_element_type=F32))
                obuf[slot, b] = acc * scale
            for b in range(nb):
                out_copy(slot, b, k1).start()

    @pl.when(k1 == nk - 1)
    def _():
        for slot in range(2):
            for b in range(nb):
                out_copy(slot, b, 0).wait()


def _four2_call(yr, yi, c, s, ca, sa):
    nb = yr.shape[0]
    r2 = c.shape[0]
    seq = FOUR_R1 * r2
    yr4 = yr.reshape(nb, FOUR_R1, r2, FOUR_W)
    yi4 = yi.reshape(nb, FOUR_R1, r2, FOUR_W)
    full = lambda shape: pl.BlockSpec(shape, lambda k: (0,) * len(shape))
    out = pl.pallas_call(
        functools.partial(_four2_body, nb=nb, scale=float(seq) ** -0.5),
        grid=(FOUR_R1,),
        in_specs=[pl.BlockSpec((nb, 1, r2, FOUR_W), lambda k: (0, k, 0, 0)),
                  pl.BlockSpec((nb, 1, r2, FOUR_W), lambda k: (0, k, 0, 0)),
                  full((r2, r2)), full((r2, r2)), full((FOUR_R1, r2)), full((FOUR_R1, r2))],
        out_specs=pl.BlockSpec(memory_space=pl.ANY),
        out_shape=jax.ShapeDtypeStruct((nb, r2, FOUR_R1, FOUR_W), F32),
        scratch_shapes=[pltpu.VMEM((2, nb, r2, FOUR_W), F32), pltpu.SemaphoreType.DMA((2,))],
        compiler_params=_cparams(1),
        name="fourier_stage2",
    )(yr4, yi4, c, s, ca, sa)
    return out.reshape(nb, seq, FOUR_W)


def _four2_tables(seq):
    r1 = FOUR_R1
    r2 = seq // r1
    assert r1 * r2 == seq and r2 % 16 == 0
    k = np.arange(r1)
    ang1 = ((k[:, None] * k[None, :]) % r1) * (2.0 * np.pi / r1)
    eye = np.eye(16)
    g = np.concatenate([np.kron(np.cos(ang1), eye), np.kron(np.sin(ang1), eye)], axis=0)
    m = np.arange(r2)
    ang2 = ((m[:, None] * m[None, :]) % r2) * (2.0 * np.pi / r2)
    alpha = ((k[:, None] * m[None, :]) % seq) * (2.0 * np.pi / seq)
    f = lambda t: jnp.asarray(t, F32)
    return f(g), (f(np.cos(ang2)), f(np.sin(ang2)), f(np.cos(alpha)), f(np.sin(alpha)))


def _route(lt, tri_ref, cnt_ref, tm):
    rowi = lax.broadcasted_iota(I32, (32, tm), 0)
    big = jnp.int32(999)

    def first_argmax(vals):
        mx = jnp.max(vals, axis=0, keepdims=True)
        return mx, jnp.min(jnp.where(vals == mx, rowi, big), axis=0, keepdims=True)

    is_grp = rowi < 4
    mg, gi = first_argmax(jnp.where(is_grp, lt, NEG_INF))
    pg = 1.0 / jnp.sum(jnp.where(is_grp, jnp.exp(jnp.where(is_grp, lt, mg) - mg), 0.0), axis=0, keepdims=True)
    est = 4 + 4 * gi
    le = jnp.where((rowi >= est) & (rowi < est + 4), lt, NEG_INF)
    m1, i1 = first_argmax(le)
    m2, i2 = first_argmax(jnp.where(rowi == i1, NEG_INF, le))
    e2 = jnp.exp(m2 - m1)
    w1 = pg / (1.0 + e2)
    w2 = pg * e2 / (1.0 + e2)
    a1 = i1 - est
    a2 = i2 - est
    code = jnp.minimum(a1, a2) * 4 + jnp.maximum(a1, a2)
    pidx = jnp.where(code == 1, 0, jnp.where(code == 6, 1, jnp.where(code == 2, 2,
           jnp.where(code == 3, 3, jnp.where(code == 7, 4, 5)))))
    slot_a = jnp.where(pidx == 0, 0, jnp.where(pidx <= 2, 2, 3))
    slot_b = jnp.where(pidx <= 1, 1, jnp.where(pidx <= 3, 0, jnp.where(pidx == 4, 1, 2)))
    wa = jnp.where(a1 == slot_a, w1, w2)
    wb = jnp.where(a1 == slot_b, w1, w2)
    bin_ = gi * 6 + pidx

    onehot = rowi == bin_
    pref = jnp.dot(onehot.astype(BF16), tri_ref[...], preferred_element_type=F32)
    carry = cnt_ref[:, 0:1]
    rank = jnp.sum(jnp.where(onehot, pref - 1.0 + carry, 0.0), axis=0, keepdims=True)
    cnt_ref[...] = jnp.broadcast_to(carry + pref[:, tm - 1:tm], cnt_ref.shape)
    return jnp.concatenate([bin_.astype(F32), rank, wa, wb, jnp.zeros((128 - 4, tm), F32)], axis=0)


def _out_body(x_ref, pup_ref, pum_ref, pun_ref, at_ref, fo_ref, wo_ref, pw_ref, ps_ref, band_ref, icnt_ref,
              mod_ref, g2_ref, w2_ref, br_ref, tri_ref, cin_ref,
              xo_ref, rows_ref, cnt_ref, *, tm, per_batch):
    b = pl.program_id(0)
    i = pl.program_id(1)
    nt = pl.num_programs(1)
    row = b if per_batch else CTX_ROW
    g1 = mod_ref[pl.ds(row, 1), pl.ds(2 * D, D)]
    sh2 = mod_ref[pl.ds(row, 1), pl.ds(3 * D, D)]
    sc2 = mod_ref[pl.ds(row, 1), pl.ds(4 * D, D)]

    @pl.when((b == 0) & (i == 0))
    def _():
        cnt_ref[...] = cin_ref[...]

    um = pum_ref[0]
    zh = jnp.zeros((HALO, POOL_W), BF16)
    uext = jnp.concatenate([jnp.where(i > 0, pup_ref[0], zh), um, jnp.where(i < nt - 1, pun_ref[0], zh)], axis=0)
    grp = lax.broadcasted_iota(I32, (POOL_CHUNK, POOL_W), 1) >> 6
    chunks = []
    for c in range(tm // POOL_CHUNK):
        uc = uext[POOL_CHUNK * c:POOL_CHUNK * (c + 1) + 2 * HALO]
        pc = jnp.zeros((POOL_CHUNK, POOL_W), F32)
        for g in range(len(POOL_WINDOWS)):
            pc = jnp.where(grp == g, jnp.dot(band_ref[g], uc, preferred_element_type=F32), pc)
        chunks.append(pc)
    pooled = jnp.concatenate(chunks, axis=0)
    y = pooled * icnt_ref[...] - um.astype(F32)
    pool_out = jnp.dot(y.astype(BF16), pw_ref[...], preferred_element_type=F32) * ps_ref[...]

    cat = jnp.concatenate([pool_out.astype(BF16), at_ref[0], fo_ref[0].astype(BF16)], axis=1)
    xm = x_ref[0] + g1 * jnp.dot(cat, wo_ref[...], preferred_element_type=F32)
    xo_ref[0] = xm

    ms = jnp.mean(xm * xm, axis=-1, keepdims=True)
    h2 = (xm * lax.rsqrt(ms + EPS) * g2_ref[...]) * (1.0 + sc2) + sh2

    hh = h2.astype(BF16)
    hl = (h2 - hh.astype(F32)).astype(BF16)
    w2 = w2_ref[...]
    p2 = jnp.dot(hh, w2, preferred_element_type=F32)
    logits = (p2[:, 0:128] + p2[:, 128:256] + jnp.dot(hl, w2[:, 0:128], preferred_element_type=F32) + br_ref[...])
    meta = _route(logits.T[0:32, :], tri_ref, cnt_ref, tm).T

    rows_ref[:, 0:D] = h2
    rows_ref[:, D:ROW_W] = meta


def _out_call(x3, pu, attn, four, w_out, pool_wbd, pool_scale, bands, icnt, mod, g2, w2, br, tri, cnt_in, *,
              per_batch):
    nb, seq, _ = x3.shape
    tm = min(TM_OUT, seq)
    nt = seq // tm
    hb = tm // HALO
    full = lambda shape: pl.BlockSpec(shape, lambda b, i: (0,) * len(shape))
    in_specs = [pl.BlockSpec((1, tm, D), lambda b, i: (b, i, 0)),
                pl.BlockSpec((1, HALO, POOL_W), lambda b, i: (b, jnp.maximum(i * hb - 1, 0), 0)),
                pl.BlockSpec((1, tm, POOL_W), lambda b, i: (b, i, 0)),
                pl.BlockSpec((1, HALO, POOL_W), lambda b, i: (b, jnp.minimum((i + 1) * hb, seq // HALO - 1), 0)),
                pl.BlockSpec((1, tm, ATTN_W), lambda b, i: (b, i, 0)),
                pl.BlockSpec((1, tm, FOUR_W), lambda b, i: (b, i, 0)),
                full((D, D)), full((POOL_W, POOL_W)), full((1, POOL_W)),
                full((len(POOL_WINDOWS), POOL_CHUNK, POOL_CHUNK + 2 * HALO)),
                pl.BlockSpec((tm, POOL_W), lambda b, i: (i, 0)),
                full((8, 6 * D)), full((1, D)), full((D, 256)), full((1, 128)), full((tm, tm)), full((32, 128))]
    args = [x3, pu, pu, pu, attn, four, w_out, pool_wbd, pool_scale, bands, icnt, mod, g2, w2, br, tri, cnt_in]
    return pl.pallas_call(
        functools.partial(_out_body, tm=tm, per_batch=per_batch),
        grid=(nb, nt),
        in_specs=in_specs,
        out_specs=[pl.BlockSpec((1, tm, D), lambda b, i: (b, i, 0)),
                   pl.BlockSpec((tm, ROW_W), lambda b, i: (b * nt + i, 0)),
                   pl.BlockSpec((32, 128), lambda b, i: (0, 0))],
        out_shape=[jax.ShapeDtypeStruct((nb, seq, D), F32),
                   jax.ShapeDtypeStruct((nb * seq, ROW_W), F32),
                   jax.ShapeDtypeStruct((32, 128), F32)],
        compiler_params=_cparams(2),
        name="out_proj_router" if per_batch else "out_proj_router_ctx",
    )(*args)


def _pool_tables(seq):
    t = np.arange(POOL_CHUNK)[:, None]
    s = np.arange(POOL_CHUNK + 2 * HALO)[None, :] - HALO
    bands = np.stack([(s >= t - w // 2) & (s <= t + w // 2 - 1) for w in POOL_WINDOWS]).astype(np.float32)
    pos = np.arange(seq)
    icnt = np.stack([1.0 / (np.minimum(pos + w // 2 - 1, seq - 1) - np.maximum(pos - w // 2, 0) + 1)
                     for w in POOL_WINDOWS], axis=1)
    return jnp.asarray(bands, BF16), jnp.asarray(np.repeat(icnt, POOL_W // len(POOL_WINDOWS), axis=1), F32)


def _row_copies(tm, make_copy):
    for r in range(tm):
        make_copy(r).start(priority=r % 2)


def _row_waits(tm, make_copy):
    def drain(r, c):
        make_copy(0).wait()
        return c

    lax.fori_loop(0, tm, drain, 0, unroll=8)


def _scatter_body(dest_ref, *refs, tm, n_tiles, n_first):
    n_h = len(refs) - 1 - 3 * SCATTER_SLOTS
    h_refs, xs_ref = refs[:n_h], refs[n_h]
    bufs = refs[n_h + 1:n_h + 1 + SCATTER_SLOTS]
    lsems = refs[n_h + 1 + SCATTER_SLOTS:n_h + 1 + 2 * SCATTER_SLOTS]
    rsems = refs[n_h + 1 + 2 * SCATTER_SLOTS:]

    def load(t, slot):
        def start(h_ref, tt):
            pltpu.make_async_copy(h_ref.at[pl.ds(tt * tm, tm)], bufs[slot], lsems[slot]).start()

        if n_h == 1:
            start(h_refs[0], t)
        else:
            @pl.when(t < n_first)
            def _():
                start(h_refs[0], t)

            @pl.when(t >= n_first)
            def _():
                start(h_refs[1], t - n_first)

    def row_copy(slot, r, d):
        return pltpu.make_async_copy(bufs[slot].at[pl.ds(r, 1)], xs_ref.at[pl.ds(d, 1)], rsems[slot])

    load(0, 0)
    load(1, 1)

    def group(g, c):
        for slot in range(SCATTER_SLOTS):
            t = g * SCATTER_SLOTS + slot
            ahead = (slot + 2) % SCATTER_SLOTS
            pltpu.make_async_copy(h_refs[0].at[pl.ds(0, tm)], bufs[slot], lsems[slot]).wait()

            @pl.when(t >= 2)
            def _():
                _row_waits(tm, lambda r: row_copy(ahead, r, 0))

            @pl.when(t + 2 < n_tiles)
            def _():
                load(t + 2, ahead)

            _row_copies(tm, lambda r: row_copy(slot, r, dest_ref[t * tm + r]))
        return c

    lax.fori_loop(0, n_tiles // SCATTER_SLOTS, group, 0)
    for t in (n_tiles - 2, n_tiles - 1):
        _row_waits(tm, lambda r: row_copy(t % SCATTER_SLOTS, r, 0))


def _scatter_call(dest, row_sets):
    tm = TM_ROW
    n_first = row_sets[0].shape[0] // tm
    n_rows = sum(r.shape[0] for r in row_sets)
    n_tiles = n_rows // tm
    assert n_tiles % SCATTER_SLOTS == 0 and n_tiles >= SCATTER_SLOTS
    return pl.pallas_call(
        functools.partial(_scatter_body, tm=tm, n_tiles=n_tiles, n_first=n_first),
        grid_spec=pltpu.PrefetchScalarGridSpec(
            num_scalar_prefetch=1,
            grid=(1,),
            in_specs=[pl.BlockSpec(memory_space=pl.ANY)] * len(row_sets),
            out_specs=pl.BlockSpec(memory_space=pl.ANY),
            scratch_shapes=([pltpu.VMEM((tm, ROW_W), F32)] * SCATTER_SLOTS
                            + [pltpu.SemaphoreType.DMA(())] * (2 * SCATTER_SLOTS))),
        out_shape=jax.ShapeDtypeStruct((n_rows, ROW_W), F32),
        compiler_params=_cparams(1),
        name="moe_scatter_rows",
    )(dest, *row_sets)


def _moe_body(tile_ref, ea_ref, eb_ref, lo_ref, hi_ref, first_ref, nv_ref,
              xs_ref, wga, wua, wda, wgb, wub, wdb, ys_ref):
    del tile_ref, ea_ref, eb_ref
    w = pl.program_id(0)

    @pl.when(w < nv_ref[0])
    def _():
        h = xs_ref[:, 0:D].astype(BF16)
        meta = xs_ref[:, D:ROW_W]
        rowi = lax.broadcasted_iota(I32, (h.shape[0], 1), 0)
        mine = (rowi >= lo_ref[w]) & (rowi < hi_ref[w])

        def expert(wg, wu, wd, gate):
            g = jnp.dot(h, wg[0, 0].astype(BF16), preferred_element_type=F32)
            u = jnp.dot(h, wu[0, 0].astype(BF16), preferred_element_type=F32)
            a = _silu(g) * u * jnp.where(mine, gate, 0.0)
            return jnp.dot(a.astype(BF16), wd[0, 0].astype(BF16), preferred_element_type=F32)

        y = expert(wga, wua, wda, meta[:, 2:3]) + expert(wgb, wub, wdb, meta[:, 3:4])

        @pl.when(first_ref[w] == 1)
        def _():
            ys_ref[...] = y

        @pl.when(first_ref[w] == 0)
        def _():
            ys_ref[...] += y


def _moe_call(items, xs, w_gate, w_up, w_down, layer):
    tm = TM_MOE
    n_items = items[0].shape[0]
    row = lambda w, tile, *_: (tile[w], 0)
    wa = lambda w, tile, ea, *_: (layer, ea[w], 0, 0)
    wb = lambda w, tile, ea, eb, *_: (layer, eb[w], 0, 0)
    up_spec = lambda f: pl.BlockSpec((1, 1, D, D_EXPERT), f)
    dn_spec = lambda f: pl.BlockSpec((1, 1, D_EXPERT, D), f)
    return pl.pallas_call(
        _moe_body,
        grid_spec=pltpu.PrefetchScalarGridSpec(
            num_scalar_prefetch=len(items),
            grid=(n_items,),
            in_specs=[pl.BlockSpec((tm, ROW_W), row),
                      up_spec(wa), up_spec(wa), dn_spec(wa), up_spec(wb), up_spec(wb), dn_spec(wb)],
            out_specs=pl.BlockSpec((tm, D), row)),
        out_shape=jax.ShapeDtypeStruct((xs.shape[0], D), F32),
        compiler_params=_cparams(1),
        name="moe_experts",
    )(*items, xs, w_gate, w_up, w_down, w_gate, w_up, w_down)


def _gather_body(dest_ref, x_ref, mod_ref, ys_ref, o_ref, ybuf0, ybuf1, sem0, sem1, *,
                 tm, seq, per_batch, dest_off):
    i = pl.program_id(0)
    n_tiles = pl.num_programs(0)
    ybufs = (ybuf0, ybuf1)
    sems = (sem0, sem1)
    row = (i * tm) // seq if per_batch else CTX_ROW
    g2 = mod_ref[pl.ds(row, 1), pl.ds(5 * D, D)]

    def row_copy(slot, r, d):
        return pltpu.make_async_copy(ys_ref.at[pl.ds(d, 1)], ybufs[slot].at[pl.ds(r, 1)], sems[slot])

    def fetch(t, slot):
        base = dest_off + t * tm
        _row_copies(tm, lambda r: row_copy(slot, r, dest_ref[base + r]))

    @pl.when(i == 0)
    def _():
        fetch(0, 0)

    for slot in range(2):
        @pl.when(i % 2 == slot)
        def _():
            @pl.when(i + 1 < n_tiles)
            def _():
                fetch(i + 1, 1 - slot)

            _row_waits(tm, lambda r: row_copy(slot, r, 0))
            o_ref[...] = x_ref[...] + g2 * ybufs[slot][...]


def _gather_call(dest, x2, mod, ys, *, seq, per_batch, dest_off):
    n = x2.shape[0]
    tm = TM_ROW
    return pl.pallas_call(
        functools.partial(_gather_body, tm=tm, seq=seq, per_batch=per_batch, dest_off=dest_off),
        grid_spec=pltpu.PrefetchScalarGridSpec(
            num_scalar_prefetch=1,
            grid=(n // tm,),
            in_specs=[pl.BlockSpec((tm, D), lambda i, *_: (i, 0)),
                      pl.BlockSpec((8, 6 * D), lambda i, *_: (0, 0)),
                      pl.BlockSpec(memory_space=pl.ANY)],
            out_specs=pl.BlockSpec((tm, D), lambda i, *_: (i, 0)),
            scratch_shapes=[pltpu.VMEM((tm, D), F32), pltpu.VMEM((tm, D), F32),
                            pltpu.SemaphoreType.DMA(()), pltpu.SemaphoreType.DMA(())]),
        out_shape=jax.ShapeDtypeStruct((n, D), F32),
        compiler_params=_cparams(1),
        name="moe_gather_rows" if per_batch else "moe_gather_rows_ctx",
    )(dest, x2, mod, ys)


def _routing_tables(bins, rank, cnt, n_rows):
    tm = TM_MOE
    counts = cnt[:N_BINS, 0].astype(I32)
    ends = jnp.cumsum(counts)
    starts = ends - counts
    ids = jnp.arange(N_BINS, dtype=I32)
    pick = lambda key, tab: jnp.sum(jnp.where(key[:, None] == ids[None, :], tab[None, :], 0), axis=1)
    dest = rank + pick(bins, starts)

    tile_first = starts // tm
    tile_last = jnp.maximum(ends - 1, 0) // tm
    n_it = jnp.where(counts > 0, tile_last - tile_first + 1, 0)
    it_end = jnp.cumsum(n_it)
    it_start = it_end - n_it
    n_valid = it_end[-1]
    n_items = n_rows // tm + N_BINS
    w = jnp.minimum(jnp.arange(n_items, dtype=I32), n_valid - 1)
    wbin = jnp.sum((w[:, None] >= it_end[None, :]).astype(I32), axis=1)
    tile = pick(wbin, tile_first) + w - pick(wbin, it_start)
    lo = jnp.clip(pick(wbin, starts) - tile * tm, 0, tm)
    hi = jnp.clip(pick(wbin, ends) - tile * tm, 0, tm)
    first = jnp.concatenate([jnp.ones((1,), I32), (tile[1:] != tile[:-1]).astype(I32)])
    pidx = wbin % 6
    six = jnp.arange(6, dtype=I32)
    slot = lambda tab: jnp.sum(jnp.where(pidx[:, None] == six[None, :], jnp.asarray(tab, I32)[None, :], 0), axis=1)
    ea = 4 * (wbin // 6) + slot(PAIR_SLOT_A)
    eb = 4 * (wbin // 6) + slot(PAIR_SLOT_B)
    return dest, (tile, ea, eb, lo, hi, first, n_valid.reshape(1))


def _block_diag(w):
    g, c, d = w.shape
    eye = jnp.asarray(np.eye(g), w.dtype)
    return (w[:, :, None, :] * eye[:, None, :, None]).reshape(g * c, g * d)


def _rope_tables(n_tokens):
    rows = n_tokens // GRID_W
    r = np.repeat(np.arange(rows), GRID_W).astype(np.float64)
    col = np.tile(np.arange(GRID_W), rows).astype(np.float64)
    half = HEAD_DIM // 2
    inv = 1.0 / (ROPE_BASE ** (np.arange(0, half, 2, dtype=np.float64) / half))
    ar = r[:, None] * inv
    ac = col[:, None] * inv
    ang = np.concatenate([ar, ar, ac, ac], axis=-1)
    sign = np.where((np.arange(HEAD_DIM) & 16) == 0, -1.0, 1.0)
    cos, sin = np.cos(ang), np.sin(ang) * sign
    return jnp.asarray(np.tile(cos, (1, 2)), F32), jnp.asarray(np.tile(sin, (1, 2)), F32)


def kernel(x, c, ctx, c_ctx, w_mod, b_mod, norm1_g, w_in, q_norm_g, k_norm_g, attn_sink, pool_w, pool_scale,
           four_w, w_out, norm2_g, w_grp, b_grp, w_rtr, b_rtr, w_gate, w_up, w_down):
    nb, seq, _ = x.shape
    lc = ctx.shape[1]
    depth = w_mod.shape[0]
    t_lat = nb * seq
    t_ctx = nb * lc

    cs = jnp.concatenate([c, c_ctx[None, :], jnp.zeros((8 - nb - 1, D), F32)], axis=0)
    m512 = jnp.asarray(np.kron(np.eye(N_HEADS), np.full((HEAD_DIM, HEAD_DIM), 1.0 / HEAD_DIM)), BF16)
    kk = np.arange(HEAD_DIM)
    ang64 = 2.0 * np.pi * ((kk[:, None] * kk[None, :]) % HEAD_DIM) / HEAD_DIM
    c64bd = jnp.asarray(np.kron(np.eye(4), np.cos(ang64)), F32)
    s64bd = jnp.asarray(np.kron(np.eye(4), np.sin(ang64)), F32)
    tri_of = lambda n: jnp.asarray(np.triu(np.ones((min(TM_OUT, n),) * 2)), BF16)
    cos2, sin2 = _rope_tables(seq)
    four_cs, four_tabs = _four2_tables(seq)
    tabs_ctx = _dft_tables(lc, min(TM_FOUR, lc))
    pool_lat = _pool_tables(seq)
    pool_ctx = _pool_tables(lc)
    w_in_b = w_in.astype(BF16)
    w_out_b = w_out.astype(BF16)
    wbd = jnp.stack([_block_diag(four_w[l]) for l in range(depth)])
    pool_wbd = jnp.stack([_block_diag(pool_w[l]) for l in range(depth)]).astype(BF16)
    wr = jnp.concatenate([w_grp, w_rtr, jnp.zeros((depth, D, 128 - 4 - N_EXPERTS), F32)], axis=2)
    wr_hi = wr.astype(BF16)
    w2 = jnp.concatenate([wr_hi, (wr - wr_hi.astype(F32)).astype(BF16)], axis=2)
    br = jnp.concatenate([b_grp, b_rtr, jnp.zeros((depth, 128 - 4 - N_EXPERTS), F32)], axis=1)

    mod_all = _mod_call(cs, w_mod, b_mod)
    ab_all = _ab_call(c64bd, s64bd, wbd)

    xc = ctx
    for l in range(depth):
        last = l == depth - 1
        mod = mod_all[l]
        g1 = norm1_g[l][None, :]
        g2 = norm2_g[l][None, :]
        qg = jnp.tile(q_norm_g[l], N_HEADS)[None, :]
        kg = jnp.tile(k_norm_g[l], KV_W // HEAD_DIM)[None, :]
        in_args = (mod, g1, w_in_b[l], m512, qg, kg, ab_all[l])

        puc, qc, kvc, uac = _in_call(xc, *in_args, None, None, rope=False, per_batch=False)
        pul, ql, kvl, ual = _in_call(x, *in_args, cos2, sin2, rope=True, per_batch=True)
        attn_l = _attn_lat_call(attn_sink[l], ql, kvl, kvc)
        four_l = _four2_call(*_four1_call(ual, four_cs), *four_tabs)

        proj = (w_out_b[l], pool_wbd[l], pool_scale[l][None, :])
        rout = (mod, g2, w2[l], br[l][None, :])
        cnt0 = jnp.zeros((32, 128), F32)
        x_mid, rows_l, cnt = _out_call(x, pul, attn_l, four_l, *proj, *pool_lat, *rout, tri_of(seq), cnt0,
                                       per_batch=True)
        row_sets = [rows_l]
        if not last:
            attn_c = _attn_ctx_call(attn_sink[l], qc, kvc)
            four_c = _four_call(uac, *tabs_ctx)
            xc_mid, rows_c, cnt = _out_call(xc, puc, attn_c, four_c, *proj, *pool_ctx, *rout, tri_of(lc), cnt,
                                            per_batch=False)
            row_sets.append(rows_c)

        route = [r[:, D:D + 2].astype(I32) for r in row_sets]
        bins = jnp.concatenate([r[:, 0] for r in route])
        rank = jnp.concatenate([r[:, 1] for r in route])
        dest, items = _routing_tables(bins, rank, cnt, bins.shape[0])
        xs = _scatter_call(dest, row_sets)
        ys = _moe_call(items, xs, w_gate, w_up, w_down, l)
        x = _gather_call(dest, x_mid.reshape(t_lat, D), mod, ys, seq=seq, per_batch=True,
                         dest_off=0).reshape(nb, seq, D)
        if not last:
            xc = _gather_call(dest, xc_mid.reshape(t_ctx, D), mod, ys, seq=lc, per_batch=False,
                              dest_off=t_lat).reshape(nb, lc, D)
    return x
```

```python
import functools

import numpy as np
import jax
import jax.numpy as jnp
from jax import lax
from jax.experimental import pallas as pl
from jax.experimental.pallas import tpu as pltpu

F32 = jnp.float32
BF16 = jnp.bfloat16
I32 = jnp.int32
HI = lax.Precision.HIGHEST

D = 1024
HEAD_DIM = 64
N_HEADS = 8
GRID_W = 64
POOL_WINDOWS = (2, 4, 8, 16)
POOL_W = 256
ATTN_W = 512
KV_W = 128
FOUR_W = 256
IN_W = 1280
N_EXPERTS = 16
D_EXPERT = 512
WINDOW = 128
ROPE_BASE = 10000.0
EPS = 1e-6
NEG_INF = -1e30
LOG2_E = 1.4426950408889634
CTX_ROW = 4
N_BINS = 24
PAIR_SLOT_A = (0, 2, 2, 3, 3, 3)
PAIR_SLOT_B = (1, 1, 0, 0, 1, 2)
META_W = 128
ROW_W = D + META_W
HALO = 16
POOL_CHUNK = 128

VMEM_LIMIT = 56 * 1024 * 1024
TM_IN = 512
TQ = 512
TM_OUT = 512
TM_FOUR = 256
FOUR_R1 = 16
FOUR1_A_CHUNK = 4
TM_MOE = 256
TM_ROW = 256
SCATTER_SLOTS = 4
MOD_TN = 1024


def _cparams(n_axes):
    return pltpu.CompilerParams(dimension_semantics=("arbitrary",) * n_axes,
                                vmem_limit_bytes=VMEM_LIMIT)


def _silu(v):
    return v / (1.0 + jnp.exp(-v))


def _mod_body(cs_ref, w_ref, b_ref, o_ref):
    s = _silu(cs_ref[...])
    w = w_ref[0]
    s_hi = s.astype(BF16)
    s_lo = (s - s_hi.astype(F32)).astype(BF16)
    w_hi = w.astype(BF16)
    w_lo = (w - w_hi.astype(F32)).astype(BF16)
    p = jnp.dot(jnp.concatenate([s_hi, s_lo], axis=0), w_hi, preferred_element_type=F32)
    o_ref[0] = p[0:8] + p[8:16] + jnp.dot(s_hi, w_lo, preferred_element_type=F32) + b_ref[0]


def _mod_call(cs, w_mod, b_mod):
    depth = w_mod.shape[0]
    return pl.pallas_call(
        _mod_body,
        grid=(depth, 6 * D // MOD_TN),
        in_specs=[pl.BlockSpec((8, D), lambda l, j: (0, 0)),
                  pl.BlockSpec((1, D, MOD_TN), lambda l, j: (l, 0, j)),
                  pl.BlockSpec((1, 1, MOD_TN), lambda l, j: (l, 0, j))],
        out_specs=pl.BlockSpec((1, 8, MOD_TN), lambda l, j: (l, 0, j)),
        out_shape=jax.ShapeDtypeStruct((depth, 8, 6 * D), F32),
        compiler_params=_cparams(2),
        name="modulation",
    )(cs, w_mod, b_mod.reshape(depth, 1, 6 * D))


def _ab_body(c_ref, s_ref, w_ref, o_ref):
    w = w_ref[0]
    ca = jnp.dot(c_ref[...], w, preferred_element_type=F32, precision=HI)
    sa = jnp.dot(s_ref[...], w, preferred_element_type=F32, precision=HI)
    o_ref[0] = (jnp.concatenate([ca, sa], axis=1) * (HEAD_DIM ** -0.5)).astype(BF16)


def _ab_call(c64bd, s64bd, wbd):
    depth = wbd.shape[0]
    return pl.pallas_call(
        _ab_body,
        grid=(depth,),
        in_specs=[pl.BlockSpec((FOUR_W, FOUR_W), lambda l: (0, 0)),
                  pl.BlockSpec((FOUR_W, FOUR_W), lambda l: (0, 0)),
                  pl.BlockSpec((1, FOUR_W, FOUR_W), lambda l: (l, 0, 0))],
        out_specs=pl.BlockSpec((1, FOUR_W, 2 * FOUR_W), lambda l: (l, 0, 0)),
        out_shape=jax.ShapeDtypeStruct((depth, FOUR_W, 2 * FOUR_W), BF16),
        compiler_params=_cparams(1),
        name="fourier_weights",
    )(c64bd, s64bd, wbd)


def _head_rms(t, m, g):
    ms = jnp.dot((t * t).astype(BF16), m, preferred_element_type=F32)
    return t * lax.rsqrt(ms + EPS) * g


def _rope(t, cos, sin_signed):
    w = t.shape[1]
    lane = lax.broadcasted_iota(I32, t.shape, 1)
    fwd = pltpu.roll(t, w - 16, 1)
    bwd = pltpu.roll(t, 16, 1)
    rot = jnp.where((lane & 16) == 0, fwd, bwd)
    return t * cos + rot * sin_signed


def _in_body(*refs, rope, per_batch):
    if rope:
        (x_ref, mod_ref, g1_ref, w_ref, m_ref, qg_ref, kg_ref, ab_ref, cos_ref, sin_ref,
         pu_ref, q_ref, kv_ref, ua_ref) = refs
    else:
        (x_ref, mod_ref, g1_ref, w_ref, m_ref, qg_ref, kg_ref, ab_ref,
         pu_ref, q_ref, kv_ref, ua_ref) = refs
    row = pl.program_id(0) if per_batch else CTX_ROW
    sh1 = mod_ref[pl.ds(row, 1), pl.ds(0, D)]
    sc1 = mod_ref[pl.ds(row, 1), pl.ds(D, D)]
    x = x_ref[0]
    ms = jnp.mean(x * x, axis=-1, keepdims=True)
    h = (x * lax.rsqrt(ms + EPS) * g1_ref[...]) * (1.0 + sc1) + sh1
    p = jnp.dot(h.astype(BF16), w_ref[...], preferred_element_type=F32)
    pu = p[:, 0:256]
    q = p[:, 256:768]
    k = p[:, 768:896]
    v = p[:, 896:1024]
    fu = p[:, 1024:1280]
    m = m_ref[...]
    q = _head_rms(q, m, qg_ref[...])
    k = _head_rms(k, m[0:KV_W, 0:KV_W], kg_ref[...])
    if rope:
        cos = cos_ref[...]
        sin = sin_ref[...]
        q = _rope(q, jnp.concatenate([cos] * 4, axis=1), jnp.concatenate([sin] * 4, axis=1))
        k = _rope(k, cos, sin)
    q = q * (HEAD_DIM ** -0.5 * LOG2_E)
    pu_ref[0] = pu.astype(BF16)
    q_ref[0] = q.astype(BF16)
    kv_ref[0] = jnp.concatenate([k, pltpu.roll(k, 64, 1), v, pltpu.roll(v, 64, 1)], axis=1).astype(BF16)
    ua_ref[0] = jnp.dot(fu.astype(BF16), ab_ref[...], preferred_element_type=F32).astype(BF16)


def _in_call(x3, mod, g1, w_in, m512, qg, kg, ab, cos2, sin2, *, rope, per_batch):
    nb, seq, _ = x3.shape
    tm = min(TM_IN, seq)
    full = lambda shape: pl.BlockSpec(shape, lambda b, i: (0,) * len(shape))
    in_specs = [pl.BlockSpec((1, tm, D), lambda b, i: (b, i, 0)),
                full((8, 6 * D)), full((1, D)), full((D, IN_W)), full((ATTN_W, ATTN_W)),
                full((1, ATTN_W)), full((1, KV_W)), full((FOUR_W, 2 * FOUR_W))]
    args = [x3, mod, g1, w_in, m512, qg, kg, ab]
    if rope:
        in_specs += [pl.BlockSpec((tm, 128), lambda b, i: (i, 0)),
                     pl.BlockSpec((tm, 128), lambda b, i: (i, 0))]
        args += [cos2, sin2]
    widths = (POOL_W, ATTN_W, 4 * KV_W, 2 * FOUR_W)
    return pl.pallas_call(
        functools.partial(_in_body, rope=rope, per_batch=per_batch),
        grid=(nb, seq // tm),
        in_specs=in_specs,
        out_specs=[pl.BlockSpec((1, tm, w), lambda b, i: (b, i, 0)) for w in widths],
        out_shape=[jax.ShapeDtypeStruct((nb, seq, w), BF16) for w in widths],
        compiler_params=_cparams(2),
        name="in_proj_rope" if rope else "in_proj_ctx",
    )(*args)


_NT = (((1,), (1,)), ((), ()))


def _stack_heads(qpair0, qpair1, lo):
    z = jnp.zeros_like(qpair0)
    parts = [jnp.where(lo, qpair0, z), jnp.where(lo, qpair1, z),
             jnp.where(lo, pltpu.roll(qpair0, 64, 1), z), jnp.where(lo, pltpu.roll(qpair1, 64, 1), z)]
    return jnp.concatenate(parts, axis=0).astype(BF16)


def _group_attention(q4, k_parts, va_parts, vb_parts, masks, sink_col):
    s_parts = []
    for kz, mk in zip(k_parts, masks):
        s = lax.dot_general(q4, kz, _NT, preferred_element_type=F32)
        if mk is not None:
            nk = s.shape[1]
            s = jnp.where(mk[None], s.reshape(4, 128, nk), NEG_INF).reshape(512, nk)
        s_parts.append(s)
    m = functools.reduce(jnp.maximum, [jnp.max(s, axis=-1, keepdims=True) for s in s_parts])
    m = jnp.maximum(m, sink_col)
    den = jnp.exp2(sink_col - m)
    oe = oo = None
    for s, va, vb in zip(s_parts, va_parts, vb_parts):
        e = jnp.exp2(s - m)
        den = den + jnp.sum(e, axis=-1, keepdims=True)
        eb = e.astype(BF16)
        pe = jnp.dot(eb[0:256], va, preferred_element_type=F32)
        po = jnp.dot(eb[256:512], vb, preferred_element_type=F32)
        oe = pe if oe is None else oe + pe
        oo = po if oo is None else oo + po
    inv = 1.0 / den
    return oe * inv[0:256], oo * inv[256:512]


def _sink_cols(sink_ref):
    rb = lax.broadcasted_iota(I32, (512, 1), 0) >> 7
    cols = []
    for kvh in range(2):
        s = [sink_ref[4 * kvh + j] * LOG2_E for j in (0, 2, 1, 3)]
        cols.append(jnp.where(rb == 0, s[0], jnp.where(rb == 1, s[1], jnp.where(rb == 2, s[2], s[3]))))
    return cols


def _attend_block(qblk, kv_parts, masks, sink_cols):
    lo = lax.broadcasted_iota(I32, (128, 128), 1) < 64
    cols = []
    for kvh in range(2):
        q4 = _stack_heads(qblk[:, 256 * kvh:256 * kvh + 128], qblk[:, 256 * kvh + 128:256 * kvh + 256], lo)
        ko = 128 * kvh
        vao = 256 + 128 * kvh
        vbo = 384 - 128 * kvh
        oe, oo = _group_attention(q4, [kv[:, ko:ko + 128] for kv in kv_parts],
                                  [kv[:, vao:vao + 128] for kv in kv_parts],
                                  [kv[:, vbo:vbo + 128] for kv in kv_parts], masks, sink_cols[kvh])
        cols.append(jnp.where(lo, oe[0:128], oo[0:128]))
        cols.append(jnp.where(lo, oe[128:256], oo[128:256]))
    return jnp.concatenate(cols, axis=1).astype(BF16)


def _attn_lat_body(sink_ref, q_ref, kvp_ref, kvm_ref, kvn_ref, kvc_ref, o_ref, kvw_ref, *, tq, seq):
    i = pl.program_id(1)
    kvw_ref[0:128] = kvp_ref[0]
    kvw_ref[128:128 + tq] = kvm_ref[0]
    kvw_ref[128 + tq:256 + tq] = kvn_ref[0]
    kvc = kvc_ref[0]
    sink_cols = _sink_cols(sink_ref)

    def sub(j, carry):
        r0 = pl.multiple_of(j * 128, 128)
        win = kvw_ref[pl.ds(r0, 3 * 128), :]
        ii = lax.broadcasted_iota(I32, (128, 3 * 128), 0)
        cc = lax.broadcasted_iota(I32, (128, 3 * 128), 1)
        base = i * tq + j * 128 - 128
        valid = (ii <= cc) & (cc <= ii + 2 * WINDOW) & (cc >= -base) & (cc < seq - base)
        qblk = q_ref[0, pl.ds(r0, 128), :].astype(F32)
        o_ref[0, pl.ds(r0, 128), :] = _attend_block(qblk, [win, kvc], [valid, None], sink_cols)
        return carry

    for j in range(tq // 128):
        sub(j, 0)


def _attn_lat_call(sink, q, kv, kvc):
    nb, seq, _ = q.shape
    lc = kvc.shape[1]
    tq = TQ
    nblk = seq // 128
    r = tq // 128
    return pl.pallas_call(
        functools.partial(_attn_lat_body, tq=tq, seq=seq),
        grid=(nb, seq // tq),
        in_specs=[pl.BlockSpec(memory_space=pltpu.SMEM),
                  pl.BlockSpec((1, tq, ATTN_W), lambda b, i: (b, i, 0)),
                  pl.BlockSpec((1, 128, 4 * KV_W), lambda b, i: (b, jnp.maximum(i * r - 1, 0), 0)),
                  pl.BlockSpec((1, tq, 4 * KV_W), lambda b, i: (b, i, 0)),
                  pl.BlockSpec((1, 128, 4 * KV_W), lambda b, i: (b, jnp.minimum((i + 1) * r, nblk - 1), 0)),
                  pl.BlockSpec((1, lc, 4 * KV_W), lambda b, i: (b, 0, 0))],
        out_specs=pl.BlockSpec((1, tq, ATTN_W), lambda b, i: (b, i, 0)),
        out_shape=jax.ShapeDtypeStruct((nb, seq, ATTN_W), BF16),
        scratch_shapes=[pltpu.VMEM((tq + 256, 4 * KV_W), BF16)],
        compiler_params=_cparams(2),
        name="attention_window",
    )(sink, q, kv, kv, kv, kvc)


def _attn_ctx_body(sink_ref, q_ref, kvc_ref, o_ref, *, lc):
    kvc = kvc_ref[0]
    sink_cols = _sink_cols(sink_ref)
    for j in range(lc // 128):
        qblk = q_ref[0, j * 128:(j + 1) * 128, :].astype(F32)
        o_ref[0, j * 128:(j + 1) * 128, :] = _attend_block(qblk, [kvc], [None], sink_cols)


def _attn_ctx_call(sink, qc, kvc):
    nb, lc, _ = qc.shape
    return pl.pallas_call(
        functools.partial(_attn_ctx_body, lc=lc),
        grid=(nb,),
        in_specs=[pl.BlockSpec(memory_space=pltpu.SMEM),
                  pl.BlockSpec((1, lc, ATTN_W), lambda b: (b, 0, 0)),
                  pl.BlockSpec((1, lc, 4 * KV_W), lambda b: (b, 0, 0))],
        out_specs=pl.BlockSpec((1, lc, ATTN_W), lambda b: (b, 0, 0)),
        out_shape=jax.ShapeDtypeStruct((nb, lc, ATTN_W), BF16),
        compiler_params=_cparams(1),
        name="attention_ctx",
    )(sink, qc, kvc)


def _four_body(ua_ref, cb_ref, sb_ref, ca_ref, sa_ref, o_ref, *, nb, scale):
    i = pl.program_id(0)
    ca = ca_ref[pl.ds(i, 1), :]
    sa = sa_ref[pl.ds(i, 1), :]
    cb = cb_ref[...]
    sb = sb_ref[...]
    ct = (ca * cb - sa * sb).astype(BF16)
    st = (sa * cb + ca * sb).astype(BF16)
    for b in range(nb):
        ua = ua_ref[b, :, 0:FOUR_W]
        ub = ua_ref[b, :, FOUR_W:2 * FOUR_W]
        r = (jnp.dot(ct, ua, preferred_element_type=F32) - jnp.dot(st, ub, preferred_element_type=F32))
        o_ref[b] = (r * scale).astype(BF16)


def _four_call(uaub, cb, sb, ca, sa):
    nb, seq, _ = uaub.shape
    tm = cb.shape[0]
    one = pl.Buffered(1)
    return pl.pallas_call(
        functools.partial(_four_body, nb=nb, scale=float(seq) ** -0.5),
        grid=(seq // tm,),
        in_specs=[pl.BlockSpec((nb, seq, 2 * FOUR_W), lambda i: (0, 0, 0), pipeline_mode=one),
                  pl.BlockSpec((tm, seq), lambda i: (0, 0), pipeline_mode=one),
                  pl.BlockSpec((tm, seq), lambda i: (0, 0), pipeline_mode=one),
                  pl.BlockSpec((seq // tm, seq), lambda i: (0, 0), pipeline_mode=one),
                  pl.BlockSpec((seq // tm, seq), lambda i: (0, 0), pipeline_mode=one)],
        out_specs=pl.BlockSpec((nb, tm, FOUR_W), lambda i: (0, i, 0)),
        out_shape=jax.ShapeDtypeStruct((nb, seq, FOUR_W), BF16),
        compiler_params=_cparams(1),
        name="fourier_dft",
    )(uaub, cb, sb, ca, sa)


def _dft_tables(seq, tm):
    n = np.arange(seq)[None, :]

    def tab(rows):
        ang = ((rows[:, None] * n) % seq) * (2.0 * np.pi / seq)
        return jnp.asarray(np.cos(ang), F32), jnp.asarray(np.sin(ang), F32)

    cb, sb = tab(np.arange(tm))
    ca, sa = tab(np.arange(seq // tm) * tm)
    return cb, sb, ca, sa


def _four1_body(x_ref, g_ref, yr_ref, yi_ref):
    g = g_ref[...].astype(BF16)
    n = FOUR_R1 * 16
    w = FOUR_W
    for aa in range(x_ref.shape[2]):
        x = x_ref[0, :, aa].reshape(n, 2 * w)
        p = jnp.dot(g, x, preferred_element_type=F32)
        yr = p[0:n, 0:w] - p[n:2 * n, w:2 * w]
        yi = -(p[0:n, w:2 * w] + p[n:2 * n, 0:w])
        yr_ref[0, :, aa] = yr.astype(BF16).reshape(FOUR_R1, 16, w)
        yi_ref[0, :, aa] = yi.astype(BF16).reshape(FOUR_R1, 16, w)


def _four1_call(uaub, g):
    nb, seq, _ = uaub.shape
    r2 = seq // FOUR_R1
    na = r2 // 16
    x = uaub.reshape(nb, FOUR_R1, na, 16, 2 * FOUR_W)
    out = jax.ShapeDtypeStruct((nb, FOUR_R1, na, 16, FOUR_W), BF16)
    ac = FOUR1_A_CHUNK
    return pl.pallas_call(
        _four1_body,
        grid=(nb, na // ac),
        in_specs=[pl.BlockSpec((1, FOUR_R1, ac, 16, 2 * FOUR_W), lambda b, j: (b, 0, j, 0, 0)),
                  pl.BlockSpec(g.shape, lambda b, j: (0, 0))],
        out_specs=[pl.BlockSpec((1, FOUR_R1, ac, 16, FOUR_W), lambda b, j: (b, 0, j, 0, 0))] * 2,
        out_shape=[out, out],
        compiler_params=_cparams(2),
        name="fourier_stage1",
    )(x, g)


def _four2_body(yr_ref, yi_ref, c_ref, s_ref, ca_ref, sa_ref, o_ref, obuf, sem, *, nb, scale):
    k1 = pl.program_id(0)
    nk = pl.num_programs(0)
    ca = ca_ref[pl.ds(k1, 1), :]
    sa = sa_ref[pl.ds(k1, 1), :]
    c = c_ref[...]
    s = s_ref[...]
    gc = (c * ca - s * sa).astype(BF16)
    gs = (s * ca + c * sa).astype(BF16)

    def out_copy(slot, b, kk):
        return pltpu.make_async_copy(obuf.at[slot, b], o_ref.at[b, :, kk, :], sem.at[slot])

    for slot in range(2):
        @pl.when(k1 % 2 == slot)
        def _():
            @pl.when(k1 >= 2)
            def _():
                for b in range(nb):
                    out_copy(slot, b, 0).wait()

            for b in range(nb):
                acc = (jnp.dot(gc, yr_ref[b, 0], preferred_element_type=F32)
                       + jnp.dot(gs, yi_ref[b, 0], preferred_element_type=F32))
                obuf[slot, b] = acc * scale
            for b in range(nb):
                out_copy(slot, b, k1).start()

    @pl.when(k1 == nk - 1)
    def _():
        for slot in range(2):
            for b in range(nb):
                out_copy(slot, b, 0).wait()


def _four2_call(yr, yi, c, s, ca, sa):
    nb = yr.shape[0]
    r2 = c.shape[0]
    seq = FOUR_R1 * r2
    yr4 = yr.reshape(nb, FOUR_R1, r2, FOUR_W)
    yi4 = yi.reshape(nb, FOUR_R1, r2, FOUR_W)
    full = lambda shape: pl.BlockSpec(shape, lambda k: (0,) * len(shape))
    out = pl.pallas_call(
        functools.partial(_four2_body, nb=nb, scale=float(seq) ** -0.5),
        grid=(FOUR_R1,),
        in_specs=[pl.BlockSpec((nb, 1, r2, FOUR_W), lambda k: (0, k, 0, 0)),
                  pl.BlockSpec((nb, 1, r2, FOUR_W), lambda k: (0, k, 0, 0)),
                  full((r2, r2)), full((r2, r2)), full((FOUR_R1, r2)), full((FOUR_R1, r2))],
        out_specs=pl.BlockSpec(memory_space=pl.ANY),
        out_shape=jax.ShapeDtypeStruct((nb, r2, FOUR_R1, FOUR_W), F32),
        scratch_shapes=[pltpu.VMEM((2, nb, r2, FOUR_W), F32), pltpu.SemaphoreType.DMA((2,))],
        compiler_params=_cparams(1),
        name="fourier_stage2",
    )(yr4, yi4, c, s, ca, sa)
    return out.reshape(nb, seq, FOUR_W)


def _four2_tables(seq):
    r1 = FOUR_R1
    r2 = seq // r1
    assert r1 * r2 == seq and r2 % 16 == 0
    k = np.arange(r1)
    ang1 = ((k[:, None] * k[None, :]) % r1) * (2.0 * np.pi / r1)
    eye = np.eye(16)
    g = np.concatenate([np.kron(np.cos(ang1), eye), np.kron(np.sin(ang1), eye)], axis=0)
    m = np.arange(r2)
    ang2 = ((m[:, None] * m[None, :]) % r2) * (2.0 * np.pi / r2)
    alpha = ((k[:, None] * m[None, :]) % seq) * (2.0 * np.pi / seq)
    f = lambda t: jnp.asarray(t, F32)
    return f(g), (f(np.cos(ang2)), f(np.sin(ang2)), f(np.cos(alpha)), f(np.sin(alpha)))


def _route(lt, tri_ref, cnt_ref, tm):
    rowi = lax.broadcasted_iota(I32, (32, tm), 0)
    big = jnp.int32(999)

    def first_argmax(vals):
        mx = jnp.max(vals, axis=0, keepdims=True)
        return mx, jnp.min(jnp.where(vals == mx, rowi, big), axis=0, keepdims=True)

    is_grp = rowi < 4
    mg, gi = first_argmax(jnp.where(is_grp, lt, NEG_INF))
    pg = 1.0 / jnp.sum(jnp.where(is_grp, jnp.exp(jnp.where(is_grp, lt, mg) - mg), 0.0), axis=0, keepdims=True)
    est = 4 + 4 * gi
    le = jnp.where((rowi >= est) & (rowi < est + 4), lt, NEG_INF)
    m1, i1 = first_argmax(le)
    m2, i2 = first_argmax(jnp.where(rowi == i1, NEG_INF, le))
    e2 = jnp.exp(m2 - m1)
    w1 = pg / (1.0 + e2)
    w2 = pg * e2 / (1.0 + e2)
    a1 = i1 - est
    a2 = i2 - est
    code = jnp.minimum(a1, a2) * 4 + jnp.maximum(a1, a2)
    pidx = jnp.where(code == 1, 0, jnp.where(code == 6, 1, jnp.where(code == 2, 2,
           jnp.where(code == 3, 3, jnp.where(code == 7, 4, 5)))))
    slot_a = jnp.where(pidx == 0, 0, jnp.where(pidx <= 2, 2, 3))
    slot_b = jnp.where(pidx <= 1, 1, jnp.where(pidx <= 3, 0, jnp.where(pidx == 4, 1, 2)))
    wa = jnp.where(a1 == slot_a, w1, w2)
    wb = jnp.where(a1 == slot_b, w1, w2)
    bin_ = gi * 6 + pidx

    onehot = rowi == bin_
    pref = jnp.dot(onehot.astype(BF16), tri_ref[...], preferred_element_type=F32)
    carry = cnt_ref[:, 0:1]
    rank = jnp.sum(jnp.where(onehot, pref - 1.0 + carry, 0.0), axis=0, keepdims=True)
    cnt_ref[...] = jnp.broadcast_to(carry + pref[:, tm - 1:tm], cnt_ref.shape)
    return jnp.concatenate([bin_.astype(F32), rank, wa, wb, jnp.zeros((128 - 4, tm), F32)], axis=0)


def _out_body(x_ref, pup_ref, pum_ref, pun_ref, at_ref, fo_ref, wo_ref, pw_ref, ps_ref, band_ref, icnt_ref,
              mod_ref, g2_ref, w2_ref, br_ref, tri_ref, cin_ref,
              xo_ref, rows_ref, cnt_ref, *, tm, per_batch):
    b = pl.program_id(0)
    i = pl.program_id(1)
    nt = pl.num_programs(1)
    row = b if per_batch else CTX_ROW
    g1 = mod_ref[pl.ds(row, 1), pl.ds(2 * D, D)]
    sh2 = mod_ref[pl.ds(row, 1), pl.ds(3 * D, D)]
    sc2 = mod_ref[pl.ds(row, 1), pl.ds(4 * D, D)]

    @pl.when((b == 0) & (i == 0))
    def _():
        cnt_ref[...] = cin_ref[...]

    um = pum_ref[0]
    zh = jnp.zeros((HALO, POOL_W), BF16)
    uext = jnp.concatenate([jnp.where(i > 0, pup_ref[0], zh), um, jnp.where(i < nt - 1, pun_ref[0], zh)], axis=0)
    grp = lax.broadcasted_iota(I32, (POOL_CHUNK, POOL_W), 1) >> 6
    chunks = []
    for c in range(tm // POOL_CHUNK):
        uc = uext[POOL_CHUNK * c:POOL_CHUNK * (c + 1) + 2 * HALO]
        pc = jnp.zeros((POOL_CHUNK, POOL_W), F32)
        for g in range(len(POOL_WINDOWS)):
            pc = jnp.where(grp == g, jnp.dot(band_ref[g], uc, preferred_element_type=F32), pc)
        chunks.append(pc)
    pooled = jnp.concatenate(chunks, axis=0)
    y = pooled * icnt_ref[...] - um.astype(F32)
    pool_out = jnp.dot(y.astype(BF16), pw_ref[...], preferred_element_type=F32) * ps_ref[...]

    cat = jnp.concatenate([pool_out.astype(BF16), at_ref[0], fo_ref[0].astype(BF16)], axis=1)
    xm = x_ref[0] + g1 * jnp.dot(cat, wo_ref[...], preferred_element_type=F32)
    xo_ref[0] = xm

    ms = jnp.mean(xm * xm, axis=-1, keepdims=True)
    h2 = (xm * lax.rsqrt(ms + EPS) * g2_ref[...]) * (1.0 + sc2) + sh2

    hh = h2.astype(BF16)
    hl = (h2 - hh.astype(F32)).astype(BF16)
    w2 = w2_ref[...]
    p2 = jnp.dot(hh, w2, preferred_element_type=F32)
    logits = (p2[:, 0:128] + p2[:, 128:256] + jnp.dot(hl, w2[:, 0:128], preferred_element_type=F32) + br_ref[...])
    meta = _route(logits.T[0:32, :], tri_ref, cnt_ref, tm).T

    rows_ref[:, 0:D] = h2
    rows_ref[:, D:ROW_W] = meta


def _out_call(x3, pu, attn, four, w_out, pool_wbd, pool_scale, bands, icnt, mod, g2, w2, br, tri, cnt_in, *,
              per_batch):
    nb, seq, _ = x3.shape
    tm = min(TM_OUT, seq)
    nt = seq // tm
    hb = tm // HALO
    full = lambda shape: pl.BlockSpec(shape, lambda b, i: (0,) * len(shape))
    in_specs = [pl.BlockSpec((1, tm, D), lambda b, i: (b, i, 0)),
                pl.BlockSpec((1, HALO, POOL_W), lambda b, i: (b, jnp.maximum(i * hb - 1, 0), 0)),
                pl.BlockSpec((1, tm, POOL_W), lambda b, i: (b, i, 0)),
                pl.BlockSpec((1, HALO, POOL_W), lambda b, i: (b, jnp.minimum((i + 1) * hb, seq // HALO - 1), 0)),
                pl.BlockSpec((1, tm, ATTN_W), lambda b, i: (b, i, 0)),
                pl.BlockSpec((1, tm, FOUR_W), lambda b, i: (b, i, 0)),
                full((D, D)), full((POOL_W, POOL_W)), full((1, POOL_W)),
                full((len(POOL_WINDOWS), POOL_CHUNK, POOL_CHUNK + 2 * HALO)),
                pl.BlockSpec((tm, POOL_W), lambda b, i: (i, 0)),
                full((8, 6 * D)), full((1, D)), full((D, 256)), full((1, 128)), full((tm, tm)), full((32, 128))]
    args = [x3, pu, pu, pu, attn, four, w_out, pool_wbd, pool_scale, bands, icnt, mod, g2, w2, br, tri, cnt_in]
    return pl.pallas_call(
        functools.partial(_out_body, tm=tm, per_batch=per_batch),
        grid=(nb, nt),
        in_specs=in_specs,
        out_specs=[pl.BlockSpec((1, tm, D), lambda b, i: (b, i, 0)),
                   pl.BlockSpec((tm, ROW_W), lambda b, i: (b * nt + i, 0)),
                   pl.BlockSpec((32, 128), lambda b, i: (0, 0))],
        out_shape=[jax.ShapeDtypeStruct((nb, seq, D), F32),
                   jax.ShapeDtypeStruct((nb * seq, ROW_W), F32),
                   jax.ShapeDtypeStruct((32, 128), F32)],
        compiler_params=_cparams(2),
        name="out_proj_router" if per_batch else "out_proj_router_ctx",
    )(*args)


def _pool_tables(seq):
    t = np.arange(POOL_CHUNK)[:, None]
    s = np.arange(POOL_CHUNK + 2 * HALO)[None, :] - HALO
    bands = np.stack([(s >= t - w // 2) & (s <= t + w // 2 - 1) for w in POOL_WINDOWS]).astype(np.float32)
    pos = np.arange(seq)
    icnt = np.stack([1.0 / (np.minimum(pos + w // 2 - 1, seq - 1) - np.maximum(pos - w // 2, 0) + 1)
                     for w in POOL_WINDOWS], axis=1)
    return jnp.asarray(bands, BF16), jnp.asarray(np.repeat(icnt, POOL_W // len(POOL_WINDOWS), axis=1), F32)


def _row_copies(tm, make_copy):
    for r in range(tm):
        make_copy(r).start(priority=r % 2)


def _row_waits(tm, make_copy):
    def drain(r, c):
        make_copy(0).wait()
        return c

    lax.fori_loop(0, tm, drain, 0, unroll=8)


def _zero_fill(ends_ref, nv_ref, xs_ref, zbuf, zsem, tm, n_out, wait):
    def piece(off, size):
        return pltpu.make_async_copy(zbuf.at[pl.ds(0, size)], xs_ref.at[pl.ds(off, size)], zsem)

    def run(cond, off, size):
        @pl.when(cond)
        def _():
            c = piece(off, size)
            c.wait() if wait else c.start()

    for b in range(N_BINS):
        end = ends_ref[b]
        pad = (tm - (end & (tm - 1))) & (tm - 1)
        head = (8 - (end & 7)) & 7
        for j in range(7):
            run(j < head, end + j, 1)
        off = end + head
        rest = pad - head
        for k in range(3, tm.bit_length() - 1):
            run(((rest >> k) & 1) == 1, pl.multiple_of(off, 8), 1 << k)
            off = off + (rest & (1 << k))

    def tail(t, c):
        c_ = piece(t * tm, tm)
        c_.wait() if wait else c_.start()
        return c

    lax.fori_loop(nv_ref[0], n_out, tail, 0)


def _scatter_body(dest_ref, ends_ref, nv_ref, *refs, tm, n_tiles, n_first, n_out):
    n_h = len(refs) - 3 - 3 * SCATTER_SLOTS
    h_refs, xs_ref = refs[:n_h], refs[n_h]
    bufs = refs[n_h + 1:n_h + 1 + SCATTER_SLOTS]
    lsems = refs[n_h + 1 + SCATTER_SLOTS:n_h + 1 + 2 * SCATTER_SLOTS]
    rsems = refs[n_h + 1 + 2 * SCATTER_SLOTS:n_h + 1 + 3 * SCATTER_SLOTS]
    zbuf, zsem = refs[-2:]
    zbuf[...] = jnp.zeros_like(zbuf)
    _zero_fill(ends_ref, nv_ref, xs_ref, zbuf, zsem, tm, n_out, wait=False)

    def load(t, slot):
        def start(h_ref, tt):
            pltpu.make_async_copy(h_ref.at[pl.ds(tt * tm, tm)], bufs[slot], lsems[slot]).start()

        if n_h == 1:
            start(h_refs[0], t)
        else:
            @pl.when(t < n_first)
            def _():
                start(h_refs[0], t)

            @pl.when(t >= n_first)
            def _():
                start(h_refs[1], t - n_first)

    def row_copy(slot, r, d):
        return pltpu.make_async_copy(bufs[slot].at[pl.ds(r, 1)], xs_ref.at[pl.ds(d, 1)], rsems[slot])

    load(0, 0)
    load(1, 1)

    def group(g, c):
        for slot in range(SCATTER_SLOTS):
            t = g * SCATTER_SLOTS + slot
            ahead = (slot + 2) % SCATTER_SLOTS
            pltpu.make_async_copy(h_refs[0].at[pl.ds(0, tm)], bufs[slot], lsems[slot]).wait()

            @pl.when(t >= 2)
            def _():
                _row_waits(tm, lambda r: row_copy(ahead, r, 0))

            @pl.when(t + 2 < n_tiles)
            def _():
                load(t + 2, ahead)

            _row_copies(tm, lambda r: row_copy(slot, r, dest_ref[t * tm + r]))
        return c

    lax.fori_loop(0, n_tiles // SCATTER_SLOTS, group, 0)
    for t in (n_tiles - 2, n_tiles - 1):
        _row_waits(tm, lambda r: row_copy(t % SCATTER_SLOTS, r, 0))
    _zero_fill(ends_ref, nv_ref, xs_ref, zbuf, zsem, tm, n_out, wait=True)


def _scatter_call(dest, bin_ends, n_valid, row_sets):
    tm = TM_ROW
    assert tm == TM_MOE
    n_first = row_sets[0].shape[0] // tm
    n_rows = sum(r.shape[0] for r in row_sets)
    n_tiles = n_rows // tm
    n_out = n_tiles + N_BINS
    assert n_tiles % SCATTER_SLOTS == 0 and n_tiles >= SCATTER_SLOTS
    return pl.pallas_call(
        functools.partial(_scatter_body, tm=tm, n_tiles=n_tiles, n_first=n_first, n_out=n_out),
        grid_spec=pltpu.PrefetchScalarGridSpec(
            num_scalar_prefetch=3,
            grid=(1,),
            in_specs=[pl.BlockSpec(memory_space=pl.ANY)] * len(row_sets),
            out_specs=pl.BlockSpec(memory_space=pl.ANY),
            scratch_shapes=([pltpu.VMEM((tm, ROW_W), F32)] * SCATTER_SLOTS
                            + [pltpu.SemaphoreType.DMA(())] * (2 * SCATTER_SLOTS)
                            + [pltpu.VMEM((tm, ROW_W), F32), pltpu.SemaphoreType.DMA(())])),
        out_shape=jax.ShapeDtypeStruct((n_out * tm, ROW_W), F32),
        compiler_params=_cparams(1),
        name="moe_scatter_rows",
    )(dest, bin_ends, n_valid, *row_sets)


def _moe_body(ea_ref, eb_ref, nv_ref, xs_ref, wga, wua, wda, wgb, wub, wdb, ys_ref):
    del ea_ref, eb_ref
    live = pl.program_id(0) < nv_ref[0]

    @pl.when(jnp.logical_not(live))
    def _():
        ys_ref[...] = jnp.zeros_like(ys_ref)

    @pl.when(live)
    def _():
        h = xs_ref[:, 0:D].astype(BF16)
        meta = xs_ref[:, D:ROW_W]

        def expert(wg, wu, wd, gate):
            g = jnp.dot(h, wg[0, 0].astype(BF16), preferred_element_type=F32)
            u = jnp.dot(h, wu[0, 0].astype(BF16), preferred_element_type=F32)
            a = _silu(g) * u * gate
            return jnp.dot(a.astype(BF16), wd[0, 0].astype(BF16), preferred_element_type=F32)

        ys_ref[...] = expert(wga, wua, wda, meta[:, 2:3]) + expert(wgb, wub, wdb, meta[:, 3:4])


def _moe_call(items, xs, w_gate, w_up, w_down, layer):
    tm = TM_MOE
    n_tiles = xs.shape[0] // tm
    row = lambda s, ea, eb, nv: (jnp.minimum(s, nv[0] - 1), 0)
    wa = lambda s, ea, eb, nv: (layer, ea[s], 0, 0)
    wb = lambda s, ea, eb, nv: (layer, eb[s], 0, 0)
    up_spec = lambda f: pl.BlockSpec((1, 1, D, D_EXPERT), f)
    dn_spec = lambda f: pl.BlockSpec((1, 1, D_EXPERT, D), f)
    return pl.pallas_call(
        _moe_body,
        grid_spec=pltpu.PrefetchScalarGridSpec(
            num_scalar_prefetch=len(items),
            grid=(n_tiles,),
            in_specs=[pl.BlockSpec((tm, ROW_W), row),
                      up_spec(wa), up_spec(wa), dn_spec(wa), up_spec(wb), up_spec(wb), dn_spec(wb)],
            out_specs=pl.BlockSpec((tm, D), lambda s, ea, eb, nv: (s, 0))),
        out_shape=jax.ShapeDtypeStruct((xs.shape[0], D), F32),
        compiler_params=_cparams(1),
        name="moe_experts",
    )(*items, xs, w_gate, w_up, w_down, w_gate, w_up, w_down)


def _gather_body(dest_ref, x_ref, mod_ref, ys_ref, o_ref, ybuf0, ybuf1, sem0, sem1, *,
                 tm, seq, per_batch, dest_off):
    i = pl.program_id(0)
    n_tiles = pl.num_programs(0)
    ybufs = (ybuf0, ybuf1)
    sems = (sem0, sem1)
    row = (i * tm) // seq if per_batch else CTX_ROW
    g2 = mod_ref[pl.ds(row, 1), pl.ds(5 * D, D)]

    def row_copy(slot, r, d):
        return pltpu.make_async_copy(ys_ref.at[pl.ds(d, 1)], ybufs[slot].at[pl.ds(r, 1)], sems[slot])

    def fetch(t, slot):
        base = dest_off + t * tm
        _row_copies(tm, lambda r: row_copy(slot, r, dest_ref[base + r]))

    @pl.when(i == 0)
    def _():
        fetch(0, 0)

    for slot in range(2):
        @pl.when(i % 2 == slot)
        def _():
            @pl.when(i + 1 < n_tiles)
            def _():
                fetch(i + 1, 1 - slot)

            _row_waits(tm, lambda r: row_copy(slot, r, 0))
            o_ref[...] = x_ref[...] + g2 * ybufs[slot][...]


def _gather_call(dest, x2, mod, ys, *, seq, per_batch, dest_off):
    n = x2.shape[0]
    tm = TM_ROW
    return pl.pallas_call(
        functools.partial(_gather_body, tm=tm, seq=seq, per_batch=per_batch, dest_off=dest_off),
        grid_spec=pltpu.PrefetchScalarGridSpec(
            num_scalar_prefetch=1,
            grid=(n // tm,),
            in_specs=[pl.BlockSpec((tm, D), lambda i, *_: (i, 0)),
                      pl.BlockSpec((8, 6 * D), lambda i, *_: (0, 0)),
                      pl.BlockSpec(memory_space=pl.ANY)],
            out_specs=pl.BlockSpec((tm, D), lambda i, *_: (i, 0)),
            scratch_shapes=[pltpu.VMEM((tm, D), F32), pltpu.VMEM((tm, D), F32),
                            pltpu.SemaphoreType.DMA(()), pltpu.SemaphoreType.DMA(())]),
        out_shape=jax.ShapeDtypeStruct((n, D), F32),
        compiler_params=_cparams(1),
        name="moe_gather_rows" if per_batch else "moe_gather_rows_ctx",
    )(dest, x2, mod, ys)


def _routing_tables(bins, rank, cnt, n_rows):
    tm = TM_MOE
    counts = cnt[:N_BINS, 0].astype(I32)
    tiles = (counts + tm - 1) // tm
    tile_end = jnp.cumsum(tiles)
    starts = (tile_end - tiles) * tm
    ids = jnp.arange(N_BINS, dtype=I32)
    pick = lambda key, tab: jnp.sum(jnp.where(key[:, None] == ids[None, :], tab[None, :], 0), axis=1)
    dest = rank + pick(bins, starts)
    n_valid = tile_end[-1]
    n_tiles = n_rows // tm + N_BINS
    s = jnp.minimum(jnp.arange(n_tiles, dtype=I32), n_valid - 1)
    tbin = jnp.sum((s[:, None] >= tile_end[None, :]).astype(I32), axis=1)
    pidx = tbin % 6
    six = jnp.arange(6, dtype=I32)
    slot = lambda tab: jnp.sum(jnp.where(pidx[:, None] == six[None, :], jnp.asarray(tab, I32)[None, :], 0), axis=1)
    ea = 4 * (tbin // 6) + slot(PAIR_SLOT_A)
    eb = 4 * (tbin // 6) + slot(PAIR_SLOT_B)
    return dest, starts + counts, (ea, eb, n_valid.reshape(1))


def _block_diag(w):
    g, c, d = w.shape
    eye = jnp.asarray(np.eye(g), w.dtype)
    return (w[:, :, None, :] * eye[:, None, :, None]).reshape(g * c, g * d)


def _rope_tables(n_tokens):
    rows = n_tokens // GRID_W
    r = np.repeat(np.arange(rows), GRID_W).astype(np.float64)
    col = np.tile(np.arange(GRID_W), rows).astype(np.float64)
    half = HEAD_DIM // 2
    inv = 1.0 / (ROPE_BASE ** (np.arange(0, half, 2, dtype=np.float64) / half))
    ar = r[:, None] * inv
    ac = col[:, None] * inv
    ang = np.concatenate([ar, ar, ac, ac], axis=-1)
    sign = np.where((np.arange(HEAD_DIM) & 16) == 0, -1.0, 1.0)
    cos, sin = np.cos(ang), np.sin(ang) * sign
    return jnp.asarray(np.tile(cos, (1, 2)), F32), jnp.asarray(np.tile(sin, (1, 2)), F32)


def kernel(x, c, ctx, c_ctx, w_mod, b_mod, norm1_g, w_in, q_norm_g, k_norm_g, attn_sink, pool_w, pool_scale,
           four_w, w_out, norm2_g, w_grp, b_grp, w_rtr, b_rtr, w_gate, w_up, w_down):
    nb, seq, _ = x.shape
    lc = ctx.shape[1]
    depth = w_mod.shape[0]
    t_lat = nb * seq
    t_ctx = nb * lc

    cs = jnp.concatenate([c, c_ctx[None, :], jnp.zeros((8 - nb - 1, D), F32)], axis=0)
    m512 = jnp.asarray(np.kron(np.eye(N_HEADS), np.full((HEAD_DIM, HEAD_DIM), 1.0 / HEAD_DIM)), BF16)
    kk = np.arange(HEAD_DIM)
    ang64 = 2.0 * np.pi * ((kk[:, None] * kk[None, :]) % HEAD_DIM) / HEAD_DIM
    c64bd = jnp.asarray(np.kron(np.eye(4), np.cos(ang64)), F32)
    s64bd = jnp.asarray(np.kron(np.eye(4), np.sin(ang64)), F32)
    tri_of = lambda n: jnp.asarray(np.triu(np.ones((min(TM_OUT, n),) * 2)), BF16)
    cos2, sin2 = _rope_tables(seq)
    four_cs, four_tabs = _four2_tables(seq)
    tabs_ctx = _dft_tables(lc, min(TM_FOUR, lc))
    pool_lat = _pool_tables(seq)
    pool_ctx = _pool_tables(lc)
    w_in_b = w_in.astype(BF16)
    w_out_b = w_out.astype(BF16)
    wbd = jnp.stack([_block_diag(four_w[l]) for l in range(depth)])
    pool_wbd = jnp.stack([_block_diag(pool_w[l]) for l in range(depth)]).astype(BF16)
    wr = jnp.concatenate([w_grp, w_rtr, jnp.zeros((depth, D, 128 - 4 - N_EXPERTS), F32)], axis=2)
    wr_hi = wr.astype(BF16)
    w2 = jnp.concatenate([wr_hi, (wr - wr_hi.astype(F32)).astype(BF16)], axis=2)
    br = jnp.concatenate([b_grp, b_rtr, jnp.zeros((depth, 128 - 4 - N_EXPERTS), F32)], axis=1)

    mod_all = _mod_call(cs, w_mod, b_mod)
    ab_all = _ab_call(c64bd, s64bd, wbd)

    xc = ctx
    for l in range(depth):
        last = l == depth - 1
        mod = mod_all[l]
        g1 = norm1_g[l][None, :]
        g2 = norm2_g[l][None, :]
        qg = jnp.tile(q_norm_g[l], N_HEADS)[None, :]
        kg = jnp.tile(k_norm_g[l], KV_W // HEAD_DIM)[None, :]
        in_args = (mod, g1, w_in_b[l], m512, qg, kg, ab_all[l])

        puc, qc, kvc, uac = _in_call(xc, *in_args, None, None, rope=False, per_batch=False)
        pul, ql, kvl, ual = _in_call(x, *in_args, cos2, sin2, rope=True, per_batch=True)
        attn_l = _attn_lat_call(attn_sink[l], ql, kvl, kvc)
        four_l = _four2_call(*_four1_call(ual, four_cs), *four_tabs)

        proj = (w_out_b[l], pool_wbd[l], pool_scale[l][None, :])
        rout = (mod, g2, w2[l], br[l][None, :])
        cnt0 = jnp.zeros((32, 128), F32)
        x_mid, rows_l, cnt = _out_call(x, pul, attn_l, four_l, *proj, *pool_lat, *rout, tri_of(seq), cnt0,
                                       per_batch=True)
        row_sets = [rows_l]
        if not last:
            attn_c = _attn_ctx_call(attn_sink[l], qc, kvc)
            four_c = _four_call(uac, *tabs_ctx)
            xc_mid, rows_c, cnt = _out_call(xc, puc, attn_c, four_c, *proj, *pool_ctx, *rout, tri_of(lc), cnt,
                                            per_batch=False)
            row_sets.append(rows_c)

        route = [r[:, D:D + 2].astype(I32) for r in row_sets]
        bins = jnp.concatenate([r[:, 0] for r in route])
        rank = jnp.concatenate([r[:, 1] for r in route])
        dest, bin_ends, items = _routing_tables(bins, rank, cnt, bins.shape[0])
        xs = _scatter_call(dest, bin_ends, items[2], row_sets)
        ys = _moe_call(items, xs, w_gate, w_up, w_down, l)
        x = _gather_call(dest, x_mid.reshape(t_lat, D), mod, ys, seq=seq, per_batch=True,
                         dest_off=0).reshape(nb, seq, D)
        if not last:
            xc = _gather_call(dest, xc_mid.reshape(t_ctx, D), mod, ys, seq=lc, per_batch=False,
                              dest_off=t_lat).reshape(nb, lc, D)
    return x
```

```python
import functools

import numpy as np
import jax
import jax.numpy as jnp
from jax import lax
from jax.experimental import pallas as pl
from jax.experimental.pallas import tpu as pltpu

F32 = jnp.float32
BF16 = jnp.bfloat16
I32 = jnp.int32
HI = lax.Precision.HIGHEST

D = 1024
HEAD_DIM = 64
N_HEADS = 8
GRID_W = 64
POOL_WINDOWS = (2, 4, 8, 16)
POOL_W = 256
ATTN_W = 512
KV_W = 128
FOUR_W = 256
IN_W = 1280
N_EXPERTS = 16
D_EXPERT = 512
WINDOW = 128
ROPE_BASE = 10000.0
EPS = 1e-6
NEG_INF = -1e30
LOG2_E = 1.4426950408889634
CTX_ROW = 4
N_BINS = 24
PAIR_SLOT_A = (0, 2, 2, 3, 3, 3)
PAIR_SLOT_B = (1, 1, 0, 0, 1, 2)
META_W = 128
ROW_W = D + META_W
ROW_TILES = ROW_W // 128
HALO = 16
POOL_CHUNK = 128

VMEM_LIMIT = 56 * 1024 * 1024
TM_IN = 512
TQ = 512
TM_OUT = 512
TM_FOUR = 256
FOUR_R1 = 16
FOUR1_A_CHUNK = 4
TM_MOE = 256
TM_ROW = 256
SCATTER_SLOTS = 4
MOD_TN = 1024


def _cparams(n_axes):
    return pltpu.CompilerParams(dimension_semantics=("arbitrary",) * n_axes,
                                vmem_limit_bytes=VMEM_LIMIT)


def _silu(v):
    return v / (1.0 + jnp.exp(-v))


def _mod_body(cs_ref, w_ref, b_ref, o_ref):
    s = _silu(cs_ref[...])
    w = w_ref[0]
    s_hi = s.astype(BF16)
    s_lo = (s - s_hi.astype(F32)).astype(BF16)
    w_hi = w.astype(BF16)
    w_lo = (w - w_hi.astype(F32)).astype(BF16)
    p = jnp.dot(jnp.concatenate([s_hi, s_lo], axis=0), w_hi, preferred_element_type=F32)
    o_ref[0] = p[0:8] + p[8:16] + jnp.dot(s_hi, w_lo, preferred_element_type=F32) + b_ref[0]


def _mod_call(cs, w_mod, b_mod):
    depth = w_mod.shape[0]
    return pl.pallas_call(
        _mod_body,
        grid=(depth, 6 * D // MOD_TN),
        in_specs=[pl.BlockSpec((8, D), lambda l, j: (0, 0)),
                  pl.BlockSpec((1, D, MOD_TN), lambda l, j: (l, 0, j)),
                  pl.BlockSpec((1, 1, MOD_TN), lambda l, j: (l, 0, j))],
        out_specs=pl.BlockSpec((1, 8, MOD_TN), lambda l, j: (l, 0, j)),
        out_shape=jax.ShapeDtypeStruct((depth, 8, 6 * D), F32),
        compiler_params=_cparams(2),
        name="modulation",
    )(cs, w_mod, b_mod.reshape(depth, 1, 6 * D))


def _ab_body(c_ref, s_ref, w_ref, o_ref):
    w = w_ref[0]
    ca = jnp.dot(c_ref[...], w, preferred_element_type=F32, precision=HI)
    sa = jnp.dot(s_ref[...], w, preferred_element_type=F32, precision=HI)
    o_ref[0] = (jnp.concatenate([ca, sa], axis=1) * (HEAD_DIM ** -0.5)).astype(BF16)


def _ab_call(c64bd, s64bd, wbd):
    depth = wbd.shape[0]
    return pl.pallas_call(
        _ab_body,
        grid=(depth,),
        in_specs=[pl.BlockSpec((FOUR_W, FOUR_W), lambda l: (0, 0)),
                  pl.BlockSpec((FOUR_W, FOUR_W), lambda l: (0, 0)),
                  pl.BlockSpec((1, FOUR_W, FOUR_W), lambda l: (l, 0, 0))],
        out_specs=pl.BlockSpec((1, FOUR_W, 2 * FOUR_W), lambda l: (l, 0, 0)),
        out_shape=jax.ShapeDtypeStruct((depth, FOUR_W, 2 * FOUR_W), BF16),
        compiler_params=_cparams(1),
        name="fourier_weights",
    )(c64bd, s64bd, wbd)


def _head_rms(t, m, g):
    ms = jnp.dot((t * t).astype(BF16), m, preferred_element_type=F32)
    return t * lax.rsqrt(ms + EPS) * g


def _rope(t, cos, sin_signed):
    w = t.shape[1]
    lane = lax.broadcasted_iota(I32, t.shape, 1)
    fwd = pltpu.roll(t, w - 16, 1)
    bwd = pltpu.roll(t, 16, 1)
    rot = jnp.where((lane & 16) == 0, fwd, bwd)
    return t * cos + rot * sin_signed


def _in_body(*refs, rope, per_batch):
    if rope:
        (x_ref, mod_ref, g1_ref, w_ref, m_ref, qg_ref, kg_ref, ab_ref, cos_ref, sin_ref,
         pu_ref, q_ref, kv_ref, ua_ref) = refs
    else:
        (x_ref, mod_ref, g1_ref, w_ref, m_ref, qg_ref, kg_ref, ab_ref,
         pu_ref, q_ref, kv_ref, ua_ref) = refs
    row = pl.program_id(0) if per_batch else CTX_ROW
    sh1 = mod_ref[pl.ds(row, 1), pl.ds(0, D)]
    sc1 = mod_ref[pl.ds(row, 1), pl.ds(D, D)]
    x = x_ref[0]
    ms = jnp.mean(x * x, axis=-1, keepdims=True)
    h = (x * lax.rsqrt(ms + EPS) * g1_ref[...]) * (1.0 + sc1) + sh1
    p = jnp.dot(h.astype(BF16), w_ref[...], preferred_element_type=F32)
    pu = p[:, 0:256]
    q = p[:, 256:768]
    k = p[:, 768:896]
    v = p[:, 896:1024]
    fu = p[:, 1024:1280]
    m = m_ref[...]
    q = _head_rms(q, m, qg_ref[...])
    k = _head_rms(k, m[0:KV_W, 0:KV_W], kg_ref[...])
    if rope:
        cos = cos_ref[...]
        sin = sin_ref[...]
        q = _rope(q, jnp.concatenate([cos] * 4, axis=1), jnp.concatenate([sin] * 4, axis=1))
        k = _rope(k, cos, sin)
    q = q * (HEAD_DIM ** -0.5 * LOG2_E)
    pu_ref[0] = pu.astype(BF16)
    q_ref[0] = q.astype(BF16)
    kv_ref[0] = jnp.concatenate([k, pltpu.roll(k, 64, 1), v, pltpu.roll(v, 64, 1)], axis=1).astype(BF16)
    ua_ref[0] = jnp.dot(fu.astype(BF16), ab_ref[...], preferred_element_type=F32).astype(BF16)


def _in_call(x3, mod, g1, w_in, m512, qg, kg, ab, cos2, sin2, *, rope, per_batch):
    nb, seq, _ = x3.shape
    tm = min(TM_IN, seq)
    full = lambda shape: pl.BlockSpec(shape, lambda b, i: (0,) * len(shape))
    in_specs = [pl.BlockSpec((1, tm, D), lambda b, i: (b, i, 0)),
                full((8, 6 * D)), full((1, D)), full((D, IN_W)), full((ATTN_W, ATTN_W)),
                full((1, ATTN_W)), full((1, KV_W)), full((FOUR_W, 2 * FOUR_W))]
    args = [x3, mod, g1, w_in, m512, qg, kg, ab]
    if rope:
        in_specs += [pl.BlockSpec((tm, 128), lambda b, i: (i, 0)),
                     pl.BlockSpec((tm, 128), lambda b, i: (i, 0))]
        args += [cos2, sin2]
    widths = (POOL_W, ATTN_W, 4 * KV_W, 2 * FOUR_W)
    return pl.pallas_call(
        functools.partial(_in_body, rope=rope, per_batch=per_batch),
        grid=(nb, seq // tm),
        in_specs=in_specs,
        out_specs=[pl.BlockSpec((1, tm, w), lambda b, i: (b, i, 0)) for w in widths],
        out_shape=[jax.ShapeDtypeStruct((nb, seq, w), BF16) for w in widths],
        compiler_params=_cparams(2),
        name="in_proj_rope" if rope else "in_proj_ctx",
    )(*args)


_NT = (((1,), (1,)), ((), ()))


def _stack_heads(qpair0, qpair1, lo):
    z = jnp.zeros_like(qpair0)
    parts = [jnp.where(lo, qpair0, z), jnp.where(lo, qpair1, z),
             jnp.where(lo, pltpu.roll(qpair0, 64, 1), z), jnp.where(lo, pltpu.roll(qpair1, 64, 1), z)]
    return jnp.concatenate(parts, axis=0).astype(BF16)


def _group_attention(q4, k_parts, va_parts, vb_parts, masks, sink_col):
    s_parts = []
    for kz, mk in zip(k_parts, masks):
        s = lax.dot_general(q4, kz, _NT, preferred_element_type=F32)
        if mk is not None:
            nk = s.shape[1]
            s = jnp.where(mk[None], s.reshape(4, 128, nk), NEG_INF).reshape(512, nk)
        s_parts.append(s)
    m = functools.reduce(jnp.maximum, [jnp.max(s, axis=-1, keepdims=True) for s in s_parts])
    m = jnp.maximum(m, sink_col)
    den = jnp.exp2(sink_col - m)
    oe = oo = None
    for s, va, vb in zip(s_parts, va_parts, vb_parts):
        e = jnp.exp2(s - m)
        den = den + jnp.sum(e, axis=-1, keepdims=True)
        eb = e.astype(BF16)
        pe = jnp.dot(eb[0:256], va, preferred_element_type=F32)
        po = jnp.dot(eb[256:512], vb, preferred_element_type=F32)
        oe = pe if oe is None else oe + pe
        oo = po if oo is None else oo + po
    inv = 1.0 / den
    return oe * inv[0:256], oo * inv[256:512]


def _sink_cols(sink_ref):
    rb = lax.broadcasted_iota(I32, (512, 1), 0) >> 7
    cols = []
    for kvh in range(2):
        s = [sink_ref[4 * kvh + j] * LOG2_E for j in (0, 2, 1, 3)]
        cols.append(jnp.where(rb == 0, s[0], jnp.where(rb == 1, s[1], jnp.where(rb == 2, s[2], s[3]))))
    return cols


def _attend_block(qblk, kv_parts, masks, sink_cols):
    lo = lax.broadcasted_iota(I32, (128, 128), 1) < 64
    cols = []
    for kvh in range(2):
        q4 = _stack_heads(qblk[:, 256 * kvh:256 * kvh + 128], qblk[:, 256 * kvh + 128:256 * kvh + 256], lo)
        ko = 128 * kvh
        vao = 256 + 128 * kvh
        vbo = 384 - 128 * kvh
        oe, oo = _group_attention(q4, [kv[:, ko:ko + 128] for kv in kv_parts],
                                  [kv[:, vao:vao + 128] for kv in kv_parts],
                                  [kv[:, vbo:vbo + 128] for kv in kv_parts], masks, sink_cols[kvh])
        cols.append(jnp.where(lo, oe[0:128], oo[0:128]))
        cols.append(jnp.where(lo, oe[128:256], oo[128:256]))
    return jnp.concatenate(cols, axis=1).astype(BF16)


def _attn_lat_body(sink_ref, q_ref, kvp_ref, kvm_ref, kvn_ref, kvc_ref, o_ref, kvw_ref, *, tq, seq):
    i = pl.program_id(1)
    kvw_ref[0:128] = kvp_ref[0]
    kvw_ref[128:128 + tq] = kvm_ref[0]
    kvw_ref[128 + tq:256 + tq] = kvn_ref[0]
    kvc = kvc_ref[0]
    sink_cols = _sink_cols(sink_ref)

    def sub(j, carry):
        r0 = pl.multiple_of(j * 128, 128)
        win = kvw_ref[pl.ds(r0, 3 * 128), :]
        ii = lax.broadcasted_iota(I32, (128, 3 * 128), 0)
        cc = lax.broadcasted_iota(I32, (128, 3 * 128), 1)
        base = i * tq + j * 128 - 128
        valid = (ii <= cc) & (cc <= ii + 2 * WINDOW) & (cc >= -base) & (cc < seq - base)
        qblk = q_ref[0, pl.ds(r0, 128), :].astype(F32)
        o_ref[0, pl.ds(r0, 128), :] = _attend_block(qblk, [win, kvc], [valid, None], sink_cols)
        return carry

    for j in range(tq // 128):
        sub(j, 0)


def _attn_lat_call(sink, q, kv, kvc):
    nb, seq, _ = q.shape
    lc = kvc.shape[1]
    tq = TQ
    nblk = seq // 128
    r = tq // 128
    return pl.pallas_call(
        functools.partial(_attn_lat_body, tq=tq, seq=seq),
        grid=(nb, seq // tq),
        in_specs=[pl.BlockSpec(memory_space=pltpu.SMEM),
                  pl.BlockSpec((1, tq, ATTN_W), lambda b, i: (b, i, 0)),
                  pl.BlockSpec((1, 128, 4 * KV_W), lambda b, i: (b, jnp.maximum(i * r - 1, 0), 0)),
                  pl.BlockSpec((1, tq, 4 * KV_W), lambda b, i: (b, i, 0)),
                  pl.BlockSpec((1, 128, 4 * KV_W), lambda b, i: (b, jnp.minimum((i + 1) * r, nblk - 1), 0)),
                  pl.BlockSpec((1, lc, 4 * KV_W), lambda b, i: (b, 0, 0))],
        out_specs=pl.BlockSpec((1, tq, ATTN_W), lambda b, i: (b, i, 0)),
        out_shape=jax.ShapeDtypeStruct((nb, seq, ATTN_W), BF16),
        scratch_shapes=[pltpu.VMEM((tq + 256, 4 * KV_W), BF16)],
        compiler_params=_cparams(2),
        name="attention_window",
    )(sink, q, kv, kv, kv, kvc)


def _attn_ctx_body(sink_ref, q_ref, kvc_ref, o_ref, *, lc):
    kvc = kvc_ref[0]
    sink_cols = _sink_cols(sink_ref)
    for j in range(lc // 128):
        qblk = q_ref[0, j * 128:(j + 1) * 128, :].astype(F32)
        o_ref[0, j * 128:(j + 1) * 128, :] = _attend_block(qblk, [kvc], [None], sink_cols)


def _attn_ctx_call(sink, qc, kvc):
    nb, lc, _ = qc.shape
    return pl.pallas_call(
        functools.partial(_attn_ctx_body, lc=lc),
        grid=(nb,),
        in_specs=[pl.BlockSpec(memory_space=pltpu.SMEM),
                  pl.BlockSpec((1, lc, ATTN_W), lambda b: (b, 0, 0)),
                  pl.BlockSpec((1, lc, 4 * KV_W), lambda b: (b, 0, 0))],
        out_specs=pl.BlockSpec((1, lc, ATTN_W), lambda b: (b, 0, 0)),
        out_shape=jax.ShapeDtypeStruct((nb, lc, ATTN_W), BF16),
        compiler_params=_cparams(1),
        name="attention_ctx",
    )(sink, qc, kvc)


def _four_body(ua_ref, cb_ref, sb_ref, ca_ref, sa_ref, o_ref, *, nb, scale):
    i = pl.program_id(0)
    ca = ca_ref[pl.ds(i, 1), :]
    sa = sa_ref[pl.ds(i, 1), :]
    cb = cb_ref[...]
    sb = sb_ref[...]
    ct = (ca * cb - sa * sb).astype(BF16)
    st = (sa * cb + ca * sb).astype(BF16)
    for b in range(nb):
        ua = ua_ref[b, :, 0:FOUR_W]
        ub = ua_ref[b, :, FOUR_W:2 * FOUR_W]
        r = (jnp.dot(ct, ua, preferred_element_type=F32) - jnp.dot(st, ub, preferred_element_type=F32))
        o_ref[b] = (r * scale).astype(BF16)


def _four_call(uaub, cb, sb, ca, sa):
    nb, seq, _ = uaub.shape
    tm = cb.shape[0]
    one = pl.Buffered(1)
    return pl.pallas_call(
        functools.partial(_four_body, nb=nb, scale=float(seq) ** -0.5),
        grid=(seq // tm,),
        in_specs=[pl.BlockSpec((nb, seq, 2 * FOUR_W), lambda i: (0, 0, 0), pipeline_mode=one),
                  pl.BlockSpec((tm, seq), lambda i: (0, 0), pipeline_mode=one),
                  pl.BlockSpec((tm, seq), lambda i: (0, 0), pipeline_mode=one),
                  pl.BlockSpec((seq // tm, seq), lambda i: (0, 0), pipeline_mode=one),
                  pl.BlockSpec((seq // tm, seq), lambda i: (0, 0), pipeline_mode=one)],
        out_specs=pl.BlockSpec((nb, tm, FOUR_W), lambda i: (0, i, 0)),
        out_shape=jax.ShapeDtypeStruct((nb, seq, FOUR_W), BF16),
        compiler_params=_cparams(1),
        name="fourier_dft",
    )(uaub, cb, sb, ca, sa)


def _dft_tables(seq, tm):
    n = np.arange(seq)[None, :]

    def tab(rows):
        ang = ((rows[:, None] * n) % seq) * (2.0 * np.pi / seq)
        return jnp.asarray(np.cos(ang), F32), jnp.asarray(np.sin(ang), F32)

    cb, sb = tab(np.arange(tm))
    ca, sa = tab(np.arange(seq // tm) * tm)
    return cb, sb, ca, sa


def _four1_body(x_ref, g_ref, yr_ref, yi_ref):
    g = g_ref[...].astype(BF16)
    n = FOUR_R1 * 16
    w = FOUR_W
    for aa in range(x_ref.shape[2]):
        x = x_ref[0, :, aa].reshape(n, 2 * w)
        p = jnp.dot(g, x, preferred_element_type=F32)
        yr = p[0:n, 0:w] - p[n:2 * n, w:2 * w]
        yi = -(p[0:n, w:2 * w] + p[n:2 * n, 0:w])
        yr_ref[0, :, aa] = yr.astype(BF16).reshape(FOUR_R1, 16, w)
        yi_ref[0, :, aa] = yi.astype(BF16).reshape(FOUR_R1, 16, w)


def _four1_call(uaub, g):
    nb, seq, _ = uaub.shape
    r2 = seq // FOUR_R1
    na = r2 // 16
    x = uaub.reshape(nb, FOUR_R1, na, 16, 2 * FOUR_W)
    out = jax.ShapeDtypeStruct((nb, FOUR_R1, na, 16, FOUR_W), BF16)
    ac = FOUR1_A_CHUNK
    return pl.pallas_call(
        _four1_body,
        grid=(nb, na // ac),
        in_specs=[pl.BlockSpec((1, FOUR_R1, ac, 16, 2 * FOUR_W), lambda b, j: (b, 0, j, 0, 0)),
                  pl.BlockSpec(g.shape, lambda b, j: (0, 0))],
        out_specs=[pl.BlockSpec((1, FOUR_R1, ac, 16, FOUR_W), lambda b, j: (b, 0, j, 0, 0))] * 2,
        out_shape=[out, out],
        compiler_params=_cparams(2),
        name="fourier_stage1",
    )(x, g)


def _four2_body(yr_ref, yi_ref, c_ref, s_ref, ca_ref, sa_ref, o_ref, obuf, sem, *, nb, scale):
    k1 = pl.program_id(0)
    nk = pl.num_programs(0)
    ca = ca_ref[pl.ds(k1, 1), :]
    sa = sa_ref[pl.ds(k1, 1), :]
    c = c_ref[...]
    s = s_ref[...]
    gc = (c * ca - s * sa).astype(BF16)
    gs = (s * ca + c * sa).astype(BF16)

    def out_copy(slot, b, kk):
        return pltpu.make_async_copy(obuf.at[slot, b], o_ref.at[b, :, kk, :], sem.at[slot])

    for slot in range(2):
        @pl.when(k1 % 2 == slot)
        def _():
            @pl.when(k1 >= 2)
            def _():
                for b in range(nb):
                    out_copy(slot, b, 0).wait()

            for b in range(nb):
                acc = (jnp.dot(gc, yr_ref[b, 0], preferred_element_type=F32)
                       + jnp.dot(gs, yi_ref[b, 0], preferred_element_type=F32))
                obuf[slot, b] = acc * scale
            for b in range(nb):
                out_copy(slot, b, k1).start()

    @pl.when(k1 == nk - 1)
    def _():
        for slot in range(2):
            for b in range(nb):
                out_copy(slot, b, 0).wait()


def _four2_call(yr, yi, c, s, ca, sa):
    nb = yr.shape[0]
    r2 = c.shape[0]
    seq = FOUR_R1 * r2
    yr4 = yr.reshape(nb, FOUR_R1, r2, FOUR_W)
    yi4 = yi.reshape(nb, FOUR_R1, r2, FOUR_W)
    full = lambda shape: pl.BlockSpec(shape, lambda k: (0,) * len(shape))
    out = pl.pallas_call(
        functools.partial(_four2_body, nb=nb, scale=float(seq) ** -0.5),
        grid=(FOUR_R1,),
        in_specs=[pl.BlockSpec((nb, 1, r2, FOUR_W), lambda k: (0, k, 0, 0)),
                  pl.BlockSpec((nb, 1, r2, FOUR_W), lambda k: (0, k, 0, 0)),
                  full((r2, r2)), full((r2, r2)), full((FOUR_R1, r2)), full((FOUR_R1, r2))],
        out_specs=pl.BlockSpec(memory_space=pl.ANY),
        out_shape=jax.ShapeDtypeStruct((nb, r2, FOUR_R1, FOUR_W), F32),
        scratch_shapes=[pltpu.VMEM((2, nb, r2, FOUR_W), F32), pltpu.SemaphoreType.DMA((2,))],
        compiler_params=_cparams(1),
        name="fourier_stage2",
    )(yr4, yi4, c, s, ca, sa)
    return out.reshape(nb, seq, FOUR_W)


def _four2_tables(seq):
    r1 = FOUR_R1
    r2 = seq // r1
    assert r1 * r2 == seq and r2 % 16 == 0
    k = np.arange(r1)
    ang1 = ((k[:, None] * k[None, :]) % r1) * (2.0 * np.pi / r1)
    eye = np.eye(16)
    g = np.concatenate([np.kron(np.cos(ang1), eye), np.kron(np.sin(ang1), eye)], axis=0)
    m = np.arange(r2)
    ang2 = ((m[:, None] * m[None, :]) % r2) * (2.0 * np.pi / r2)
    alpha = ((k[:, None] * m[None, :]) % seq) * (2.0 * np.pi / seq)
    f = lambda t: jnp.asarray(t, F32)
    return f(g), (f(np.cos(ang2)), f(np.sin(ang2)), f(np.cos(alpha)), f(np.sin(alpha)))


def _route(lt, tri_ref, cnt_ref, tm):
    rowi = lax.broadcasted_iota(I32, (32, tm), 0)
    big = jnp.int32(999)

    def first_argmax(vals):
        mx = jnp.max(vals, axis=0, keepdims=True)
        return mx, jnp.min(jnp.where(vals == mx, rowi, big), axis=0, keepdims=True)

    is_grp = rowi < 4
    mg, gi = first_argmax(jnp.where(is_grp, lt, NEG_INF))
    pg = 1.0 / jnp.sum(jnp.where(is_grp, jnp.exp(jnp.where(is_grp, lt, mg) - mg), 0.0), axis=0, keepdims=True)
    est = 4 + 4 * gi
    le = jnp.where((rowi >= est) & (rowi < est + 4), lt, NEG_INF)
    m1, i1 = first_argmax(le)
    m2, i2 = first_argmax(jnp.where(rowi == i1, NEG_INF, le))
    e2 = jnp.exp(m2 - m1)
    w1 = pg / (1.0 + e2)
    w2 = pg * e2 / (1.0 + e2)
    a1 = i1 - est
    a2 = i2 - est
    code = jnp.minimum(a1, a2) * 4 + jnp.maximum(a1, a2)
    pidx = jnp.where(code == 1, 0, jnp.where(code == 6, 1, jnp.where(code == 2, 2,
           jnp.where(code == 3, 3, jnp.where(code == 7, 4, 5)))))
    slot_a = jnp.where(pidx == 0, 0, jnp.where(pidx <= 2, 2, 3))
    slot_b = jnp.where(pidx <= 1, 1, jnp.where(pidx <= 3, 0, jnp.where(pidx == 4, 1, 2)))
    wa = jnp.where(a1 == slot_a, w1, w2)
    wb = jnp.where(a1 == slot_b, w1, w2)
    bin_ = gi * 6 + pidx

    onehot = rowi == bin_
    pref = jnp.dot(onehot.astype(BF16), tri_ref[...], preferred_element_type=F32)
    carry = cnt_ref[:, 0:1]
    rank = jnp.sum(jnp.where(onehot, pref - 1.0 + carry, 0.0), axis=0, keepdims=True)
    cnt_ref[...] = jnp.broadcast_to(carry + pref[:, tm - 1:tm], cnt_ref.shape)
    return jnp.concatenate([bin_.astype(F32), rank, wa, wb, jnp.zeros((128 - 4, tm), F32)], axis=0)


def _out_body(x_ref, pup_ref, pum_ref, pun_ref, at_ref, fo_ref, wo_ref, pw_ref, ps_ref, band_ref, icnt_ref,
              mod_ref, g2_ref, w2_ref, br_ref, tri_ref, cin_ref,
              xo_ref, rows_ref, cnt_ref, *, tm, per_batch):
    b = pl.program_id(0)
    i = pl.program_id(1)
    nt = pl.num_programs(1)
    row = b if per_batch else CTX_ROW
    g1 = mod_ref[pl.ds(row, 1), pl.ds(2 * D, D)]
    sh2 = mod_ref[pl.ds(row, 1), pl.ds(3 * D, D)]
    sc2 = mod_ref[pl.ds(row, 1), pl.ds(4 * D, D)]

    @pl.when((b == 0) & (i == 0))
    def _():
        cnt_ref[...] = cin_ref[...]

    um = pum_ref[0]
    zh = jnp.zeros((HALO, POOL_W), BF16)
    uext = jnp.concatenate([jnp.where(i > 0, pup_ref[0], zh), um, jnp.where(i < nt - 1, pun_ref[0], zh)], axis=0)
    grp = lax.broadcasted_iota(I32, (POOL_CHUNK, POOL_W), 1) >> 6
    chunks = []
    for c in range(tm // POOL_CHUNK):
        uc = uext[POOL_CHUNK * c:POOL_CHUNK * (c + 1) + 2 * HALO]
        pc = jnp.zeros((POOL_CHUNK, POOL_W), F32)
        for g in range(len(POOL_WINDOWS)):
            pc = jnp.where(grp == g, jnp.dot(band_ref[g], uc, preferred_element_type=F32), pc)
        chunks.append(pc)
    pooled = jnp.concatenate(chunks, axis=0)
    y = pooled * icnt_ref[...] - um.astype(F32)
    pool_out = jnp.dot(y.astype(BF16), pw_ref[...], preferred_element_type=F32) * ps_ref[...]

    cat = jnp.concatenate([pool_out.astype(BF16), at_ref[0], fo_ref[0].astype(BF16)], axis=1)
    xm = x_ref[0] + g1 * jnp.dot(cat, wo_ref[...], preferred_element_type=F32)
    xo_ref[0] = xm

    ms = jnp.mean(xm * xm, axis=-1, keepdims=True)
    h2 = (xm * lax.rsqrt(ms + EPS) * g2_ref[...]) * (1.0 + sc2) + sh2

    hh = h2.astype(BF16)
    hl = (h2 - hh.astype(F32)).astype(BF16)
    w2 = w2_ref[...]
    p2 = jnp.dot(hh, w2, preferred_element_type=F32)
    logits = (p2[:, 0:128] + p2[:, 128:256] + jnp.dot(hl, w2[:, 0:128], preferred_element_type=F32) + br_ref[...])
    meta = _route(logits.T[0:32, :], tri_ref, cnt_ref, tm).T

    rows_ref[:, 0:D] = h2
    rows_ref[:, D:ROW_W] = meta


def _out_call(x3, pu, attn, four, w_out, pool_wbd, pool_scale, bands, icnt, mod, g2, w2, br, tri, cnt_in, *,
              per_batch):
    nb, seq, _ = x3.shape
    tm = min(TM_OUT, seq)
    nt = seq // tm
    hb = tm // HALO
    full = lambda shape: pl.BlockSpec(shape, lambda b, i: (0,) * len(shape))
    in_specs = [pl.BlockSpec((1, tm, D), lambda b, i: (b, i, 0)),
                pl.BlockSpec((1, HALO, POOL_W), lambda b, i: (b, jnp.maximum(i * hb - 1, 0), 0)),
                pl.BlockSpec((1, tm, POOL_W), lambda b, i: (b, i, 0)),
                pl.BlockSpec((1, HALO, POOL_W), lambda b, i: (b, jnp.minimum((i + 1) * hb, seq // HALO - 1), 0)),
                pl.BlockSpec((1, tm, ATTN_W), lambda b, i: (b, i, 0)),
                pl.BlockSpec((1, tm, FOUR_W), lambda b, i: (b, i, 0)),
                full((D, D)), full((POOL_W, POOL_W)), full((1, POOL_W)),
                full((len(POOL_WINDOWS), POOL_CHUNK, POOL_CHUNK + 2 * HALO)),
                pl.BlockSpec((tm, POOL_W), lambda b, i: (i, 0)),
                full((8, 6 * D)), full((1, D)), full((D, 256)), full((1, 128)), full((tm, tm)), full((32, 128))]
    args = [x3, pu, pu, pu, attn, four, w_out, pool_wbd, pool_scale, bands, icnt, mod, g2, w2, br, tri, cnt_in]
    return pl.pallas_call(
        functools.partial(_out_body, tm=tm, per_batch=per_batch),
        grid=(nb, nt),
        in_specs=in_specs,
        out_specs=[pl.BlockSpec((1, tm, D), lambda b, i: (b, i, 0)),
                   pl.BlockSpec((tm, ROW_W), lambda b, i: (b * nt + i, 0)),
                   pl.BlockSpec((32, 128), lambda b, i: (0, 0))],
        out_shape=[jax.ShapeDtypeStruct((nb, seq, D), F32),
                   jax.ShapeDtypeStruct((nb * seq, ROW_W), F32),
                   jax.ShapeDtypeStruct((32, 128), F32)],
        compiler_params=_cparams(2),
        name="out_proj_router" if per_batch else "out_proj_router_ctx",
    )(*args)


def _pool_tables(seq):
    t = np.arange(POOL_CHUNK)[:, None]
    s = np.arange(POOL_CHUNK + 2 * HALO)[None, :] - HALO
    bands = np.stack([(s >= t - w // 2) & (s <= t + w // 2 - 1) for w in POOL_WINDOWS]).astype(np.float32)
    pos = np.arange(seq)
    icnt = np.stack([1.0 / (np.minimum(pos + w // 2 - 1, seq - 1) - np.maximum(pos - w // 2, 0) + 1)
                     for w in POOL_WINDOWS], axis=1)
    return jnp.asarray(bands, BF16), jnp.asarray(np.repeat(icnt, POOL_W // len(POOL_WINDOWS), axis=1), F32)


def _row_copies(tm, make_copy):
    for r in range(tm):
        make_copy(r).start(priority=r % 2)


def _row_waits(tm, make_copy):
    def drain(r, c):
        make_copy(0).wait()
        return c

    lax.fori_loop(0, tm, drain, 0, unroll=8)


def _zero_fill(ends_ref, nv_ref, xs_ref, zbuf, zsem, tm, n_out, wait):
    def piece(off, size):
        return pltpu.make_async_copy(zbuf.at[pl.ds(0, size)], xs_ref.at[pl.ds(off, size)], zsem)

    def run(cond, off, size):
        @pl.when(cond)
        def _():
            c = piece(off, size)
            c.wait() if wait else c.start()

    for b in range(N_BINS):
        off = ends_ref[b]
        pad = (tm - (off & (tm - 1))) & (tm - 1)
        for k in range(tm.bit_length() - 1):
            run(((pad >> k) & 1) == 1, off, 1 << k)
            off = off + (pad & (1 << k))

    def tail(t, c):
        c_ = piece(t * tm, tm)
        c_.wait() if wait else c_.start()
        return c

    lax.fori_loop(nv_ref[0], n_out, tail, 0)


def _scatter_body(dest_ref, ends_ref, nv_ref, *refs, tm, n_tiles, n_first, n_out):
    n_h = len(refs) - 3 - 3 * SCATTER_SLOTS
    h_refs, xs_ref = refs[:n_h], refs[n_h]
    bufs = refs[n_h + 1:n_h + 1 + SCATTER_SLOTS]
    lsems = refs[n_h + 1 + SCATTER_SLOTS:n_h + 1 + 2 * SCATTER_SLOTS]
    rsems = refs[n_h + 1 + 2 * SCATTER_SLOTS:n_h + 1 + 3 * SCATTER_SLOTS]
    zbuf, zsem = refs[-2:]
    zbuf[...] = jnp.zeros_like(zbuf)
    _zero_fill(ends_ref, nv_ref, xs_ref, zbuf, zsem, tm, n_out, wait=False)

    def lane_block_copy(h_ref, tt, slot, j):
        return pltpu.make_async_copy(h_ref.at[pl.ds(tt * tm, tm), pl.ds(128 * j, 128)],
                                     bufs[slot].at[:, j, :], lsems[slot])

    def load(t, slot):
        def start(h_ref, tt):
            for j in range(ROW_TILES):
                lane_block_copy(h_ref, tt, slot, j).start()

        if n_h == 1:
            start(h_refs[0], t)
        else:
            @pl.when(t < n_first)
            def _():
                start(h_refs[0], t)

            @pl.when(t >= n_first)
            def _():
                start(h_refs[1], t - n_first)

    def row_copy(slot, r, d):
        return pltpu.make_async_copy(bufs[slot].at[r], xs_ref.at[d], rsems[slot])

    load(0, 0)
    load(1, 1)

    def group(g, c):
        for slot in range(SCATTER_SLOTS):
            t = g * SCATTER_SLOTS + slot
            ahead = (slot + 2) % SCATTER_SLOTS
            for j in range(ROW_TILES):
                lane_block_copy(h_refs[0], 0, slot, j).wait()

            @pl.when(t >= 2)
            def _():
                _row_waits(tm, lambda r: row_copy(ahead, r, 0))

            @pl.when(t + 2 < n_tiles)
            def _():
                load(t + 2, ahead)

            _row_copies(tm, lambda r: row_copy(slot, r, dest_ref[t * tm + r]))
        return c

    lax.fori_loop(0, n_tiles // SCATTER_SLOTS, group, 0)
    for t in (n_tiles - 2, n_tiles - 1):
        _row_waits(tm, lambda r: row_copy(t % SCATTER_SLOTS, r, 0))
    _zero_fill(ends_ref, nv_ref, xs_ref, zbuf, zsem, tm, n_out, wait=True)


def _scatter_call(dest, bin_ends, n_valid, row_sets):
    tm = TM_ROW
    assert tm == TM_MOE
    n_first = row_sets[0].shape[0] // tm
    n_rows = sum(r.shape[0] for r in row_sets)
    n_tiles = n_rows // tm
    n_out = n_tiles + N_BINS
    assert n_tiles % SCATTER_SLOTS == 0 and n_tiles >= SCATTER_SLOTS
    return pl.pallas_call(
        functools.partial(_scatter_body, tm=tm, n_tiles=n_tiles, n_first=n_first, n_out=n_out),
        grid_spec=pltpu.PrefetchScalarGridSpec(
            num_scalar_prefetch=3,
            grid=(1,),
            in_specs=[pl.BlockSpec(memory_space=pl.ANY)] * len(row_sets),
            out_specs=pl.BlockSpec(memory_space=pl.ANY),
            scratch_shapes=([pltpu.VMEM((tm, ROW_TILES, 128), F32)] * SCATTER_SLOTS
                            + [pltpu.SemaphoreType.DMA(())] * (2 * SCATTER_SLOTS)
                            + [pltpu.VMEM((tm, ROW_TILES, 128), F32), pltpu.SemaphoreType.DMA(())])),
        out_shape=jax.ShapeDtypeStruct((n_out * tm, ROW_TILES, 128), F32),
        compiler_params=_cparams(1),
        name="moe_scatter_rows",
    )(dest, bin_ends, n_valid, *row_sets)


def _moe_body(ea_ref, eb_ref, nv_ref, xs_ref, wga, wua, wda, wgb, wub, wdb, ys_ref, xbuf0, xbuf1, sem0, sem1, *, tm):
    del ea_ref, eb_ref
    s = pl.program_id(0)
    nv = nv_ref[0]
    xbufs = (xbuf0, xbuf1)
    sems = (sem0, sem1)

    def lane_block_copy(t, slot, j):
        return pltpu.make_async_copy(xs_ref.at[pl.ds(t * tm, tm), j, :], xbufs[slot].at[:, pl.ds(128 * j, 128)],
                                     sems[slot])

    def fetch(t, slot):
        for j in range(ROW_TILES):
            lane_block_copy(t, slot, j).start()

    @pl.when(s == 0)
    def _():
        fetch(0, 0)

    @pl.when(s >= nv)
    def _():
        ys_ref[...] = jnp.zeros_like(ys_ref)

    for slot in range(2):
        @pl.when((s < nv) & (s % 2 == slot))
        def _():
            @pl.when(s + 1 < nv)
            def _():
                fetch(s + 1, 1 - slot)

            for j in range(ROW_TILES):
                lane_block_copy(0, slot, j).wait()
            xs = xbufs[slot]
            h = xs[:, 0:D].astype(BF16)
            meta = xs[:, D:ROW_W]

            def expert(wg, wu, wd, gate):
                g = jnp.dot(h, wg[0, 0].astype(BF16), preferred_element_type=F32)
                u = jnp.dot(h, wu[0, 0].astype(BF16), preferred_element_type=F32)
                a = _silu(g) * u * gate
                return jnp.dot(a.astype(BF16), wd[0, 0].astype(BF16), preferred_element_type=F32)

            ys_ref[...] = expert(wga, wua, wda, meta[:, 2:3]) + expert(wgb, wub, wdb, meta[:, 3:4])


def _moe_call(items, xs, w_gate, w_up, w_down, layer):
    tm = TM_MOE
    n_tiles = xs.shape[0] // tm
    wa = lambda s, ea, eb, nv: (layer, ea[s], 0, 0)
    wb = lambda s, ea, eb, nv: (layer, eb[s], 0, 0)
    up_spec = lambda f: pl.BlockSpec((1, 1, D, D_EXPERT), f)
    dn_spec = lambda f: pl.BlockSpec((1, 1, D_EXPERT, D), f)
    return pl.pallas_call(
        functools.partial(_moe_body, tm=tm),
        grid_spec=pltpu.PrefetchScalarGridSpec(
            num_scalar_prefetch=len(items),
            grid=(n_tiles,),
            in_specs=[pl.BlockSpec(memory_space=pl.ANY),
                      up_spec(wa), up_spec(wa), dn_spec(wa), up_spec(wb), up_spec(wb), dn_spec(wb)],
            out_specs=pl.BlockSpec((tm, D), lambda s, ea, eb, nv: (s, 0)),
            scratch_shapes=[pltpu.VMEM((tm, ROW_W), F32), pltpu.VMEM((tm, ROW_W), F32),
                            pltpu.SemaphoreType.DMA(()), pltpu.SemaphoreType.DMA(())]),
        out_shape=jax.ShapeDtypeStruct((xs.shape[0], D), F32),
        compiler_params=_cparams(1),
        name="moe_experts",
    )(*items, xs, w_gate, w_up, w_down, w_gate, w_up, w_down)


def _gather_body(dest_ref, x_ref, mod_ref, ys_ref, o_ref, ybuf0, ybuf1, sem0, sem1, *,
                 tm, seq, per_batch, dest_off):
    i = pl.program_id(0)
    n_tiles = pl.num_programs(0)
    ybufs = (ybuf0, ybuf1)
    sems = (sem0, sem1)
    row = (i * tm) // seq if per_batch else CTX_ROW
    g2 = mod_ref[pl.ds(row, 1), pl.ds(5 * D, D)]

    def row_copy(slot, r, d):
        return pltpu.make_async_copy(ys_ref.at[pl.ds(d, 1)], ybufs[slot].at[pl.ds(r, 1)], sems[slot])

    def fetch(t, slot):
        base = dest_off + t * tm
        _row_copies(tm, lambda r: row_copy(slot, r, dest_ref[base + r]))

    @pl.when(i == 0)
    def _():
        fetch(0, 0)

    for slot in range(2):
        @pl.when(i % 2 == slot)
        def _():
            @pl.when(i + 1 < n_tiles)
            def _():
                fetch(i + 1, 1 - slot)

            _row_waits(tm, lambda r: row_copy(slot, r, 0))
            o_ref[...] = x_ref[...] + g2 * ybufs[slot][...]


def _gather_call(dest, x2, mod, ys, *, seq, per_batch, dest_off):
    n = x2.shape[0]
    tm = TM_ROW
    return pl.pallas_call(
        functools.partial(_gather_body, tm=tm, seq=seq, per_batch=per_batch, dest_off=dest_off),
        grid_spec=pltpu.PrefetchScalarGridSpec(
            num_scalar_prefetch=1,
            grid=(n // tm,),
            in_specs=[pl.BlockSpec((tm, D), lambda i, *_: (i, 0)),
                      pl.BlockSpec((8, 6 * D), lambda i, *_: (0, 0)),
                      pl.BlockSpec(memory_space=pl.ANY)],
            out_specs=pl.BlockSpec((tm, D), lambda i, *_: (i, 0)),
            scratch_shapes=[pltpu.VMEM((tm, D), F32), pltpu.VMEM((tm, D), F32),
                            pltpu.SemaphoreType.DMA(()), pltpu.SemaphoreType.DMA(())]),
        out_shape=jax.ShapeDtypeStruct((n, D), F32),
        compiler_params=_cparams(1),
        name="moe_gather_rows" if per_batch else "moe_gather_rows_ctx",
    )(dest, x2, mod, ys)


def _routing_tables(bins, rank, cnt, n_rows):
    tm = TM_MOE
    counts = cnt[:N_BINS, 0].astype(I32)
    tiles = (counts + tm - 1) // tm
    tile_end = jnp.cumsum(tiles)
    starts = (tile_end - tiles) * tm
    ids = jnp.arange(N_BINS, dtype=I32)
    pick = lambda key, tab: jnp.sum(jnp.where(key[:, None] == ids[None, :], tab[None, :], 0), axis=1)
    dest = rank + pick(bins, starts)
    n_valid = tile_end[-1]
    n_tiles = n_rows // tm + N_BINS
    s = jnp.minimum(jnp.arange(n_tiles, dtype=I32), n_valid - 1)
    tbin = jnp.sum((s[:, None] >= tile_end[None, :]).astype(I32), axis=1)
    pidx = tbin % 6
    six = jnp.arange(6, dtype=I32)
    slot = lambda tab: jnp.sum(jnp.where(pidx[:, None] == six[None, :], jnp.asarray(tab, I32)[None, :], 0), axis=1)
    ea = 4 * (tbin // 6) + slot(PAIR_SLOT_A)
    eb = 4 * (tbin // 6) + slot(PAIR_SLOT_B)
    return dest, starts + counts, (ea, eb, n_valid.reshape(1))


def _block_diag(w):
    g, c, d = w.shape
    eye = jnp.asarray(np.eye(g), w.dtype)
    return (w[:, :, None, :] * eye[:, None, :, None]).reshape(g * c, g * d)


def _rope_tables(n_tokens):
    rows = n_tokens // GRID_W
    r = np.repeat(np.arange(rows), GRID_W).astype(np.float64)
    col = np.tile(np.arange(GRID_W), rows).astype(np.float64)
    half = HEAD_DIM // 2
    inv = 1.0 / (ROPE_BASE ** (np.arange(0, half, 2, dtype=np.float64) / half))
    ar = r[:, None] * inv
    ac = col[:, None] * inv
    ang = np.concatenate([ar, ar, ac, ac], axis=-1)
    sign = np.where((np.arange(HEAD_DIM) & 16) == 0, -1.0, 1.0)
    cos, sin = np.cos(ang), np.sin(ang) * sign
    return jnp.asarray(np.tile(cos, (1, 2)), F32), jnp.asarray(np.tile(sin, (1, 2)), F32)


def kernel(x, c, ctx, c_ctx, w_mod, b_mod, norm1_g, w_in, q_norm_g, k_norm_g, attn_sink, pool_w, pool_scale,
           four_w, w_out, norm2_g, w_grp, b_grp, w_rtr, b_rtr, w_gate, w_up, w_down):
    nb, seq, _ = x.shape
    lc = ctx.shape[1]
    depth = w_mod.shape[0]
    t_lat = nb * seq
    t_ctx = nb * lc

    cs = jnp.concatenate([c, c_ctx[None, :], jnp.zeros((8 - nb - 1, D), F32)], axis=0)
    m512 = jnp.asarray(np.kron(np.eye(N_HEADS), np.full((HEAD_DIM, HEAD_DIM), 1.0 / HEAD_DIM)), BF16)
    kk = np.arange(HEAD_DIM)
    ang64 = 2.0 * np.pi * ((kk[:, None] * kk[None, :]) % HEAD_DIM) / HEAD_DIM
    c64bd = jnp.asarray(np.kron(np.eye(4), np.cos(ang64)), F32)
    s64bd = jnp.asarray(np.kron(np.eye(4), np.sin(ang64)), F32)
    tri_of = lambda n: jnp.asarray(np.triu(np.ones((min(TM_OUT, n),) * 2)), BF16)
    cos2, sin2 = _rope_tables(seq)
    four_cs, four_tabs = _four2_tables(seq)
    tabs_ctx = _dft_tables(lc, min(TM_FOUR, lc))
    pool_lat = _pool_tables(seq)
    pool_ctx = _pool_tables(lc)
    w_in_b = w_in.astype(BF16)
    w_out_b = w_out.astype(BF16)
    wbd = jnp.stack([_block_diag(four_w[l]) for l in range(depth)])
    pool_wbd = jnp.stack([_block_diag(pool_w[l]) for l in range(depth)]).astype(BF16)
    wr = jnp.concatenate([w_grp, w_rtr, jnp.zeros((depth, D, 128 - 4 - N_EXPERTS), F32)], axis=2)
    wr_hi = wr.astype(BF16)
    w2 = jnp.concatenate([wr_hi, (wr - wr_hi.astype(F32)).astype(BF16)], axis=2)
    br = jnp.concatenate([b_grp, b_rtr, jnp.zeros((depth, 128 - 4 - N_EXPERTS), F32)], axis=1)

    mod_all = _mod_call(cs, w_mod, b_mod)
    ab_all = _ab_call(c64bd, s64bd, wbd)

    xc = ctx
    for l in range(depth):
        last = l == depth - 1
        mod = mod_all[l]
        g1 = norm1_g[l][None, :]
        g2 = norm2_g[l][None, :]
        qg = jnp.tile(q_norm_g[l], N_HEADS)[None, :]
        kg = jnp.tile(k_norm_g[l], KV_W // HEAD_DIM)[None, :]
        in_args = (mod, g1, w_in_b[l], m512, qg, kg, ab_all[l])

        puc, qc, kvc, uac = _in_call(xc, *in_args, None, None, rope=False, per_batch=False)
        pul, ql, kvl, ual = _in_call(x, *in_args, cos2, sin2, rope=True, per_batch=True)
        attn_l = _attn_lat_call(attn_sink[l], ql, kvl, kvc)
        four_l = _four2_call(*_four1_call(ual, four_cs), *four_tabs)

        proj = (w_out_b[l], pool_wbd[l], pool_scale[l][None, :])
        rout = (mod, g2, w2[l], br[l][None, :])
        cnt0 = jnp.zeros((32, 128), F32)
        x_mid, rows_l, cnt = _out_call(x, pul, attn_l, four_l, *proj, *pool_lat, *rout, tri_of(seq), cnt0,
                                       per_batch=True)
        row_sets = [rows_l]
        if not last:
            attn_c = _attn_ctx_call(attn_sink[l], qc, kvc)
            four_c = _four_call(uac, *tabs_ctx)
            xc_mid, rows_c, cnt = _out_call(xc, puc, attn_c, four_c, *proj, *pool_ctx, *rout, tri_of(lc), cnt,
                                            per_batch=False)
            row_sets.append(rows_c)

        route = [r[:, D:D + 2].astype(I32) for r in row_sets]
        bins = jnp.concatenate([r[:, 0] for r in route])
        rank = jnp.concatenate([r[:, 1] for r in route])
        dest, bin_ends, items = _routing_tables(bins, rank, cnt, bins.shape[0])
        xs = _scatter_call(dest, bin_ends, items[2], row_sets)
        ys = _moe_call(items, xs, w_gate, w_up, w_down, l)
        x = _gather_call(dest, x_mid.reshape(t_lat, D), mod, ys, seq=seq, per_batch=True,
                         dest_off=0).reshape(nb, seq, D)
        if not last:
            xc = _gather_call(dest, xc_mid.reshape(t_ctx, D), mod, ys, seq=lc, per_batch=False,
                              dest_off=t_lat).reshape(nb, lc, D)
    return x
```

```python
import functools

import numpy as np
import jax
import jax.numpy as jnp
from jax import lax
from jax.experimental import pallas as pl
from jax.experimental.pallas import tpu as pltpu

F32 = jnp.float32
BF16 = jnp.bfloat16
I32 = jnp.int32
HI = lax.Precision.HIGHEST

D = 1024
HEAD_DIM = 64
N_HEADS = 8
GRID_W = 64
POOL_WINDOWS = (2, 4, 8, 16)
POOL_W = 256
ATTN_W = 512
KV_W = 128
FOUR_W = 256
IN_W = 1280
N_EXPERTS = 16
D_EXPERT = 512
WINDOW = 128
ROPE_BASE = 10000.0
EPS = 1e-6
NEG_INF = -1e30
LOG2_E = 1.4426950408889634
CTX_ROW = 4
N_BINS = 24
PAIR_SLOT_A = (0, 2, 2, 3, 3, 3)
PAIR_SLOT_B = (1, 1, 0, 0, 1, 2)
META_W = 128
ROW_W = D + META_W
ROW_TILES = ROW_W // 128
HALO = 16
POOL_CHUNK = 128

VMEM_LIMIT = 56 * 1024 * 1024
TM_IN = 1024
TQ = 1024
TM_OUT = 1024
TM_FOUR = 256
FOUR_R1 = 16
FOUR1_A_CHUNK = 4
TM_MOE = 256
TM_ROW = 256
SCATTER_SLOTS = 4
MOD_TN = 1024


def _cparams(n_axes):
    return pltpu.CompilerParams(dimension_semantics=("arbitrary",) * n_axes,
                                vmem_limit_bytes=VMEM_LIMIT)


def _silu(v):
    return v / (1.0 + jnp.exp(-v))


def _mod_body(cs_ref, w_ref, b_ref, o_ref):
    s = _silu(cs_ref[...])
    w = w_ref[0]
    s_hi = s.astype(BF16)
    s_lo = (s - s_hi.astype(F32)).astype(BF16)
    w_hi = w.astype(BF16)
    w_lo = (w - w_hi.astype(F32)).astype(BF16)
    p = jnp.dot(jnp.concatenate([s_hi, s_lo], axis=0), w_hi, preferred_element_type=F32)
    o_ref[0] = p[0:8] + p[8:16] + jnp.dot(s_hi, w_lo, preferred_element_type=F32) + b_ref[0]


def _mod_call(cs, w_mod, b_mod):
    depth = w_mod.shape[0]
    return pl.pallas_call(
        _mod_body,
        grid=(depth, 6 * D // MOD_TN),
        in_specs=[pl.BlockSpec((8, D), lambda l, j: (0, 0)),
                  pl.BlockSpec((1, D, MOD_TN), lambda l, j: (l, 0, j)),
                  pl.BlockSpec((1, 1, MOD_TN), lambda l, j: (l, 0, j))],
        out_specs=pl.BlockSpec((1, 8, MOD_TN), lambda l, j: (l, 0, j)),
        out_shape=jax.ShapeDtypeStruct((depth, 8, 6 * D), F32),
        compiler_params=_cparams(2),
        name="modulation",
    )(cs, w_mod, b_mod.reshape(depth, 1, 6 * D))


def _ab_body(c_ref, s_ref, w_ref, o_ref):
    w = w_ref[0]
    ca = jnp.dot(c_ref[...], w, preferred_element_type=F32, precision=HI)
    sa = jnp.dot(s_ref[...], w, preferred_element_type=F32, precision=HI)
    o_ref[0] = (jnp.concatenate([ca, sa], axis=1) * (HEAD_DIM ** -0.5)).astype(BF16)


def _ab_call(c64bd, s64bd, wbd):
    depth = wbd.shape[0]
    return pl.pallas_call(
        _ab_body,
        grid=(depth,),
        in_specs=[pl.BlockSpec((FOUR_W, FOUR_W), lambda l: (0, 0)),
                  pl.BlockSpec((FOUR_W, FOUR_W), lambda l: (0, 0)),
                  pl.BlockSpec((1, FOUR_W, FOUR_W), lambda l: (l, 0, 0))],
        out_specs=pl.BlockSpec((1, FOUR_W, 2 * FOUR_W), lambda l: (l, 0, 0)),
        out_shape=jax.ShapeDtypeStruct((depth, FOUR_W, 2 * FOUR_W), BF16),
        compiler_params=_cparams(1),
        name="fourier_weights",
    )(c64bd, s64bd, wbd)


def _head_rms(t, m, g):
    ms = jnp.dot((t * t).astype(BF16), m, preferred_element_type=F32)
    return t * lax.rsqrt(ms + EPS) * g


def _rope(t, cos, sin_signed):
    w = t.shape[1]
    lane = lax.broadcasted_iota(I32, t.shape, 1)
    fwd = pltpu.roll(t, w - 16, 1)
    bwd = pltpu.roll(t, 16, 1)
    rot = jnp.where((lane & 16) == 0, fwd, bwd)
    return t * cos + rot * sin_signed


def _in_body(*refs, rope, per_batch):
    if rope:
        (x_ref, mod_ref, g1_ref, w_ref, m_ref, qg_ref, kg_ref, ab_ref, cos_ref, sin_ref,
         pu_ref, q_ref, kv_ref, ua_ref) = refs
    else:
        (x_ref, mod_ref, g1_ref, w_ref, m_ref, qg_ref, kg_ref, ab_ref,
         pu_ref, q_ref, kv_ref, ua_ref) = refs
    row = pl.program_id(0) if per_batch else CTX_ROW
    sh1 = mod_ref[pl.ds(row, 1), pl.ds(0, D)]
    sc1 = mod_ref[pl.ds(row, 1), pl.ds(D, D)]
    x = x_ref[0]
    ms = jnp.mean(x * x, axis=-1, keepdims=True)
    h = (x * lax.rsqrt(ms + EPS) * g1_ref[...]) * (1.0 + sc1) + sh1
    p = jnp.dot(h.astype(BF16), w_ref[...], preferred_element_type=F32)
    pu = p[:, 0:256]
    q = p[:, 256:768]
    k = p[:, 768:896]
    v = p[:, 896:1024]
    fu = p[:, 1024:1280]
    m = m_ref[...]
    q = _head_rms(q, m, qg_ref[...])
    k = _head_rms(k, m[0:KV_W, 0:KV_W], kg_ref[...])
    if rope:
        cos = cos_ref[...]
        sin = sin_ref[...]
        q = _rope(q, jnp.concatenate([cos] * 4, axis=1), jnp.concatenate([sin] * 4, axis=1))
        k = _rope(k, cos, sin)
    q = q * (HEAD_DIM ** -0.5 * LOG2_E)
    pu_ref[0] = pu.astype(BF16)
    q_ref[0] = q.astype(BF16)
    kv_ref[0] = jnp.concatenate([k, pltpu.roll(k, 64, 1), v, pltpu.roll(v, 64, 1)], axis=1).astype(BF16)
    ua_ref[0] = jnp.dot(fu.astype(BF16), ab_ref[...], preferred_element_type=F32).astype(BF16)


def _in_call(x3, mod, g1, w_in, m512, qg, kg, ab, cos2, sin2, *, rope, per_batch):
    nb, seq, _ = x3.shape
    tm = min(TM_IN, seq)
    full = lambda shape: pl.BlockSpec(shape, lambda b, i: (0,) * len(shape))
    in_specs = [pl.BlockSpec((1, tm, D), lambda b, i: (b, i, 0)),
                full((8, 6 * D)), full((1, D)), full((D, IN_W)), full((ATTN_W, ATTN_W)),
                full((1, ATTN_W)), full((1, KV_W)), full((FOUR_W, 2 * FOUR_W))]
    args = [x3, mod, g1, w_in, m512, qg, kg, ab]
    if rope:
        in_specs += [pl.BlockSpec((tm, 128), lambda b, i: (i, 0)),
                     pl.BlockSpec((tm, 128), lambda b, i: (i, 0))]
        args += [cos2, sin2]
    widths = (POOL_W, ATTN_W, 4 * KV_W, 2 * FOUR_W)
    return pl.pallas_call(
        functools.partial(_in_body, rope=rope, per_batch=per_batch),
        grid=(nb, seq // tm),
        in_specs=in_specs,
        out_specs=[pl.BlockSpec((1, tm, w), lambda b, i: (b, i, 0)) for w in widths],
        out_shape=[jax.ShapeDtypeStruct((nb, seq, w), BF16) for w in widths],
        compiler_params=_cparams(2),
        name="in_proj_rope" if rope else "in_proj_ctx",
    )(*args)


_NT = (((1,), (1,)), ((), ()))


def _stack_heads(qpair0, qpair1, lo):
    z = jnp.zeros_like(qpair0)
    parts = [jnp.where(lo, qpair0, z), jnp.where(lo, qpair1, z),
             jnp.where(lo, pltpu.roll(qpair0, 64, 1), z), jnp.where(lo, pltpu.roll(qpair1, 64, 1), z)]
    return jnp.concatenate(parts, axis=0).astype(BF16)


def _group_attention(q4, k_parts, va_parts, vb_parts, masks, sink_col):
    s_parts = []
    for kz, mk in zip(k_parts, masks):
        s = lax.dot_general(q4, kz, _NT, preferred_element_type=F32)
        if mk is not None:
            nk = s.shape[1]
            s = jnp.where(mk[None], s.reshape(4, 128, nk), NEG_INF).reshape(512, nk)
        s_parts.append(s)
    m = functools.reduce(jnp.maximum, [jnp.max(s, axis=-1, keepdims=True) for s in s_parts])
    m = jnp.maximum(m, sink_col)
    den = jnp.exp2(sink_col - m)
    oe = oo = None
    for s, va, vb in zip(s_parts, va_parts, vb_parts):
        e = jnp.exp2(s - m)
        den = den + jnp.sum(e, axis=-1, keepdims=True)
        eb = e.astype(BF16)
        pe = jnp.dot(eb[0:256], va, preferred_element_type=F32)
        po = jnp.dot(eb[256:512], vb, preferred_element_type=F32)
        oe = pe if oe is None else oe + pe
        oo = po if oo is None else oo + po
    inv = 1.0 / den
    return oe * inv[0:256], oo * inv[256:512]


def _sink_cols(sink_ref):
    rb = lax.broadcasted_iota(I32, (512, 1), 0) >> 7
    cols = []
    for kvh in range(2):
        s = [sink_ref[4 * kvh + j] * LOG2_E for j in (0, 2, 1, 3)]
        cols.append(jnp.where(rb == 0, s[0], jnp.where(rb == 1, s[1], jnp.where(rb == 2, s[2], s[3]))))
    return cols


def _attend_block(qblk, kv_parts, masks, sink_cols):
    lo = lax.broadcasted_iota(I32, (128, 128), 1) < 64
    cols = []
    for kvh in range(2):
        q4 = _stack_heads(qblk[:, 256 * kvh:256 * kvh + 128], qblk[:, 256 * kvh + 128:256 * kvh + 256], lo)
        ko = 128 * kvh
        vao = 256 + 128 * kvh
        vbo = 384 - 128 * kvh
        oe, oo = _group_attention(q4, [kv[:, ko:ko + 128] for kv in kv_parts],
                                  [kv[:, vao:vao + 128] for kv in kv_parts],
                                  [kv[:, vbo:vbo + 128] for kv in kv_parts], masks, sink_cols[kvh])
        cols.append(jnp.where(lo, oe[0:128], oo[0:128]))
        cols.append(jnp.where(lo, oe[128:256], oo[128:256]))
    return jnp.concatenate(cols, axis=1).astype(BF16)


def _attn_lat_body(sink_ref, q_ref, kvp_ref, kvm_ref, kvn_ref, kvc_ref, o_ref, kvw_ref, *, tq, seq):
    i = pl.program_id(1)
    kvw_ref[0:128] = kvp_ref[0]
    kvw_ref[128:128 + tq] = kvm_ref[0]
    kvw_ref[128 + tq:256 + tq] = kvn_ref[0]
    kvc = kvc_ref[0]
    sink_cols = _sink_cols(sink_ref)

    def sub(j, carry):
        r0 = pl.multiple_of(j * 128, 128)
        win = kvw_ref[pl.ds(r0, 3 * 128), :]
        ii = lax.broadcasted_iota(I32, (128, 3 * 128), 0)
        cc = lax.broadcasted_iota(I32, (128, 3 * 128), 1)
        base = i * tq + j * 128 - 128
        valid = (ii <= cc) & (cc <= ii + 2 * WINDOW) & (cc >= -base) & (cc < seq - base)
        qblk = q_ref[0, pl.ds(r0, 128), :].astype(F32)
        o_ref[0, pl.ds(r0, 128), :] = _attend_block(qblk, [win, kvc], [valid, None], sink_cols)
        return carry

    for j in range(tq // 128):
        sub(j, 0)


def _attn_lat_call(sink, q, kv, kvc):
    nb, seq, _ = q.shape
    lc = kvc.shape[1]
    tq = TQ
    nblk = seq // 128
    r = tq // 128
    return pl.pallas_call(
        functools.partial(_attn_lat_body, tq=tq, seq=seq),
        grid=(nb, seq // tq),
        in_specs=[pl.BlockSpec(memory_space=pltpu.SMEM),
                  pl.BlockSpec((1, tq, ATTN_W), lambda b, i: (b, i, 0)),
                  pl.BlockSpec((1, 128, 4 * KV_W), lambda b, i: (b, jnp.maximum(i * r - 1, 0), 0)),
                  pl.BlockSpec((1, tq, 4 * KV_W), lambda b, i: (b, i, 0)),
                  pl.BlockSpec((1, 128, 4 * KV_W), lambda b, i: (b, jnp.minimum((i + 1) * r, nblk - 1), 0)),
                  pl.BlockSpec((1, lc, 4 * KV_W), lambda b, i: (b, 0, 0))],
        out_specs=pl.BlockSpec((1, tq, ATTN_W), lambda b, i: (b, i, 0)),
        out_shape=jax.ShapeDtypeStruct((nb, seq, ATTN_W), BF16),
        scratch_shapes=[pltpu.VMEM((tq + 256, 4 * KV_W), BF16)],
        compiler_params=_cparams(2),
        name="attention_window",
    )(sink, q, kv, kv, kv, kvc)


def _attn_ctx_body(sink_ref, q_ref, kvc_ref, o_ref, *, lc):
    kvc = kvc_ref[0]
    sink_cols = _sink_cols(sink_ref)
    for j in range(lc // 128):
        qblk = q_ref[0, j * 128:(j + 1) * 128, :].astype(F32)
        o_ref[0, j * 128:(j + 1) * 128, :] = _attend_block(qblk, [kvc], [None], sink_cols)


def _attn_ctx_call(sink, qc, kvc):
    nb, lc, _ = qc.shape
    return pl.pallas_call(
        functools.partial(_attn_ctx_body, lc=lc),
        grid=(nb,),
        in_specs=[pl.BlockSpec(memory_space=pltpu.SMEM),
                  pl.BlockSpec((1, lc, ATTN_W), lambda b: (b, 0, 0)),
                  pl.BlockSpec((1, lc, 4 * KV_W), lambda b: (b, 0, 0))],
        out_specs=pl.BlockSpec((1, lc, ATTN_W), lambda b: (b, 0, 0)),
        out_shape=jax.ShapeDtypeStruct((nb, lc, ATTN_W), BF16),
        compiler_params=_cparams(1),
        name="attention_ctx",
    )(sink, qc, kvc)


def _four_body(ua_ref, cb_ref, sb_ref, ca_ref, sa_ref, o_ref, *, nb, scale):
    i = pl.program_id(0)
    ca = ca_ref[pl.ds(i, 1), :]
    sa = sa_ref[pl.ds(i, 1), :]
    cb = cb_ref[...]
    sb = sb_ref[...]
    ct = (ca * cb - sa * sb).astype(BF16)
    st = (sa * cb + ca * sb).astype(BF16)
    for b in range(nb):
        ua = ua_ref[b, :, 0:FOUR_W]
        ub = ua_ref[b, :, FOUR_W:2 * FOUR_W]
        r = (jnp.dot(ct, ua, preferred_element_type=F32) - jnp.dot(st, ub, preferred_element_type=F32))
        o_ref[b] = (r * scale).astype(BF16)


def _four_call(uaub, cb, sb, ca, sa):
    nb, seq, _ = uaub.shape
    tm = cb.shape[0]
    one = pl.Buffered(1)
    return pl.pallas_call(
        functools.partial(_four_body, nb=nb, scale=float(seq) ** -0.5),
        grid=(seq // tm,),
        in_specs=[pl.BlockSpec((nb, seq, 2 * FOUR_W), lambda i: (0, 0, 0), pipeline_mode=one),
                  pl.BlockSpec((tm, seq), lambda i: (0, 0), pipeline_mode=one),
                  pl.BlockSpec((tm, seq), lambda i: (0, 0), pipeline_mode=one),
                  pl.BlockSpec((seq // tm, seq), lambda i: (0, 0), pipeline_mode=one),
                  pl.BlockSpec((seq // tm, seq), lambda i: (0, 0), pipeline_mode=one)],
        out_specs=pl.BlockSpec((nb, tm, FOUR_W), lambda i: (0, i, 0)),
        out_shape=jax.ShapeDtypeStruct((nb, seq, FOUR_W), BF16),
        compiler_params=_cparams(1),
        name="fourier_dft",
    )(uaub, cb, sb, ca, sa)


def _dft_tables(seq, tm):
    n = np.arange(seq)[None, :]

    def tab(rows):
        ang = ((rows[:, None] * n) % seq) * (2.0 * np.pi / seq)
        return jnp.asarray(np.cos(ang), F32), jnp.asarray(np.sin(ang), F32)

    cb, sb = tab(np.arange(tm))
    ca, sa = tab(np.arange(seq // tm) * tm)
    return cb, sb, ca, sa


def _four1_body(x_ref, g_ref, yr_ref, yi_ref):
    g = g_ref[...].astype(BF16)
    n = FOUR_R1 * 16
    w = FOUR_W
    for aa in range(x_ref.shape[2]):
        x = x_ref[0, :, aa].reshape(n, 2 * w)
        p = jnp.dot(g, x, preferred_element_type=F32)
        yr = p[0:n, 0:w] - p[n:2 * n, w:2 * w]
        yi = -(p[0:n, w:2 * w] + p[n:2 * n, 0:w])
        yr_ref[0, :, aa] = yr.astype(BF16).reshape(FOUR_R1, 16, w)
        yi_ref[0, :, aa] = yi.astype(BF16).reshape(FOUR_R1, 16, w)


def _four1_call(uaub, g):
    nb, seq, _ = uaub.shape
    r2 = seq // FOUR_R1
    na = r2 // 16
    x = uaub.reshape(nb, FOUR_R1, na, 16, 2 * FOUR_W)
    out = jax.ShapeDtypeStruct((nb, FOUR_R1, na, 16, FOUR_W), BF16)
    ac = FOUR1_A_CHUNK
    return pl.pallas_call(
        _four1_body,
        grid=(nb, na // ac),
        in_specs=[pl.BlockSpec((1, FOUR_R1, ac, 16, 2 * FOUR_W), lambda b, j: (b, 0, j, 0, 0)),
                  pl.BlockSpec(g.shape, lambda b, j: (0, 0))],
        out_specs=[pl.BlockSpec((1, FOUR_R1, ac, 16, FOUR_W), lambda b, j: (b, 0, j, 0, 0))] * 2,
        out_shape=[out, out],
        compiler_params=_cparams(2),
        name="fourier_stage1",
    )(x, g)


def _four2_body(yr_ref, yi_ref, c_ref, s_ref, ca_ref, sa_ref, o_ref, obuf, sem, *, nb, scale):
    k1 = pl.program_id(0)
    nk = pl.num_programs(0)
    ca = ca_ref[pl.ds(k1, 1), :]
    sa = sa_ref[pl.ds(k1, 1), :]
    c = c_ref[...]
    s = s_ref[...]
    gc = (c * ca - s * sa).astype(BF16)
    gs = (s * ca + c * sa).astype(BF16)

    def out_copy(slot, b, kk):
        return pltpu.make_async_copy(obuf.at[slot, b], o_ref.at[b, :, kk, :], sem.at[slot])

    for slot in range(2):
        @pl.when(k1 % 2 == slot)
        def _():
            @pl.when(k1 >= 2)
            def _():
                for b in range(nb):
                    out_copy(slot, b, 0).wait()

            for b in range(nb):
                acc = (jnp.dot(gc, yr_ref[b, 0], preferred_element_type=F32)
                       + jnp.dot(gs, yi_ref[b, 0], preferred_element_type=F32))
                obuf[slot, b] = acc * scale
            for b in range(nb):
                out_copy(slot, b, k1).start()

    @pl.when(k1 == nk - 1)
    def _():
        for slot in range(2):
            for b in range(nb):
                out_copy(slot, b, 0).wait()


def _four2_call(yr, yi, c, s, ca, sa):
    nb = yr.shape[0]
    r2 = c.shape[0]
    seq = FOUR_R1 * r2
    yr4 = yr.reshape(nb, FOUR_R1, r2, FOUR_W)
    yi4 = yi.reshape(nb, FOUR_R1, r2, FOUR_W)
    full = lambda shape: pl.BlockSpec(shape, lambda k: (0,) * len(shape))
    out = pl.pallas_call(
        functools.partial(_four2_body, nb=nb, scale=float(seq) ** -0.5),
        grid=(FOUR_R1,),
        in_specs=[pl.BlockSpec((nb, 1, r2, FOUR_W), lambda k: (0, k, 0, 0)),
                  pl.BlockSpec((nb, 1, r2, FOUR_W), lambda k: (0, k, 0, 0)),
                  full((r2, r2)), full((r2, r2)), full((FOUR_R1, r2)), full((FOUR_R1, r2))],
        out_specs=pl.BlockSpec(memory_space=pl.ANY),
        out_shape=jax.ShapeDtypeStruct((nb, r2, FOUR_R1, FOUR_W), F32),
        scratch_shapes=[pltpu.VMEM((2, nb, r2, FOUR_W), F32), pltpu.SemaphoreType.DMA((2,))],
        compiler_params=_cparams(1),
        name="fourier_stage2",
    )(yr4, yi4, c, s, ca, sa)
    return out.reshape(nb, seq, FOUR_W)


def _four2_tables(seq):
    r1 = FOUR_R1
    r2 = seq // r1
    assert r1 * r2 == seq and r2 % 16 == 0
    k = np.arange(r1)
    ang1 = ((k[:, None] * k[None, :]) % r1) * (2.0 * np.pi / r1)
    eye = np.eye(16)
    g = np.concatenate([np.kron(np.cos(ang1), eye), np.kron(np.sin(ang1), eye)], axis=0)
    m = np.arange(r2)
    ang2 = ((m[:, None] * m[None, :]) % r2) * (2.0 * np.pi / r2)
    alpha = ((k[:, None] * m[None, :]) % seq) * (2.0 * np.pi / seq)
    f = lambda t: jnp.asarray(t, F32)
    return f(g), (f(np.cos(ang2)), f(np.sin(ang2)), f(np.cos(alpha)), f(np.sin(alpha)))


def _route(lt, tri_ref, cnt_ref, tm):
    rowi = lax.broadcasted_iota(I32, (32, tm), 0)
    big = jnp.int32(999)

    def first_argmax(vals):
        mx = jnp.max(vals, axis=0, keepdims=True)
        return mx, jnp.min(jnp.where(vals == mx, rowi, big), axis=0, keepdims=True)

    is_grp = rowi < 4
    mg, gi = first_argmax(jnp.where(is_grp, lt, NEG_INF))
    pg = 1.0 / jnp.sum(jnp.where(is_grp, jnp.exp(jnp.where(is_grp, lt, mg) - mg), 0.0), axis=0, keepdims=True)
    est = 4 + 4 * gi
    le = jnp.where((rowi >= est) & (rowi < est + 4), lt, NEG_INF)
    m1, i1 = first_argmax(le)
    m2, i2 = first_argmax(jnp.where(rowi == i1, NEG_INF, le))
    e2 = jnp.exp(m2 - m1)
    w1 = pg / (1.0 + e2)
    w2 = pg * e2 / (1.0 + e2)
    a1 = i1 - est
    a2 = i2 - est
    code = jnp.minimum(a1, a2) * 4 + jnp.maximum(a1, a2)
    pidx = jnp.where(code == 1, 0, jnp.where(code == 6, 1, jnp.where(code == 2, 2,
           jnp.where(code == 3, 3, jnp.where(code == 7, 4, 5)))))
    slot_a = jnp.where(pidx == 0, 0, jnp.where(pidx <= 2, 2, 3))
    slot_b = jnp.where(pidx <= 1, 1, jnp.where(pidx <= 3, 0, jnp.where(pidx == 4, 1, 2)))
    wa = jnp.where(a1 == slot_a, w1, w2)
    wb = jnp.where(a1 == slot_b, w1, w2)
    bin_ = gi * 6 + pidx

    onehot = rowi == bin_
    pref = jnp.dot(onehot.astype(BF16), tri_ref[...], preferred_element_type=F32)
    carry = cnt_ref[:, 0:1]
    rank = jnp.sum(jnp.where(onehot, pref - 1.0 + carry, 0.0), axis=0, keepdims=True)
    cnt_ref[...] = jnp.broadcast_to(carry + pref[:, tm - 1:tm], cnt_ref.shape)
    return jnp.concatenate([bin_.astype(F32), rank, wa, wb, jnp.zeros((128 - 4, tm), F32)], axis=0)


def _out_body(x_ref, pup_ref, pum_ref, pun_ref, at_ref, fo_ref, wo_ref, pw_ref, ps_ref, band_ref, icnt_ref,
              mod_ref, g2_ref, w2_ref, br_ref, tri_ref, cin_ref,
              xo_ref, rows_ref, cnt_ref, *, tm, per_batch):
    b = pl.program_id(0)
    i = pl.program_id(1)
    nt = pl.num_programs(1)
    row = b if per_batch else CTX_ROW
    g1 = mod_ref[pl.ds(row, 1), pl.ds(2 * D, D)]
    sh2 = mod_ref[pl.ds(row, 1), pl.ds(3 * D, D)]
    sc2 = mod_ref[pl.ds(row, 1), pl.ds(4 * D, D)]

    @pl.when((b == 0) & (i == 0))
    def _():
        cnt_ref[...] = cin_ref[...]

    um = pum_ref[0]
    zh = jnp.zeros((HALO, POOL_W), BF16)
    uext = jnp.concatenate([jnp.where(i > 0, pup_ref[0], zh), um, jnp.where(i < nt - 1, pun_ref[0], zh)], axis=0)
    grp = lax.broadcasted_iota(I32, (POOL_CHUNK, POOL_W), 1) >> 6
    chunks = []
    for c in range(tm // POOL_CHUNK):
        uc = uext[POOL_CHUNK * c:POOL_CHUNK * (c + 1) + 2 * HALO]
        pc = jnp.zeros((POOL_CHUNK, POOL_W), F32)
        for g in range(len(POOL_WINDOWS)):
            pc = jnp.where(grp == g, jnp.dot(band_ref[g], uc, preferred_element_type=F32), pc)
        chunks.append(pc)
    pooled = jnp.concatenate(chunks, axis=0)
    y = pooled * icnt_ref[...] - um.astype(F32)
    pool_out = jnp.dot(y.astype(BF16), pw_ref[...], preferred_element_type=F32) * ps_ref[...]

    cat = jnp.concatenate([pool_out.astype(BF16), at_ref[0], fo_ref[0].astype(BF16)], axis=1)
    xm = x_ref[0] + g1 * jnp.dot(cat, wo_ref[...], preferred_element_type=F32)
    xo_ref[0] = xm

    ms = jnp.mean(xm * xm, axis=-1, keepdims=True)
    h2 = (xm * lax.rsqrt(ms + EPS) * g2_ref[...]) * (1.0 + sc2) + sh2

    hh = h2.astype(BF16)
    hl = (h2 - hh.astype(F32)).astype(BF16)
    w2 = w2_ref[...]
    p2 = jnp.dot(hh, w2, preferred_element_type=F32)
    logits = (p2[:, 0:128] + p2[:, 128:256] + jnp.dot(hl, w2[:, 0:128], preferred_element_type=F32) + br_ref[...])
    meta = _route(logits.T[0:32, :], tri_ref, cnt_ref, tm).T

    rows_ref[:, 0:D] = h2
    rows_ref[:, D:ROW_W] = meta


def _out_call(x3, pu, attn, four, w_out, pool_wbd, pool_scale, bands, icnt, mod, g2, w2, br, tri, cnt_in, *,
              per_batch):
    nb, seq, _ = x3.shape
    tm = min(TM_OUT, seq)
    nt = seq // tm
    hb = tm // HALO
    full = lambda shape: pl.BlockSpec(shape, lambda b, i: (0,) * len(shape))
    in_specs = [pl.BlockSpec((1, tm, D), lambda b, i: (b, i, 0)),
                pl.BlockSpec((1, HALO, POOL_W), lambda b, i: (b, jnp.maximum(i * hb - 1, 0), 0)),
                pl.BlockSpec((1, tm, POOL_W), lambda b, i: (b, i, 0)),
                pl.BlockSpec((1, HALO, POOL_W), lambda b, i: (b, jnp.minimum((i + 1) * hb, seq // HALO - 1), 0)),
                pl.BlockSpec((1, tm, ATTN_W), lambda b, i: (b, i, 0)),
                pl.BlockSpec((1, tm, FOUR_W), lambda b, i: (b, i, 0)),
                full((D, D)), full((POOL_W, POOL_W)), full((1, POOL_W)),
                full((len(POOL_WINDOWS), POOL_CHUNK, POOL_CHUNK + 2 * HALO)),
                pl.BlockSpec((tm, POOL_W), lambda b, i: (i, 0)),
                full((8, 6 * D)), full((1, D)), full((D, 256)), full((1, 128)), full((tm, tm)), full((32, 128))]
    args = [x3, pu, pu, pu, attn, four, w_out, pool_wbd, pool_scale, bands, icnt, mod, g2, w2, br, tri, cnt_in]
    return pl.pallas_call(
        functools.partial(_out_body, tm=tm, per_batch=per_batch),
        grid=(nb, nt),
        in_specs=in_specs,
        out_specs=[pl.BlockSpec((1, tm, D), lambda b, i: (b, i, 0)),
                   pl.BlockSpec((tm, ROW_W), lambda b, i: (b * nt + i, 0)),
                   pl.BlockSpec((32, 128), lambda b, i: (0, 0))],
        out_shape=[jax.ShapeDtypeStruct((nb, seq, D), F32),
                   jax.ShapeDtypeStruct((nb * seq, ROW_W), F32),
                   jax.ShapeDtypeStruct((32, 128), F32)],
        compiler_params=_cparams(2),
        name="out_proj_router" if per_batch else "out_proj_router_ctx",
    )(*args)


def _pool_tables(seq):
    t = np.arange(POOL_CHUNK)[:, None]
    s = np.arange(POOL_CHUNK + 2 * HALO)[None, :] - HALO
    bands = np.stack([(s >= t - w // 2) & (s <= t + w // 2 - 1) for w in POOL_WINDOWS]).astype(np.float32)
    pos = np.arange(seq)
    icnt = np.stack([1.0 / (np.minimum(pos + w // 2 - 1, seq - 1) - np.maximum(pos - w // 2, 0) + 1)
                     for w in POOL_WINDOWS], axis=1)
    return jnp.asarray(bands, BF16), jnp.asarray(np.repeat(icnt, POOL_W // len(POOL_WINDOWS), axis=1), F32)


def _row_copies(tm, make_copy):
    for r in range(tm):
        make_copy(r).start(priority=r % 2)


def _row_waits(tm, make_copy):
    def drain(r, c):
        make_copy(0).wait()
        return c

    lax.fori_loop(0, tm, drain, 0, unroll=8)


def _zero_fill(ends_ref, nv_ref, xs_ref, zbuf, zsem, tm, n_out, wait):
    def piece(off, size):
        return pltpu.make_async_copy(zbuf.at[pl.ds(0, size)], xs_ref.at[pl.ds(off, size)], zsem)

    def run(cond, off, size):
        @pl.when(cond)
        def _():
            c = piece(off, size)
            c.wait() if wait else c.start()

    for b in range(N_BINS):
        off = ends_ref[b]
        pad = (tm - (off & (tm - 1))) & (tm - 1)
        for k in range(tm.bit_length() - 1):
            run(((pad >> k) & 1) == 1, off, 1 << k)
            off = off + (pad & (1 << k))

    def tail(t, c):
        c_ = piece(t * tm, tm)
        c_.wait() if wait else c_.start()
        return c

    lax.fori_loop(nv_ref[0], n_out, tail, 0)


def _scatter_body(dest_ref, ends_ref, nv_ref, *refs, tm, n_tiles, n_first, n_out):
    n_h = len(refs) - 3 - 3 * SCATTER_SLOTS
    h_refs, xs_ref = refs[:n_h], refs[n_h]
    bufs = refs[n_h + 1:n_h + 1 + SCATTER_SLOTS]
    lsems = refs[n_h + 1 + SCATTER_SLOTS:n_h + 1 + 2 * SCATTER_SLOTS]
    rsems = refs[n_h + 1 + 2 * SCATTER_SLOTS:n_h + 1 + 3 * SCATTER_SLOTS]
    zbuf, zsem = refs[-2:]
    zbuf[...] = jnp.zeros_like(zbuf)
    _zero_fill(ends_ref, nv_ref, xs_ref, zbuf, zsem, tm, n_out, wait=False)

    def lane_block_copy(h_ref, tt, slot, j):
        return pltpu.make_async_copy(h_ref.at[pl.ds(tt * tm, tm), pl.ds(128 * j, 128)],
                                     bufs[slot].at[:, j, :], lsems[slot])

    def load(t, slot):
        def start(h_ref, tt):
            for j in range(ROW_TILES):
                lane_block_copy(h_ref, tt, slot, j).start()

        if n_h == 1:
            start(h_refs[0], t)
        else:
            @pl.when(t < n_first)
            def _():
                start(h_refs[0], t)

            @pl.when(t >= n_first)
            def _():
                start(h_refs[1], t - n_first)

    def row_copy(slot, r, d):
        return pltpu.make_async_copy(bufs[slot].at[r], xs_ref.at[d], rsems[slot])

    load(0, 0)
    load(1, 1)

    def group(g, c):
        for slot in range(SCATTER_SLOTS):
            t = g * SCATTER_SLOTS + slot
            ahead = (slot + 2) % SCATTER_SLOTS
            for j in range(ROW_TILES):
                lane_block_copy(h_refs[0], 0, slot, j).wait()

            @pl.when(t >= 2)
            def _():
                _row_waits(tm, lambda r: row_copy(ahead, r, 0))

            @pl.when(t + 2 < n_tiles)
            def _():
                load(t + 2, ahead)

            _row_copies(tm, lambda r: row_copy(slot, r, dest_ref[t * tm + r]))
        return c

    lax.fori_loop(0, n_tiles // SCATTER_SLOTS, group, 0)
    for t in (n_tiles - 2, n_tiles - 1):
        _row_waits(tm, lambda r: row_copy(t % SCATTER_SLOTS, r, 0))
    _zero_fill(ends_ref, nv_ref, xs_ref, zbuf, zsem, tm, n_out, wait=True)


def _scatter_call(dest, bin_ends, n_valid, row_sets):
    tm = TM_ROW
    assert tm == TM_MOE
    n_first = row_sets[0].shape[0] // tm
    n_rows = sum(r.shape[0] for r in row_sets)
    n_tiles = n_rows // tm
    n_out = n_tiles + N_BINS
    assert n_tiles % SCATTER_SLOTS == 0 and n_tiles >= SCATTER_SLOTS
    return pl.pallas_call(
        functools.partial(_scatter_body, tm=tm, n_tiles=n_tiles, n_first=n_first, n_out=n_out),
        grid_spec=pltpu.PrefetchScalarGridSpec(
            num_scalar_prefetch=3,
            grid=(1,),
            in_specs=[pl.BlockSpec(memory_space=pl.ANY)] * len(row_sets),
            out_specs=pl.BlockSpec(memory_space=pl.ANY),
            scratch_shapes=([pltpu.VMEM((tm, ROW_TILES, 128), F32)] * SCATTER_SLOTS
                            + [pltpu.SemaphoreType.DMA(())] * (2 * SCATTER_SLOTS)
                            + [pltpu.VMEM((tm, ROW_TILES, 128), F32), pltpu.SemaphoreType.DMA(())])),
        out_shape=jax.ShapeDtypeStruct((n_out * tm, ROW_TILES, 128), F32),
        compiler_params=_cparams(1),
        name="moe_scatter_rows",
    )(dest, bin_ends, n_valid, *row_sets)


def _moe_body(ea_ref, eb_ref, nv_ref, xs_ref, wga, wua, wda, wgb, wub, wdb, ys_ref, xbuf0, xbuf1, sem0, sem1, *, tm):
    del ea_ref, eb_ref
    s = pl.program_id(0)
    nv = nv_ref[0]
    xbufs = (xbuf0, xbuf1)
    sems = (sem0, sem1)

    def lane_block_copy(t, slot, j):
        return pltpu.make_async_copy(xs_ref.at[pl.ds(t * tm, tm), j, :], xbufs[slot].at[:, pl.ds(128 * j, 128)],
                                     sems[slot])

    def fetch(t, slot):
        for j in range(ROW_TILES):
            lane_block_copy(t, slot, j).start()

    @pl.when(s == 0)
    def _():
        fetch(0, 0)

    @pl.when(s >= nv)
    def _():
        ys_ref[...] = jnp.zeros_like(ys_ref)

    for slot in range(2):
        @pl.when((s < nv) & (s % 2 == slot))
        def _():
            @pl.when(s + 1 < nv)
            def _():
                fetch(s + 1, 1 - slot)

            for j in range(ROW_TILES):
                lane_block_copy(0, slot, j).wait()
            xs = xbufs[slot]
            h = xs[:, 0:D].astype(BF16)
            meta = xs[:, D:ROW_W]

            def expert(wg, wu, wd, gate):
                g = jnp.dot(h, wg[0, 0].astype(BF16), preferred_element_type=F32)
                u = jnp.dot(h, wu[0, 0].astype(BF16), preferred_element_type=F32)
                a = _silu(g) * u * gate
                return jnp.dot(a.astype(BF16), wd[0, 0].astype(BF16), preferred_element_type=F32)

            ys_ref[...] = expert(wga, wua, wda, meta[:, 2:3]) + expert(wgb, wub, wdb, meta[:, 3:4])


def _moe_call(items, xs, w_gate, w_up, w_down, layer):
    tm = TM_MOE
    n_tiles = xs.shape[0] // tm
    wa = lambda s, ea, eb, nv: (layer, ea[s], 0, 0)
    wb = lambda s, ea, eb, nv: (layer, eb[s], 0, 0)
    up_spec = lambda f: pl.BlockSpec((1, 1, D, D_EXPERT), f)
    dn_spec = lambda f: pl.BlockSpec((1, 1, D_EXPERT, D), f)
    return pl.pallas_call(
        functools.partial(_moe_body, tm=tm),
        grid_spec=pltpu.PrefetchScalarGridSpec(
            num_scalar_prefetch=len(items),
            grid=(n_tiles,),
            in_specs=[pl.BlockSpec(memory_space=pl.ANY),
                      up_spec(wa), up_spec(wa), dn_spec(wa), up_spec(wb), up_spec(wb), dn_spec(wb)],
            out_specs=pl.BlockSpec((tm, D), lambda s, ea, eb, nv: (s, 0)),
            scratch_shapes=[pltpu.VMEM((tm, ROW_W), F32), pltpu.VMEM((tm, ROW_W), F32),
                            pltpu.SemaphoreType.DMA(()), pltpu.SemaphoreType.DMA(())]),
        out_shape=jax.ShapeDtypeStruct((xs.shape[0], D), F32),
        compiler_params=_cparams(1),
        name="moe_experts",
    )(*items, xs, w_gate, w_up, w_down, w_gate, w_up, w_down)


def _gather_body(dest_ref, x_ref, mod_ref, ys_ref, o_ref, ybuf0, ybuf1, sem0, sem1, *,
                 tm, seq, per_batch, dest_off):
    i = pl.program_id(0)
    n_tiles = pl.num_programs(0)
    ybufs = (ybuf0, ybuf1)
    sems = (sem0, sem1)
    row = (i * tm) // seq if per_batch else CTX_ROW
    g2 = mod_ref[pl.ds(row, 1), pl.ds(5 * D, D)]

    def row_copy(slot, r, d):
        return pltpu.make_async_copy(ys_ref.at[pl.ds(d, 1)], ybufs[slot].at[pl.ds(r, 1)], sems[slot])

    def fetch(t, slot):
        base = dest_off + t * tm
        _row_copies(tm, lambda r: row_copy(slot, r, dest_ref[base + r]))

    @pl.when(i == 0)
    def _():
        fetch(0, 0)

    for slot in range(2):
        @pl.when(i % 2 == slot)
        def _():
            @pl.when(i + 1 < n_tiles)
            def _():
                fetch(i + 1, 1 - slot)

            _row_waits(tm, lambda r: row_copy(slot, r, 0))
            o_ref[...] = x_ref[...] + g2 * ybufs[slot][...]


def _gather_call(dest, x2, mod, ys, *, seq, per_batch, dest_off):
    n = x2.shape[0]
    tm = TM_ROW
    return pl.pallas_call(
        functools.partial(_gather_body, tm=tm, seq=seq, per_batch=per_batch, dest_off=dest_off),
        grid_spec=pltpu.PrefetchScalarGridSpec(
            num_scalar_prefetch=1,
            grid=(n // tm,),
            in_specs=[pl.BlockSpec((tm, D), lambda i, *_: (i, 0)),
                      pl.BlockSpec((8, 6 * D), lambda i, *_: (0, 0)),
                      pl.BlockSpec(memory_space=pl.ANY)],
            out_specs=pl.BlockSpec((tm, D), lambda i, *_: (i, 0)),
            scratch_shapes=[pltpu.VMEM((tm, D), F32), pltpu.VMEM((tm, D), F32),
                            pltpu.SemaphoreType.DMA(()), pltpu.SemaphoreType.DMA(())]),
        out_shape=jax.ShapeDtypeStruct((n, D), F32),
        compiler_params=_cparams(1),
        name="moe_gather_rows" if per_batch else "moe_gather_rows_ctx",
    )(dest, x2, mod, ys)


def _routing_tables(bins, rank, cnt, n_rows):
    tm = TM_MOE
    counts = cnt[:N_BINS, 0].astype(I32)
    tiles = (counts + tm - 1) // tm
    tile_end = jnp.cumsum(tiles)
    starts = (tile_end - tiles) * tm
    ids = jnp.arange(N_BINS, dtype=I32)
    pick = lambda key, tab: jnp.sum(jnp.where(key[:, None] == ids[None, :], tab[None, :], 0), axis=1)
    dest = rank + pick(bins, starts)
    n_valid = tile_end[-1]
    n_tiles = n_rows // tm + N_BINS
    s = jnp.minimum(jnp.arange(n_tiles, dtype=I32), n_valid - 1)
    tbin = jnp.sum((s[:, None] >= tile_end[None, :]).astype(I32), axis=1)
    pidx = tbin % 6
    six = jnp.arange(6, dtype=I32)
    slot = lambda tab: jnp.sum(jnp.where(pidx[:, None] == six[None, :], jnp.asarray(tab, I32)[None, :], 0), axis=1)
    ea = 4 * (tbin // 6) + slot(PAIR_SLOT_A)
    eb = 4 * (tbin // 6) + slot(PAIR_SLOT_B)
    return dest, starts + counts, (ea, eb, n_valid.reshape(1))


def _block_diag(w):
    g, c, d = w.shape
    eye = jnp.asarray(np.eye(g), w.dtype)
    return (w[:, :, None, :] * eye[:, None, :, None]).reshape(g * c, g * d)


def _rope_tables(n_tokens):
    rows = n_tokens // GRID_W
    r = np.repeat(np.arange(rows), GRID_W).astype(np.float64)
    col = np.tile(np.arange(GRID_W), rows).astype(np.float64)
    half = HEAD_DIM // 2
    inv = 1.0 / (ROPE_BASE ** (np.arange(0, half, 2, dtype=np.float64) / half))
    ar = r[:, None] * inv
    ac = col[:, None] * inv
    ang = np.concatenate([ar, ar, ac, ac], axis=-1)
    sign = np.where((np.arange(HEAD_DIM) & 16) == 0, -1.0, 1.0)
    cos, sin = np.cos(ang), np.sin(ang) * sign
    return jnp.asarray(np.tile(cos, (1, 2)), F32), jnp.asarray(np.tile(sin, (1, 2)), F32)


def kernel(x, c, ctx, c_ctx, w_mod, b_mod, norm1_g, w_in, q_norm_g, k_norm_g, attn_sink, pool_w, pool_scale,
           four_w, w_out, norm2_g, w_grp, b_grp, w_rtr, b_rtr, w_gate, w_up, w_down):
    nb, seq, _ = x.shape
    lc = ctx.shape[1]
    depth = w_mod.shape[0]
    t_lat = nb * seq
    t_ctx = nb * lc

    cs = jnp.concatenate([c, c_ctx[None, :], jnp.zeros((8 - nb - 1, D), F32)], axis=0)
    m512 = jnp.asarray(np.kron(np.eye(N_HEADS), np.full((HEAD_DIM, HEAD_DIM), 1.0 / HEAD_DIM)), BF16)
    kk = np.arange(HEAD_DIM)
    ang64 = 2.0 * np.pi * ((kk[:, None] * kk[None, :]) % HEAD_DIM) / HEAD_DIM
    c64bd = jnp.asarray(np.kron(np.eye(4), np.cos(ang64)), F32)
    s64bd = jnp.asarray(np.kron(np.eye(4), np.sin(ang64)), F32)
    tri_of = lambda n: jnp.asarray(np.triu(np.ones((min(TM_OUT, n),) * 2)), BF16)
    cos2, sin2 = _rope_tables(seq)
    four_cs, four_tabs = _four2_tables(seq)
    tabs_ctx = _dft_tables(lc, min(TM_FOUR, lc))
    pool_lat = _pool_tables(seq)
    pool_ctx = _pool_tables(lc)
    w_in_b = w_in.astype(BF16)
    w_out_b = w_out.astype(BF16)
    wbd = jnp.stack([_block_diag(four_w[l]) for l in range(depth)])
    pool_wbd = jnp.stack([_block_diag(pool_w[l]) for l in range(depth)]).astype(BF16)
    wr = jnp.concatenate([w_grp, w_rtr, jnp.zeros((depth, D, 128 - 4 - N_EXPERTS), F32)], axis=2)
    wr_hi = wr.astype(BF16)
    w2 = jnp.concatenate([wr_hi, (wr - wr_hi.astype(F32)).astype(BF16)], axis=2)
    br = jnp.concatenate([b_grp, b_rtr, jnp.zeros((depth, 128 - 4 - N_EXPERTS), F32)], axis=1)

    mod_all = _mod_call(cs, w_mod, b_mod)
    ab_all = _ab_call(c64bd, s64bd, wbd)

    xc = ctx
    for l in range(depth):
        last = l == depth - 1
        mod = mod_all[l]
        g1 = norm1_g[l][None, :]
        g2 = norm2_g[l][None, :]
        qg = jnp.tile(q_norm_g[l], N_HEADS)[None, :]
        kg = jnp.tile(k_norm_g[l], KV_W // HEAD_DIM)[None, :]
        in_args = (mod, g1, w_in_b[l], m512, qg, kg, ab_all[l])

        puc, qc, kvc, uac = _in_call(xc, *in_args, None, None, rope=False, per_batch=False)
        pul, ql, kvl, ual = _in_call(x, *in_args, cos2, sin2, rope=True, per_batch=True)
        attn_l = _attn_lat_call(attn_sink[l], ql, kvl, kvc)
        four_l = _four2_call(*_four1_call(ual, four_cs), *four_tabs)

        proj = (w_out_b[l], pool_wbd[l], pool_scale[l][None, :])
        rout = (mod, g2, w2[l], br[l][None, :])
        cnt0 = jnp.zeros((32, 128), F32)
        x_mid, rows_l, cnt = _out_call(x, pul, attn_l, four_l, *proj, *pool_lat, *rout, tri_of(seq), cnt0,
                                       per_batch=True)
        row_sets = [rows_l]
        if not last:
            attn_c = _attn_ctx_call(attn_sink[l], qc, kvc)
            four_c = _four_call(uac, *tabs_ctx)
            xc_mid, rows_c, cnt = _out_call(xc, puc, attn_c, four_c, *proj, *pool_ctx, *rout, tri_of(lc), cnt,
                                            per_batch=False)
            row_sets.append(rows_c)

        route = [r[:, D:D + 2].astype(I32) for r in row_sets]
        bins = jnp.concatenate([r[:, 0] for r in route])
        rank = jnp.concatenate([r[:, 1] for r in route])
        dest, bin_ends, items = _routing_tables(bins, rank, cnt, bins.shape[0])
        xs = _scatter_call(dest, bin_ends, items[2], row_sets)
        ys = _moe_call(items, xs, w_gate, w_up, w_down, l)
        x = _gather_call(dest, x_mid.reshape(t_lat, D), mod, ys, seq=seq, per_batch=True,
                         dest_off=0).reshape(nb, seq, D)
        if not last:
            xc = _gather_call(dest, xc_mid.reshape(t_ctx, D), mod, ys, seq=lc, per_batch=False,
                              dest_off=t_lat).reshape(nb, lc, D)
    return x
```

```python
import functools

import numpy as np
import jax
import jax.numpy as jnp
from jax import lax
from jax.experimental import pallas as pl
from jax.experimental.pallas import tpu as pltpu

F32 = jnp.float32
BF16 = jnp.bfloat16
I32 = jnp.int32
HI = lax.Precision.HIGHEST

D = 1024
HEAD_DIM = 64
N_HEADS = 8
GRID_W = 64
POOL_WINDOWS = (2, 4, 8, 16)
POOL_W = 256
ATTN_W = 512
KV_W = 128
FOUR_W = 256
IN_W = 1280
N_EXPERTS = 16
D_EXPERT = 512
WINDOW = 128
ROPE_BASE = 10000.0
EPS = 1e-6
NEG_INF = -1e30
LOG2_E = 1.4426950408889634
CTX_ROW = 4
N_BINS = 24
PAIR_SLOT_A = (0, 2, 2, 3, 3, 3)
PAIR_SLOT_B = (1, 1, 0, 0, 1, 2)
META_W = 128
ROW_W = D + META_W
ROW_TILES = ROW_W // 128
HALO = 16
POOL_CHUNK = 128

VMEM_LIMIT = 56 * 1024 * 1024
TM_IN = 1024
TQ = 1024
TM_OUT = 1024
TM_FOUR = 256
FOUR_R1 = 16
FOUR1_A_CHUNK = 8
TM_MOE = 256
TM_ROW = 256
TM_GATHER = 512
SCATTER_SLOTS = 4
MOD_TN = 1024


def _cparams(n_axes):
    return pltpu.CompilerParams(dimension_semantics=("arbitrary",) * n_axes,
                                vmem_limit_bytes=VMEM_LIMIT)


def _silu(v):
    return v / (1.0 + jnp.exp(-v))


def _mod_body(cs_ref, w_ref, b_ref, o_ref):
    s = _silu(cs_ref[...])
    w = w_ref[0]
    s_hi = s.astype(BF16)
    s_lo = (s - s_hi.astype(F32)).astype(BF16)
    w_hi = w.astype(BF16)
    w_lo = (w - w_hi.astype(F32)).astype(BF16)
    p = jnp.dot(jnp.concatenate([s_hi, s_lo], axis=0), w_hi, preferred_element_type=F32)
    o_ref[0] = p[0:8] + p[8:16] + jnp.dot(s_hi, w_lo, preferred_element_type=F32) + b_ref[0]


def _mod_call(cs, w_mod, b_mod):
    depth = w_mod.shape[0]
    return pl.pallas_call(
        _mod_body,
        grid=(depth, 6 * D // MOD_TN),
        in_specs=[pl.BlockSpec((8, D), lambda l, j: (0, 0)),
                  pl.BlockSpec((1, D, MOD_TN), lambda l, j: (l, 0, j)),
                  pl.BlockSpec((1, 1, MOD_TN), lambda l, j: (l, 0, j))],
        out_specs=pl.BlockSpec((1, 8, MOD_TN), lambda l, j: (l, 0, j)),
        out_shape=jax.ShapeDtypeStruct((depth, 8, 6 * D), F32),
        compiler_params=_cparams(2),
        name="modulation",
    )(cs, w_mod, b_mod.reshape(depth, 1, 6 * D))


def _ab_body(c_ref, s_ref, w_ref, o_ref):
    w = w_ref[0]
    ca = jnp.dot(c_ref[...], w, preferred_element_type=F32, precision=HI)
    sa = jnp.dot(s_ref[...], w, preferred_element_type=F32, precision=HI)
    o_ref[0] = (jnp.concatenate([ca, sa], axis=1) * (HEAD_DIM ** -0.5)).astype(BF16)


def _ab_call(c64bd, s64bd, wbd):
    depth = wbd.shape[0]
    return pl.pallas_call(
        _ab_body,
        grid=(depth,),
        in_specs=[pl.BlockSpec((FOUR_W, FOUR_W), lambda l: (0, 0)),
                  pl.BlockSpec((FOUR_W, FOUR_W), lambda l: (0, 0)),
                  pl.BlockSpec((1, FOUR_W, FOUR_W), lambda l: (l, 0, 0))],
        out_specs=pl.BlockSpec((1, FOUR_W, 2 * FOUR_W), lambda l: (l, 0, 0)),
        out_shape=jax.ShapeDtypeStruct((depth, FOUR_W, 2 * FOUR_W), BF16),
        compiler_params=_cparams(1),
        name="fourier_weights",
    )(c64bd, s64bd, wbd)


def _head_rms(t, m, g):
    ms = jnp.dot((t * t).astype(BF16), m, preferred_element_type=F32)
    return t * lax.rsqrt(ms + EPS) * g


def _rope(t, cos, sin_signed):
    w = t.shape[1]
    lane = lax.broadcasted_iota(I32, t.shape, 1)
    fwd = pltpu.roll(t, w - 16, 1)
    bwd = pltpu.roll(t, 16, 1)
    rot = jnp.where((lane & 16) == 0, fwd, bwd)
    return t * cos + rot * sin_signed


def _in_body(*refs, rope, per_batch):
    if rope:
        (x_ref, mod_ref, g1_ref, w_ref, m_ref, qg_ref, kg_ref, ab_ref, cos_ref, sin_ref,
         pu_ref, q_ref, kv_ref, ua_ref) = refs
    else:
        (x_ref, mod_ref, g1_ref, w_ref, m_ref, qg_ref, kg_ref, ab_ref,
         pu_ref, q_ref, kv_ref, ua_ref) = refs
    row = pl.program_id(0) if per_batch else CTX_ROW
    sh1 = mod_ref[pl.ds(row, 1), pl.ds(0, D)]
    sc1 = mod_ref[pl.ds(row, 1), pl.ds(D, D)]
    x = x_ref[0]
    ms = jnp.mean(x * x, axis=-1, keepdims=True)
    h = (x * lax.rsqrt(ms + EPS) * g1_ref[...]) * (1.0 + sc1) + sh1
    p = jnp.dot(h.astype(BF16), w_ref[...], preferred_element_type=F32)
    pu = p[:, 0:256]
    q = p[:, 256:768]
    k = p[:, 768:896]
    v = p[:, 896:1024]
    fu = p[:, 1024:1280]
    m = m_ref[...]
    q = _head_rms(q, m, qg_ref[...])
    k = _head_rms(k, m[0:KV_W, 0:KV_W], kg_ref[...])
    if rope:
        cos = cos_ref[...]
        sin = sin_ref[...]
        q = _rope(q, jnp.concatenate([cos] * 4, axis=1), jnp.concatenate([sin] * 4, axis=1))
        k = _rope(k, cos, sin)
    q = q * (HEAD_DIM ** -0.5 * LOG2_E)
    pu_ref[0] = pu.astype(BF16)
    q_ref[0] = q.astype(BF16)
    kv_ref[0] = jnp.concatenate([k, pltpu.roll(k, 64, 1), v, pltpu.roll(v, 64, 1)], axis=1).astype(BF16)
    ua_ref[0] = jnp.dot(fu.astype(BF16), ab_ref[...], preferred_element_type=F32).astype(BF16)


def _in_call(x3, mod, g1, w_in, m512, qg, kg, ab, cos2, sin2, *, rope, per_batch):
    nb, seq, _ = x3.shape
    tm = min(TM_IN, seq)
    full = lambda shape: pl.BlockSpec(shape, lambda b, i: (0,) * len(shape))
    in_specs = [pl.BlockSpec((1, tm, D), lambda b, i: (b, i, 0)),
                full((8, 6 * D)), full((1, D)), full((D, IN_W)), full((ATTN_W, ATTN_W)),
                full((1, ATTN_W)), full((1, KV_W)), full((FOUR_W, 2 * FOUR_W))]
    args = [x3, mod, g1, w_in, m512, qg, kg, ab]
    if rope:
        in_specs += [pl.BlockSpec((tm, 128), lambda b, i: (i, 0)),
                     pl.BlockSpec((tm, 128), lambda b, i: (i, 0))]
        args += [cos2, sin2]
    widths = (POOL_W, ATTN_W, 4 * KV_W, 2 * FOUR_W)
    return pl.pallas_call(
        functools.partial(_in_body, rope=rope, per_batch=per_batch),
        grid=(nb, seq // tm),
        in_specs=in_specs,
        out_specs=[pl.BlockSpec((1, tm, w), lambda b, i: (b, i, 0)) for w in widths],
        out_shape=[jax.ShapeDtypeStruct((nb, seq, w), BF16) for w in widths],
        compiler_params=_cparams(2),
        name="in_proj_rope" if rope else "in_proj_ctx",
    )(*args)


_NT = (((1,), (1,)), ((), ()))


def _stack_heads(qpair0, qpair1, lo):
    z = jnp.zeros_like(qpair0)
    parts = [jnp.where(lo, qpair0, z), jnp.where(lo, qpair1, z),
             jnp.where(lo, pltpu.roll(qpair0, 64, 1), z), jnp.where(lo, pltpu.roll(qpair1, 64, 1), z)]
    return jnp.concatenate(parts, axis=0).astype(BF16)


def _group_attention(q4, k_parts, va_parts, vb_parts, masks, sink_col):
    s_parts = []
    for kz, mk in zip(k_parts, masks):
        s = lax.dot_general(q4, kz, _NT, preferred_element_type=F32)
        if mk is not None:
            nk = s.shape[1]
            s = jnp.where(mk[None], s.reshape(4, 128, nk), NEG_INF).reshape(512, nk)
        s_parts.append(s)
    m = functools.reduce(jnp.maximum, [jnp.max(s, axis=-1, keepdims=True) for s in s_parts])
    m = jnp.maximum(m, sink_col)
    den = jnp.exp2(sink_col - m)
    oe = oo = None
    for s, va, vb in zip(s_parts, va_parts, vb_parts):
        e = jnp.exp2(s - m)
        den = den + jnp.sum(e, axis=-1, keepdims=True)
        eb = e.astype(BF16)
        pe = jnp.dot(eb[0:256], va, preferred_element_type=F32)
        po = jnp.dot(eb[256:512], vb, preferred_element_type=F32)
        oe = pe if oe is None else oe + pe
        oo = po if oo is None else oo + po
    inv = 1.0 / den
    return oe * inv[0:256], oo * inv[256:512]


def _sink_cols(sink_ref):
    rb = lax.broadcasted_iota(I32, (512, 1), 0) >> 7
    cols = []
    for kvh in range(2):
        s = [sink_ref[4 * kvh + j] * LOG2_E for j in (0, 2, 1, 3)]
        cols.append(jnp.where(rb == 0, s[0], jnp.where(rb == 1, s[1], jnp.where(rb == 2, s[2], s[3]))))
    return cols


def _attend_block(qblk, kv_parts, masks, sink_cols):
    lo = lax.broadcasted_iota(I32, (128, 128), 1) < 64
    cols = []
    for kvh in range(2):
        q4 = _stack_heads(qblk[:, 256 * kvh:256 * kvh + 128], qblk[:, 256 * kvh + 128:256 * kvh + 256], lo)
        ko = 128 * kvh
        vao = 256 + 128 * kvh
        vbo = 384 - 128 * kvh
        oe, oo = _group_attention(q4, [kv[:, ko:ko + 128] for kv in kv_parts],
                                  [kv[:, vao:vao + 128] for kv in kv_parts],
                                  [kv[:, vbo:vbo + 128] for kv in kv_parts], masks, sink_cols[kvh])
        cols.append(jnp.where(lo, oe[0:128], oo[0:128]))
        cols.append(jnp.where(lo, oe[128:256], oo[128:256]))
    return jnp.concatenate(cols, axis=1).astype(BF16)


def _attn_lat_body(sink_ref, q_ref, kvp_ref, kvm_ref, kvn_ref, kvc_ref, o_ref, kvw_ref, *, tq, seq):
    i = pl.program_id(1)
    kvw_ref[0:128] = kvp_ref[0]
    kvw_ref[128:128 + tq] = kvm_ref[0]
    kvw_ref[128 + tq:256 + tq] = kvn_ref[0]
    kvc = kvc_ref[0]
    sink_cols = _sink_cols(sink_ref)

    def sub(j, carry):
        r0 = pl.multiple_of(j * 128, 128)
        win = kvw_ref[pl.ds(r0, 3 * 128), :]
        ii = lax.broadcasted_iota(I32, (128, 3 * 128), 0)
        cc = lax.broadcasted_iota(I32, (128, 3 * 128), 1)
        base = i * tq + j * 128 - 128
        valid = (ii <= cc) & (cc <= ii + 2 * WINDOW) & (cc >= -base) & (cc < seq - base)
        qblk = q_ref[0, pl.ds(r0, 128), :].astype(F32)
        o_ref[0, pl.ds(r0, 128), :] = _attend_block(qblk, [win, kvc], [valid, None], sink_cols)
        return carry

    for j in range(tq // 128):
        sub(j, 0)


def _attn_lat_call(sink, q, kv, kvc):
    nb, seq, _ = q.shape
    lc = kvc.shape[1]
    tq = TQ
    nblk = seq // 128
    r = tq // 128
    return pl.pallas_call(
        functools.partial(_attn_lat_body, tq=tq, seq=seq),
        grid=(nb, seq // tq),
        in_specs=[pl.BlockSpec(memory_space=pltpu.SMEM),
                  pl.BlockSpec((1, tq, ATTN_W), lambda b, i: (b, i, 0)),
                  pl.BlockSpec((1, 128, 4 * KV_W), lambda b, i: (b, jnp.maximum(i * r - 1, 0), 0)),
                  pl.BlockSpec((1, tq, 4 * KV_W), lambda b, i: (b, i, 0)),
                  pl.BlockSpec((1, 128, 4 * KV_W), lambda b, i: (b, jnp.minimum((i + 1) * r, nblk - 1), 0)),
                  pl.BlockSpec((1, lc, 4 * KV_W), lambda b, i: (b, 0, 0))],
        out_specs=pl.BlockSpec((1, tq, ATTN_W), lambda b, i: (b, i, 0)),
        out_shape=jax.ShapeDtypeStruct((nb, seq, ATTN_W), BF16),
        scratch_shapes=[pltpu.VMEM((tq + 256, 4 * KV_W), BF16)],
        compiler_params=_cparams(2),
        name="attention_window",
    )(sink, q, kv, kv, kv, kvc)


def _attn_ctx_body(sink_ref, q_ref, kvc_ref, o_ref, *, lc):
    kvc = kvc_ref[0]
    sink_cols = _sink_cols(sink_ref)
    for j in range(lc // 128):
        qblk = q_ref[0, j * 128:(j + 1) * 128, :].astype(F32)
        o_ref[0, j * 128:(j + 1) * 128, :] = _attend_block(qblk, [kvc], [None], sink_cols)


def _attn_ctx_call(sink, qc, kvc):
    nb, lc, _ = qc.shape
    return pl.pallas_call(
        functools.partial(_attn_ctx_body, lc=lc),
        grid=(nb,),
        in_specs=[pl.BlockSpec(memory_space=pltpu.SMEM),
                  pl.BlockSpec((1, lc, ATTN_W), lambda b: (b, 0, 0)),
                  pl.BlockSpec((1, lc, 4 * KV_W), lambda b: (b, 0, 0))],
        out_specs=pl.BlockSpec((1, lc, ATTN_W), lambda b: (b, 0, 0)),
        out_shape=jax.ShapeDtypeStruct((nb, lc, ATTN_W), BF16),
        compiler_params=_cparams(1),
        name="attention_ctx",
    )(sink, qc, kvc)


def _four_body(ua_ref, cb_ref, sb_ref, ca_ref, sa_ref, o_ref, *, nb, scale):
    i = pl.program_id(0)
    ca = ca_ref[pl.ds(i, 1), :]
    sa = sa_ref[pl.ds(i, 1), :]
    cb = cb_ref[...]
    sb = sb_ref[...]
    ct = (ca * cb - sa * sb).astype(BF16)
    st = (sa * cb + ca * sb).astype(BF16)
    for b in range(nb):
        ua = ua_ref[b, :, 0:FOUR_W]
        ub = ua_ref[b, :, FOUR_W:2 * FOUR_W]
        r = (jnp.dot(ct, ua, preferred_element_type=F32) - jnp.dot(st, ub, preferred_element_type=F32))
        o_ref[b] = (r * scale).astype(BF16)


def _four_call(uaub, cb, sb, ca, sa):
    nb, seq, _ = uaub.shape
    tm = cb.shape[0]
    one = pl.Buffered(1)
    return pl.pallas_call(
        functools.partial(_four_body, nb=nb, scale=float(seq) ** -0.5),
        grid=(seq // tm,),
        in_specs=[pl.BlockSpec((nb, seq, 2 * FOUR_W), lambda i: (0, 0, 0), pipeline_mode=one),
                  pl.BlockSpec((tm, seq), lambda i: (0, 0), pipeline_mode=one),
                  pl.BlockSpec((tm, seq), lambda i: (0, 0), pipeline_mode=one),
                  pl.BlockSpec((seq // tm, seq), lambda i: (0, 0), pipeline_mode=one),
                  pl.BlockSpec((seq // tm, seq), lambda i: (0, 0), pipeline_mode=one)],
        out_specs=pl.BlockSpec((nb, tm, FOUR_W), lambda i: (0, i, 0)),
        out_shape=jax.ShapeDtypeStruct((nb, seq, FOUR_W), BF16),
        compiler_params=_cparams(1),
        name="fourier_dft",
    )(uaub, cb, sb, ca, sa)


def _dft_tables(seq, tm):
    n = np.arange(seq)[None, :]

    def tab(rows):
        ang = ((rows[:, None] * n) % seq) * (2.0 * np.pi / seq)
        return jnp.asarray(np.cos(ang), F32), jnp.asarray(np.sin(ang), F32)

    cb, sb = tab(np.arange(tm))
    ca, sa = tab(np.arange(seq // tm) * tm)
    return cb, sb, ca, sa


def _four1_body(x_ref, g_ref, yr_ref, yi_ref):
    g = g_ref[...].astype(BF16)
    n = FOUR_R1 * 16
    w = FOUR_W
    for aa in range(x_ref.shape[2]):
        x = x_ref[0, :, aa].reshape(n, 2 * w)
        p = jnp.dot(g, x, preferred_element_type=F32)
        yr = p[0:n, 0:w] - p[n:2 * n, w:2 * w]
        yi = -(p[0:n, w:2 * w] + p[n:2 * n, 0:w])
        yr_ref[0, :, aa] = yr.astype(BF16).reshape(FOUR_R1, 16, w)
        yi_ref[0, :, aa] = yi.astype(BF16).reshape(FOUR_R1, 16, w)


def _four1_call(uaub, g):
    nb, seq, _ = uaub.shape
    r2 = seq // FOUR_R1
    na = r2 // 16
    x = uaub.reshape(nb, FOUR_R1, na, 16, 2 * FOUR_W)
    out = jax.ShapeDtypeStruct((nb, FOUR_R1, na, 16, FOUR_W), BF16)
    ac = FOUR1_A_CHUNK
    return pl.pallas_call(
        _four1_body,
        grid=(nb, na // ac),
        in_specs=[pl.BlockSpec((1, FOUR_R1, ac, 16, 2 * FOUR_W), lambda b, j: (b, 0, j, 0, 0)),
                  pl.BlockSpec(g.shape, lambda b, j: (0, 0))],
        out_specs=[pl.BlockSpec((1, FOUR_R1, ac, 16, FOUR_W), lambda b, j: (b, 0, j, 0, 0))] * 2,
        out_shape=[out, out],
        compiler_params=_cparams(2),
        name="fourier_stage1",
    )(x, g)


def _four2_body(yr_ref, yi_ref, c_ref, s_ref, ca_ref, sa_ref, o_ref, obuf, sem, *, nb, scale):
    k1 = pl.program_id(0)
    nk = pl.num_programs(0)
    ca = ca_ref[pl.ds(k1, 1), :]
    sa = sa_ref[pl.ds(k1, 1), :]
    c = c_ref[...]
    s = s_ref[...]
    gc = (c * ca - s * sa).astype(BF16)
    gs = (s * ca + c * sa).astype(BF16)

    def out_copy(slot, b, kk):
        return pltpu.make_async_copy(obuf.at[slot, b], o_ref.at[b, :, kk, :], sem.at[slot])

    for slot in range(2):
        @pl.when(k1 % 2 == slot)
        def _():
            @pl.when(k1 >= 2)
            def _():
                for b in range(nb):
                    out_copy(slot, b, 0).wait()

            for b in range(nb):
                acc = (jnp.dot(gc, yr_ref[b, 0], preferred_element_type=F32)
                       + jnp.dot(gs, yi_ref[b, 0], preferred_element_type=F32))
                obuf[slot, b] = acc * scale
            for b in range(nb):
                out_copy(slot, b, k1).start()

    @pl.when(k1 == nk - 1)
    def _():
        for slot in range(2):
            for b in range(nb):
                out_copy(slot, b, 0).wait()


def _four2_call(yr, yi, c, s, ca, sa):
    nb = yr.shape[0]
    r2 = c.shape[0]
    seq = FOUR_R1 * r2
    yr4 = yr.reshape(nb, FOUR_R1, r2, FOUR_W)
    yi4 = yi.reshape(nb, FOUR_R1, r2, FOUR_W)
    full = lambda shape: pl.BlockSpec(shape, lambda k: (0,) * len(shape))
    out = pl.pallas_call(
        functools.partial(_four2_body, nb=nb, scale=float(seq) ** -0.5),
        grid=(FOUR_R1,),
        in_specs=[pl.BlockSpec((nb, 1, r2, FOUR_W), lambda k: (0, k, 0, 0)),
                  pl.BlockSpec((nb, 1, r2, FOUR_W), lambda k: (0, k, 0, 0)),
                  full((r2, r2)), full((r2, r2)), full((FOUR_R1, r2)), full((FOUR_R1, r2))],
        out_specs=pl.BlockSpec(memory_space=pl.ANY),
        out_shape=jax.ShapeDtypeStruct((nb, r2, FOUR_R1, FOUR_W), F32),
        scratch_shapes=[pltpu.VMEM((2, nb, r2, FOUR_W), F32), pltpu.SemaphoreType.DMA((2,))],
        compiler_params=_cparams(1),
        name="fourier_stage2",
    )(yr4, yi4, c, s, ca, sa)
    return out.reshape(nb, seq, FOUR_W)


def _four2_tables(seq):
    r1 = FOUR_R1
    r2 = seq // r1
    assert r1 * r2 == seq and r2 % 16 == 0
    k = np.arange(r1)
    ang1 = ((k[:, None] * k[None, :]) % r1) * (2.0 * np.pi / r1)
    eye = np.eye(16)
    g = np.concatenate([np.kron(np.cos(ang1), eye), np.kron(np.sin(ang1), eye)], axis=0)
    m = np.arange(r2)
    ang2 = ((m[:, None] * m[None, :]) % r2) * (2.0 * np.pi / r2)
    alpha = ((k[:, None] * m[None, :]) % seq) * (2.0 * np.pi / seq)
    f = lambda t: jnp.asarray(t, F32)
    return f(g), (f(np.cos(ang2)), f(np.sin(ang2)), f(np.cos(alpha)), f(np.sin(alpha)))


def _route(lt, tri_ref, cnt_ref, tm):
    rowi = lax.broadcasted_iota(I32, (32, tm), 0)
    big = jnp.int32(999)

    def first_argmax(vals):
        mx = jnp.max(vals, axis=0, keepdims=True)
        return mx, jnp.min(jnp.where(vals == mx, rowi, big), axis=0, keepdims=True)

    is_grp = rowi < 4
    mg, gi = first_argmax(jnp.where(is_grp, lt, NEG_INF))
    pg = 1.0 / jnp.sum(jnp.where(is_grp, jnp.exp(jnp.where(is_grp, lt, mg) - mg), 0.0), axis=0, keepdims=True)
    est = 4 + 4 * gi
    le = jnp.where((rowi >= est) & (rowi < est + 4), lt, NEG_INF)
    m1, i1 = first_argmax(le)
    m2, i2 = first_argmax(jnp.where(rowi == i1, NEG_INF, le))
    e2 = jnp.exp(m2 - m1)
    w1 = pg / (1.0 + e2)
    w2 = pg * e2 / (1.0 + e2)
    a1 = i1 - est
    a2 = i2 - est
    code = jnp.minimum(a1, a2) * 4 + jnp.maximum(a1, a2)
    pidx = jnp.where(code == 1, 0, jnp.where(code == 6, 1, jnp.where(code == 2, 2,
           jnp.where(code == 3, 3, jnp.where(code == 7, 4, 5)))))
    slot_a = jnp.where(pidx == 0, 0, jnp.where(pidx <= 2, 2, 3))
    slot_b = jnp.where(pidx <= 1, 1, jnp.where(pidx <= 3, 0, jnp.where(pidx == 4, 1, 2)))
    wa = jnp.where(a1 == slot_a, w1, w2)
    wb = jnp.where(a1 == slot_b, w1, w2)
    bin_ = gi * 6 + pidx

    onehot = rowi == bin_
    pref = jnp.dot(onehot.astype(BF16), tri_ref[...], preferred_element_type=F32)
    carry = cnt_ref[:, 0:1]
    rank = jnp.sum(jnp.where(onehot, pref - 1.0 + carry, 0.0), axis=0, keepdims=True)
    cnt_ref[...] = jnp.broadcast_to(carry + pref[:, tm - 1:tm], cnt_ref.shape)
    return jnp.concatenate([bin_.astype(F32), rank, wa, wb, jnp.zeros((128 - 4, tm), F32)], axis=0)


def _out_body(x_ref, pup_ref, pum_ref, pun_ref, at_ref, fo_ref, wo_ref, pw_ref, ps_ref, band_ref, icnt_ref,
              mod_ref, g2_ref, w2_ref, br_ref, tri_ref, cin_ref,
              xo_ref, rows_ref, cnt_ref, *, tm, per_batch):
    b = pl.program_id(0)
    i = pl.program_id(1)
    nt = pl.num_programs(1)
    row = b if per_batch else CTX_ROW
    g1 = mod_ref[pl.ds(row, 1), pl.ds(2 * D, D)]
    sh2 = mod_ref[pl.ds(row, 1), pl.ds(3 * D, D)]
    sc2 = mod_ref[pl.ds(row, 1), pl.ds(4 * D, D)]

    @pl.when((b == 0) & (i == 0))
    def _():
        cnt_ref[...] = cin_ref[...]

    um = pum_ref[0]
    zh = jnp.zeros((HALO, POOL_W), BF16)
    uext = jnp.concatenate([jnp.where(i > 0, pup_ref[0], zh), um, jnp.where(i < nt - 1, pun_ref[0], zh)], axis=0)
    grp = lax.broadcasted_iota(I32, (POOL_CHUNK, POOL_W), 1) >> 6
    chunks = []
    for c in range(tm // POOL_CHUNK):
        uc = uext[POOL_CHUNK * c:POOL_CHUNK * (c + 1) + 2 * HALO]
        pc = jnp.zeros((POOL_CHUNK, POOL_W), F32)
        for g in range(len(POOL_WINDOWS)):
            pc = jnp.where(grp == g, jnp.dot(band_ref[g], uc, preferred_element_type=F32), pc)
        chunks.append(pc)
    pooled = jnp.concatenate(chunks, axis=0)
    y = pooled * icnt_ref[...] - um.astype(F32)
    pool_out = jnp.dot(y.astype(BF16), pw_ref[...], preferred_element_type=F32) * ps_ref[...]

    cat = jnp.concatenate([pool_out.astype(BF16), at_ref[0], fo_ref[0].astype(BF16)], axis=1)
    xm = x_ref[0] + g1 * jnp.dot(cat, wo_ref[...], preferred_element_type=F32)
    xo_ref[0] = xm

    ms = jnp.mean(xm * xm, axis=-1, keepdims=True)
    h2 = (xm * lax.rsqrt(ms + EPS) * g2_ref[...]) * (1.0 + sc2) + sh2

    hh = h2.astype(BF16)
    hl = (h2 - hh.astype(F32)).astype(BF16)
    w2 = w2_ref[...]
    p2 = jnp.dot(hh, w2, preferred_element_type=F32)
    logits = (p2[:, 0:128] + p2[:, 128:256] + jnp.dot(hl, w2[:, 0:128], preferred_element_type=F32) + br_ref[...])
    meta = _route(logits.T[0:32, :], tri_ref, cnt_ref, tm).T

    rows_ref[:, 0:D] = h2
    rows_ref[:, D:ROW_W] = meta


def _out_call(x3, pu, attn, four, w_out, pool_wbd, pool_scale, bands, icnt, mod, g2, w2, br, tri, cnt_in, *,
              per_batch):
    nb, seq, _ = x3.shape
    tm = min(TM_OUT, seq)
    nt = seq // tm
    hb = tm // HALO
    full = lambda shape: pl.BlockSpec(shape, lambda b, i: (0,) * len(shape))
    in_specs = [pl.BlockSpec((1, tm, D), lambda b, i: (b, i, 0)),
                pl.BlockSpec((1, HALO, POOL_W), lambda b, i: (b, jnp.maximum(i * hb - 1, 0), 0)),
                pl.BlockSpec((1, tm, POOL_W), lambda b, i: (b, i, 0)),
                pl.BlockSpec((1, HALO, POOL_W), lambda b, i: (b, jnp.minimum((i + 1) * hb, seq // HALO - 1), 0)),
                pl.BlockSpec((1, tm, ATTN_W), lambda b, i: (b, i, 0)),
                pl.BlockSpec((1, tm, FOUR_W), lambda b, i: (b, i, 0)),
                full((D, D)), full((POOL_W, POOL_W)), full((1, POOL_W)),
                full((len(POOL_WINDOWS), POOL_CHUNK, POOL_CHUNK + 2 * HALO)),
                pl.BlockSpec((tm, POOL_W), lambda b, i: (i, 0)),
                full((8, 6 * D)), full((1, D)), full((D, 256)), full((1, 128)), full((tm, tm)), full((32, 128))]
    args = [x3, pu, pu, pu, attn, four, w_out, pool_wbd, pool_scale, bands, icnt, mod, g2, w2, br, tri, cnt_in]
    return pl.pallas_call(
        functools.partial(_out_body, tm=tm, per_batch=per_batch),
        grid=(nb, nt),
        in_specs=in_specs,
        out_specs=[pl.BlockSpec((1, tm, D), lambda b, i: (b, i, 0)),
                   pl.BlockSpec((tm, ROW_W), lambda b, i: (b * nt + i, 0)),
                   pl.BlockSpec((32, 128), lambda b, i: (0, 0))],
        out_shape=[jax.ShapeDtypeStruct((nb, seq, D), F32),
                   jax.ShapeDtypeStruct((nb * seq, ROW_W), F32),
                   jax.ShapeDtypeStruct((32, 128), F32)],
        compiler_params=_cparams(2),
        name="out_proj_router" if per_batch else "out_proj_router_ctx",
    )(*args)


def _pool_tables(seq):
    t = np.arange(POOL_CHUNK)[:, None]
    s = np.arange(POOL_CHUNK + 2 * HALO)[None, :] - HALO
    bands = np.stack([(s >= t - w // 2) & (s <= t + w // 2 - 1) for w in POOL_WINDOWS]).astype(np.float32)
    pos = np.arange(seq)
    icnt = np.stack([1.0 / (np.minimum(pos + w // 2 - 1, seq - 1) - np.maximum(pos - w // 2, 0) + 1)
                     for w in POOL_WINDOWS], axis=1)
    return jnp.asarray(bands, BF16), jnp.asarray(np.repeat(icnt, POOL_W // len(POOL_WINDOWS), axis=1), F32)


def _row_copies(tm, make_copy):
    for r in range(tm):
        make_copy(r).start(priority=r % 2)


def _row_waits(tm, make_copy):
    def drain(r, c):
        make_copy(0).wait()
        return c

    lax.fori_loop(0, tm, drain, 0, unroll=8)


def _zero_fill(ends_ref, nv_ref, xs_ref, zbuf, zsem, tm, n_out, wait):
    def piece(off, size):
        return pltpu.make_async_copy(zbuf.at[pl.ds(0, size)], xs_ref.at[pl.ds(off, size)], zsem)

    def run(cond, off, size):
        @pl.when(cond)
        def _():
            c = piece(off, size)
            c.wait() if wait else c.start()

    for b in range(N_BINS):
        off = ends_ref[b]
        pad = (tm - (off & (tm - 1))) & (tm - 1)
        for k in range(tm.bit_length() - 1):
            run(((pad >> k) & 1) == 1, off, 1 << k)
            off = off + (pad & (1 << k))

    def tail(t, c):
        c_ = piece(t * tm, tm)
        c_.wait() if wait else c_.start()
        return c

    lax.fori_loop(nv_ref[0], n_out, tail, 0)


def _scatter_body(dest_ref, ends_ref, nv_ref, *refs, tm, n_tiles, n_first, n_out):
    n_h = len(refs) - 3 - 3 * SCATTER_SLOTS
    h_refs, xs_ref = refs[:n_h], refs[n_h]
    bufs = refs[n_h + 1:n_h + 1 + SCATTER_SLOTS]
    lsems = refs[n_h + 1 + SCATTER_SLOTS:n_h + 1 + 2 * SCATTER_SLOTS]
    rsems = refs[n_h + 1 + 2 * SCATTER_SLOTS:n_h + 1 + 3 * SCATTER_SLOTS]
    zbuf, zsem = refs[-2:]
    zbuf[...] = jnp.zeros_like(zbuf)
    _zero_fill(ends_ref, nv_ref, xs_ref, zbuf, zsem, tm, n_out, wait=False)

    def lane_block_copy(h_ref, tt, slot, j):
        return pltpu.make_async_copy(h_ref.at[pl.ds(tt * tm, tm), pl.ds(128 * j, 128)],
                                     bufs[slot].at[:, j, :], lsems[slot])

    def load(t, slot):
        def start(h_ref, tt):
            for j in range(ROW_TILES):
                lane_block_copy(h_ref, tt, slot, j).start()

        if n_h == 1:
            start(h_refs[0], t)
        else:
            @pl.when(t < n_first)
            def _():
                start(h_refs[0], t)

            @pl.when(t >= n_first)
            def _():
                start(h_refs[1], t - n_first)

    def row_copy(slot, r, d):
        return pltpu.make_async_copy(bufs[slot].at[r], xs_ref.at[d], rsems[slot])

    load(0, 0)
    load(1, 1)

    def group(g, c):
        for slot in range(SCATTER_SLOTS):
            t = g * SCATTER_SLOTS + slot
            ahead = (slot + 2) % SCATTER_SLOTS
            for j in range(ROW_TILES):
                lane_block_copy(h_refs[0], 0, slot, j).wait()

            @pl.when(t >= 2)
            def _():
                _row_waits(tm, lambda r: row_copy(ahead, r, 0))

            @pl.when(t + 2 < n_tiles)
            def _():
                load(t + 2, ahead)

            _row_copies(tm, lambda r: row_copy(slot, r, dest_ref[t * tm + r]))
        return c

    lax.fori_loop(0, n_tiles // SCATTER_SLOTS, group, 0)
    for t in (n_tiles - 2, n_tiles - 1):
        _row_waits(tm, lambda r: row_copy(t % SCATTER_SLOTS, r, 0))
    _zero_fill(ends_ref, nv_ref, xs_ref, zbuf, zsem, tm, n_out, wait=True)


def _scatter_call(dest, bin_ends, n_valid, row_sets):
    tm = TM_ROW
    assert tm == TM_MOE
    n_first = row_sets[0].shape[0] // tm
    n_rows = sum(r.shape[0] for r in row_sets)
    n_tiles = n_rows // tm
    n_out = n_tiles + N_BINS
    assert n_tiles % SCATTER_SLOTS == 0 and n_tiles >= SCATTER_SLOTS
    return pl.pallas_call(
        functools.partial(_scatter_body, tm=tm, n_tiles=n_tiles, n_first=n_first, n_out=n_out),
        grid_spec=pltpu.PrefetchScalarGridSpec(
            num_scalar_prefetch=3,
            grid=(1,),
            in_specs=[pl.BlockSpec(memory_space=pl.ANY)] * len(row_sets),
            out_specs=pl.BlockSpec(memory_space=pl.ANY),
            scratch_shapes=([pltpu.VMEM((tm, ROW_TILES, 128), F32)] * SCATTER_SLOTS
                            + [pltpu.SemaphoreType.DMA(())] * (2 * SCATTER_SLOTS)
                            + [pltpu.VMEM((tm, ROW_TILES, 128), F32), pltpu.SemaphoreType.DMA(())])),
        out_shape=jax.ShapeDtypeStruct((n_out * tm, ROW_TILES, 128), F32),
        compiler_params=_cparams(1),
        name="moe_scatter_rows",
    )(dest, bin_ends, n_valid, *row_sets)


def _moe_body(ea_ref, eb_ref, nv_ref, xs_ref, wga, wua, wda, wgb, wub, wdb, ys_ref, xbuf0, xbuf1, sem0, sem1, *, tm):
    del ea_ref, eb_ref
    s = pl.program_id(0)
    nv = nv_ref[0]
    xbufs = (xbuf0, xbuf1)
    sems = (sem0, sem1)

    def lane_block_copy(t, slot, j):
        return pltpu.make_async_copy(xs_ref.at[pl.ds(t * tm, tm), j, :], xbufs[slot].at[:, pl.ds(128 * j, 128)],
                                     sems[slot])

    def fetch(t, slot):
        for j in range(ROW_TILES):
            lane_block_copy(t, slot, j).start()

    @pl.when(s == 0)
    def _():
        fetch(0, 0)

    @pl.when(s >= nv)
    def _():
        ys_ref[...] = jnp.zeros_like(ys_ref)

    for slot in range(2):
        @pl.when((s < nv) & (s % 2 == slot))
        def _():
            @pl.when(s + 1 < nv)
            def _():
                fetch(s + 1, 1 - slot)

            for j in range(ROW_TILES):
                lane_block_copy(0, slot, j).wait()
            xs = xbufs[slot]
            h = xs[:, 0:D].astype(BF16)
            meta = xs[:, D:ROW_W]

            def expert(wg, wu, wd, gate):
                g = jnp.dot(h, wg[0, 0].astype(BF16), preferred_element_type=F32)
                u = jnp.dot(h, wu[0, 0].astype(BF16), preferred_element_type=F32)
                a = _silu(g) * u * gate
                return jnp.dot(a.astype(BF16), wd[0, 0].astype(BF16), preferred_element_type=F32)

            ys_ref[...] = expert(wga, wua, wda, meta[:, 2:3]) + expert(wgb, wub, wdb, meta[:, 3:4])


def _moe_call(items, xs, w_gate, w_up, w_down, layer):
    tm = TM_MOE
    n_tiles = xs.shape[0] // tm
    wa = lambda s, ea, eb, nv: (layer, ea[s], 0, 0)
    wb = lambda s, ea, eb, nv: (layer, eb[s], 0, 0)
    up_spec = lambda f: pl.BlockSpec((1, 1, D, D_EXPERT), f)
    dn_spec = lambda f: pl.BlockSpec((1, 1, D_EXPERT, D), f)
    return pl.pallas_call(
        functools.partial(_moe_body, tm=tm),
        grid_spec=pltpu.PrefetchScalarGridSpec(
            num_scalar_prefetch=len(items),
            grid=(n_tiles,),
            in_specs=[pl.BlockSpec(memory_space=pl.ANY),
                      up_spec(wa), up_spec(wa), dn_spec(wa), up_spec(wb), up_spec(wb), dn_spec(wb)],
            out_specs=pl.BlockSpec((tm, D), lambda s, ea, eb, nv: (s, 0)),
            scratch_shapes=[pltpu.VMEM((tm, ROW_W), F32), pltpu.VMEM((tm, ROW_W), F32),
                            pltpu.SemaphoreType.DMA(()), pltpu.SemaphoreType.DMA(())]),
        out_shape=jax.ShapeDtypeStruct((xs.shape[0], D), F32),
        compiler_params=_cparams(1),
        name="moe_experts",
    )(*items, xs, w_gate, w_up, w_down, w_gate, w_up, w_down)


def _gather_body(dest_ref, x_ref, mod_ref, ys_ref, o_ref, ybuf0, ybuf1, sem0, sem1, *,
                 tm, seq, per_batch, dest_off):
    i = pl.program_id(0)
    n_tiles = pl.num_programs(0)
    ybufs = (ybuf0, ybuf1)
    sems = (sem0, sem1)
    row = (i * tm) // seq if per_batch else CTX_ROW
    g2 = mod_ref[pl.ds(row, 1), pl.ds(5 * D, D)]

    def row_copy(slot, r, d):
        return pltpu.make_async_copy(ys_ref.at[pl.ds(d, 1)], ybufs[slot].at[pl.ds(r, 1)], sems[slot])

    def fetch(t, slot):
        base = dest_off + t * tm
        _row_copies(tm, lambda r: row_copy(slot, r, dest_ref[base + r]))

    @pl.when(i == 0)
    def _():
        fetch(0, 0)

    for slot in range(2):
        @pl.when(i % 2 == slot)
        def _():
            @pl.when(i + 1 < n_tiles)
            def _():
                fetch(i + 1, 1 - slot)

            _row_waits(tm, lambda r: row_copy(slot, r, 0))
            o_ref[...] = x_ref[...] + g2 * ybufs[slot][...]


def _gather_call(dest, x2, mod, ys, *, seq, per_batch, dest_off):
    n = x2.shape[0]
    tm = min(TM_GATHER, seq)
    return pl.pallas_call(
        functools.partial(_gather_body, tm=tm, seq=seq, per_batch=per_batch, dest_off=dest_off),
        grid_spec=pltpu.PrefetchScalarGridSpec(
            num_scalar_prefetch=1,
            grid=(n // tm,),
            in_specs=[pl.BlockSpec((tm, D), lambda i, *_: (i, 0)),
                      pl.BlockSpec((8, 6 * D), lambda i, *_: (0, 0)),
                      pl.BlockSpec(memory_space=pl.ANY)],
            out_specs=pl.BlockSpec((tm, D), lambda i, *_: (i, 0)),
            scratch_shapes=[pltpu.VMEM((tm, D), F32), pltpu.VMEM((tm, D), F32),
                            pltpu.SemaphoreType.DMA(()), pltpu.SemaphoreType.DMA(())]),
        out_shape=jax.ShapeDtypeStruct((n, D), F32),
        compiler_params=_cparams(1),
        name="moe_gather_rows" if per_batch else "moe_gather_rows_ctx",
    )(dest, x2, mod, ys)


def _routing_tables(bins, rank, cnt, n_rows):
    tm = TM_MOE
    counts = cnt[:N_BINS, 0].astype(I32)
    tiles = (counts + tm - 1) // tm
    tile_end = jnp.cumsum(tiles)
    starts = (tile_end - tiles) * tm
    ids = jnp.arange(N_BINS, dtype=I32)
    pick = lambda key, tab: jnp.sum(jnp.where(key[:, None] == ids[None, :], tab[None, :], 0), axis=1)
    dest = rank + pick(bins, starts)
    n_valid = tile_end[-1]
    n_tiles = n_rows // tm + N_BINS
    s = jnp.minimum(jnp.arange(n_tiles, dtype=I32), n_valid - 1)
    tbin = jnp.sum((s[:, None] >= tile_end[None, :]).astype(I32), axis=1)
    pidx = tbin % 6
    six = jnp.arange(6, dtype=I32)
    slot = lambda tab: jnp.sum(jnp.where(pidx[:, None] == six[None, :], jnp.asarray(tab, I32)[None, :], 0), axis=1)
    ea = 4 * (tbin // 6) + slot(PAIR_SLOT_A)
    eb = 4 * (tbin // 6) + slot(PAIR_SLOT_B)
    return dest, starts + counts, (ea, eb, n_valid.reshape(1))


def _block_diag(w):
    g, c, d = w.shape
    eye = jnp.asarray(np.eye(g), w.dtype)
    return (w[:, :, None, :] * eye[:, None, :, None]).reshape(g * c, g * d)


def _rope_tables(n_tokens):
    rows = n_tokens // GRID_W
    r = np.repeat(np.arange(rows), GRID_W).astype(np.float64)
    col = np.tile(np.arange(GRID_W), rows).astype(np.float64)
    half = HEAD_DIM // 2
    inv = 1.0 / (ROPE_BASE ** (np.arange(0, half, 2, dtype=np.float64) / half))
    ar = r[:, None] * inv
    ac = col[:, None] * inv
    ang = np.concatenate([ar, ar, ac, ac], axis=-1)
    sign = np.where((np.arange(HEAD_DIM) & 16) == 0, -1.0, 1.0)
    cos, sin = np.cos(ang), np.sin(ang) * sign
    return jnp.asarray(np.tile(cos, (1, 2)), F32), jnp.asarray(np.tile(sin, (1, 2)), F32)


def kernel(x, c, ctx, c_ctx, w_mod, b_mod, norm1_g, w_in, q_norm_g, k_norm_g, attn_sink, pool_w, pool_scale,
           four_w, w_out, norm2_g, w_grp, b_grp, w_rtr, b_rtr, w_gate, w_up, w_down):
    nb, seq, _ = x.shape
    lc = ctx.shape[1]
    depth = w_mod.shape[0]
    t_lat = nb * seq
    t_ctx = nb * lc

    cs = jnp.concatenate([c, c_ctx[None, :], jnp.zeros((8 - nb - 1, D), F32)], axis=0)
    m512 = jnp.asarray(np.kron(np.eye(N_HEADS), np.full((HEAD_DIM, HEAD_DIM), 1.0 / HEAD_DIM)), BF16)
    kk = np.arange(HEAD_DIM)
    ang64 = 2.0 * np.pi * ((kk[:, None] * kk[None, :]) % HEAD_DIM) / HEAD_DIM
    c64bd = jnp.asarray(np.kron(np.eye(4), np.cos(ang64)), F32)
    s64bd = jnp.asarray(np.kron(np.eye(4), np.sin(ang64)), F32)
    tri_of = lambda n: jnp.asarray(np.triu(np.ones((min(TM_OUT, n),) * 2)), BF16)
    cos2, sin2 = _rope_tables(seq)
    four_cs, four_tabs = _four2_tables(seq)
    tabs_ctx = _dft_tables(lc, min(TM_FOUR, lc))
    pool_lat = _pool_tables(seq)
    pool_ctx = _pool_tables(lc)
    w_in_b = w_in.astype(BF16)
    w_out_b = w_out.astype(BF16)
    wbd = jnp.stack([_block_diag(four_w[l]) for l in range(depth)])
    pool_wbd = jnp.stack([_block_diag(pool_w[l]) for l in range(depth)]).astype(BF16)
    wr = jnp.concatenate([w_grp, w_rtr, jnp.zeros((depth, D, 128 - 4 - N_EXPERTS), F32)], axis=2)
    wr_hi = wr.astype(BF16)
    w2 = jnp.concatenate([wr_hi, (wr - wr_hi.astype(F32)).astype(BF16)], axis=2)
    br = jnp.concatenate([b_grp, b_rtr, jnp.zeros((depth, 128 - 4 - N_EXPERTS), F32)], axis=1)

    mod_all = _mod_call(cs, w_mod, b_mod)
    ab_all = _ab_call(c64bd, s64bd, wbd)

    xc = ctx
    for l in range(depth):
        last = l == depth - 1
        mod = mod_all[l]
        g1 = norm1_g[l][None, :]
        g2 = norm2_g[l][None, :]
        qg = jnp.tile(q_norm_g[l], N_HEADS)[None, :]
        kg = jnp.tile(k_norm_g[l], KV_W // HEAD_DIM)[None, :]
        in_args = (mod, g1, w_in_b[l], m512, qg, kg, ab_all[l])

        puc, qc, kvc, uac = _in_call(xc, *in_args, None, None, rope=False, per_batch=False)
        pul, ql, kvl, ual = _in_call(x, *in_args, cos2, sin2, rope=True, per_batch=True)
        attn_l = _attn_lat_call(attn_sink[l], ql, kvl, kvc)
        four_l = _four2_call(*_four1_call(ual, four_cs), *four_tabs)

        proj = (w_out_b[l], pool_wbd[l], pool_scale[l][None, :])
        rout = (mod, g2, w2[l], br[l][None, :])
        cnt0 = jnp.zeros((32, 128), F32)
        x_mid, rows_l, cnt = _out_call(x, pul, attn_l, four_l, *proj, *pool_lat, *rout, tri_of(seq), cnt0,
                                       per_batch=True)
        row_sets = [rows_l]
        if not last:
            attn_c = _attn_ctx_call(attn_sink[l], qc, kvc)
            four_c = _four_call(uac, *tabs_ctx)
            xc_mid, rows_c, cnt = _out_call(xc, puc, attn_c, four_c, *proj, *pool_ctx, *rout, tri_of(lc), cnt,
                                            per_batch=False)
            row_sets.append(rows_c)

        route = [r[:, D:D + 2].astype(I32) for r in row_sets]
        bins = jnp.concatenate([r[:, 0] for r in route])
        rank = jnp.concatenate([r[:, 1] for r in route])
        dest, bin_ends, items = _routing_tables(bins, rank, cnt, bins.shape[0])
        xs = _scatter_call(dest, bin_ends, items[2], row_sets)
        ys = _moe_call(items, xs, w_gate, w_up, w_down, l)
        x = _gather_call(dest, x_mid.reshape(t_lat, D), mod, ys, seq=seq, per_batch=True,
                         dest_off=0).reshape(nb, seq, D)
        if not last:
            xc = _gather_call(dest, xc_mid.reshape(t_ctx, D), mod, ys, seq=lc, per_batch=False,
                              dest_off=t_lat).reshape(nb, lc, D)
    return x
```

```python
import functools

import numpy as np
import jax
import jax.numpy as jnp
from jax import lax
from jax.experimental import pallas as pl
from jax.experimental.pallas import tpu as pltpu

F32 = jnp.float32
BF16 = jnp.bfloat16
I32 = jnp.int32
HI = lax.Precision.HIGHEST

D = 1024
HEAD_DIM = 64
N_HEADS = 8
GRID_W = 64
POOL_WINDOWS = (2, 4, 8, 16)
POOL_W = 256
ATTN_W = 512
KV_W = 128
FOUR_W = 256
IN_W = 1280
N_EXPERTS = 16
D_EXPERT = 512
WINDOW = 128
ROPE_BASE = 10000.0
EPS = 1e-6
NEG_INF = -1e30
LOG2_E = 1.4426950408889634
CTX_ROW = 4
N_BINS = 24
PAIR_SLOT_A = (0, 2, 2, 3, 3, 3)
PAIR_SLOT_B = (1, 1, 0, 0, 1, 2)
META_W = 128
ROW_W = D + META_W
ROW_TILES = ROW_W // 128
HALO = 16
POOL_CHUNK = 128

VMEM_LIMIT = 56 * 1024 * 1024
TM_IN = 1024
TQ = 1024
TM_OUT = 1024
TM_FOUR = 256
FOUR_R1 = 16
FOUR1_A_CHUNK = 8
TM_MOE = 256
TM_ROW = 256
TM_GATHER = 1024
SCATTER_SLOTS = 4
MOD_TN = 1024


def _cparams(n_axes):
    return pltpu.CompilerParams(dimension_semantics=("arbitrary",) * n_axes,
                                vmem_limit_bytes=VMEM_LIMIT)


def _silu(v):
    return v / (1.0 + jnp.exp(-v))


def _mod_body(cs_ref, w_ref, b_ref, o_ref):
    s = _silu(cs_ref[...])
    w = w_ref[0]
    s_hi = s.astype(BF16)
    s_lo = (s - s_hi.astype(F32)).astype(BF16)
    w_hi = w.astype(BF16)
    w_lo = (w - w_hi.astype(F32)).astype(BF16)
    p = jnp.dot(jnp.concatenate([s_hi, s_lo], axis=0), w_hi, preferred_element_type=F32)
    o_ref[0] = p[0:8] + p[8:16] + jnp.dot(s_hi, w_lo, preferred_element_type=F32) + b_ref[0]


def _mod_call(cs, w_mod, b_mod):
    depth = w_mod.shape[0]
    return pl.pallas_call(
        _mod_body,
        grid=(depth, 6 * D // MOD_TN),
        in_specs=[pl.BlockSpec((8, D), lambda l, j: (0, 0)),
                  pl.BlockSpec((1, D, MOD_TN), lambda l, j: (l, 0, j)),
                  pl.BlockSpec((1, 1, MOD_TN), lambda l, j: (l, 0, j))],
        out_specs=pl.BlockSpec((1, 8, MOD_TN), lambda l, j: (l, 0, j)),
        out_shape=jax.ShapeDtypeStruct((depth, 8, 6 * D), F32),
        compiler_params=_cparams(2),
        name="modulation",
    )(cs, w_mod, b_mod.reshape(depth, 1, 6 * D))


def _ab_body(c_ref, s_ref, w_ref, o_ref):
    w = w_ref[0]
    ca = jnp.dot(c_ref[...], w, preferred_element_type=F32, precision=HI)
    sa = jnp.dot(s_ref[...], w, preferred_element_type=F32, precision=HI)
    o_ref[0] = (jnp.concatenate([ca, sa], axis=1) * (HEAD_DIM ** -0.5)).astype(BF16)


def _ab_call(c64bd, s64bd, wbd):
    depth = wbd.shape[0]
    return pl.pallas_call(
        _ab_body,
        grid=(depth,),
        in_specs=[pl.BlockSpec((FOUR_W, FOUR_W), lambda l: (0, 0)),
                  pl.BlockSpec((FOUR_W, FOUR_W), lambda l: (0, 0)),
                  pl.BlockSpec((1, FOUR_W, FOUR_W), lambda l: (l, 0, 0))],
        out_specs=pl.BlockSpec((1, FOUR_W, 2 * FOUR_W), lambda l: (l, 0, 0)),
        out_shape=jax.ShapeDtypeStruct((depth, FOUR_W, 2 * FOUR_W), BF16),
        compiler_params=_cparams(1),
        name="fourier_weights",
    )(c64bd, s64bd, wbd)


def _head_rms(t, m, g):
    ms = jnp.dot((t * t).astype(BF16), m, preferred_element_type=F32)
    return t * lax.rsqrt(ms + EPS) * g


def _rope(t, cos, sin_signed):
    w = t.shape[1]
    lane = lax.broadcasted_iota(I32, t.shape, 1)
    fwd = pltpu.roll(t, w - 16, 1)
    bwd = pltpu.roll(t, 16, 1)
    rot = jnp.where((lane & 16) == 0, fwd, bwd)
    return t * cos + rot * sin_signed


def _in_body(*refs, rope, per_batch):
    if rope:
        (x_ref, mod_ref, g1_ref, w_ref, m_ref, qg_ref, kg_ref, ab_ref, cos_ref, sin_ref,
         pu_ref, q_ref, kv_ref, ua_ref) = refs
    else:
        (x_ref, mod_ref, g1_ref, w_ref, m_ref, qg_ref, kg_ref, ab_ref,
         pu_ref, q_ref, kv_ref, ua_ref) = refs
    row = pl.program_id(0) if per_batch else CTX_ROW
    sh1 = mod_ref[pl.ds(row, 1), pl.ds(0, D)]
    sc1 = mod_ref[pl.ds(row, 1), pl.ds(D, D)]
    x = x_ref[0]
    ms = jnp.mean(x * x, axis=-1, keepdims=True)
    h = (x * lax.rsqrt(ms + EPS) * g1_ref[...]) * (1.0 + sc1) + sh1
    p = jnp.dot(h.astype(BF16), w_ref[...], preferred_element_type=F32)
    pu = p[:, 0:256]
    q = p[:, 256:768]
    k = p[:, 768:896]
    v = p[:, 896:1024]
    fu = p[:, 1024:1280]
    m = m_ref[...]
    q = _head_rms(q, m, qg_ref[...])
    k = _head_rms(k, m[0:KV_W, 0:KV_W], kg_ref[...])
    if rope:
        cos = cos_ref[...]
        sin = sin_ref[...]
        q = _rope(q, jnp.concatenate([cos] * 4, axis=1), jnp.concatenate([sin] * 4, axis=1))
        k = _rope(k, cos, sin)
    q = q * (HEAD_DIM ** -0.5 * LOG2_E)
    pu_ref[0] = pu.astype(BF16)
    q_ref[0] = q.astype(BF16)
    kv_ref[0] = jnp.concatenate([k, pltpu.roll(k, 64, 1), v, pltpu.roll(v, 64, 1)], axis=1).astype(BF16)
    ua_ref[0] = jnp.dot(fu.astype(BF16), ab_ref[...], preferred_element_type=F32).astype(BF16)


def _in_call(x3, mod, g1, w_in, m512, qg, kg, ab, cos2, sin2, *, rope, per_batch):
    nb, seq, _ = x3.shape
    tm = min(TM_IN, seq)
    full = lambda shape: pl.BlockSpec(shape, lambda b, i: (0,) * len(shape))
    in_specs = [pl.BlockSpec((1, tm, D), lambda b, i: (b, i, 0)),
                full((8, 6 * D)), full((1, D)), full((D, IN_W)), full((ATTN_W, ATTN_W)),
                full((1, ATTN_W)), full((1, KV_W)), full((FOUR_W, 2 * FOUR_W))]
    args = [x3, mod, g1, w_in, m512, qg, kg, ab]
    if rope:
        in_specs += [pl.BlockSpec((tm, 128), lambda b, i: (i, 0)),
                     pl.BlockSpec((tm, 128), lambda b, i: (i, 0))]
        args += [cos2, sin2]
    widths = (POOL_W, ATTN_W, 4 * KV_W, 2 * FOUR_W)
    return pl.pallas_call(
        functools.partial(_in_body, rope=rope, per_batch=per_batch),
        grid=(nb, seq // tm),
        in_specs=in_specs,
        out_specs=[pl.BlockSpec((1, tm, w), lambda b, i: (b, i, 0)) for w in widths],
        out_shape=[jax.ShapeDtypeStruct((nb, seq, w), BF16) for w in widths],
        compiler_params=_cparams(2),
        name="in_proj_rope" if rope else "in_proj_ctx",
    )(*args)


_NT = (((1,), (1,)), ((), ()))


def _stack_heads(qpair0, qpair1, lo):
    z = jnp.zeros_like(qpair0)
    parts = [jnp.where(lo, qpair0, z), jnp.where(lo, qpair1, z),
             jnp.where(lo, pltpu.roll(qpair0, 64, 1), z), jnp.where(lo, pltpu.roll(qpair1, 64, 1), z)]
    return jnp.concatenate(parts, axis=0).astype(BF16)


def _group_attention(q4, k_parts, va_parts, vb_parts, masks, sink_col):
    s_parts = []
    for kz, mk in zip(k_parts, masks):
        s = lax.dot_general(q4, kz, _NT, preferred_element_type=F32)
        if mk is not None:
            nk = s.shape[1]
            s = jnp.where(mk[None], s.reshape(4, 128, nk), NEG_INF).reshape(512, nk)
        s_parts.append(s)
    m = functools.reduce(jnp.maximum, [jnp.max(s, axis=-1, keepdims=True) for s in s_parts])
    m = jnp.maximum(m, sink_col)
    den = jnp.exp2(sink_col - m)
    oe = oo = None
    for s, va, vb in zip(s_parts, va_parts, vb_parts):
        e = jnp.exp2(s - m)
        den = den + jnp.sum(e, axis=-1, keepdims=True)
        eb = e.astype(BF16)
        pe = jnp.dot(eb[0:256], va, preferred_element_type=F32)
        po = jnp.dot(eb[256:512], vb, preferred_element_type=F32)
        oe = pe if oe is None else oe + pe
        oo = po if oo is None else oo + po
    inv = 1.0 / den
    return oe * inv[0:256], oo * inv[256:512]


def _sink_cols(sink_ref):
    rb = lax.broadcasted_iota(I32, (512, 1), 0) >> 7
    cols = []
    for kvh in range(2):
        s = [sink_ref[4 * kvh + j] * LOG2_E for j in (0, 2, 1, 3)]
        cols.append(jnp.where(rb == 0, s[0], jnp.where(rb == 1, s[1], jnp.where(rb == 2, s[2], s[3]))))
    return cols


def _attend_block(qblk, kv_parts, masks, sink_cols):
    lo = lax.broadcasted_iota(I32, (128, 128), 1) < 64
    cols = []
    for kvh in range(2):
        q4 = _stack_heads(qblk[:, 256 * kvh:256 * kvh + 128], qblk[:, 256 * kvh + 128:256 * kvh + 256], lo)
        ko = 128 * kvh
        vao = 256 + 128 * kvh
        vbo = 384 - 128 * kvh
        oe, oo = _group_attention(q4, [kv[:, ko:ko + 128] for kv in kv_parts],
                                  [kv[:, vao:vao + 128] for kv in kv_parts],
                                  [kv[:, vbo:vbo + 128] for kv in kv_parts], masks, sink_cols[kvh])
        cols.append(jnp.where(lo, oe[0:128], oo[0:128]))
        cols.append(jnp.where(lo, oe[128:256], oo[128:256]))
    return jnp.concatenate(cols, axis=1).astype(BF16)


def _attn_lat_body(sink_ref, q_ref, kvp_ref, kvm_ref, kvn_ref, kvc_ref, o_ref, kvw_ref, *, tq, seq):
    i = pl.program_id(1)
    kvw_ref[0:128] = kvp_ref[0]
    kvw_ref[128:128 + tq] = kvm_ref[0]
    kvw_ref[128 + tq:256 + tq] = kvn_ref[0]
    kvc = kvc_ref[0]
    sink_cols = _sink_cols(sink_ref)

    def sub(j, carry):
        r0 = pl.multiple_of(j * 128, 128)
        win = kvw_ref[pl.ds(r0, 3 * 128), :]
        ii = lax.broadcasted_iota(I32, (128, 3 * 128), 0)
        cc = lax.broadcasted_iota(I32, (128, 3 * 128), 1)
        base = i * tq + j * 128 - 128
        valid = (ii <= cc) & (cc <= ii + 2 * WINDOW) & (cc >= -base) & (cc < seq - base)
        qblk = q_ref[0, pl.ds(r0, 128), :].astype(F32)
        o_ref[0, pl.ds(r0, 128), :] = _attend_block(qblk, [win, kvc], [valid, None], sink_cols)
        return carry

    for j in range(tq // 128):
        sub(j, 0)


def _attn_lat_call(sink, q, kv, kvc):
    nb, seq, _ = q.shape
    lc = kvc.shape[1]
    tq = TQ
    nblk = seq // 128
    r = tq // 128
    return pl.pallas_call(
        functools.partial(_attn_lat_body, tq=tq, seq=seq),
        grid=(nb, seq // tq),
        in_specs=[pl.BlockSpec(memory_space=pltpu.SMEM),
                  pl.BlockSpec((1, tq, ATTN_W), lambda b, i: (b, i, 0)),
                  pl.BlockSpec((1, 128, 4 * KV_W), lambda b, i: (b, jnp.maximum(i * r - 1, 0), 0)),
                  pl.BlockSpec((1, tq, 4 * KV_W), lambda b, i: (b, i, 0)),
                  pl.BlockSpec((1, 128, 4 * KV_W), lambda b, i: (b, jnp.minimum((i + 1) * r, nblk - 1), 0)),
                  pl.BlockSpec((1, lc, 4 * KV_W), lambda b, i: (b, 0, 0))],
        out_specs=pl.BlockSpec((1, tq, ATTN_W), lambda b, i: (b, i, 0)),
        out_shape=jax.ShapeDtypeStruct((nb, seq, ATTN_W), BF16),
        scratch_shapes=[pltpu.VMEM((tq + 256, 4 * KV_W), BF16)],
        compiler_params=_cparams(2),
        name="attention_window",
    )(sink, q, kv, kv, kv, kvc)


def _attn_ctx_body(sink_ref, q_ref, kvc_ref, o_ref, *, lc):
    kvc = kvc_ref[0]
    sink_cols = _sink_cols(sink_ref)
    for j in range(lc // 128):
        qblk = q_ref[0, j * 128:(j + 1) * 128, :].astype(F32)
        o_ref[0, j * 128:(j + 1) * 128, :] = _attend_block(qblk, [kvc], [None], sink_cols)


def _attn_ctx_call(sink, qc, kvc):
    nb, lc, _ = qc.shape
    return pl.pallas_call(
        functools.partial(_attn_ctx_body, lc=lc),
        grid=(nb,),
        in_specs=[pl.BlockSpec(memory_space=pltpu.SMEM),
                  pl.BlockSpec((1, lc, ATTN_W), lambda b: (b, 0, 0)),
                  pl.BlockSpec((1, lc, 4 * KV_W), lambda b: (b, 0, 0))],
        out_specs=pl.BlockSpec((1, lc, ATTN_W), lambda b: (b, 0, 0)),
        out_shape=jax.ShapeDtypeStruct((nb, lc, ATTN_W), BF16),
        compiler_params=_cparams(1),
        name="attention_ctx",
    )(sink, qc, kvc)


def _four_body(ua_ref, cb_ref, sb_ref, ca_ref, sa_ref, o_ref, *, nb, scale):
    i = pl.program_id(0)
    ca = ca_ref[pl.ds(i, 1), :]
    sa = sa_ref[pl.ds(i, 1), :]
    cb = cb_ref[...]
    sb = sb_ref[...]
    ct = (ca * cb - sa * sb).astype(BF16)
    st = (sa * cb + ca * sb).astype(BF16)
    for b in range(nb):
        ua = ua_ref[b, :, 0:FOUR_W]
        ub = ua_ref[b, :, FOUR_W:2 * FOUR_W]
        r = (jnp.dot(ct, ua, preferred_element_type=F32) - jnp.dot(st, ub, preferred_element_type=F32))
        o_ref[b] = (r * scale).astype(BF16)


def _four_call(uaub, cb, sb, ca, sa):
    nb, seq, _ = uaub.shape
    tm = cb.shape[0]
    one = pl.Buffered(1)
    return pl.pallas_call(
        functools.partial(_four_body, nb=nb, scale=float(seq) ** -0.5),
        grid=(seq // tm,),
        in_specs=[pl.BlockSpec((nb, seq, 2 * FOUR_W), lambda i: (0, 0, 0), pipeline_mode=one),
                  pl.BlockSpec((tm, seq), lambda i: (0, 0), pipeline_mode=one),
                  pl.BlockSpec((tm, seq), lambda i: (0, 0), pipeline_mode=one),
                  pl.BlockSpec((seq // tm, seq), lambda i: (0, 0), pipeline_mode=one),
                  pl.BlockSpec((seq // tm, seq), lambda i: (0, 0), pipeline_mode=one)],
        out_specs=pl.BlockSpec((nb, tm, FOUR_W), lambda i: (0, i, 0)),
        out_shape=jax.ShapeDtypeStruct((nb, seq, FOUR_W), BF16),
        compiler_params=_cparams(1),
        name="fourier_dft",
    )(uaub, cb, sb, ca, sa)


def _dft_tables(seq, tm):
    n = np.arange(seq)[None, :]

    def tab(rows):
        ang = ((rows[:, None] * n) % seq) * (2.0 * np.pi / seq)
        return jnp.asarray(np.cos(ang), F32), jnp.asarray(np.sin(ang), F32)

    cb, sb = tab(np.arange(tm))
    ca, sa = tab(np.arange(seq // tm) * tm)
    return cb, sb, ca, sa


def _four1_body(x_ref, g_ref, yr_ref, yi_ref):
    g = g_ref[...].astype(BF16)
    n = FOUR_R1 * 16
    w = FOUR_W
    for aa in range(x_ref.shape[2]):
        x = x_ref[0, :, aa].reshape(n, 2 * w)
        p = jnp.dot(g, x, preferred_element_type=F32)
        yr = p[0:n, 0:w] - p[n:2 * n, w:2 * w]
        yi = -(p[0:n, w:2 * w] + p[n:2 * n, 0:w])
        yr_ref[0, :, aa] = yr.astype(BF16).reshape(FOUR_R1, 16, w)
        yi_ref[0, :, aa] = yi.astype(BF16).reshape(FOUR_R1, 16, w)


def _four1_call(uaub, g):
    nb, seq, _ = uaub.shape
    r2 = seq // FOUR_R1
    na = r2 // 16
    x = uaub.reshape(nb, FOUR_R1, na, 16, 2 * FOUR_W)
    out = jax.ShapeDtypeStruct((nb, FOUR_R1, na, 16, FOUR_W), BF16)
    ac = FOUR1_A_CHUNK
    return pl.pallas_call(
        _four1_body,
        grid=(nb, na // ac),
        in_specs=[pl.BlockSpec((1, FOUR_R1, ac, 16, 2 * FOUR_W), lambda b, j: (b, 0, j, 0, 0)),
                  pl.BlockSpec(g.shape, lambda b, j: (0, 0))],
        out_specs=[pl.BlockSpec((1, FOUR_R1, ac, 16, FOUR_W), lambda b, j: (b, 0, j, 0, 0))] * 2,
        out_shape=[out, out],
        compiler_params=_cparams(2),
        name="fourier_stage1",
    )(x, g)


def _four2_body(yr_ref, yi_ref, c_ref, s_ref, ca_ref, sa_ref, o_ref, obuf, sem, *, nb, scale):
    k1 = pl.program_id(0)
    nk = pl.num_programs(0)
    ca = ca_ref[pl.ds(k1, 1), :]
    sa = sa_ref[pl.ds(k1, 1), :]
    c = c_ref[...]
    s = s_ref[...]
    gc = (c * ca - s * sa).astype(BF16)
    gs = (s * ca + c * sa).astype(BF16)

    def out_copy(slot, b, kk):
        return pltpu.make_async_copy(obuf.at[slot, b], o_ref.at[b, :, kk, :], sem.at[slot])

    for slot in range(2):
        @pl.when(k1 % 2 == slot)
        def _():
            @pl.when(k1 >= 2)
            def _():
                for b in range(nb):
                    out_copy(slot, b, 0).wait()

            for b in range(nb):
                acc = (jnp.dot(gc, yr_ref[b, 0], preferred_element_type=F32)
                       + jnp.dot(gs, yi_ref[b, 0], preferred_element_type=F32))
                obuf[slot, b] = acc * scale
            for b in range(nb):
                out_copy(slot, b, k1).start()

    @pl.when(k1 == nk - 1)
    def _():
        for slot in range(2):
            for b in range(nb):
                out_copy(slot, b, 0).wait()


def _four2_call(yr, yi, c, s, ca, sa):
    nb = yr.shape[0]
    r2 = c.shape[0]
    seq = FOUR_R1 * r2
    yr4 = yr.reshape(nb, FOUR_R1, r2, FOUR_W)
    yi4 = yi.reshape(nb, FOUR_R1, r2, FOUR_W)
    full = lambda shape: pl.BlockSpec(shape, lambda k: (0,) * len(shape))
    out = pl.pallas_call(
        functools.partial(_four2_body, nb=nb, scale=float(seq) ** -0.5),
        grid=(FOUR_R1,),
        in_specs=[pl.BlockSpec((nb, 1, r2, FOUR_W), lambda k: (0, k, 0, 0)),
                  pl.BlockSpec((nb, 1, r2, FOUR_W), lambda k: (0, k, 0, 0)),
                  full((r2, r2)), full((r2, r2)), full((FOUR_R1, r2)), full((FOUR_R1, r2))],
        out_specs=pl.BlockSpec(memory_space=pl.ANY),
        out_shape=jax.ShapeDtypeStruct((nb, r2, FOUR_R1, FOUR_W), F32),
        scratch_shapes=[pltpu.VMEM((2, nb, r2, FOUR_W), F32), pltpu.SemaphoreType.DMA((2,))],
        compiler_params=_cparams(1),
        name="fourier_stage2",
    )(yr4, yi4, c, s, ca, sa)
    return out.reshape(nb, seq, FOUR_W)


def _four2_tables(seq):
    r1 = FOUR_R1
    r2 = seq // r1
    assert r1 * r2 == seq and r2 % 16 == 0
    k = np.arange(r1)
    ang1 = ((k[:, None] * k[None, :]) % r1) * (2.0 * np.pi / r1)
    eye = np.eye(16)
    g = np.concatenate([np.kron(np.cos(ang1), eye), np.kron(np.sin(ang1), eye)], axis=0)
    m = np.arange(r2)
    ang2 = ((m[:, None] * m[None, :]) % r2) * (2.0 * np.pi / r2)
    alpha = ((k[:, None] * m[None, :]) % seq) * (2.0 * np.pi / seq)
    f = lambda t: jnp.asarray(t, F32)
    return f(g), (f(np.cos(ang2)), f(np.sin(ang2)), f(np.cos(alpha)), f(np.sin(alpha)))


def _route(lt, tri_ref, cnt_ref, tm):
    rowi = lax.broadcasted_iota(I32, (32, tm), 0)
    big = jnp.int32(999)

    def first_argmax(vals):
        mx = jnp.max(vals, axis=0, keepdims=True)
        return mx, jnp.min(jnp.where(vals == mx, rowi, big), axis=0, keepdims=True)

    is_grp = rowi < 4
    mg, gi = first_argmax(jnp.where(is_grp, lt, NEG_INF))
    pg = 1.0 / jnp.sum(jnp.where(is_grp, jnp.exp(jnp.where(is_grp, lt, mg) - mg), 0.0), axis=0, keepdims=True)
    est = 4 + 4 * gi
    le = jnp.where((rowi >= est) & (rowi < est + 4), lt, NEG_INF)
    m1, i1 = first_argmax(le)
    m2, i2 = first_argmax(jnp.where(rowi == i1, NEG_INF, le))
    e2 = jnp.exp(m2 - m1)
    w1 = pg / (1.0 + e2)
    w2 = pg * e2 / (1.0 + e2)
    a1 = i1 - est
    a2 = i2 - est
    code = jnp.minimum(a1, a2) * 4 + jnp.maximum(a1, a2)
    pidx = jnp.where(code == 1, 0, jnp.where(code == 6, 1, jnp.where(code == 2, 2,
           jnp.where(code == 3, 3, jnp.where(code == 7, 4, 5)))))
    slot_a = jnp.where(pidx == 0, 0, jnp.where(pidx <= 2, 2, 3))
    slot_b = jnp.where(pidx <= 1, 1, jnp.where(pidx <= 3, 0, jnp.where(pidx == 4, 1, 2)))
    wa = jnp.where(a1 == slot_a, w1, w2)
    wb = jnp.where(a1 == slot_b, w1, w2)
    bin_ = gi * 6 + pidx

    onehot = rowi == bin_
    pref = jnp.dot(onehot.astype(BF16), tri_ref[...], preferred_element_type=F32)
    carry = cnt_ref[:, 0:1]
    rank = jnp.sum(jnp.where(onehot, pref - 1.0 + carry, 0.0), axis=0, keepdims=True)
    cnt_ref[...] = jnp.broadcast_to(carry + pref[:, tm - 1:tm], cnt_ref.shape)
    return jnp.concatenate([bin_.astype(F32), rank, wa, wb, jnp.zeros((128 - 4, tm), F32)], axis=0)


def _out_body(x_ref, pup_ref, pum_ref, pun_ref, at_ref, fo_ref, wo_ref, pw_ref, ps_ref, band_ref, icnt_ref,
              mod_ref, g2_ref, w2_ref, br_ref, tri_ref, cin_ref,
              xo_ref, rows_ref, cnt_ref, *, tm, per_batch):
    b = pl.program_id(0)
    i = pl.program_id(1)
    nt = pl.num_programs(1)
    row = b if per_batch else CTX_ROW
    g1 = mod_ref[pl.ds(row, 1), pl.ds(2 * D, D)]
    sh2 = mod_ref[pl.ds(row, 1), pl.ds(3 * D, D)]
    sc2 = mod_ref[pl.ds(row, 1), pl.ds(4 * D, D)]

    @pl.when((b == 0) & (i == 0))
    def _():
        cnt_ref[...] = cin_ref[...]

    um = pum_ref[0]
    zh = jnp.zeros((HALO, POOL_W), BF16)
    uext = jnp.concatenate([jnp.where(i > 0, pup_ref[0], zh), um, jnp.where(i < nt - 1, pun_ref[0], zh)], axis=0)
    grp = lax.broadcasted_iota(I32, (POOL_CHUNK, POOL_W), 1) >> 6
    chunks = []
    for c in range(tm // POOL_CHUNK):
        uc = uext[POOL_CHUNK * c:POOL_CHUNK * (c + 1) + 2 * HALO]
        pc = jnp.zeros((POOL_CHUNK, POOL_W), F32)
        for g in range(len(POOL_WINDOWS)):
            pc = jnp.where(grp == g, jnp.dot(band_ref[g], uc, preferred_element_type=F32), pc)
        chunks.append(pc)
    pooled = jnp.concatenate(chunks, axis=0)
    y = pooled * icnt_ref[...] - um.astype(F32)
    pool_out = jnp.dot(y.astype(BF16), pw_ref[...], preferred_element_type=F32) * ps_ref[...]

    cat = jnp.concatenate([pool_out.astype(BF16), at_ref[0], fo_ref[0].astype(BF16)], axis=1)
    xm = x_ref[0] + g1 * jnp.dot(cat, wo_ref[...], preferred_element_type=F32)
    xo_ref[0] = xm

    ms = jnp.mean(xm * xm, axis=-1, keepdims=True)
    h2 = (xm * lax.rsqrt(ms + EPS) * g2_ref[...]) * (1.0 + sc2) + sh2

    hh = h2.astype(BF16)
    hl = (h2 - hh.astype(F32)).astype(BF16)
    w2 = w2_ref[...]
    p2 = jnp.dot(hh, w2, preferred_element_type=F32)
    logits = (p2[:, 0:128] + p2[:, 128:256] + jnp.dot(hl, w2[:, 0:128], preferred_element_type=F32) + br_ref[...])
    meta = _route(logits.T[0:32, :], tri_ref, cnt_ref, tm).T

    rows_ref[:, 0:D] = h2
    rows_ref[:, D:ROW_W] = meta


def _out_call(x3, pu, attn, four, w_out, pool_wbd, pool_scale, bands, icnt, mod, g2, w2, br, tri, cnt_in, *,
              per_batch):
    nb, seq, _ = x3.shape
    tm = min(TM_OUT, seq)
    nt = seq // tm
    hb = tm // HALO
    full = lambda shape: pl.BlockSpec(shape, lambda b, i: (0,) * len(shape))
    in_specs = [pl.BlockSpec((1, tm, D), lambda b, i: (b, i, 0)),
                pl.BlockSpec((1, HALO, POOL_W), lambda b, i: (b, jnp.maximum(i * hb - 1, 0), 0)),
                pl.BlockSpec((1, tm, POOL_W), lambda b, i: (b, i, 0)),
                pl.BlockSpec((1, HALO, POOL_W), lambda b, i: (b, jnp.minimum((i + 1) * hb, seq // HALO - 1), 0)),
                pl.BlockSpec((1, tm, ATTN_W), lambda b, i: (b, i, 0)),
                pl.BlockSpec((1, tm, FOUR_W), lambda b, i: (b, i, 0)),
                full((D, D)), full((POOL_W, POOL_W)), full((1, POOL_W)),
                full((len(POOL_WINDOWS), POOL_CHUNK, POOL_CHUNK + 2 * HALO)),
                pl.BlockSpec((tm, POOL_W), lambda b, i: (i, 0)),
                full((8, 6 * D)), full((1, D)), full((D, 256)), full((1, 128)), full((tm, tm)), full((32, 128))]
    args = [x3, pu, pu, pu, attn, four, w_out, pool_wbd, pool_scale, bands, icnt, mod, g2, w2, br, tri, cnt_in]
    return pl.pallas_call(
        functools.partial(_out_body, tm=tm, per_batch=per_batch),
        grid=(nb, nt),
        in_specs=in_specs,
        out_specs=[pl.BlockSpec((1, tm, D), lambda b, i: (b, i, 0)),
                   pl.BlockSpec((tm, ROW_W), lambda b, i: (b * nt + i, 0)),
                   pl.BlockSpec((32, 128), lambda b, i: (0, 0))],
        out_shape=[jax.ShapeDtypeStruct((nb, seq, D), F32),
                   jax.ShapeDtypeStruct((nb * seq, ROW_W), F32),
                   jax.ShapeDtypeStruct((32, 128), F32)],
        compiler_params=_cparams(2),
        name="out_proj_router" if per_batch else "out_proj_router_ctx",
    )(*args)


def _pool_tables(seq):
    t = np.arange(POOL_CHUNK)[:, None]
    s = np.arange(POOL_CHUNK + 2 * HALO)[None, :] - HALO
    bands = np.stack([(s >= t - w // 2) & (s <= t + w // 2 - 1) for w in POOL_WINDOWS]).astype(np.float32)
    pos = np.arange(seq)
    icnt = np.stack([1.0 / (np.minimum(pos + w // 2 - 1, seq - 1) - np.maximum(pos - w // 2, 0) + 1)
                     for w in POOL_WINDOWS], axis=1)
    return jnp.asarray(bands, BF16), jnp.asarray(np.repeat(icnt, POOL_W // len(POOL_WINDOWS), axis=1), F32)


def _row_copies(tm, make_copy):
    for r in range(tm):
        make_copy(r).start(priority=r % 2)


def _row_waits(tm, make_copy):
    def drain(r, c):
        make_copy(0).wait()
        return c

    lax.fori_loop(0, tm, drain, 0, unroll=8)


def _zero_fill(ends_ref, nv_ref, xs_ref, zbuf, zsem, tm, n_out, wait):
    def piece(off, size):
        return pltpu.make_async_copy(zbuf.at[pl.ds(0, size)], xs_ref.at[pl.ds(off, size)], zsem)

    def run(cond, off, size):
        @pl.when(cond)
        def _():
            c = piece(off, size)
            c.wait() if wait else c.start()

    for b in range(N_BINS):
        off = ends_ref[b]
        pad = (tm - (off & (tm - 1))) & (tm - 1)
        for k in range(tm.bit_length() - 1):
            run(((pad >> k) & 1) == 1, off, 1 << k)
            off = off + (pad & (1 << k))

    def tail(t, c):
        c_ = piece(t * tm, tm)
        c_.wait() if wait else c_.start()
        return c

    lax.fori_loop(nv_ref[0], n_out, tail, 0)


def _scatter_body(dest_ref, ends_ref, nv_ref, *refs, tm, n_tiles, n_first, n_out):
    n_h = len(refs) - 3 - 3 * SCATTER_SLOTS
    h_refs, xs_ref = refs[:n_h], refs[n_h]
    bufs = refs[n_h + 1:n_h + 1 + SCATTER_SLOTS]
    lsems = refs[n_h + 1 + SCATTER_SLOTS:n_h + 1 + 2 * SCATTER_SLOTS]
    rsems = refs[n_h + 1 + 2 * SCATTER_SLOTS:n_h + 1 + 3 * SCATTER_SLOTS]
    zbuf, zsem = refs[-2:]
    zbuf[...] = jnp.zeros_like(zbuf)
    _zero_fill(ends_ref, nv_ref, xs_ref, zbuf, zsem, tm, n_out, wait=False)

    def lane_block_copy(h_ref, tt, slot, j):
        return pltpu.make_async_copy(h_ref.at[pl.ds(tt * tm, tm), pl.ds(128 * j, 128)],
                                     bufs[slot].at[:, j, :], lsems[slot])

    def load(t, slot):
        def start(h_ref, tt):
            for j in range(ROW_TILES):
                lane_block_copy(h_ref, tt, slot, j).start()

        if n_h == 1:
            start(h_refs[0], t)
        else:
            @pl.when(t < n_first)
            def _():
                start(h_refs[0], t)

            @pl.when(t >= n_first)
            def _():
                start(h_refs[1], t - n_first)

    def row_copy(slot, r, d):
        return pltpu.make_async_copy(bufs[slot].at[r], xs_ref.at[d], rsems[slot])

    load(0, 0)
    load(1, 1)

    def group(g, c):
        for slot in range(SCATTER_SLOTS):
            t = g * SCATTER_SLOTS + slot
            ahead = (slot + 2) % SCATTER_SLOTS
            for j in range(ROW_TILES):
                lane_block_copy(h_refs[0], 0, slot, j).wait()

            @pl.when(t >= 2)
            def _():
                _row_waits(tm, lambda r: row_copy(ahead, r, 0))

            @pl.when(t + 2 < n_tiles)
            def _():
                load(t + 2, ahead)

            _row_copies(tm, lambda r: row_copy(slot, r, dest_ref[t * tm + r]))
        return c

    lax.fori_loop(0, n_tiles // SCATTER_SLOTS, group, 0)
    for t in (n_tiles - 2, n_tiles - 1):
        _row_waits(tm, lambda r: row_copy(t % SCATTER_SLOTS, r, 0))
    _zero_fill(ends_ref, nv_ref, xs_ref, zbuf, zsem, tm, n_out, wait=True)


def _scatter_call(dest, bin_ends, n_valid, row_sets):
    tm = TM_ROW
    assert tm == TM_MOE
    n_first = row_sets[0].shape[0] // tm
    n_rows = sum(r.shape[0] for r in row_sets)
    n_tiles = n_rows // tm
    n_out = n_tiles + N_BINS
    assert n_tiles % SCATTER_SLOTS == 0 and n_tiles >= SCATTER_SLOTS
    return pl.pallas_call(
        functools.partial(_scatter_body, tm=tm, n_tiles=n_tiles, n_first=n_first, n_out=n_out),
        grid_spec=pltpu.PrefetchScalarGridSpec(
            num_scalar_prefetch=3,
            grid=(1,),
            in_specs=[pl.BlockSpec(memory_space=pl.ANY)] * len(row_sets),
            out_specs=pl.BlockSpec(memory_space=pl.ANY),
            scratch_shapes=([pltpu.VMEM((tm, ROW_TILES, 128), F32)] * SCATTER_SLOTS
                            + [pltpu.SemaphoreType.DMA(())] * (2 * SCATTER_SLOTS)
                            + [pltpu.VMEM((tm, ROW_TILES, 128), F32), pltpu.SemaphoreType.DMA(())])),
        out_shape=jax.ShapeDtypeStruct((n_out * tm, ROW_TILES, 128), F32),
        compiler_params=_cparams(1),
        name="moe_scatter_rows",
    )(dest, bin_ends, n_valid, *row_sets)


def _moe_body(ea_ref, eb_ref, nv_ref, xs_ref, wga, wua, wda, wgb, wub, wdb, ys_ref, xbuf0, xbuf1, sem0, sem1, *, tm):
    del ea_ref, eb_ref
    s = pl.program_id(0)
    nv = nv_ref[0]
    xbufs = (xbuf0, xbuf1)
    sems = (sem0, sem1)

    def lane_block_copy(t, slot, j):
        return pltpu.make_async_copy(xs_ref.at[pl.ds(t * tm, tm), j, :], xbufs[slot].at[:, pl.ds(128 * j, 128)],
                                     sems[slot])

    def fetch(t, slot):
        for j in range(ROW_TILES):
            lane_block_copy(t, slot, j).start()

    @pl.when(s == 0)
    def _():
        fetch(0, 0)

    @pl.when(s >= nv)
    def _():
        ys_ref[...] = jnp.zeros_like(ys_ref)

    for slot in range(2):
        @pl.when((s < nv) & (s % 2 == slot))
        def _():
            @pl.when(s + 1 < nv)
            def _():
                fetch(s + 1, 1 - slot)

            for j in range(ROW_TILES):
                lane_block_copy(0, slot, j).wait()
            xs = xbufs[slot]
            h = xs[:, 0:D].astype(BF16)
            meta = xs[:, D:ROW_W]

            def expert(wg, wu, wd, gate):
                g = jnp.dot(h, wg[0, 0].astype(BF16), preferred_element_type=F32)
                u = jnp.dot(h, wu[0, 0].astype(BF16), preferred_element_type=F32)
                a = _silu(g) * u * gate
                return jnp.dot(a.astype(BF16), wd[0, 0].astype(BF16), preferred_element_type=F32)

            ys_ref[...] = expert(wga, wua, wda, meta[:, 2:3]) + expert(wgb, wub, wdb, meta[:, 3:4])


def _moe_call(items, xs, w_gate, w_up, w_down, layer):
    tm = TM_MOE
    n_tiles = xs.shape[0] // tm
    wa = lambda s, ea, eb, nv: (layer, ea[s], 0, 0)
    wb = lambda s, ea, eb, nv: (layer, eb[s], 0, 0)
    up_spec = lambda f: pl.BlockSpec((1, 1, D, D_EXPERT), f)
    dn_spec = lambda f: pl.BlockSpec((1, 1, D_EXPERT, D), f)
    return pl.pallas_call(
        functools.partial(_moe_body, tm=tm),
        grid_spec=pltpu.PrefetchScalarGridSpec(
            num_scalar_prefetch=len(items),
            grid=(n_tiles,),
            in_specs=[pl.BlockSpec(memory_space=pl.ANY),
                      up_spec(wa), up_spec(wa), dn_spec(wa), up_spec(wb), up_spec(wb), dn_spec(wb)],
            out_specs=pl.BlockSpec((tm, D), lambda s, ea, eb, nv: (s, 0)),
            scratch_shapes=[pltpu.VMEM((tm, ROW_W), F32), pltpu.VMEM((tm, ROW_W), F32),
                            pltpu.SemaphoreType.DMA(()), pltpu.SemaphoreType.DMA(())]),
        out_shape=jax.ShapeDtypeStruct((xs.shape[0], D), F32),
        compiler_params=_cparams(1),
        name="moe_experts",
    )(*items, xs, w_gate, w_up, w_down, w_gate, w_up, w_down)


def _gather_body(dest_ref, x_ref, mod_ref, ys_ref, o_ref, ybuf0, ybuf1, sem0, sem1, *,
                 tm, seq, per_batch, dest_off):
    i = pl.program_id(0)
    n_tiles = pl.num_programs(0)
    ybufs = (ybuf0, ybuf1)
    sems = (sem0, sem1)
    row = (i * tm) // seq if per_batch else CTX_ROW
    g2 = mod_ref[pl.ds(row, 1), pl.ds(5 * D, D)]

    def row_copy(slot, r, d):
        return pltpu.make_async_copy(ys_ref.at[pl.ds(d, 1)], ybufs[slot].at[pl.ds(r, 1)], sems[slot])

    def fetch(t, slot):
        base = dest_off + t * tm
        _row_copies(tm, lambda r: row_copy(slot, r, dest_ref[base + r]))

    @pl.when(i == 0)
    def _():
        fetch(0, 0)

    for slot in range(2):
        @pl.when(i % 2 == slot)
        def _():
            @pl.when(i + 1 < n_tiles)
            def _():
                fetch(i + 1, 1 - slot)

            _row_waits(tm, lambda r: row_copy(slot, r, 0))
            o_ref[...] = x_ref[...] + g2 * ybufs[slot][...]


def _gather_call(dest, x2, mod, ys, *, seq, per_batch, dest_off):
    n = x2.shape[0]
    tm = min(TM_GATHER, seq)
    return pl.pallas_call(
        functools.partial(_gather_body, tm=tm, seq=seq, per_batch=per_batch, dest_off=dest_off),
        grid_spec=pltpu.PrefetchScalarGridSpec(
            num_scalar_prefetch=1,
            grid=(n // tm,),
            in_specs=[pl.BlockSpec((tm, D), lambda i, *_: (i, 0)),
                      pl.BlockSpec((8, 6 * D), lambda i, *_: (0, 0)),
                      pl.BlockSpec(memory_space=pl.ANY)],
            out_specs=pl.BlockSpec((tm, D), lambda i, *_: (i, 0)),
            scratch_shapes=[pltpu.VMEM((tm, D), F32), pltpu.VMEM((tm, D), F32),
                            pltpu.SemaphoreType.DMA(()), pltpu.SemaphoreType.DMA(())]),
        out_shape=jax.ShapeDtypeStruct((n, D), F32),
        compiler_params=_cparams(1),
        name="moe_gather_rows" if per_batch else "moe_gather_rows_ctx",
    )(dest, x2, mod, ys)


def _routing_tables(bins, rank, cnt, n_rows):
    tm = TM_MOE
    counts = cnt[:N_BINS, 0].astype(I32)
    tiles = (counts + tm - 1) // tm
    tile_end = jnp.cumsum(tiles)
    starts = (tile_end - tiles) * tm
    ids = jnp.arange(N_BINS, dtype=I32)
    pick = lambda key, tab: jnp.sum(jnp.where(key[:, None] == ids[None, :], tab[None, :], 0), axis=1)
    dest = rank + pick(bins, starts)
    n_valid = tile_end[-1]
    n_tiles = n_rows // tm + N_BINS
    s = jnp.minimum(jnp.arange(n_tiles, dtype=I32), n_valid - 1)
    tbin = jnp.sum((s[:, None] >= tile_end[None, :]).astype(I32), axis=1)
    pidx = tbin % 6
    six = jnp.arange(6, dtype=I32)
    slot = lambda tab: jnp.sum(jnp.where(pidx[:, None] == six[None, :], jnp.asarray(tab, I32)[None, :], 0), axis=1)
    ea = 4 * (tbin // 6) + slot(PAIR_SLOT_A)
    eb = 4 * (tbin // 6) + slot(PAIR_SLOT_B)
    return dest, starts + counts, (ea, eb, n_valid.reshape(1))


def _block_diag(w):
    g, c, d = w.shape
    eye = jnp.asarray(np.eye(g), w.dtype)
    return (w[:, :, None, :] * eye[:, None, :, None]).reshape(g * c, g * d)


def _rope_tables(n_tokens):
    rows = n_tokens // GRID_W
    r = np.repeat(np.arange(rows), GRID_W).astype(np.float64)
    col = np.tile(np.arange(GRID_W), rows).astype(np.float64)
    half = HEAD_DIM // 2
    inv = 1.0 / (ROPE_BASE ** (np.arange(0, half, 2, dtype=np.float64) / half))
    ar = r[:, None] * inv
    ac = col[:, None] * inv
    ang = np.concatenate([ar, ar, ac, ac], axis=-1)
    sign = np.where((np.arange(HEAD_DIM) & 16) == 0, -1.0, 1.0)
    cos, sin = np.cos(ang), np.sin(ang) * sign
    return jnp.asarray(np.tile(cos, (1, 2)), F32), jnp.asarray(np.tile(sin, (1, 2)), F32)


def kernel(x, c, ctx, c_ctx, w_mod, b_mod, norm1_g, w_in, q_norm_g, k_norm_g, attn_sink, pool_w, pool_scale,
           four_w, w_out, norm2_g, w_grp, b_grp, w_rtr, b_rtr, w_gate, w_up, w_down):
    nb, seq, _ = x.shape
    lc = ctx.shape[1]
    depth = w_mod.shape[0]
    t_lat = nb * seq
    t_ctx = nb * lc

    cs = jnp.concatenate([c, c_ctx[None, :], jnp.zeros((8 - nb - 1, D), F32)], axis=0)
    m512 = jnp.asarray(np.kron(np.eye(N_HEADS), np.full((HEAD_DIM, HEAD_DIM), 1.0 / HEAD_DIM)), BF16)
    kk = np.arange(HEAD_DIM)
    ang64 = 2.0 * np.pi * ((kk[:, None] * kk[None, :]) % HEAD_DIM) / HEAD_DIM
    c64bd = jnp.asarray(np.kron(np.eye(4), np.cos(ang64)), F32)
    s64bd = jnp.asarray(np.kron(np.eye(4), np.sin(ang64)), F32)
    tri_of = lambda n: jnp.asarray(np.triu(np.ones((min(TM_OUT, n),) * 2)), BF16)
    cos2, sin2 = _rope_tables(seq)
    four_cs, four_tabs = _four2_tables(seq)
    tabs_ctx = _dft_tables(lc, min(TM_FOUR, lc))
    pool_lat = _pool_tables(seq)
    pool_ctx = _pool_tables(lc)
    w_in_b = w_in.astype(BF16)
    w_out_b = w_out.astype(BF16)
    wbd = jnp.stack([_block_diag(four_w[l]) for l in range(depth)])
    pool_wbd = jnp.stack([_block_diag(pool_w[l]) for l in range(depth)]).astype(BF16)
    wr = jnp.concatenate([w_grp, w_rtr, jnp.zeros((depth, D, 128 - 4 - N_EXPERTS), F32)], axis=2)
    wr_hi = wr.astype(BF16)
    w2 = jnp.concatenate([wr_hi, (wr - wr_hi.astype(F32)).astype(BF16)], axis=2)
    br = jnp.concatenate([b_grp, b_rtr, jnp.zeros((depth, 128 - 4 - N_EXPERTS), F32)], axis=1)

    mod_all = _mod_call(cs, w_mod, b_mod)
    ab_all = _ab_call(c64bd, s64bd, wbd)

    xc = ctx
    for l in range(depth):
        last = l == depth - 1
        mod = mod_all[l]
        g1 = norm1_g[l][None, :]
        g2 = norm2_g[l][None, :]
        qg = jnp.tile(q_norm_g[l], N_HEADS)[None, :]
        kg = jnp.tile(k_norm_g[l], KV_W // HEAD_DIM)[None, :]
        in_args = (mod, g1, w_in_b[l], m512, qg, kg, ab_all[l])

        puc, qc, kvc, uac = _in_call(xc, *in_args, None, None, rope=False, per_batch=False)
        pul, ql, kvl, ual = _in_call(x, *in_args, cos2, sin2, rope=True, per_batch=True)
        attn_l = _attn_lat_call(attn_sink[l], ql, kvl, kvc)
        four_l = _four2_call(*_four1_call(ual, four_cs), *four_tabs)

        proj = (w_out_b[l], pool_wbd[l], pool_scale[l][None, :])
        rout = (mod, g2, w2[l], br[l][None, :])
        cnt0 = jnp.zeros((32, 128), F32)
        x_mid, rows_l, cnt = _out_call(x, pul, attn_l, four_l, *proj, *pool_lat, *rout, tri_of(seq), cnt0,
                                       per_batch=True)
        row_sets = [rows_l]
        if not last:
            attn_c = _attn_ctx_call(attn_sink[l], qc, kvc)
            four_c = _four_call(uac, *tabs_ctx)
            xc_mid, rows_c, cnt = _out_call(xc, puc, attn_c, four_c, *proj, *pool_ctx, *rout, tri_of(lc), cnt,
                                            per_batch=False)
            row_sets.append(rows_c)

        route = [r[:, D:D + 2].astype(I32) for r in row_sets]
        bins = jnp.concatenate([r[:, 0] for r in route])
        rank = jnp.concatenate([r[:, 1] for r in route])
        dest, bin_ends, items = _routing_tables(bins, rank, cnt, bins.shape[0])
        xs = _scatter_call(dest, bin_ends, items[2], row_sets)
        ys = _moe_call(items, xs, w_gate, w_up, w_down, l)
        x = _gather_call(dest, x_mid.reshape(t_lat, D), mod, ys, seq=seq, per_batch=True,
                         dest_off=0).reshape(nb, seq, D)
        if not last:
            xc = _gather_call(dest, xc_mid.reshape(t_ctx, D), mod, ys, seq=lc, per_batch=False,
                              dest_off=t_lat).reshape(nb, lc, D)
    return x
```

```python
import functools

import numpy as np
import jax
import jax.numpy as jnp
from jax import lax
from jax.experimental import pallas as pl
from jax.experimental.pallas import tpu as pltpu

F32 = jnp.float32
BF16 = jnp.bfloat16
I32 = jnp.int32
HI = lax.Precision.HIGHEST

D = 1024
HEAD_DIM = 64
N_HEADS = 8
GRID_W = 64
POOL_WINDOWS = (2, 4, 8, 16)
POOL_W = 256
ATTN_W = 512
KV_W = 128
FOUR_W = 256
IN_W = 1280
N_EXPERTS = 16
D_EXPERT = 512
WINDOW = 128
ROPE_BASE = 10000.0
EPS = 1e-6
NEG_INF = -1e30
LOG2_E = 1.4426950408889634
CTX_ROW = 4
N_BINS = 24
PAIR_SLOT_A = (0, 2, 2, 3, 3, 3)
PAIR_SLOT_B = (1, 1, 0, 0, 1, 2)
META_W = 128
ROW_W = D + META_W
ROW_TILES = ROW_W // 128
HALO = 16
POOL_CHUNK = 128

VMEM_LIMIT = 56 * 1024 * 1024
TM_IN = 1024
TM_IN_GATHER = 512
TQ = 1024
TM_OUT = 1024
TM_FOUR = 256
FOUR_R1 = 16
FOUR1_A_CHUNK = 8
TM_MOE = 256
TM_ROW = 256
TM_GATHER = 512
SCATTER_SLOTS = 4
MOD_TN = 1024


def _cparams(n_axes):
    return pltpu.CompilerParams(dimension_semantics=("arbitrary",) * n_axes,
                                vmem_limit_bytes=VMEM_LIMIT)


def _silu(v):
    return v / (1.0 + jnp.exp(-v))


def _mod_body(cs_ref, w_ref, b_ref, o_ref):
    s = _silu(cs_ref[...])
    w = w_ref[0]
    s_hi = s.astype(BF16)
    s_lo = (s - s_hi.astype(F32)).astype(BF16)
    w_hi = w.astype(BF16)
    w_lo = (w - w_hi.astype(F32)).astype(BF16)
    p = jnp.dot(jnp.concatenate([s_hi, s_lo], axis=0), w_hi, preferred_element_type=F32)
    o_ref[0] = p[0:8] + p[8:16] + jnp.dot(s_hi, w_lo, preferred_element_type=F32) + b_ref[0]


def _mod_call(cs, w_mod, b_mod):
    depth = w_mod.shape[0]
    return pl.pallas_call(
        _mod_body,
        grid=(depth, 6 * D // MOD_TN),
        in_specs=[pl.BlockSpec((8, D), lambda l, j: (0, 0)),
                  pl.BlockSpec((1, D, MOD_TN), lambda l, j: (l, 0, j)),
                  pl.BlockSpec((1, 1, MOD_TN), lambda l, j: (l, 0, j))],
        out_specs=pl.BlockSpec((1, 8, MOD_TN), lambda l, j: (l, 0, j)),
        out_shape=jax.ShapeDtypeStruct((depth, 8, 6 * D), F32),
        compiler_params=_cparams(2),
        name="modulation",
    )(cs, w_mod, b_mod.reshape(depth, 1, 6 * D))


def _ab_body(c_ref, s_ref, w_ref, o_ref):
    w = w_ref[0]
    ca = jnp.dot(c_ref[...], w, preferred_element_type=F32, precision=HI)
    sa = jnp.dot(s_ref[...], w, preferred_element_type=F32, precision=HI)
    o_ref[0] = (jnp.concatenate([ca, sa], axis=1) * (HEAD_DIM ** -0.5)).astype(BF16)


def _ab_call(c64bd, s64bd, wbd):
    depth = wbd.shape[0]
    return pl.pallas_call(
        _ab_body,
        grid=(depth,),
        in_specs=[pl.BlockSpec((FOUR_W, FOUR_W), lambda l: (0, 0)),
                  pl.BlockSpec((FOUR_W, FOUR_W), lambda l: (0, 0)),
                  pl.BlockSpec((1, FOUR_W, FOUR_W), lambda l: (l, 0, 0))],
        out_specs=pl.BlockSpec((1, FOUR_W, 2 * FOUR_W), lambda l: (l, 0, 0)),
        out_shape=jax.ShapeDtypeStruct((depth, FOUR_W, 2 * FOUR_W), BF16),
        compiler_params=_cparams(1),
        name="fourier_weights",
    )(c64bd, s64bd, wbd)


def _head_rms(t, m, g):
    ms = jnp.dot((t * t).astype(BF16), m, preferred_element_type=F32)
    return t * lax.rsqrt(ms + EPS) * g


def _rope(t, cos, sin_signed):
    w = t.shape[1]
    lane = lax.broadcasted_iota(I32, t.shape, 1)
    fwd = pltpu.roll(t, w - 16, 1)
    bwd = pltpu.roll(t, 16, 1)
    rot = jnp.where((lane & 16) == 0, fwd, bwd)
    return t * cos + rot * sin_signed


def _in_body(*refs, rope, per_batch):
    if rope:
        (x_ref, mod_ref, g1_ref, w_ref, m_ref, qg_ref, kg_ref, ab_ref, cos_ref, sin_ref,
         pu_ref, q_ref, kv_ref, ua_ref) = refs
    else:
        (x_ref, mod_ref, g1_ref, w_ref, m_ref, qg_ref, kg_ref, ab_ref,
         pu_ref, q_ref, kv_ref, ua_ref) = refs
        cos_ref = sin_ref = None
    row = pl.program_id(0) if per_batch else CTX_ROW
    _in_project(x_ref[0], row, mod_ref, g1_ref, w_ref, m_ref, qg_ref, kg_ref, ab_ref, cos_ref, sin_ref,
                pu_ref, q_ref, kv_ref, ua_ref)


def _in_gather_body(dest_ref, x_ref, modp_ref, ys_ref, mod_ref, g1_ref, w_ref, m_ref, qg_ref, kg_ref, ab_ref,
                    cos_ref, sin_ref, xo_ref, pu_ref, q_ref, kv_ref, ua_ref, ybuf0, ybuf1, sem0, sem1, *, tm):
    b = pl.program_id(0)
    t = b * pl.num_programs(1) + pl.program_id(1)
    n_tiles = pl.num_programs(0) * pl.num_programs(1)
    ybufs = (ybuf0, ybuf1)
    sems = (sem0, sem1)
    g2 = modp_ref[pl.ds(b, 1), pl.ds(5 * D, D)]

    def row_copy(slot, r, d):
        return pltpu.make_async_copy(ys_ref.at[pl.ds(d, 1)], ybufs[slot].at[pl.ds(r, 1)], sems[slot])

    def fetch(tt, slot):
        _row_copies(tm, lambda r: row_copy(slot, r, dest_ref[tt * tm + r]))

    @pl.when(t == 0)
    def _():
        fetch(0, 0)

    for slot in range(2):
        @pl.when(t % 2 == slot)
        def _():
            @pl.when(t + 1 < n_tiles)
            def _():
                fetch(t + 1, 1 - slot)

            _row_waits(tm, lambda r: row_copy(slot, r, 0))
            xo_ref[0] = x_ref[0] + g2 * ybufs[slot][...]

    _in_project(xo_ref[0], b, mod_ref, g1_ref, w_ref, m_ref, qg_ref, kg_ref, ab_ref, cos_ref, sin_ref,
                pu_ref, q_ref, kv_ref, ua_ref)


def _in_project(x, row, mod_ref, g1_ref, w_ref, m_ref, qg_ref, kg_ref, ab_ref, cos_ref, sin_ref,
                pu_ref, q_ref, kv_ref, ua_ref):
    rope = cos_ref is not None
    sh1 = mod_ref[pl.ds(row, 1), pl.ds(0, D)]
    sc1 = mod_ref[pl.ds(row, 1), pl.ds(D, D)]
    ms = jnp.mean(x * x, axis=-1, keepdims=True)
    h = (x * lax.rsqrt(ms + EPS) * g1_ref[...]) * (1.0 + sc1) + sh1
    p = jnp.dot(h.astype(BF16), w_ref[...], preferred_element_type=F32)
    pu = p[:, 0:256]
    q = p[:, 256:768]
    k = p[:, 768:896]
    v = p[:, 896:1024]
    fu = p[:, 1024:1280]
    m = m_ref[...]
    q = _head_rms(q, m, qg_ref[...])
    k = _head_rms(k, m[0:KV_W, 0:KV_W], kg_ref[...])
    if rope:
        cos = cos_ref[...]
        sin = sin_ref[...]
        q = _rope(q, jnp.concatenate([cos] * 4, axis=1), jnp.concatenate([sin] * 4, axis=1))
        k = _rope(k, cos, sin)
    q = q * (HEAD_DIM ** -0.5 * LOG2_E)
    pu_ref[0] = pu.astype(BF16)
    q_ref[0] = q.astype(BF16)
    kv_ref[0] = jnp.concatenate([k, pltpu.roll(k, 64, 1), v, pltpu.roll(v, 64, 1)], axis=1).astype(BF16)
    ua_ref[0] = jnp.dot(fu.astype(BF16), ab_ref[...], preferred_element_type=F32).astype(BF16)


def _in_call(x3, mod, g1, w_in, m512, qg, kg, ab, cos2, sin2, *, rope, per_batch):
    nb, seq, _ = x3.shape
    tm = min(TM_IN, seq)
    full = lambda shape: pl.BlockSpec(shape, lambda b, i: (0,) * len(shape))
    in_specs = [pl.BlockSpec((1, tm, D), lambda b, i: (b, i, 0)),
                full((8, 6 * D)), full((1, D)), full((D, IN_W)), full((ATTN_W, ATTN_W)),
                full((1, ATTN_W)), full((1, KV_W)), full((FOUR_W, 2 * FOUR_W))]
    args = [x3, mod, g1, w_in, m512, qg, kg, ab]
    if rope:
        in_specs += [pl.BlockSpec((tm, 128), lambda b, i: (i, 0)),
                     pl.BlockSpec((tm, 128), lambda b, i: (i, 0))]
        args += [cos2, sin2]
    widths = (POOL_W, ATTN_W, 4 * KV_W, 2 * FOUR_W)
    return pl.pallas_call(
        functools.partial(_in_body, rope=rope, per_batch=per_batch),
        grid=(nb, seq // tm),
        in_specs=in_specs,
        out_specs=[pl.BlockSpec((1, tm, w), lambda b, i: (b, i, 0)) for w in widths],
        out_shape=[jax.ShapeDtypeStruct((nb, seq, w), BF16) for w in widths],
        compiler_params=_cparams(2),
        name="in_proj_rope" if rope else "in_proj_ctx",
    )(*args)


def _in_gather_call(dest, x_mid, mod_prev, ys, mod, g1, w_in, m512, qg, kg, ab, cos2, sin2):
    nb, seq, _ = x_mid.shape
    tm = min(TM_IN_GATHER, seq)
    full = lambda shape: pl.BlockSpec(shape, lambda b, i, d: (0,) * len(shape))
    tile = lambda w: pl.BlockSpec((1, tm, w), lambda b, i, d: (b, i, 0))
    widths = (POOL_W, ATTN_W, 4 * KV_W, 2 * FOUR_W)
    return pl.pallas_call(
        functools.partial(_in_gather_body, tm=tm),
        grid_spec=pltpu.PrefetchScalarGridSpec(
            num_scalar_prefetch=1,
            grid=(nb, seq // tm),
            in_specs=[tile(D), full((8, 6 * D)), pl.BlockSpec(memory_space=pl.ANY),
                      full((8, 6 * D)), full((1, D)), full((D, IN_W)), full((ATTN_W, ATTN_W)),
                      full((1, ATTN_W)), full((1, KV_W)), full((FOUR_W, 2 * FOUR_W)),
                      pl.BlockSpec((tm, 128), lambda b, i, d: (i, 0)),
                      pl.BlockSpec((tm, 128), lambda b, i, d: (i, 0))],
            out_specs=[tile(D)] + [tile(w) for w in widths],
            scratch_shapes=[pltpu.VMEM((tm, D), F32), pltpu.VMEM((tm, D), F32),
                            pltpu.SemaphoreType.DMA(()), pltpu.SemaphoreType.DMA(())]),
        out_shape=[jax.ShapeDtypeStruct((nb, seq, D), F32)]
                  + [jax.ShapeDtypeStruct((nb, seq, w), BF16) for w in widths],
        compiler_params=_cparams(2),
        name="in_proj_gather",
    )(dest, x_mid, mod_prev, ys, mod, g1, w_in, m512, qg, kg, ab, cos2, sin2)


_NT = (((1,), (1,)), ((), ()))


def _stack_heads(qpair0, qpair1, lo):
    z = jnp.zeros_like(qpair0)
    parts = [jnp.where(lo, qpair0, z), jnp.where(lo, qpair1, z),
             jnp.where(lo, pltpu.roll(qpair0, 64, 1), z), jnp.where(lo, pltpu.roll(qpair1, 64, 1), z)]
    return jnp.concatenate(parts, axis=0).astype(BF16)


def _group_attention(q4, k_parts, va_parts, vb_parts, masks, sink_col):
    s_parts = []
    for kz, mk in zip(k_parts, masks):
        s = lax.dot_general(q4, kz, _NT, preferred_element_type=F32)
        if mk is not None:
            nk = s.shape[1]
            s = jnp.where(mk[None], s.reshape(4, 128, nk), NEG_INF).reshape(512, nk)
        s_parts.append(s)
    m = functools.reduce(jnp.maximum, [jnp.max(s, axis=-1, keepdims=True) for s in s_parts])
    m = jnp.maximum(m, sink_col)
    den = jnp.exp2(sink_col - m)
    oe = oo = None
    for s, va, vb in zip(s_parts, va_parts, vb_parts):
        e = jnp.exp2(s - m)
        den = den + jnp.sum(e, axis=-1, keepdims=True)
        eb = e.astype(BF16)
        pe = jnp.dot(eb[0:256], va, preferred_element_type=F32)
        po = jnp.dot(eb[256:512], vb, preferred_element_type=F32)
        oe = pe if oe is None else oe + pe
        oo = po if oo is None else oo + po
    inv = 1.0 / den
    return oe * inv[0:256], oo * inv[256:512]


def _sink_cols(sink_ref):
    rb = lax.broadcasted_iota(I32, (512, 1), 0) >> 7
    cols = []
    for kvh in range(2):
        s = [sink_ref[4 * kvh + j] * LOG2_E for j in (0, 2, 1, 3)]
        cols.append(jnp.where(rb == 0, s[0], jnp.where(rb == 1, s[1], jnp.where(rb == 2, s[2], s[3]))))
    return cols


def _attend_block(qblk, kv_parts, masks, sink_cols):
    lo = lax.broadcasted_iota(I32, (128, 128), 1) < 64
    cols = []
    for kvh in range(2):
        q4 = _stack_heads(qblk[:, 256 * kvh:256 * kvh + 128], qblk[:, 256 * kvh + 128:256 * kvh + 256], lo)
        ko = 128 * kvh
        vao = 256 + 128 * kvh
        vbo = 384 - 128 * kvh
        oe, oo = _group_attention(q4, [kv[:, ko:ko + 128] for kv in kv_parts],
                                  [kv[:, vao:vao + 128] for kv in kv_parts],
                                  [kv[:, vbo:vbo + 128] for kv in kv_parts], masks, sink_cols[kvh])
        cols.append(jnp.where(lo, oe[0:128], oo[0:128]))
        cols.append(jnp.where(lo, oe[128:256], oo[128:256]))
    return jnp.concatenate(cols, axis=1).astype(BF16)


def _attn_lat_body(sink_ref, q_ref, kvp_ref, kvm_ref, kvn_ref, kvc_ref, o_ref, kvw_ref, *, tq, seq):
    i = pl.program_id(1)
    kvw_ref[0:128] = kvp_ref[0]
    kvw_ref[128:128 + tq] = kvm_ref[0]
    kvw_ref[128 + tq:256 + tq] = kvn_ref[0]
    kvc = kvc_ref[0]
    sink_cols = _sink_cols(sink_ref)

    def sub(j, carry):
        r0 = pl.multiple_of(j * 128, 128)
        win = kvw_ref[pl.ds(r0, 3 * 128), :]
        ii = lax.broadcasted_iota(I32, (128, 3 * 128), 0)
        cc = lax.broadcasted_iota(I32, (128, 3 * 128), 1)
        base = i * tq + j * 128 - 128
        valid = (ii <= cc) & (cc <= ii + 2 * WINDOW) & (cc >= -base) & (cc < seq - base)
        qblk = q_ref[0, pl.ds(r0, 128), :].astype(F32)
        o_ref[0, pl.ds(r0, 128), :] = _attend_block(qblk, [win, kvc], [valid, None], sink_cols)
        return carry

    for j in range(tq // 128):
        sub(j, 0)


def _attn_lat_call(sink, q, kv, kvc):
    nb, seq, _ = q.shape
    lc = kvc.shape[1]
    tq = TQ
    nblk = seq // 128
    r = tq // 128
    return pl.pallas_call(
        functools.partial(_attn_lat_body, tq=tq, seq=seq),
        grid=(nb, seq // tq),
        in_specs=[pl.BlockSpec(memory_space=pltpu.SMEM),
                  pl.BlockSpec((1, tq, ATTN_W), lambda b, i: (b, i, 0)),
                  pl.BlockSpec((1, 128, 4 * KV_W), lambda b, i: (b, jnp.maximum(i * r - 1, 0), 0)),
                  pl.BlockSpec((1, tq, 4 * KV_W), lambda b, i: (b, i, 0)),
                  pl.BlockSpec((1, 128, 4 * KV_W), lambda b, i: (b, jnp.minimum((i + 1) * r, nblk - 1), 0)),
                  pl.BlockSpec((1, lc, 4 * KV_W), lambda b, i: (b, 0, 0))],
        out_specs=pl.BlockSpec((1, tq, ATTN_W), lambda b, i: (b, i, 0)),
        out_shape=jax.ShapeDtypeStruct((nb, seq, ATTN_W), BF16),
        scratch_shapes=[pltpu.VMEM((tq + 256, 4 * KV_W), BF16)],
        compiler_params=_cparams(2),
        name="attention_window",
    )(sink, q, kv, kv, kv, kvc)


def _attn_ctx_body(sink_ref, q_ref, kvc_ref, o_ref, *, lc):
    kvc = kvc_ref[0]
    sink_cols = _sink_cols(sink_ref)
    for j in range(lc // 128):
        qblk = q_ref[0, j * 128:(j + 1) * 128, :].astype(F32)
        o_ref[0, j * 128:(j + 1) * 128, :] = _attend_block(qblk, [kvc], [None], sink_cols)


def _attn_ctx_call(sink, qc, kvc):
    nb, lc, _ = qc.shape
    return pl.pallas_call(
        functools.partial(_attn_ctx_body, lc=lc),
        grid=(nb,),
        in_specs=[pl.BlockSpec(memory_space=pltpu.SMEM),
                  pl.BlockSpec((1, lc, ATTN_W), lambda b: (b, 0, 0)),
                  pl.BlockSpec((1, lc, 4 * KV_W), lambda b: (b, 0, 0))],
        out_specs=pl.BlockSpec((1, lc, ATTN_W), lambda b: (b, 0, 0)),
        out_shape=jax.ShapeDtypeStruct((nb, lc, ATTN_W), BF16),
        compiler_params=_cparams(1),
        name="attention_ctx",
    )(sink, qc, kvc)


def _four_body(ua_ref, cb_ref, sb_ref, ca_ref, sa_ref, o_ref, *, nb, scale):
    i = pl.program_id(0)
    ca = ca_ref[pl.ds(i, 1), :]
    sa = sa_ref[pl.ds(i, 1), :]
    cb = cb_ref[...]
    sb = sb_ref[...]
    ct = (ca * cb - sa * sb).astype(BF16)
    st = (sa * cb + ca * sb).astype(BF16)
    for b in range(nb):
        ua = ua_ref[b, :, 0:FOUR_W]
        ub = ua_ref[b, :, FOUR_W:2 * FOUR_W]
        r = (jnp.dot(ct, ua, preferred_element_type=F32) - jnp.dot(st, ub, preferred_element_type=F32))
        o_ref[b] = (r * scale).astype(BF16)


def _four_call(uaub, cb, sb, ca, sa):
    nb, seq, _ = uaub.shape
    tm = cb.shape[0]
    one = pl.Buffered(1)
    return pl.pallas_call(
        functools.partial(_four_body, nb=nb, scale=float(seq) ** -0.5),
        grid=(seq // tm,),
        in_specs=[pl.BlockSpec((nb, seq, 2 * FOUR_W), lambda i: (0, 0, 0), pipeline_mode=one),
                  pl.BlockSpec((tm, seq), lambda i: (0, 0), pipeline_mode=one),
                  pl.BlockSpec((tm, seq), lambda i: (0, 0), pipeline_mode=one),
                  pl.BlockSpec((seq // tm, seq), lambda i: (0, 0), pipeline_mode=one),
                  pl.BlockSpec((seq // tm, seq), lambda i: (0, 0), pipeline_mode=one)],
        out_specs=pl.BlockSpec((nb, tm, FOUR_W), lambda i: (0, i, 0)),
        out_shape=jax.ShapeDtypeStruct((nb, seq, FOUR_W), BF16),
        compiler_params=_cparams(1),
        name="fourier_dft",
    )(uaub, cb, sb, ca, sa)


def _dft_tables(seq, tm):
    n = np.arange(seq)[None, :]

    def tab(rows):
        ang = ((rows[:, None] * n) % seq) * (2.0 * np.pi / seq)
        return jnp.asarray(np.cos(ang), F32), jnp.asarray(np.sin(ang), F32)

    cb, sb = tab(np.arange(tm))
    ca, sa = tab(np.arange(seq // tm) * tm)
    return cb, sb, ca, sa


def _four1_body(x_ref, g_ref, yr_ref, yi_ref):
    g = g_ref[...].astype(BF16)
    n = FOUR_R1 * 16
    w = FOUR_W
    for aa in range(x_ref.shape[2]):
        x = x_ref[0, :, aa].reshape(n, 2 * w)
        p = jnp.dot(g, x, preferred_element_type=F32)
        yr = p[0:n, 0:w] - p[n:2 * n, w:2 * w]
        yi = -(p[0:n, w:2 * w] + p[n:2 * n, 0:w])
        yr_ref[0, :, aa] = yr.astype(BF16).reshape(FOUR_R1, 16, w)
        yi_ref[0, :, aa] = yi.astype(BF16).reshape(FOUR_R1, 16, w)


def _four1_call(uaub, g):
    nb, seq, _ = uaub.shape
    r2 = seq // FOUR_R1
    na = r2 // 16
    x = uaub.reshape(nb, FOUR_R1, na, 16, 2 * FOUR_W)
    out = jax.ShapeDtypeStruct((nb, FOUR_R1, na, 16, FOUR_W), BF16)
    ac = FOUR1_A_CHUNK
    return pl.pallas_call(
        _four1_body,
        grid=(nb, na // ac),
        in_specs=[pl.BlockSpec((1, FOUR_R1, ac, 16, 2 * FOUR_W), lambda b, j: (b, 0, j, 0, 0)),
                  pl.BlockSpec(g.shape, lambda b, j: (0, 0))],
        out_specs=[pl.BlockSpec((1, FOUR_R1, ac, 16, FOUR_W), lambda b, j: (b, 0, j, 0, 0))] * 2,
        out_shape=[out, out],
        compiler_params=_cparams(2),
        name="fourier_stage1",
    )(x, g)


def _four2_body(yr_ref, yi_ref, c_ref, s_ref, ca_ref, sa_ref, o_ref, obuf, sem, *, nb, scale):
    k1 = pl.program_id(0)
    nk = pl.num_programs(0)
    ca = ca_ref[pl.ds(k1, 1), :]
    sa = sa_ref[pl.ds(k1, 1), :]
    c = c_ref[...]
    s = s_ref[...]
    gc = (c * ca - s * sa).astype(BF16)
    gs = (s * ca + c * sa).astype(BF16)

    def out_copy(slot, b, kk):
        return pltpu.make_async_copy(obuf.at[slot, b], o_ref.at[b, :, kk, :], sem.at[slot])

    for slot in range(2):
        @pl.when(k1 % 2 == slot)
        def _():
            @pl.when(k1 >= 2)
            def _():
                for b in range(nb):
                    out_copy(slot, b, 0).wait()

            for b in range(nb):
                acc = (jnp.dot(gc, yr_ref[b, 0], preferred_element_type=F32)
                       + jnp.dot(gs, yi_ref[b, 0], preferred_element_type=F32))
                obuf[slot, b] = acc * scale
            for b in range(nb):
                out_copy(slot, b, k1).start()

    @pl.when(k1 == nk - 1)
    def _():
        for slot in range(2):
            for b in range(nb):
                out_copy(slot, b, 0).wait()


def _four2_call(yr, yi, c, s, ca, sa):
    nb = yr.shape[0]
    r2 = c.shape[0]
    seq = FOUR_R1 * r2
    yr4 = yr.reshape(nb, FOUR_R1, r2, FOUR_W)
    yi4 = yi.reshape(nb, FOUR_R1, r2, FOUR_W)
    full = lambda shape: pl.BlockSpec(shape, lambda k: (0,) * len(shape))
    out = pl.pallas_call(
        functools.partial(_four2_body, nb=nb, scale=float(seq) ** -0.5),
        grid=(FOUR_R1,),
        in_specs=[pl.BlockSpec((nb, 1, r2, FOUR_W), lambda k: (0, k, 0, 0)),
                  pl.BlockSpec((nb, 1, r2, FOUR_W), lambda k: (0, k, 0, 0)),
                  full((r2, r2)), full((r2, r2)), full((FOUR_R1, r2)), full((FOUR_R1, r2))],
        out_specs=pl.BlockSpec(memory_space=pl.ANY),
        out_shape=jax.ShapeDtypeStruct((nb, r2, FOUR_R1, FOUR_W), F32),
        scratch_shapes=[pltpu.VMEM((2, nb, r2, FOUR_W), F32), pltpu.SemaphoreType.DMA((2,))],
        compiler_params=_cparams(1),
        name="fourier_stage2",
    )(yr4, yi4, c, s, ca, sa)
    return out.reshape(nb, seq, FOUR_W)


def _four2_tables(seq):
    r1 = FOUR_R1
    r2 = seq // r1
    assert r1 * r2 == seq and r2 % 16 == 0
    k = np.arange(r1)
    ang1 = ((k[:, None] * k[None, :]) % r1) * (2.0 * np.pi / r1)
    eye = np.eye(16)
    g = np.concatenate([np.kron(np.cos(ang1), eye), np.kron(np.sin(ang1), eye)], axis=0)
    m = np.arange(r2)
    ang2 = ((m[:, None] * m[None, :]) % r2) * (2.0 * np.pi / r2)
    alpha = ((k[:, None] * m[None, :]) % seq) * (2.0 * np.pi / seq)
    f = lambda t: jnp.asarray(t, F32)
    return f(g), (f(np.cos(ang2)), f(np.sin(ang2)), f(np.cos(alpha)), f(np.sin(alpha)))


def _route(lt, tri_ref, cnt_ref, tm):
    rowi = lax.broadcasted_iota(I32, (32, tm), 0)
    big = jnp.int32(999)

    def first_argmax(vals):
        mx = jnp.max(vals, axis=0, keepdims=True)
        return mx, jnp.min(jnp.where(vals == mx, rowi, big), axis=0, keepdims=True)

    is_grp = rowi < 4
    mg, gi = first_argmax(jnp.where(is_grp, lt, NEG_INF))
    pg = 1.0 / jnp.sum(jnp.where(is_grp, jnp.exp(jnp.where(is_grp, lt, mg) - mg), 0.0), axis=0, keepdims=True)
    est = 4 + 4 * gi
    le = jnp.where((rowi >= est) & (rowi < est + 4), lt, NEG_INF)
    m1, i1 = first_argmax(le)
    m2, i2 = first_argmax(jnp.where(rowi == i1, NEG_INF, le))
    e2 = jnp.exp(m2 - m1)
    w1 = pg / (1.0 + e2)
    w2 = pg * e2 / (1.0 + e2)
    a1 = i1 - est
    a2 = i2 - est
    code = jnp.minimum(a1, a2) * 4 + jnp.maximum(a1, a2)
    pidx = jnp.where(code == 1, 0, jnp.where(code == 6, 1, jnp.where(code == 2, 2,
           jnp.where(code == 3, 3, jnp.where(code == 7, 4, 5)))))
    slot_a = jnp.where(pidx == 0, 0, jnp.where(pidx <= 2, 2, 3))
    slot_b = jnp.where(pidx <= 1, 1, jnp.where(pidx <= 3, 0, jnp.where(pidx == 4, 1, 2)))
    wa = jnp.where(a1 == slot_a, w1, w2)
    wb = jnp.where(a1 == slot_b, w1, w2)
    bin_ = gi * 6 + pidx

    onehot = rowi == bin_
    pref = jnp.dot(onehot.astype(BF16), tri_ref[...], preferred_element_type=F32)
    carry = cnt_ref[:, 0:1]
    rank = jnp.sum(jnp.where(onehot, pref - 1.0 + carry, 0.0), axis=0, keepdims=True)
    cnt_ref[...] = jnp.broadcast_to(carry + pref[:, tm - 1:tm], cnt_ref.shape)
    return jnp.concatenate([bin_.astype(F32), rank, wa, wb, jnp.zeros((128 - 4, tm), F32)], axis=0)


def _out_body(x_ref, pup_ref, pum_ref, pun_ref, at_ref, fo_ref, wo_ref, pw_ref, ps_ref, band_ref, icnt_ref,
              mod_ref, g2_ref, w2_ref, br_ref, tri_ref, cin_ref,
              xo_ref, rows_ref, cnt_ref, *, tm, per_batch):
    b = pl.program_id(0)
    i = pl.program_id(1)
    nt = pl.num_programs(1)
    row = b if per_batch else CTX_ROW
    g1 = mod_ref[pl.ds(row, 1), pl.ds(2 * D, D)]
    sh2 = mod_ref[pl.ds(row, 1), pl.ds(3 * D, D)]
    sc2 = mod_ref[pl.ds(row, 1), pl.ds(4 * D, D)]

    @pl.when((b == 0) & (i == 0))
    def _():
        cnt_ref[...] = cin_ref[...]

    um = pum_ref[0]
    zh = jnp.zeros((HALO, POOL_W), BF16)
    uext = jnp.concatenate([jnp.where(i > 0, pup_ref[0], zh), um, jnp.where(i < nt - 1, pun_ref[0], zh)], axis=0)
    grp = lax.broadcasted_iota(I32, (POOL_CHUNK, POOL_W), 1) >> 6
    chunks = []
    for c in range(tm // POOL_CHUNK):
        uc = uext[POOL_CHUNK * c:POOL_CHUNK * (c + 1) + 2 * HALO]
        pc = jnp.zeros((POOL_CHUNK, POOL_W), F32)
        for g in range(len(POOL_WINDOWS)):
            pc = jnp.where(grp == g, jnp.dot(band_ref[g], uc, preferred_element_type=F32), pc)
        chunks.append(pc)
    pooled = jnp.concatenate(chunks, axis=0)
    y = pooled * icnt_ref[...] - um.astype(F32)
    pool_out = jnp.dot(y.astype(BF16), pw_ref[...], preferred_element_type=F32) * ps_ref[...]

    cat = jnp.concatenate([pool_out.astype(BF16), at_ref[0], fo_ref[0].astype(BF16)], axis=1)
    xm = x_ref[0] + g1 * jnp.dot(cat, wo_ref[...], preferred_element_type=F32)
    xo_ref[0] = xm

    ms = jnp.mean(xm * xm, axis=-1, keepdims=True)
    h2 = (xm * lax.rsqrt(ms + EPS) * g2_ref[...]) * (1.0 + sc2) + sh2

    hh = h2.astype(BF16)
    hl = (h2 - hh.astype(F32)).astype(BF16)
    w2 = w2_ref[...]
    p2 = jnp.dot(hh, w2, preferred_element_type=F32)
    logits = (p2[:, 0:128] + p2[:, 128:256] + jnp.dot(hl, w2[:, 0:128], preferred_element_type=F32) + br_ref[...])
    meta = _route(logits.T[0:32, :], tri_ref, cnt_ref, tm).T

    rows_ref[:, 0:D] = h2
    rows_ref[:, D:ROW_W] = meta


def _out_call(x3, pu, attn, four, w_out, pool_wbd, pool_scale, bands, icnt, mod, g2, w2, br, tri, cnt_in, *,
              per_batch):
    nb, seq, _ = x3.shape
    tm = min(TM_OUT, seq)
    nt = seq // tm
    hb = tm // HALO
    full = lambda shape: pl.BlockSpec(shape, lambda b, i: (0,) * len(shape))
    in_specs = [pl.BlockSpec((1, tm, D), lambda b, i: (b, i, 0)),
                pl.BlockSpec((1, HALO, POOL_W), lambda b, i: (b, jnp.maximum(i * hb - 1, 0), 0)),
                pl.BlockSpec((1, tm, POOL_W), lambda b, i: (b, i, 0)),
                pl.BlockSpec((1, HALO, POOL_W), lambda b, i: (b, jnp.minimum((i + 1) * hb, seq // HALO - 1), 0)),
                pl.BlockSpec((1, tm, ATTN_W), lambda b, i: (b, i, 0)),
                pl.BlockSpec((1, tm, FOUR_W), lambda b, i: (b, i, 0)),
                full((D, D)), full((POOL_W, POOL_W)), full((1, POOL_W)),
                full((len(POOL_WINDOWS), POOL_CHUNK, POOL_CHUNK + 2 * HALO)),
                pl.BlockSpec((tm, POOL_W), lambda b, i: (i, 0)),
                full((8, 6 * D)), full((1, D)), full((D, 256)), full((1, 128)), full((tm, tm)), full((32, 128))]
    args = [x3, pu, pu, pu, attn, four, w_out, pool_wbd, pool_scale, bands, icnt, mod, g2, w2, br, tri, cnt_in]
    return pl.pallas_call(
        functools.partial(_out_body, tm=tm, per_batch=per_batch),
        grid=(nb, nt),
        in_specs=in_specs,
        out_specs=[pl.BlockSpec((1, tm, D), lambda b, i: (b, i, 0)),
                   pl.BlockSpec((tm, ROW_W), lambda b, i: (b * nt + i, 0)),
                   pl.BlockSpec((32, 128), lambda b, i: (0, 0))],
        out_shape=[jax.ShapeDtypeStruct((nb, seq, D), F32),
                   jax.ShapeDtypeStruct((nb * seq, ROW_W), F32),
                   jax.ShapeDtypeStruct((32, 128), F32)],
        compiler_params=_cparams(2),
        name="out_proj_router" if per_batch else "out_proj_router_ctx",
    )(*args)


def _pool_tables(seq):
    t = np.arange(POOL_CHUNK)[:, None]
    s = np.arange(POOL_CHUNK + 2 * HALO)[None, :] - HALO
    bands = np.stack([(s >= t - w // 2) & (s <= t + w // 2 - 1) for w in POOL_WINDOWS]).astype(np.float32)
    pos = np.arange(seq)
    icnt = np.stack([1.0 / (np.minimum(pos + w // 2 - 1, seq - 1) - np.maximum(pos - w // 2, 0) + 1)
                     for w in POOL_WINDOWS], axis=1)
    return jnp.asarray(bands, BF16), jnp.asarray(np.repeat(icnt, POOL_W // len(POOL_WINDOWS), axis=1), F32)


def _row_copies(tm, make_copy):
    for r in range(tm):
        make_copy(r).start(priority=r % 2)


def _row_waits(tm, make_copy):
    def drain(r, c):
        make_copy(0).wait()
        return c

    lax.fori_loop(0, tm, drain, 0, unroll=8)


def _zero_fill(ends_ref, nv_ref, xs_ref, zbuf, zsem, tm, n_out, wait):
    def piece(off, size):
        return pltpu.make_async_copy(zbuf.at[pl.ds(0, size)], xs_ref.at[pl.ds(off, size)], zsem)

    def run(cond, off, size):
        @pl.when(cond)
        def _():
            c = piece(off, size)
            c.wait() if wait else c.start()

    for b in range(N_BINS):
        off = ends_ref[b]
        pad = (tm - (off & (tm - 1))) & (tm - 1)
        for k in range(tm.bit_length() - 1):
            run(((pad >> k) & 1) == 1, off, 1 << k)
            off = off + (pad & (1 << k))

    def tail(t, c):
        c_ = piece(t * tm, tm)
        c_.wait() if wait else c_.start()
        return c

    lax.fori_loop(nv_ref[0], n_out, tail, 0)


def _scatter_body(dest_ref, ends_ref, nv_ref, *refs, tm, n_tiles, n_first, n_out):
    n_h = len(refs) - 3 - 3 * SCATTER_SLOTS
    h_refs, xs_ref = refs[:n_h], refs[n_h]
    bufs = refs[n_h + 1:n_h + 1 + SCATTER_SLOTS]
    lsems = refs[n_h + 1 + SCATTER_SLOTS:n_h + 1 + 2 * SCATTER_SLOTS]
    rsems = refs[n_h + 1 + 2 * SCATTER_SLOTS:n_h + 1 + 3 * SCATTER_SLOTS]
    zbuf, zsem = refs[-2:]
    zbuf[...] = jnp.zeros_like(zbuf)
    _zero_fill(ends_ref, nv_ref, xs_ref, zbuf, zsem, tm, n_out, wait=False)

    def lane_block_copy(h_ref, tt, slot, j):
        return pltpu.make_async_copy(h_ref.at[pl.ds(tt * tm, tm), pl.ds(128 * j, 128)],
                                     bufs[slot].at[:, j, :], lsems[slot])

    def load(t, slot):
        def start(h_ref, tt):
            for j in range(ROW_TILES):
                lane_block_copy(h_ref, tt, slot, j).start()

        if n_h == 1:
            start(h_refs[0], t)
        else:
            @pl.when(t < n_first)
            def _():
                start(h_refs[0], t)

            @pl.when(t >= n_first)
            def _():
                start(h_refs[1], t - n_first)

    def row_copy(slot, r, d):
        return pltpu.make_async_copy(bufs[slot].at[r], xs_ref.at[d], rsems[slot])

    load(0, 0)
    load(1, 1)

    def group(g, c):
        for slot in range(SCATTER_SLOTS):
            t = g * SCATTER_SLOTS + slot
            ahead = (slot + 2) % SCATTER_SLOTS
            for j in range(ROW_TILES):
                lane_block_copy(h_refs[0], 0, slot, j).wait()

            @pl.when(t >= 2)
            def _():
                _row_waits(tm, lambda r: row_copy(ahead, r, 0))

            @pl.when(t + 2 < n_tiles)
            def _():
                load(t + 2, ahead)

            _row_copies(tm, lambda r: row_copy(slot, r, dest_ref[t * tm + r]))
        return c

    lax.fori_loop(0, n_tiles // SCATTER_SLOTS, group, 0)
    for t in (n_tiles - 2, n_tiles - 1):
        _row_waits(tm, lambda r: row_copy(t % SCATTER_SLOTS, r, 0))
    _zero_fill(ends_ref, nv_ref, xs_ref, zbuf, zsem, tm, n_out, wait=True)


def _scatter_call(dest, bin_ends, n_valid, row_sets):
    tm = TM_ROW
    assert tm == TM_MOE
    n_first = row_sets[0].shape[0] // tm
    n_rows = sum(r.shape[0] for r in row_sets)
    n_tiles = n_rows // tm
    n_out = n_tiles + N_BINS
    assert n_tiles % SCATTER_SLOTS == 0 and n_tiles >= SCATTER_SLOTS
    return pl.pallas_call(
        functools.partial(_scatter_body, tm=tm, n_tiles=n_tiles, n_first=n_first, n_out=n_out),
        grid_spec=pltpu.PrefetchScalarGridSpec(
            num_scalar_prefetch=3,
            grid=(1,),
            in_specs=[pl.BlockSpec(memory_space=pl.ANY)] * len(row_sets),
            out_specs=pl.BlockSpec(memory_space=pl.ANY),
            scratch_shapes=([pltpu.VMEM((tm, ROW_TILES, 128), F32)] * SCATTER_SLOTS
                            + [pltpu.SemaphoreType.DMA(())] * (2 * SCATTER_SLOTS)
                            + [pltpu.VMEM((tm, ROW_TILES, 128), F32), pltpu.SemaphoreType.DMA(())])),
        out_shape=jax.ShapeDtypeStruct((n_out * tm, ROW_TILES, 128), F32),
        compiler_params=_cparams(1),
        name="moe_scatter_rows",
    )(dest, bin_ends, n_valid, *row_sets)


def _moe_body(ea_ref, eb_ref, nv_ref, xs_ref, wga, wua, wda, wgb, wub, wdb, ys_ref, xbuf0, xbuf1, sem0, sem1, *, tm):
    del ea_ref, eb_ref
    s = pl.program_id(0)
    nv = nv_ref[0]
    xbufs = (xbuf0, xbuf1)
    sems = (sem0, sem1)

    def lane_block_copy(t, slot, j):
        return pltpu.make_async_copy(xs_ref.at[pl.ds(t * tm, tm), j, :], xbufs[slot].at[:, pl.ds(128 * j, 128)],
                                     sems[slot])

    def fetch(t, slot):
        for j in range(ROW_TILES):
            lane_block_copy(t, slot, j).start()

    @pl.when(s == 0)
    def _():
        fetch(0, 0)

    @pl.when(s >= nv)
    def _():
        ys_ref[...] = jnp.zeros_like(ys_ref)

    for slot in range(2):
        @pl.when((s < nv) & (s % 2 == slot))
        def _():
            @pl.when(s + 1 < nv)
            def _():
                fetch(s + 1, 1 - slot)

            for j in range(ROW_TILES):
                lane_block_copy(0, slot, j).wait()
            xs = xbufs[slot]
            h = xs[:, 0:D].astype(BF16)
            meta = xs[:, D:ROW_W]

            def expert(wg, wu, wd, gate):
                g = jnp.dot(h, wg[0, 0].astype(BF16), preferred_element_type=F32)
                u = jnp.dot(h, wu[0, 0].astype(BF16), preferred_element_type=F32)
                a = _silu(g) * u * gate
                return jnp.dot(a.astype(BF16), wd[0, 0].astype(BF16), preferred_element_type=F32)

            ys_ref[...] = expert(wga, wua, wda, meta[:, 2:3]) + expert(wgb, wub, wdb, meta[:, 3:4])


def _moe_call(items, xs, w_gate, w_up, w_down, layer):
    tm = TM_MOE
    n_tiles = xs.shape[0] // tm
    wa = lambda s, ea, eb, nv: (layer, ea[s], 0, 0)
    wb = lambda s, ea, eb, nv: (layer, eb[s], 0, 0)
    up_spec = lambda f: pl.BlockSpec((1, 1, D, D_EXPERT), f)
    dn_spec = lambda f: pl.BlockSpec((1, 1, D_EXPERT, D), f)
    return pl.pallas_call(
        functools.partial(_moe_body, tm=tm),
        grid_spec=pltpu.PrefetchScalarGridSpec(
            num_scalar_prefetch=len(items),
            grid=(n_tiles,),
            in_specs=[pl.BlockSpec(memory_space=pl.ANY),
                      up_spec(wa), up_spec(wa), dn_spec(wa), up_spec(wb), up_spec(wb), dn_spec(wb)],
            out_specs=pl.BlockSpec((tm, D), lambda s, ea, eb, nv: (s, 0)),
            scratch_shapes=[pltpu.VMEM((tm, ROW_W), F32), pltpu.VMEM((tm, ROW_W), F32),
                            pltpu.SemaphoreType.DMA(()), pltpu.SemaphoreType.DMA(())]),
        out_shape=jax.ShapeDtypeStruct((xs.shape[0], D), F32),
        compiler_params=_cparams(1),
        name="moe_experts",
    )(*items, xs, w_gate, w_up, w_down, w_gate, w_up, w_down)


def _gather_body(dest_ref, x_ref, mod_ref, ys_ref, o_ref, ybuf0, ybuf1, sem0, sem1, *,
                 tm, seq, per_batch, dest_off):
    i = pl.program_id(0)
    n_tiles = pl.num_programs(0)
    ybufs = (ybuf0, ybuf1)
    sems = (sem0, sem1)
    row = (i * tm) // seq if per_batch else CTX_ROW
    g2 = mod_ref[pl.ds(row, 1), pl.ds(5 * D, D)]

    def row_copy(slot, r, d):
        return pltpu.make_async_copy(ys_ref.at[pl.ds(d, 1)], ybufs[slot].at[pl.ds(r, 1)], sems[slot])

    def fetch(t, slot):
        base = dest_off + t * tm
        _row_copies(tm, lambda r: row_copy(slot, r, dest_ref[base + r]))

    @pl.when(i == 0)
    def _():
        fetch(0, 0)

    for slot in range(2):
        @pl.when(i % 2 == slot)
        def _():
            @pl.when(i + 1 < n_tiles)
            def _():
                fetch(i + 1, 1 - slot)

            _row_waits(tm, lambda r: row_copy(slot, r, 0))
            o_ref[...] = x_ref[...] + g2 * ybufs[slot][...]


def _gather_call(dest, x2, mod, ys, *, seq, per_batch, dest_off):
    n = x2.shape[0]
    tm = min(TM_GATHER, seq)
    return pl.pallas_call(
        functools.partial(_gather_body, tm=tm, seq=seq, per_batch=per_batch, dest_off=dest_off),
        grid_spec=pltpu.PrefetchScalarGridSpec(
            num_scalar_prefetch=1,
            grid=(n // tm,),
            in_specs=[pl.BlockSpec((tm, D), lambda i, *_: (i, 0)),
                      pl.BlockSpec((8, 6 * D), lambda i, *_: (0, 0)),
                      pl.BlockSpec(memory_space=pl.ANY)],
            out_specs=pl.BlockSpec((tm, D), lambda i, *_: (i, 0)),
            scratch_shapes=[pltpu.VMEM((tm, D), F32), pltpu.VMEM((tm, D), F32),
                            pltpu.SemaphoreType.DMA(()), pltpu.SemaphoreType.DMA(())]),
        out_shape=jax.ShapeDtypeStruct((n, D), F32),
        compiler_params=_cparams(1),
        name="moe_gather_rows" if per_batch else "moe_gather_rows_ctx",
    )(dest, x2, mod, ys)


def _routing_tables(bins, rank, cnt, n_rows):
    tm = TM_MOE
    counts = cnt[:N_BINS, 0].astype(I32)
    tiles = (counts + tm - 1) // tm
    tile_end = jnp.cumsum(tiles)
    starts = (tile_end - tiles) * tm
    ids = jnp.arange(N_BINS, dtype=I32)
    pick = lambda key, tab: jnp.sum(jnp.where(key[:, None] == ids[None, :], tab[None, :], 0), axis=1)
    dest = rank + pick(bins, starts)
    n_valid = tile_end[-1]
    n_tiles = n_rows // tm + N_BINS
    s = jnp.minimum(jnp.arange(n_tiles, dtype=I32), n_valid - 1)
    tbin = jnp.sum((s[:, None] >= tile_end[None, :]).astype(I32), axis=1)
    pidx = tbin % 6
    six = jnp.arange(6, dtype=I32)
    slot = lambda tab: jnp.sum(jnp.where(pidx[:, None] == six[None, :], jnp.asarray(tab, I32)[None, :], 0), axis=1)
    ea = 4 * (tbin // 6) + slot(PAIR_SLOT_A)
    eb = 4 * (tbin // 6) + slot(PAIR_SLOT_B)
    return dest, starts + counts, (ea, eb, n_valid.reshape(1))


def _block_diag(w):
    g, c, d = w.shape
    eye = jnp.asarray(np.eye(g), w.dtype)
    return (w[:, :, None, :] * eye[:, None, :, None]).reshape(g * c, g * d)


def _rope_tables(n_tokens):
    rows = n_tokens // GRID_W
    r = np.repeat(np.arange(rows), GRID_W).astype(np.float64)
    col = np.tile(np.arange(GRID_W), rows).astype(np.float64)
    half = HEAD_DIM // 2
    inv = 1.0 / (ROPE_BASE ** (np.arange(0, half, 2, dtype=np.float64) / half))
    ar = r[:, None] * inv
    ac = col[:, None] * inv
    ang = np.concatenate([ar, ar, ac, ac], axis=-1)
    sign = np.where((np.arange(HEAD_DIM) & 16) == 0, -1.0, 1.0)
    cos, sin = np.cos(ang), np.sin(ang) * sign
    return jnp.asarray(np.tile(cos, (1, 2)), F32), jnp.asarray(np.tile(sin, (1, 2)), F32)


def kernel(x, c, ctx, c_ctx, w_mod, b_mod, norm1_g, w_in, q_norm_g, k_norm_g, attn_sink, pool_w, pool_scale,
           four_w, w_out, norm2_g, w_grp, b_grp, w_rtr, b_rtr, w_gate, w_up, w_down):
    nb, seq, _ = x.shape
    lc = ctx.shape[1]
    depth = w_mod.shape[0]
    t_lat = nb * seq
    t_ctx = nb * lc

    cs = jnp.concatenate([c, c_ctx[None, :], jnp.zeros((8 - nb - 1, D), F32)], axis=0)
    m512 = jnp.asarray(np.kron(np.eye(N_HEADS), np.full((HEAD_DIM, HEAD_DIM), 1.0 / HEAD_DIM)), BF16)
    kk = np.arange(HEAD_DIM)
    ang64 = 2.0 * np.pi * ((kk[:, None] * kk[None, :]) % HEAD_DIM) / HEAD_DIM
    c64bd = jnp.asarray(np.kron(np.eye(4), np.cos(ang64)), F32)
    s64bd = jnp.asarray(np.kron(np.eye(4), np.sin(ang64)), F32)
    tri_of = lambda n: jnp.asarray(np.triu(np.ones((min(TM_OUT, n),) * 2)), BF16)
    cos2, sin2 = _rope_tables(seq)
    four_cs, four_tabs = _four2_tables(seq)
    tabs_ctx = _dft_tables(lc, min(TM_FOUR, lc))
    pool_lat = _pool_tables(seq)
    pool_ctx = _pool_tables(lc)
    w_in_b = w_in.astype(BF16)
    w_out_b = w_out.astype(BF16)
    wbd = jnp.stack([_block_diag(four_w[l]) for l in range(depth)])
    pool_wbd = jnp.stack([_block_diag(pool_w[l]) for l in range(depth)]).astype(BF16)
    wr = jnp.concatenate([w_grp, w_rtr, jnp.zeros((depth, D, 128 - 4 - N_EXPERTS), F32)], axis=2)
    wr_hi = wr.astype(BF16)
    w2 = jnp.concatenate([wr_hi, (wr - wr_hi.astype(F32)).astype(BF16)], axis=2)
    br = jnp.concatenate([b_grp, b_rtr, jnp.zeros((depth, 128 - 4 - N_EXPERTS), F32)], axis=1)

    mod_all = _mod_call(cs, w_mod, b_mod)
    ab_all = _ab_call(c64bd, s64bd, wbd)

    xc = ctx
    pending = None
    for l in range(depth):
        last = l == depth - 1
        mod = mod_all[l]
        g1 = norm1_g[l][None, :]
        g2 = norm2_g[l][None, :]
        qg = jnp.tile(q_norm_g[l], N_HEADS)[None, :]
        kg = jnp.tile(k_norm_g[l], KV_W // HEAD_DIM)[None, :]
        in_args = (mod, g1, w_in_b[l], m512, qg, kg, ab_all[l])

        puc, qc, kvc, uac = _in_call(xc, *in_args, None, None, rope=False, per_batch=False)
        if pending is None:
            pul, ql, kvl, ual = _in_call(x, *in_args, cos2, sin2, rope=True, per_batch=True)
        else:
            x, pul, ql, kvl, ual = _in_gather_call(*pending, *in_args, cos2, sin2)
        attn_l = _attn_lat_call(attn_sink[l], ql, kvl, kvc)
        four_l = _four2_call(*_four1_call(ual, four_cs), *four_tabs)

        proj = (w_out_b[l], pool_wbd[l], pool_scale[l][None, :])
        rout = (mod, g2, w2[l], br[l][None, :])
        cnt0 = jnp.zeros((32, 128), F32)
        x_mid, rows_l, cnt = _out_call(x, pul, attn_l, four_l, *proj, *pool_lat, *rout, tri_of(seq), cnt0,
                                       per_batch=True)
        row_sets = [rows_l]
        if not last:
            attn_c = _attn_ctx_call(attn_sink[l], qc, kvc)
            four_c = _four_call(uac, *tabs_ctx)
            xc_mid, rows_c, cnt = _out_call(xc, puc, attn_c, four_c, *proj, *pool_ctx, *rout, tri_of(lc), cnt,
                                            per_batch=False)
            row_sets.append(rows_c)

        route = [r[:, D:D + 2].astype(I32) for r in row_sets]
        bins = jnp.concatenate([r[:, 0] for r in route])
        rank = jnp.concatenate([r[:, 1] for r in route])
        dest, bin_ends, items = _routing_tables(bins, rank, cnt, bins.shape[0])
        xs = _scatter_call(dest, bin_ends, items[2], row_sets)
        ys = _moe_call(items, xs, w_gate, w_up, w_down, l)
        if last:
            x = _gather_call(dest, x_mid.reshape(t_lat, D), mod, ys, seq=seq, per_batch=True,
                             dest_off=0).reshape(nb, seq, D)
        else:
            pending = (dest, x_mid, mod, ys)
            xc = _gather_call(dest, xc_mid.reshape(t_ctx, D), mod, ys, seq=lc, per_batch=False,
                              dest_off=t_lat).reshape(nb, lc, D)
    return x
```

```python
import functools

import numpy as np
import jax
import jax.numpy as jnp
from jax import lax
from jax.experimental import pallas as pl
from jax.experimental.pallas import tpu as pltpu

F32 = jnp.float32
BF16 = jnp.bfloat16
I32 = jnp.int32
HI = lax.Precision.HIGHEST

D = 1024
HEAD_DIM = 64
N_HEADS = 8
GRID_W = 64
POOL_WINDOWS = (2, 4, 8, 16)
POOL_W = 256
ATTN_W = 512
KV_W = 128
FOUR_W = 256
IN_W = 1280
N_EXPERTS = 16
D_EXPERT = 512
WINDOW = 128
ROPE_BASE = 10000.0
EPS = 1e-6
NEG_INF = -1e30
LOG2_E = 1.4426950408889634
CTX_ROW = 4
N_BINS = 24
PAIR_SLOT_A = (0, 2, 2, 3, 3, 3)
PAIR_SLOT_B = (1, 1, 0, 0, 1, 2)
META_W = 128
ROW_W = D + META_W
ROW_TILES = ROW_W // 128
HALO = 16
POOL_CHUNK = 128

VMEM_LIMIT = 56 * 1024 * 1024
TM_IN = 1024
TQ = 1024
TM_OUT = 1024
TM_FOUR = 256
FOUR_R1 = 16
FOUR1_A_CHUNK = 8
TM_MOE = 512
TM_ROW = 512
TM_GATHER = 512
SCATTER_SLOTS = 4
MOD_TN = 1024


def _cparams(n_axes):
    return pltpu.CompilerParams(dimension_semantics=("arbitrary",) * n_axes,
                                vmem_limit_bytes=VMEM_LIMIT)


def _silu(v):
    return v / (1.0 + jnp.exp(-v))


def _mod_body(cs_ref, w_ref, b_ref, o_ref):
    s = _silu(cs_ref[...])
    w = w_ref[0]
    s_hi = s.astype(BF16)
    s_lo = (s - s_hi.astype(F32)).astype(BF16)
    w_hi = w.astype(BF16)
    w_lo = (w - w_hi.astype(F32)).astype(BF16)
    p = jnp.dot(jnp.concatenate([s_hi, s_lo], axis=0), w_hi, preferred_element_type=F32)
    o_ref[0] = p[0:8] + p[8:16] + jnp.dot(s_hi, w_lo, preferred_element_type=F32) + b_ref[0]


def _mod_call(cs, w_mod, b_mod):
    depth = w_mod.shape[0]
    return pl.pallas_call(
        _mod_body,
        grid=(depth, 6 * D // MOD_TN),
        in_specs=[pl.BlockSpec((8, D), lambda l, j: (0, 0)),
                  pl.BlockSpec((1, D, MOD_TN), lambda l, j: (l, 0, j)),
                  pl.BlockSpec((1, 1, MOD_TN), lambda l, j: (l, 0, j))],
        out_specs=pl.BlockSpec((1, 8, MOD_TN), lambda l, j: (l, 0, j)),
        out_shape=jax.ShapeDtypeStruct((depth, 8, 6 * D), F32),
        compiler_params=_cparams(2),
        name="modulation",
    )(cs, w_mod, b_mod.reshape(depth, 1, 6 * D))


def _ab_body(c_ref, s_ref, w_ref, o_ref):
    w = w_ref[0]
    ca = jnp.dot(c_ref[...], w, preferred_element_type=F32, precision=HI)
    sa = jnp.dot(s_ref[...], w, preferred_element_type=F32, precision=HI)
    o_ref[0] = (jnp.concatenate([ca, sa], axis=1) * (HEAD_DIM ** -0.5)).astype(BF16)


def _ab_call(c64bd, s64bd, wbd):
    depth = wbd.shape[0]
    return pl.pallas_call(
        _ab_body,
        grid=(depth,),
        in_specs=[pl.BlockSpec((FOUR_W, FOUR_W), lambda l: (0, 0)),
                  pl.BlockSpec((FOUR_W, FOUR_W), lambda l: (0, 0)),
                  pl.BlockSpec((1, FOUR_W, FOUR_W), lambda l: (l, 0, 0))],
        out_specs=pl.BlockSpec((1, FOUR_W, 2 * FOUR_W), lambda l: (l, 0, 0)),
        out_shape=jax.ShapeDtypeStruct((depth, FOUR_W, 2 * FOUR_W), BF16),
        compiler_params=_cparams(1),
        name="fourier_weights",
    )(c64bd, s64bd, wbd)


def _head_rms(t, m, g):
    ms = jnp.dot((t * t).astype(BF16), m, preferred_element_type=F32)
    return t * lax.rsqrt(ms + EPS) * g


def _rope(t, cos, sin_signed):
    w = t.shape[1]
    lane = lax.broadcasted_iota(I32, t.shape, 1)
    fwd = pltpu.roll(t, w - 16, 1)
    bwd = pltpu.roll(t, 16, 1)
    rot = jnp.where((lane & 16) == 0, fwd, bwd)
    return t * cos + rot * sin_signed


def _in_body(*refs, rope, per_batch):
    if rope:
        (x_ref, mod_ref, g1_ref, w_ref, m_ref, qg_ref, kg_ref, ab_ref, cos_ref, sin_ref,
         pu_ref, q_ref, kv_ref, ua_ref) = refs
    else:
        (x_ref, mod_ref, g1_ref, w_ref, m_ref, qg_ref, kg_ref, ab_ref,
         pu_ref, q_ref, kv_ref, ua_ref) = refs
    row = pl.program_id(0) if per_batch else CTX_ROW
    sh1 = mod_ref[pl.ds(row, 1), pl.ds(0, D)]
    sc1 = mod_ref[pl.ds(row, 1), pl.ds(D, D)]
    x = x_ref[0]
    ms = jnp.mean(x * x, axis=-1, keepdims=True)
    h = (x * lax.rsqrt(ms + EPS) * g1_ref[...]) * (1.0 + sc1) + sh1
    p = jnp.dot(h.astype(BF16), w_ref[...], preferred_element_type=F32)
    pu = p[:, 0:256]
    q = p[:, 256:768]
    k = p[:, 768:896]
    v = p[:, 896:1024]
    fu = p[:, 1024:1280]
    m = m_ref[...]
    q = _head_rms(q, m, qg_ref[...])
    k = _head_rms(k, m[0:KV_W, 0:KV_W], kg_ref[...])
    if rope:
        cos = cos_ref[...]
        sin = sin_ref[...]
        q = _rope(q, jnp.concatenate([cos] * 4, axis=1), jnp.concatenate([sin] * 4, axis=1))
        k = _rope(k, cos, sin)
    q = q * (HEAD_DIM ** -0.5 * LOG2_E)
    pu_ref[0] = pu.astype(BF16)
    q_ref[0] = q.astype(BF16)
    kv_ref[0] = jnp.concatenate([k, pltpu.roll(k, 64, 1), v, pltpu.roll(v, 64, 1)], axis=1).astype(BF16)
    ua_ref[0] = jnp.dot(fu.astype(BF16), ab_ref[...], preferred_element_type=F32).astype(BF16)


def _in_call(x3, mod, g1, w_in, m512, qg, kg, ab, cos2, sin2, *, rope, per_batch):
    nb, seq, _ = x3.shape
    tm = min(TM_IN, seq)
    full = lambda shape: pl.BlockSpec(shape, lambda b, i: (0,) * len(shape))
    in_specs = [pl.BlockSpec((1, tm, D), lambda b, i: (b, i, 0)),
                full((8, 6 * D)), full((1, D)), full((D, IN_W)), full((ATTN_W, ATTN_W)),
                full((1, ATTN_W)), full((1, KV_W)), full((FOUR_W, 2 * FOUR_W))]
    args = [x3, mod, g1, w_in, m512, qg, kg, ab]
    if rope:
        in_specs += [pl.BlockSpec((tm, 128), lambda b, i: (i, 0)),
                     pl.BlockSpec((tm, 128), lambda b, i: (i, 0))]
        args += [cos2, sin2]
    widths = (POOL_W, ATTN_W, 4 * KV_W, 2 * FOUR_W)
    return pl.pallas_call(
        functools.partial(_in_body, rope=rope, per_batch=per_batch),
        grid=(nb, seq // tm),
        in_specs=in_specs,
        out_specs=[pl.BlockSpec((1, tm, w), lambda b, i: (b, i, 0)) for w in widths],
        out_shape=[jax.ShapeDtypeStruct((nb, seq, w), BF16) for w in widths],
        compiler_params=_cparams(2),
        name="in_proj_rope" if rope else "in_proj_ctx",
    )(*args)


_NT = (((1,), (1,)), ((), ()))


def _stack_heads(qpair0, qpair1, lo):
    z = jnp.zeros_like(qpair0)
    parts = [jnp.where(lo, qpair0, z), jnp.where(lo, qpair1, z),
             jnp.where(lo, pltpu.roll(qpair0, 64, 1), z), jnp.where(lo, pltpu.roll(qpair1, 64, 1), z)]
    return jnp.concatenate(parts, axis=0).astype(BF16)


def _group_attention(q4, k_parts, va_parts, vb_parts, masks, sink_col):
    s_parts = []
    for kz, mk in zip(k_parts, masks):
        s = lax.dot_general(q4, kz, _NT, preferred_element_type=F32)
        if mk is not None:
            nk = s.shape[1]
            s = jnp.where(mk[None], s.reshape(4, 128, nk), NEG_INF).reshape(512, nk)
        s_parts.append(s)
    m = functools.reduce(jnp.maximum, [jnp.max(s, axis=-1, keepdims=True) for s in s_parts])
    m = jnp.maximum(m, sink_col)
    den = jnp.exp2(sink_col - m)
    oe = oo = None
    for s, va, vb in zip(s_parts, va_parts, vb_parts):
        e = jnp.exp2(s - m)
        den = den + jnp.sum(e, axis=-1, keepdims=True)
        eb = e.astype(BF16)
        pe = jnp.dot(eb[0:256], va, preferred_element_type=F32)
        po = jnp.dot(eb[256:512], vb, preferred_element_type=F32)
        oe = pe if oe is None else oe + pe
        oo = po if oo is None else oo + po
    inv = 1.0 / den
    return oe * inv[0:256], oo * inv[256:512]


def _sink_cols(sink_ref):
    rb = lax.broadcasted_iota(I32, (512, 1), 0) >> 7
    cols = []
    for kvh in range(2):
        s = [sink_ref[4 * kvh + j] * LOG2_E for j in (0, 2, 1, 3)]
        cols.append(jnp.where(rb == 0, s[0], jnp.where(rb == 1, s[1], jnp.where(rb == 2, s[2], s[3]))))
    return cols


def _attend_block(qblk, kv_parts, masks, sink_cols):
    lo = lax.broadcasted_iota(I32, (128, 128), 1) < 64
    cols = []
    for kvh in range(2):
        q4 = _stack_heads(qblk[:, 256 * kvh:256 * kvh + 128], qblk[:, 256 * kvh + 128:256 * kvh + 256], lo)
        ko = 128 * kvh
        vao = 256 + 128 * kvh
        vbo = 384 - 128 * kvh
        oe, oo = _group_attention(q4, [kv[:, ko:ko + 128] for kv in kv_parts],
                                  [kv[:, vao:vao + 128] for kv in kv_parts],
                                  [kv[:, vbo:vbo + 128] for kv in kv_parts], masks, sink_cols[kvh])
        cols.append(jnp.where(lo, oe[0:128], oo[0:128]))
        cols.append(jnp.where(lo, oe[128:256], oo[128:256]))
    return jnp.concatenate(cols, axis=1).astype(BF16)


def _attn_lat_body(sink_ref, q_ref, kvp_ref, kvm_ref, kvn_ref, kvc_ref, o_ref, kvw_ref, *, tq, seq):
    i = pl.program_id(1)
    kvw_ref[0:128] = kvp_ref[0]
    kvw_ref[128:128 + tq] = kvm_ref[0]
    kvw_ref[128 + tq:256 + tq] = kvn_ref[0]
    kvc = kvc_ref[0]
    sink_cols = _sink_cols(sink_ref)

    def sub(j, carry):
        r0 = pl.multiple_of(j * 128, 128)
        win = kvw_ref[pl.ds(r0, 3 * 128), :]
        ii = lax.broadcasted_iota(I32, (128, 3 * 128), 0)
        cc = lax.broadcasted_iota(I32, (128, 3 * 128), 1)
        base = i * tq + j * 128 - 128
        valid = (ii <= cc) & (cc <= ii + 2 * WINDOW) & (cc >= -base) & (cc < seq - base)
        qblk = q_ref[0, pl.ds(r0, 128), :].astype(F32)
        o_ref[0, pl.ds(r0, 128), :] = _attend_block(qblk, [win, kvc], [valid, None], sink_cols)
        return carry

    for j in range(tq // 128):
        sub(j, 0)


def _attn_lat_call(sink, q, kv, kvc):
    nb, seq, _ = q.shape
    lc = kvc.shape[1]
    tq = TQ
    nblk = seq // 128
    r = tq // 128
    return pl.pallas_call(
        functools.partial(_attn_lat_body, tq=tq, seq=seq),
        grid=(nb, seq // tq),
        in_specs=[pl.BlockSpec(memory_space=pltpu.SMEM),
                  pl.BlockSpec((1, tq, ATTN_W), lambda b, i: (b, i, 0)),
                  pl.BlockSpec((1, 128, 4 * KV_W), lambda b, i: (b, jnp.maximum(i * r - 1, 0), 0)),
                  pl.BlockSpec((1, tq, 4 * KV_W), lambda b, i: (b, i, 0)),
                  pl.BlockSpec((1, 128, 4 * KV_W), lambda b, i: (b, jnp.minimum((i + 1) * r, nblk - 1), 0)),
                  pl.BlockSpec((1, lc, 4 * KV_W), lambda b, i: (b, 0, 0))],
        out_specs=pl.BlockSpec((1, tq, ATTN_W), lambda b, i: (b, i, 0)),
        out_shape=jax.ShapeDtypeStruct((nb, seq, ATTN_W), BF16),
        scratch_shapes=[pltpu.VMEM((tq + 256, 4 * KV_W), BF16)],
        compiler_params=_cparams(2),
        name="attention_window",
    )(sink, q, kv, kv, kv, kvc)


def _attn_ctx_body(sink_ref, q_ref, kvc_ref, o_ref, *, lc):
    kvc = kvc_ref[0]
    sink_cols = _sink_cols(sink_ref)
    for j in range(lc // 128):
        qblk = q_ref[0, j * 128:(j + 1) * 128, :].astype(F32)
        o_ref[0, j * 128:(j + 1) * 128, :] = _attend_block(qblk, [kvc], [None], sink_cols)


def _attn_ctx_call(sink, qc, kvc):
    nb, lc, _ = qc.shape
    return pl.pallas_call(
        functools.partial(_attn_ctx_body, lc=lc),
        grid=(nb,),
        in_specs=[pl.BlockSpec(memory_space=pltpu.SMEM),
                  pl.BlockSpec((1, lc, ATTN_W), lambda b: (b, 0, 0)),
                  pl.BlockSpec((1, lc, 4 * KV_W), lambda b: (b, 0, 0))],
        out_specs=pl.BlockSpec((1, lc, ATTN_W), lambda b: (b, 0, 0)),
        out_shape=jax.ShapeDtypeStruct((nb, lc, ATTN_W), BF16),
        compiler_params=_cparams(1),
        name="attention_ctx",
    )(sink, qc, kvc)


def _four_body(ua_ref, cb_ref, sb_ref, ca_ref, sa_ref, o_ref, *, nb, scale):
    i = pl.program_id(0)
    ca = ca_ref[pl.ds(i, 1), :]
    sa = sa_ref[pl.ds(i, 1), :]
    cb = cb_ref[...]
    sb = sb_ref[...]
    ct = (ca * cb - sa * sb).astype(BF16)
    st = (sa * cb + ca * sb).astype(BF16)
    for b in range(nb):
        ua = ua_ref[b, :, 0:FOUR_W]
        ub = ua_ref[b, :, FOUR_W:2 * FOUR_W]
        r = (jnp.dot(ct, ua, preferred_element_type=F32) - jnp.dot(st, ub, preferred_element_type=F32))
        o_ref[b] = (r * scale).astype(BF16)


def _four_call(uaub, cb, sb, ca, sa):
    nb, seq, _ = uaub.shape
    tm = cb.shape[0]
    one = pl.Buffered(1)
    return pl.pallas_call(
        functools.partial(_four_body, nb=nb, scale=float(seq) ** -0.5),
        grid=(seq // tm,),
        in_specs=[pl.BlockSpec((nb, seq, 2 * FOUR_W), lambda i: (0, 0, 0), pipeline_mode=one),
                  pl.BlockSpec((tm, seq), lambda i: (0, 0), pipeline_mode=one),
                  pl.BlockSpec((tm, seq), lambda i: (0, 0), pipeline_mode=one),
                  pl.BlockSpec((seq // tm, seq), lambda i: (0, 0), pipeline_mode=one),
                  pl.BlockSpec((seq // tm, seq), lambda i: (0, 0), pipeline_mode=one)],
        out_specs=pl.BlockSpec((nb, tm, FOUR_W), lambda i: (0, i, 0)),
        out_shape=jax.ShapeDtypeStruct((nb, seq, FOUR_W), BF16),
        compiler_params=_cparams(1),
        name="fourier_dft",
    )(uaub, cb, sb, ca, sa)


def _dft_tables(seq, tm):
    n = np.arange(seq)[None, :]

    def tab(rows):
        ang = ((rows[:, None] * n) % seq) * (2.0 * np.pi / seq)
        return jnp.asarray(np.cos(ang), F32), jnp.asarray(np.sin(ang), F32)

    cb, sb = tab(np.arange(tm))
    ca, sa = tab(np.arange(seq // tm) * tm)
    return cb, sb, ca, sa


def _four1_body(x_ref, g_ref, yr_ref, yi_ref):
    g = g_ref[...].astype(BF16)
    n = FOUR_R1 * 16
    w = FOUR_W
    for aa in range(x_ref.shape[2]):
        x = x_ref[0, :, aa].reshape(n, 2 * w)
        p = jnp.dot(g, x, preferred_element_type=F32)
        yr = p[0:n, 0:w] - p[n:2 * n, w:2 * w]
        yi = -(p[0:n, w:2 * w] + p[n:2 * n, 0:w])
        yr_ref[0, :, aa] = yr.astype(BF16).reshape(FOUR_R1, 16, w)
        yi_ref[0, :, aa] = yi.astype(BF16).reshape(FOUR_R1, 16, w)


def _four1_call(uaub, g):
    nb, seq, _ = uaub.shape
    r2 = seq // FOUR_R1
    na = r2 // 16
    x = uaub.reshape(nb, FOUR_R1, na, 16, 2 * FOUR_W)
    out = jax.ShapeDtypeStruct((nb, FOUR_R1, na, 16, FOUR_W), BF16)
    ac = FOUR1_A_CHUNK
    return pl.pallas_call(
        _four1_body,
        grid=(nb, na // ac),
        in_specs=[pl.BlockSpec((1, FOUR_R1, ac, 16, 2 * FOUR_W), lambda b, j: (b, 0, j, 0, 0)),
                  pl.BlockSpec(g.shape, lambda b, j: (0, 0))],
        out_specs=[pl.BlockSpec((1, FOUR_R1, ac, 16, FOUR_W), lambda b, j: (b, 0, j, 0, 0))] * 2,
        out_shape=[out, out],
        compiler_params=_cparams(2),
        name="fourier_stage1",
    )(x, g)


def _four2_body(yr_ref, yi_ref, c_ref, s_ref, ca_ref, sa_ref, o_ref, obuf, sem, *, nb, scale):
    k1 = pl.program_id(0)
    nk = pl.num_programs(0)
    ca = ca_ref[pl.ds(k1, 1), :]
    sa = sa_ref[pl.ds(k1, 1), :]
    c = c_ref[...]
    s = s_ref[...]
    gc = (c * ca - s * sa).astype(BF16)
    gs = (s * ca + c * sa).astype(BF16)

    def out_copy(slot, b, kk):
        return pltpu.make_async_copy(obuf.at[slot, b], o_ref.at[b, :, kk, :], sem.at[slot])

    for slot in range(2):
        @pl.when(k1 % 2 == slot)
        def _():
            @pl.when(k1 >= 2)
            def _():
                for b in range(nb):
                    out_copy(slot, b, 0).wait()

            for b in range(nb):
                acc = (jnp.dot(gc, yr_ref[b, 0], preferred_element_type=F32)
                       + jnp.dot(gs, yi_ref[b, 0], preferred_element_type=F32))
                obuf[slot, b] = acc * scale
            for b in range(nb):
                out_copy(slot, b, k1).start()

    @pl.when(k1 == nk - 1)
    def _():
        for slot in range(2):
            for b in range(nb):
                out_copy(slot, b, 0).wait()


def _four2_call(yr, yi, c, s, ca, sa):
    nb = yr.shape[0]
    r2 = c.shape[0]
    seq = FOUR_R1 * r2
    yr4 = yr.reshape(nb, FOUR_R1, r2, FOUR_W)
    yi4 = yi.reshape(nb, FOUR_R1, r2, FOUR_W)
    full = lambda shape: pl.BlockSpec(shape, lambda k: (0,) * len(shape))
    out = pl.pallas_call(
        functools.partial(_four2_body, nb=nb, scale=float(seq) ** -0.5),
        grid=(FOUR_R1,),
        in_specs=[pl.BlockSpec((nb, 1, r2, FOUR_W), lambda k: (0, k, 0, 0)),
                  pl.BlockSpec((nb, 1, r2, FOUR_W), lambda k: (0, k, 0, 0)),
                  full((r2, r2)), full((r2, r2)), full((FOUR_R1, r2)), full((FOUR_R1, r2))],
        out_specs=pl.BlockSpec(memory_space=pl.ANY),
        out_shape=jax.ShapeDtypeStruct((nb, r2, FOUR_R1, FOUR_W), F32),
        scratch_shapes=[pltpu.VMEM((2, nb, r2, FOUR_W), F32), pltpu.SemaphoreType.DMA((2,))],
        compiler_params=_cparams(1),
        name="fourier_stage2",
    )(yr4, yi4, c, s, ca, sa)
    return out.reshape(nb, seq, FOUR_W)


def _four2_tables(seq):
    r1 = FOUR_R1
    r2 = seq // r1
    assert r1 * r2 == seq and r2 % 16 == 0
    k = np.arange(r1)
    ang1 = ((k[:, None] * k[None, :]) % r1) * (2.0 * np.pi / r1)
    eye = np.eye(16)
    g = np.concatenate([np.kron(np.cos(ang1), eye), np.kron(np.sin(ang1), eye)], axis=0)
    m = np.arange(r2)
    ang2 = ((m[:, None] * m[None, :]) % r2) * (2.0 * np.pi / r2)
    alpha = ((k[:, None] * m[None, :]) % seq) * (2.0 * np.pi / seq)
    f = lambda t: jnp.asarray(t, F32)
    return f(g), (f(np.cos(ang2)), f(np.sin(ang2)), f(np.cos(alpha)), f(np.sin(alpha)))


def _route(lt, tri_ref, cnt_ref, tm):
    rowi = lax.broadcasted_iota(I32, (32, tm), 0)
    big = jnp.int32(999)

    def first_argmax(vals):
        mx = jnp.max(vals, axis=0, keepdims=True)
        return mx, jnp.min(jnp.where(vals == mx, rowi, big), axis=0, keepdims=True)

    is_grp = rowi < 4
    mg, gi = first_argmax(jnp.where(is_grp, lt, NEG_INF))
    pg = 1.0 / jnp.sum(jnp.where(is_grp, jnp.exp(jnp.where(is_grp, lt, mg) - mg), 0.0), axis=0, keepdims=True)
    est = 4 + 4 * gi
    le = jnp.where((rowi >= est) & (rowi < est + 4), lt, NEG_INF)
    m1, i1 = first_argmax(le)
    m2, i2 = first_argmax(jnp.where(rowi == i1, NEG_INF, le))
    e2 = jnp.exp(m2 - m1)
    w1 = pg / (1.0 + e2)
    w2 = pg * e2 / (1.0 + e2)
    a1 = i1 - est
    a2 = i2 - est
    code = jnp.minimum(a1, a2) * 4 + jnp.maximum(a1, a2)
    pidx = jnp.where(code == 1, 0, jnp.where(code == 6, 1, jnp.where(code == 2, 2,
           jnp.where(code == 3, 3, jnp.where(code == 7, 4, 5)))))
    slot_a = jnp.where(pidx == 0, 0, jnp.where(pidx <= 2, 2, 3))
    slot_b = jnp.where(pidx <= 1, 1, jnp.where(pidx <= 3, 0, jnp.where(pidx == 4, 1, 2)))
    wa = jnp.where(a1 == slot_a, w1, w2)
    wb = jnp.where(a1 == slot_b, w1, w2)
    bin_ = gi * 6 + pidx

    onehot = rowi == bin_
    pref = jnp.dot(onehot.astype(BF16), tri_ref[...], preferred_element_type=F32)
    carry = cnt_ref[:, 0:1]
    rank = jnp.sum(jnp.where(onehot, pref - 1.0 + carry, 0.0), axis=0, keepdims=True)
    cnt_ref[...] = jnp.broadcast_to(carry + pref[:, tm - 1:tm], cnt_ref.shape)
    return jnp.concatenate([bin_.astype(F32), rank, wa, wb, jnp.zeros((128 - 4, tm), F32)], axis=0)


def _out_body(x_ref, pup_ref, pum_ref, pun_ref, at_ref, fo_ref, wo_ref, pw_ref, ps_ref, band_ref, icnt_ref,
              mod_ref, g2_ref, w2_ref, br_ref, tri_ref, cin_ref,
              xo_ref, rows_ref, cnt_ref, *, tm, per_batch):
    b = pl.program_id(0)
    i = pl.program_id(1)
    nt = pl.num_programs(1)
    row = b if per_batch else CTX_ROW
    g1 = mod_ref[pl.ds(row, 1), pl.ds(2 * D, D)]
    sh2 = mod_ref[pl.ds(row, 1), pl.ds(3 * D, D)]
    sc2 = mod_ref[pl.ds(row, 1), pl.ds(4 * D, D)]

    @pl.when((b == 0) & (i == 0))
    def _():
        cnt_ref[...] = cin_ref[...]

    um = pum_ref[0]
    zh = jnp.zeros((HALO, POOL_W), BF16)
    uext = jnp.concatenate([jnp.where(i > 0, pup_ref[0], zh), um, jnp.where(i < nt - 1, pun_ref[0], zh)], axis=0)
    grp = lax.broadcasted_iota(I32, (POOL_CHUNK, POOL_W), 1) >> 6
    chunks = []
    for c in range(tm // POOL_CHUNK):
        uc = uext[POOL_CHUNK * c:POOL_CHUNK * (c + 1) + 2 * HALO]
        pc = jnp.zeros((POOL_CHUNK, POOL_W), F32)
        for g in range(len(POOL_WINDOWS)):
            pc = jnp.where(grp == g, jnp.dot(band_ref[g], uc, preferred_element_type=F32), pc)
        chunks.append(pc)
    pooled = jnp.concatenate(chunks, axis=0)
    y = pooled * icnt_ref[...] - um.astype(F32)
    pool_out = jnp.dot(y.astype(BF16), pw_ref[...], preferred_element_type=F32) * ps_ref[...]

    cat = jnp.concatenate([pool_out.astype(BF16), at_ref[0], fo_ref[0].astype(BF16)], axis=1)
    xm = x_ref[0] + g1 * jnp.dot(cat, wo_ref[...], preferred_element_type=F32)
    xo_ref[0] = xm

    ms = jnp.mean(xm * xm, axis=-1, keepdims=True)
    h2 = (xm * lax.rsqrt(ms + EPS) * g2_ref[...]) * (1.0 + sc2) + sh2

    hh = h2.astype(BF16)
    hl = (h2 - hh.astype(F32)).astype(BF16)
    w2 = w2_ref[...]
    p2 = jnp.dot(hh, w2, preferred_element_type=F32)
    logits = (p2[:, 0:128] + p2[:, 128:256] + jnp.dot(hl, w2[:, 0:128], preferred_element_type=F32) + br_ref[...])
    meta = _route(logits.T[0:32, :], tri_ref, cnt_ref, tm).T

    rows_ref[:, 0:D] = h2
    rows_ref[:, D:ROW_W] = meta


def _out_call(x3, pu, attn, four, w_out, pool_wbd, pool_scale, bands, icnt, mod, g2, w2, br, tri, cnt_in, *,
              per_batch):
    nb, seq, _ = x3.shape
    tm = min(TM_OUT, seq)
    nt = seq // tm
    hb = tm // HALO
    full = lambda shape: pl.BlockSpec(shape, lambda b, i: (0,) * len(shape))
    in_specs = [pl.BlockSpec((1, tm, D), lambda b, i: (b, i, 0)),
                pl.BlockSpec((1, HALO, POOL_W), lambda b, i: (b, jnp.maximum(i * hb - 1, 0), 0)),
                pl.BlockSpec((1, tm, POOL_W), lambda b, i: (b, i, 0)),
                pl.BlockSpec((1, HALO, POOL_W), lambda b, i: (b, jnp.minimum((i + 1) * hb, seq // HALO - 1), 0)),
                pl.BlockSpec((1, tm, ATTN_W), lambda b, i: (b, i, 0)),
                pl.BlockSpec((1, tm, FOUR_W), lambda b, i: (b, i, 0)),
                full((D, D)), full((POOL_W, POOL_W)), full((1, POOL_W)),
                full((len(POOL_WINDOWS), POOL_CHUNK, POOL_CHUNK + 2 * HALO)),
                pl.BlockSpec((tm, POOL_W), lambda b, i: (i, 0)),
                full((8, 6 * D)), full((1, D)), full((D, 256)), full((1, 128)), full((tm, tm)), full((32, 128))]
    args = [x3, pu, pu, pu, attn, four, w_out, pool_wbd, pool_scale, bands, icnt, mod, g2, w2, br, tri, cnt_in]
    return pl.pallas_call(
        functools.partial(_out_body, tm=tm, per_batch=per_batch),
        grid=(nb, nt),
        in_specs=in_specs,
        out_specs=[pl.BlockSpec((1, tm, D), lambda b, i: (b, i, 0)),
                   pl.BlockSpec((tm, ROW_W), lambda b, i: (b * nt + i, 0)),
                   pl.BlockSpec((32, 128), lambda b, i: (0, 0))],
        out_shape=[jax.ShapeDtypeStruct((nb, seq, D), F32),
                   jax.ShapeDtypeStruct((nb * seq, ROW_W), F32),
                   jax.ShapeDtypeStruct((32, 128), F32)],
        compiler_params=_cparams(2),
        name="out_proj_router" if per_batch else "out_proj_router_ctx",
    )(*args)


def _pool_tables(seq):
    t = np.arange(POOL_CHUNK)[:, None]
    s = np.arange(POOL_CHUNK + 2 * HALO)[None, :] - HALO
    bands = np.stack([(s >= t - w // 2) & (s <= t + w // 2 - 1) for w in POOL_WINDOWS]).astype(np.float32)
    pos = np.arange(seq)
    icnt = np.stack([1.0 / (np.minimum(pos + w // 2 - 1, seq - 1) - np.maximum(pos - w // 2, 0) + 1)
                     for w in POOL_WINDOWS], axis=1)
    return jnp.asarray(bands, BF16), jnp.asarray(np.repeat(icnt, POOL_W // len(POOL_WINDOWS), axis=1), F32)


def _row_copies(tm, make_copy):
    for r in range(tm):
        make_copy(r).start(priority=r % 2)


def _row_waits(tm, make_copy):
    def drain(r, c):
        make_copy(0).wait()
        return c

    lax.fori_loop(0, tm, drain, 0, unroll=8)


def _zero_fill(ends_ref, nv_ref, xs_ref, zbuf, zsem, tm, n_out, wait):
    def piece(off, size):
        return pltpu.make_async_copy(zbuf.at[pl.ds(0, size)], xs_ref.at[pl.ds(off, size)], zsem)

    def run(cond, off, size):
        @pl.when(cond)
        def _():
            c = piece(off, size)
            c.wait() if wait else c.start()

    for b in range(N_BINS):
        off = ends_ref[b]
        pad = (tm - (off & (tm - 1))) & (tm - 1)
        for k in range(tm.bit_length() - 1):
            run(((pad >> k) & 1) == 1, off, 1 << k)
            off = off + (pad & (1 << k))

    def tail(t, c):
        c_ = piece(t * tm, tm)
        c_.wait() if wait else c_.start()
        return c

    lax.fori_loop(nv_ref[0], n_out, tail, 0)


def _scatter_body(dest_ref, ends_ref, nv_ref, *refs, tm, n_tiles, n_first, n_out):
    n_h = len(refs) - 3 - 3 * SCATTER_SLOTS
    h_refs, xs_ref = refs[:n_h], refs[n_h]
    bufs = refs[n_h + 1:n_h + 1 + SCATTER_SLOTS]
    lsems = refs[n_h + 1 + SCATTER_SLOTS:n_h + 1 + 2 * SCATTER_SLOTS]
    rsems = refs[n_h + 1 + 2 * SCATTER_SLOTS:n_h + 1 + 3 * SCATTER_SLOTS]
    zbuf, zsem = refs[-2:]
    zbuf[...] = jnp.zeros_like(zbuf)
    _zero_fill(ends_ref, nv_ref, xs_ref, zbuf, zsem, tm, n_out, wait=False)

    def lane_block_copy(h_ref, tt, slot, j):
        return pltpu.make_async_copy(h_ref.at[pl.ds(tt * tm, tm), pl.ds(128 * j, 128)],
                                     bufs[slot].at[:, j, :], lsems[slot])

    def load(t, slot):
        def start(h_ref, tt):
            for j in range(ROW_TILES):
                lane_block_copy(h_ref, tt, slot, j).start()

        if n_h == 1:
            start(h_refs[0], t)
        else:
            @pl.when(t < n_first)
            def _():
                start(h_refs[0], t)

            @pl.when(t >= n_first)
            def _():
                start(h_refs[1], t - n_first)

    def row_copy(slot, r, d):
        return pltpu.make_async_copy(bufs[slot].at[r], xs_ref.at[d], rsems[slot])

    load(0, 0)
    load(1, 1)

    def tile_step(t, slot):
        ahead = (slot + 2) % SCATTER_SLOTS
        for j in range(ROW_TILES):
            lane_block_copy(h_refs[0], 0, slot, j).wait()

        @pl.when(t >= 2)
        def _():
            _row_waits(tm, lambda r: row_copy(ahead, r, 0))

        @pl.when(t + 2 < n_tiles)
        def _():
            load(t + 2, ahead)

        _row_copies(tm, lambda r: row_copy(slot, r, dest_ref[t * tm + r]))

    def group(g, c):
        for slot in range(SCATTER_SLOTS):
            tile_step(g * SCATTER_SLOTS + slot, slot)
        return c

    n_groups = n_tiles // SCATTER_SLOTS
    lax.fori_loop(0, n_groups, group, 0)
    for t in range(n_groups * SCATTER_SLOTS, n_tiles):
        tile_step(jnp.int32(t), t % SCATTER_SLOTS)
    for t in (n_tiles - 2, n_tiles - 1):
        _row_waits(tm, lambda r: row_copy(t % SCATTER_SLOTS, r, 0))
    _zero_fill(ends_ref, nv_ref, xs_ref, zbuf, zsem, tm, n_out, wait=True)


def _scatter_call(dest, bin_ends, n_valid, row_sets):
    tm = TM_ROW
    assert tm == TM_MOE
    n_first = row_sets[0].shape[0] // tm
    n_rows = sum(r.shape[0] for r in row_sets)
    n_tiles = n_rows // tm
    n_out = n_tiles + N_BINS
    assert n_tiles >= SCATTER_SLOTS
    return pl.pallas_call(
        functools.partial(_scatter_body, tm=tm, n_tiles=n_tiles, n_first=n_first, n_out=n_out),
        grid_spec=pltpu.PrefetchScalarGridSpec(
            num_scalar_prefetch=3,
            grid=(1,),
            in_specs=[pl.BlockSpec(memory_space=pl.ANY)] * len(row_sets),
            out_specs=pl.BlockSpec(memory_space=pl.ANY),
            scratch_shapes=([pltpu.VMEM((tm, ROW_TILES, 128), F32)] * SCATTER_SLOTS
                            + [pltpu.SemaphoreType.DMA(())] * (2 * SCATTER_SLOTS)
                            + [pltpu.VMEM((tm, ROW_TILES, 128), F32), pltpu.SemaphoreType.DMA(())])),
        out_shape=jax.ShapeDtypeStruct((n_out * tm, ROW_TILES, 128), F32),
        compiler_params=_cparams(1),
        name="moe_scatter_rows",
    )(dest, bin_ends, n_valid, *row_sets)


def _moe_body(ea_ref, eb_ref, nv_ref, xs_ref, wga, wua, wda, wgb, wub, wdb, ys_ref, xbuf0, xbuf1, sem0, sem1, *, tm):
    del ea_ref, eb_ref
    s = pl.program_id(0)
    nv = nv_ref[0]
    xbufs = (xbuf0, xbuf1)
    sems = (sem0, sem1)

    def lane_block_copy(t, slot, j):
        return pltpu.make_async_copy(xs_ref.at[pl.ds(t * tm, tm), j, :], xbufs[slot].at[:, pl.ds(128 * j, 128)],
                                     sems[slot])

    def fetch(t, slot):
        for j in range(ROW_TILES):
            lane_block_copy(t, slot, j).start()

    @pl.when(s == 0)
    def _():
        fetch(0, 0)

    @pl.when(s >= nv)
    def _():
        ys_ref[...] = jnp.zeros_like(ys_ref)

    for slot in range(2):
        @pl.when((s < nv) & (s % 2 == slot))
        def _():
            @pl.when(s + 1 < nv)
            def _():
                fetch(s + 1, 1 - slot)

            for j in range(ROW_TILES):
                lane_block_copy(0, slot, j).wait()
            xs = xbufs[slot]
            h = xs[:, 0:D].astype(BF16)
            meta = xs[:, D:ROW_W]

            def expert(wg, wu, wd, gate):
                g = jnp.dot(h, wg[0, 0].astype(BF16), preferred_element_type=F32)
                u = jnp.dot(h, wu[0, 0].astype(BF16), preferred_element_type=F32)
                a = _silu(g) * u * gate
                return jnp.dot(a.astype(BF16), wd[0, 0].astype(BF16), preferred_element_type=F32)

            ys_ref[...] = expert(wga, wua, wda, meta[:, 2:3]) + expert(wgb, wub, wdb, meta[:, 3:4])


def _moe_call(items, xs, w_gate, w_up, w_down, layer):
    tm = TM_MOE
    n_tiles = xs.shape[0] // tm
    wa = lambda s, ea, eb, nv: (layer, ea[s], 0, 0)
    wb = lambda s, ea, eb, nv: (layer, eb[s], 0, 0)
    up_spec = lambda f: pl.BlockSpec((1, 1, D, D_EXPERT), f)
    dn_spec = lambda f: pl.BlockSpec((1, 1, D_EXPERT, D), f)
    return pl.pallas_call(
        functools.partial(_moe_body, tm=tm),
        grid_spec=pltpu.PrefetchScalarGridSpec(
            num_scalar_prefetch=len(items),
            grid=(n_tiles,),
            in_specs=[pl.BlockSpec(memory_space=pl.ANY),
                      up_spec(wa), up_spec(wa), dn_spec(wa), up_spec(wb), up_spec(wb), dn_spec(wb)],
            out_specs=pl.BlockSpec((tm, D), lambda s, ea, eb, nv: (s, 0)),
            scratch_shapes=[pltpu.VMEM((tm, ROW_W), F32), pltpu.VMEM((tm, ROW_W), F32),
                            pltpu.SemaphoreType.DMA(()), pltpu.SemaphoreType.DMA(())]),
        out_shape=jax.ShapeDtypeStruct((xs.shape[0], D), F32),
        compiler_params=_cparams(1),
        name="moe_experts",
    )(*items, xs, w_gate, w_up, w_down, w_gate, w_up, w_down)


def _gather_body(dest_ref, x_ref, mod_ref, ys_ref, o_ref, ybuf0, ybuf1, sem0, sem1, *,
                 tm, seq, per_batch, dest_off):
    i = pl.program_id(0)
    n_tiles = pl.num_programs(0)
    ybufs = (ybuf0, ybuf1)
    sems = (sem0, sem1)
    row = (i * tm) // seq if per_batch else CTX_ROW
    g2 = mod_ref[pl.ds(row, 1), pl.ds(5 * D, D)]

    def row_copy(slot, r, d):
        return pltpu.make_async_copy(ys_ref.at[pl.ds(d, 1)], ybufs[slot].at[pl.ds(r, 1)], sems[slot])

    def fetch(t, slot):
        base = dest_off + t * tm
        _row_copies(tm, lambda r: row_copy(slot, r, dest_ref[base + r]))

    @pl.when(i == 0)
    def _():
        fetch(0, 0)

    for slot in range(2):
        @pl.when(i % 2 == slot)
        def _():
            @pl.when(i + 1 < n_tiles)
            def _():
                fetch(i + 1, 1 - slot)

            _row_waits(tm, lambda r: row_copy(slot, r, 0))
            o_ref[...] = x_ref[...] + g2 * ybufs[slot][...]


def _gather_call(dest, x2, mod, ys, *, seq, per_batch, dest_off):
    n = x2.shape[0]
    tm = min(TM_GATHER, seq)
    return pl.pallas_call(
        functools.partial(_gather_body, tm=tm, seq=seq, per_batch=per_batch, dest_off=dest_off),
        grid_spec=pltpu.PrefetchScalarGridSpec(
            num_scalar_prefetch=1,
            grid=(n // tm,),
            in_specs=[pl.BlockSpec((tm, D), lambda i, *_: (i, 0)),
                      pl.BlockSpec((8, 6 * D), lambda i, *_: (0, 0)),
                      pl.BlockSpec(memory_space=pl.ANY)],
            out_specs=pl.BlockSpec((tm, D), lambda i, *_: (i, 0)),
            scratch_shapes=[pltpu.VMEM((tm, D), F32), pltpu.VMEM((tm, D), F32),
                            pltpu.SemaphoreType.DMA(()), pltpu.SemaphoreType.DMA(())]),
        out_shape=jax.ShapeDtypeStruct((n, D), F32),
        compiler_params=_cparams(1),
        name="moe_gather_rows" if per_batch else "moe_gather_rows_ctx",
    )(dest, x2, mod, ys)


def _routing_tables(bins, rank, cnt, n_rows):
    tm = TM_MOE
    counts = cnt[:N_BINS, 0].astype(I32)
    tiles = (counts + tm - 1) // tm
    tile_end = jnp.cumsum(tiles)
    starts = (tile_end - tiles) * tm
    ids = jnp.arange(N_BINS, dtype=I32)
    pick = lambda key, tab: jnp.sum(jnp.where(key[:, None] == ids[None, :], tab[None, :], 0), axis=1)
    dest = rank + pick(bins, starts)
    n_valid = tile_end[-1]
    n_tiles = n_rows // tm + N_BINS
    s = jnp.minimum(jnp.arange(n_tiles, dtype=I32), n_valid - 1)
    tbin = jnp.sum((s[:, None] >= tile_end[None, :]).astype(I32), axis=1)
    pidx = tbin % 6
    six = jnp.arange(6, dtype=I32)
    slot = lambda tab: jnp.sum(jnp.where(pidx[:, None] == six[None, :], jnp.asarray(tab, I32)[None, :], 0), axis=1)
    ea = 4 * (tbin // 6) + slot(PAIR_SLOT_A)
    eb = 4 * (tbin // 6) + slot(PAIR_SLOT_B)
    return dest, starts + counts, (ea, eb, n_valid.reshape(1))


def _block_diag(w):
    g, c, d = w.shape
    eye = jnp.asarray(np.eye(g), w.dtype)
    return (w[:, :, None, :] * eye[:, None, :, None]).reshape(g * c, g * d)


def _rope_tables(n_tokens):
    rows = n_tokens // GRID_W
    r = np.repeat(np.arange(rows), GRID_W).astype(np.float64)
    col = np.tile(np.arange(GRID_W), rows).astype(np.float64)
    half = HEAD_DIM // 2
    inv = 1.0 / (ROPE_BASE ** (np.arange(0, half, 2, dtype=np.float64) / half))
    ar = r[:, None] * inv
    ac = col[:, None] * inv
    ang = np.concatenate([ar, ar, ac, ac], axis=-1)
    sign = np.where((np.arange(HEAD_DIM) & 16) == 0, -1.0, 1.0)
    cos, sin = np.cos(ang), np.sin(ang) * sign
    return jnp.asarray(np.tile(cos, (1, 2)), F32), jnp.asarray(np.tile(sin, (1, 2)), F32)


def kernel(x, c, ctx, c_ctx, w_mod, b_mod, norm1_g, w_in, q_norm_g, k_norm_g, attn_sink, pool_w, pool_scale,
           four_w, w_out, norm2_g, w_grp, b_grp, w_rtr, b_rtr, w_gate, w_up, w_down):
    nb, seq, _ = x.shape
    lc = ctx.shape[1]
    depth = w_mod.shape[0]
    t_lat = nb * seq
    t_ctx = nb * lc

    cs = jnp.concatenate([c, c_ctx[None, :], jnp.zeros((8 - nb - 1, D), F32)], axis=0)
    m512 = jnp.asarray(np.kron(np.eye(N_HEADS), np.full((HEAD_DIM, HEAD_DIM), 1.0 / HEAD_DIM)), BF16)
    kk = np.arange(HEAD_DIM)
    ang64 = 2.0 * np.pi * ((kk[:, None] * kk[None, :]) % HEAD_DIM) / HEAD_DIM
    c64bd = jnp.asarray(np.kron(np.eye(4), np.cos(ang64)), F32)
    s64bd = jnp.asarray(np.kron(np.eye(4), np.sin(ang64)), F32)
    tri_of = lambda n: jnp.asarray(np.triu(np.ones((min(TM_OUT, n),) * 2)), BF16)
    cos2, sin2 = _rope_tables(seq)
    four_cs, four_tabs = _four2_tables(seq)
    tabs_ctx = _dft_tables(lc, min(TM_FOUR, lc))
    pool_lat = _pool_tables(seq)
    pool_ctx = _pool_tables(lc)
    w_in_b = w_in.astype(BF16)
    w_out_b = w_out.astype(BF16)
    wbd = jnp.stack([_block_diag(four_w[l]) for l in range(depth)])
    pool_wbd = jnp.stack([_block_diag(pool_w[l]) for l in range(depth)]).astype(BF16)
    wr = jnp.concatenate([w_grp, w_rtr, jnp.zeros((depth, D, 128 - 4 - N_EXPERTS), F32)], axis=2)
    wr_hi = wr.astype(BF16)
    w2 = jnp.concatenate([wr_hi, (wr - wr_hi.astype(F32)).astype(BF16)], axis=2)
    br = jnp.concatenate([b_grp, b_rtr, jnp.zeros((depth, 128 - 4 - N_EXPERTS), F32)], axis=1)

    mod_all = _mod_call(cs, w_mod, b_mod)
    ab_all = _ab_call(c64bd, s64bd, wbd)

    xc = ctx
    for l in range(depth):
        last = l == depth - 1
        mod = mod_all[l]
        g1 = norm1_g[l][None, :]
        g2 = norm2_g[l][None, :]
        qg = jnp.tile(q_norm_g[l], N_HEADS)[None, :]
        kg = jnp.tile(k_norm_g[l], KV_W // HEAD_DIM)[None, :]
        in_args = (mod, g1, w_in_b[l], m512, qg, kg, ab_all[l])

        puc, qc, kvc, uac = _in_call(xc, *in_args, None, None, rope=False, per_batch=False)
        pul, ql, kvl, ual = _in_call(x, *in_args, cos2, sin2, rope=True, per_batch=True)
        attn_l = _attn_lat_call(attn_sink[l], ql, kvl, kvc)
        four_l = _four2_call(*_four1_call(ual, four_cs), *four_tabs)

        proj = (w_out_b[l], pool_wbd[l], pool_scale[l][None, :])
        rout = (mod, g2, w2[l], br[l][None, :])
        cnt0 = jnp.zeros((32, 128), F32)
        x_mid, rows_l, cnt = _out_call(x, pul, attn_l, four_l, *proj, *pool_lat, *rout, tri_of(seq), cnt0,
                                       per_batch=True)
        row_sets = [rows_l]
        if not last:
            attn_c = _attn_ctx_call(attn_sink[l], qc, kvc)
            four_c = _four_call(uac, *tabs_ctx)
            xc_mid, rows_c, cnt = _out_call(xc, puc, attn_c, four_c, *proj, *pool_ctx, *rout, tri_of(lc), cnt,
                                            per_batch=False)
            row_sets.append(rows_c)

        route = [r[:, D:D + 2].astype(I32) for r in row_sets]
        bins = jnp.concatenate([r[:, 0] for r in route])
        rank = jnp.concatenate([r[:, 1] for r in route])
        dest, bin_ends, items = _routing_tables(bins, rank, cnt, bins.shape[0])
        xs = _scatter_call(dest, bin_ends, items[2], row_sets)
        ys = _moe_call(items, xs, w_gate, w_up, w_down, l)
        x = _gather_call(dest, x_mid.reshape(t_lat, D), mod, ys, seq=seq, per_batch=True,
                         dest_off=0).reshape(nb, seq, D)
        if not last:
            xc = _gather_call(dest, xc_mid.reshape(t_ctx, D), mod, ys, seq=lc, per_batch=False,
                              dest_off=t_lat).reshape(nb, lc, D)
    return x
```

```python
import functools

import numpy as np
import jax
import jax.numpy as jnp
from jax import lax
from jax.experimental import pallas as pl
from jax.experimental.pallas import tpu as pltpu

F32 = jnp.float32
BF16 = jnp.bfloat16
I32 = jnp.int32
HI = lax.Precision.HIGHEST

D = 1024
HEAD_DIM = 64
N_HEADS = 8
GRID_W = 64
POOL_WINDOWS = (2, 4, 8, 16)
POOL_W = 256
ATTN_W = 512
KV_W = 128
FOUR_W = 256
IN_W = 1280
N_EXPERTS = 16
D_EXPERT = 512
WINDOW = 128
ROPE_BASE = 10000.0
EPS = 1e-6
NEG_INF = -1e30
LOG2_E = 1.4426950408889634
CTX_ROW = 4
N_BINS = 24
PAIR_SLOT_A = (0, 2, 2, 3, 3, 3)
PAIR_SLOT_B = (1, 1, 0, 0, 1, 2)
META_W = 128
ROW_W = D + META_W
ROW_TILES = ROW_W // 128
HALO = 16
POOL_CHUNK = 128

VMEM_LIMIT = 56 * 1024 * 1024
TM_IN = 1024
TQ = 1024
TM_OUT = 1024
TM_FOUR = 256
FOUR_R1 = 16
FOUR1_A_CHUNK = 8
TM_MOE = 512
TM_ROW = 512
TM_GATHER = 512
SCATTER_SLOTS = 4
MOD_TN = 1024


def _cparams(n_axes):
    return pltpu.CompilerParams(dimension_semantics=("arbitrary",) * n_axes,
                                vmem_limit_bytes=VMEM_LIMIT)


def _silu(v):
    return v / (1.0 + jnp.exp(-v))


def _mod_body(cs_ref, w_ref, b_ref, o_ref):
    s = _silu(cs_ref[...])
    w = w_ref[0]
    s_hi = s.astype(BF16)
    s_lo = (s - s_hi.astype(F32)).astype(BF16)
    w_hi = w.astype(BF16)
    w_lo = (w - w_hi.astype(F32)).astype(BF16)
    p = jnp.dot(jnp.concatenate([s_hi, s_lo], axis=0), w_hi, preferred_element_type=F32)
    o_ref[0] = p[0:8] + p[8:16] + jnp.dot(s_hi, w_lo, preferred_element_type=F32) + b_ref[0]


def _mod_call(cs, w_mod, b_mod):
    depth = w_mod.shape[0]
    return pl.pallas_call(
        _mod_body,
        grid=(depth, 6 * D // MOD_TN),
        in_specs=[pl.BlockSpec((8, D), lambda l, j: (0, 0)),
                  pl.BlockSpec((1, D, MOD_TN), lambda l, j: (l, 0, j)),
                  pl.BlockSpec((1, 1, MOD_TN), lambda l, j: (l, 0, j))],
        out_specs=pl.BlockSpec((1, 8, MOD_TN), lambda l, j: (l, 0, j)),
        out_shape=jax.ShapeDtypeStruct((depth, 8, 6 * D), F32),
        compiler_params=_cparams(2),
        name="modulation",
    )(cs, w_mod, b_mod.reshape(depth, 1, 6 * D))


def _ab_body(c_ref, s_ref, w_ref, o_ref):
    w = w_ref[0]
    ca = jnp.dot(c_ref[...], w, preferred_element_type=F32, precision=HI)
    sa = jnp.dot(s_ref[...], w, preferred_element_type=F32, precision=HI)
    o_ref[0] = (jnp.concatenate([ca, sa], axis=1) * (HEAD_DIM ** -0.5)).astype(BF16)


def _ab_call(c64bd, s64bd, wbd):
    depth = wbd.shape[0]
    return pl.pallas_call(
        _ab_body,
        grid=(depth,),
        in_specs=[pl.BlockSpec((FOUR_W, FOUR_W), lambda l: (0, 0)),
                  pl.BlockSpec((FOUR_W, FOUR_W), lambda l: (0, 0)),
                  pl.BlockSpec((1, FOUR_W, FOUR_W), lambda l: (l, 0, 0))],
        out_specs=pl.BlockSpec((1, FOUR_W, 2 * FOUR_W), lambda l: (l, 0, 0)),
        out_shape=jax.ShapeDtypeStruct((depth, FOUR_W, 2 * FOUR_W), BF16),
        compiler_params=_cparams(1),
        name="fourier_weights",
    )(c64bd, s64bd, wbd)


def _head_rms(t, m, g):
    ms = jnp.dot((t * t).astype(BF16), m, preferred_element_type=F32)
    return t * lax.rsqrt(ms + EPS) * g


def _rope(t, cos, sin_signed):
    w = t.shape[1]
    lane = lax.broadcasted_iota(I32, t.shape, 1)
    fwd = pltpu.roll(t, w - 16, 1)
    bwd = pltpu.roll(t, 16, 1)
    rot = jnp.where((lane & 16) == 0, fwd, bwd)
    return t * cos + rot * sin_signed


def _in_body(*refs, rope, per_batch):
    if rope:
        (x_ref, mod_ref, g1_ref, w_ref, m_ref, qg_ref, kg_ref, ab_ref, cos_ref, sin_ref,
         pu_ref, q_ref, kv_ref, ua_ref) = refs
    else:
        (x_ref, mod_ref, g1_ref, w_ref, m_ref, qg_ref, kg_ref, ab_ref,
         pu_ref, q_ref, kv_ref, ua_ref) = refs
    row = pl.program_id(0) if per_batch else CTX_ROW
    sh1 = mod_ref[pl.ds(row, 1), pl.ds(0, D)]
    sc1 = mod_ref[pl.ds(row, 1), pl.ds(D, D)]
    x = x_ref[0]
    ms = jnp.mean(x * x, axis=-1, keepdims=True)
    h = (x * lax.rsqrt(ms + EPS) * g1_ref[...]) * (1.0 + sc1) + sh1
    p = jnp.dot(h.astype(BF16), w_ref[...], preferred_element_type=F32)
    pu = p[:, 0:256]
    q = p[:, 256:768]
    k = p[:, 768:896]
    v = p[:, 896:1024]
    fu = p[:, 1024:1280]
    m = m_ref[...]
    q = _head_rms(q, m, qg_ref[...])
    k = _head_rms(k, m[0:KV_W, 0:KV_W], kg_ref[...])
    if rope:
        cos = cos_ref[...]
        sin = sin_ref[...]
        q = _rope(q, jnp.concatenate([cos] * 4, axis=1), jnp.concatenate([sin] * 4, axis=1))
        k = _rope(k, cos, sin)
    q = q * (HEAD_DIM ** -0.5 * LOG2_E)
    pu_ref[0] = pu.astype(BF16)
    q_ref[0] = q.astype(BF16)
    kv_ref[0] = jnp.concatenate([k, pltpu.roll(k, 64, 1), v, pltpu.roll(v, 64, 1)], axis=1).astype(BF16)
    ua_ref[0] = jnp.dot(fu.astype(BF16), ab_ref[...], preferred_element_type=F32).astype(BF16)


def _in_call(x3, mod, g1, w_in, m512, qg, kg, ab, cos2, sin2, *, rope, per_batch):
    nb, seq, _ = x3.shape
    tm = min(TM_IN, seq)
    full = lambda shape: pl.BlockSpec(shape, lambda b, i: (0,) * len(shape))
    in_specs = [pl.BlockSpec((1, tm, D), lambda b, i: (b, i, 0)),
                full((8, 6 * D)), full((1, D)), full((D, IN_W)), full((ATTN_W, ATTN_W)),
                full((1, ATTN_W)), full((1, KV_W)), full((FOUR_W, 2 * FOUR_W))]
    args = [x3, mod, g1, w_in, m512, qg, kg, ab]
    if rope:
        in_specs += [pl.BlockSpec((tm, 128), lambda b, i: (i, 0)),
                     pl.BlockSpec((tm, 128), lambda b, i: (i, 0))]
        args += [cos2, sin2]
    widths = (POOL_W, ATTN_W, 4 * KV_W, 2 * FOUR_W)
    return pl.pallas_call(
        functools.partial(_in_body, rope=rope, per_batch=per_batch),
        grid=(nb, seq // tm),
        in_specs=in_specs,
        out_specs=[pl.BlockSpec((1, tm, w), lambda b, i: (b, i, 0)) for w in widths],
        out_shape=[jax.ShapeDtypeStruct((nb, seq, w), BF16) for w in widths],
        compiler_params=_cparams(2),
        name="in_proj_rope" if rope else "in_proj_ctx",
    )(*args)


_NT = (((1,), (1,)), ((), ()))


def _stack_heads(qpair0, qpair1, lo):
    z = jnp.zeros_like(qpair0)
    parts = [jnp.where(lo, qpair0, z), jnp.where(lo, qpair1, z),
             jnp.where(lo, pltpu.roll(qpair0, 64, 1), z), jnp.where(lo, pltpu.roll(qpair1, 64, 1), z)]
    return jnp.concatenate(parts, axis=0).astype(BF16)


def _group_attention(q4, k_parts, va_parts, vb_parts, masks, sink_col):
    s_parts = []
    for kz, mk in zip(k_parts, masks):
        s = lax.dot_general(q4, kz, _NT, preferred_element_type=F32)
        if mk is not None:
            nk = s.shape[1]
            s = jnp.where(mk[None], s.reshape(4, 128, nk), NEG_INF).reshape(512, nk)
        s_parts.append(s)
    m = functools.reduce(jnp.maximum, [jnp.max(s, axis=-1, keepdims=True) for s in s_parts])
    m = jnp.maximum(m, sink_col)
    den = jnp.exp2(sink_col - m)
    oe = oo = None
    for s, va, vb in zip(s_parts, va_parts, vb_parts):
        e = jnp.exp2(s - m)
        den = den + jnp.sum(e, axis=-1, keepdims=True)
        eb = e.astype(BF16)
        pe = jnp.dot(eb[0:256], va, preferred_element_type=F32)
        po = jnp.dot(eb[256:512], vb, preferred_element_type=F32)
        oe = pe if oe is None else oe + pe
        oo = po if oo is None else oo + po
    inv = 1.0 / den
    return oe * inv[0:256], oo * inv[256:512]


def _sink_cols(sink_ref):
    rb = lax.broadcasted_iota(I32, (512, 1), 0) >> 7
    cols = []
    for kvh in range(2):
        s = [sink_ref[4 * kvh + j] * LOG2_E for j in (0, 2, 1, 3)]
        cols.append(jnp.where(rb == 0, s[0], jnp.where(rb == 1, s[1], jnp.where(rb == 2, s[2], s[3]))))
    return cols


def _attend_block(qblk, kv_parts, masks, sink_cols):
    lo = lax.broadcasted_iota(I32, (128, 128), 1) < 64
    cols = []
    for kvh in range(2):
        q4 = _stack_heads(qblk[:, 256 * kvh:256 * kvh + 128], qblk[:, 256 * kvh + 128:256 * kvh + 256], lo)
        ko = 128 * kvh
        vao = 256 + 128 * kvh
        vbo = 384 - 128 * kvh
        oe, oo = _group_attention(q4, [kv[:, ko:ko + 128] for kv in kv_parts],
                                  [kv[:, vao:vao + 128] for kv in kv_parts],
                                  [kv[:, vbo:vbo + 128] for kv in kv_parts], masks, sink_cols[kvh])
        cols.append(jnp.where(lo, oe[0:128], oo[0:128]))
        cols.append(jnp.where(lo, oe[128:256], oo[128:256]))
    return jnp.concatenate(cols, axis=1).astype(BF16)


def _attn_lat_body(sink_ref, q_ref, kvp_ref, kvm_ref, kvn_ref, kvc_ref, o_ref, kvw_ref, *, tq, seq):
    i = pl.program_id(1)
    kvw_ref[0:128] = kvp_ref[0]
    kvw_ref[128:128 + tq] = kvm_ref[0]
    kvw_ref[128 + tq:256 + tq] = kvn_ref[0]
    kvc = kvc_ref[0]
    sink_cols = _sink_cols(sink_ref)

    def sub(j, carry):
        r0 = pl.multiple_of(j * 128, 128)
        win = kvw_ref[pl.ds(r0, 3 * 128), :]
        ii = lax.broadcasted_iota(I32, (128, 3 * 128), 0)
        cc = lax.broadcasted_iota(I32, (128, 3 * 128), 1)
        base = i * tq + j * 128 - 128
        valid = (ii <= cc) & (cc <= ii + 2 * WINDOW) & (cc >= -base) & (cc < seq - base)
        qblk = q_ref[0, pl.ds(r0, 128), :].astype(F32)
        o_ref[0, pl.ds(r0, 128), :] = _attend_block(qblk, [win, kvc], [valid, None], sink_cols)
        return carry

    for j in range(tq // 128):
        sub(j, 0)


def _attn_lat_call(sink, q, kv, kvc):
    nb, seq, _ = q.shape
    lc = kvc.shape[1]
    tq = TQ
    nblk = seq // 128
    r = tq // 128
    return pl.pallas_call(
        functools.partial(_attn_lat_body, tq=tq, seq=seq),
        grid=(nb, seq // tq),
        in_specs=[pl.BlockSpec(memory_space=pltpu.SMEM),
                  pl.BlockSpec((1, tq, ATTN_W), lambda b, i: (b, i, 0)),
                  pl.BlockSpec((1, 128, 4 * KV_W), lambda b, i: (b, jnp.maximum(i * r - 1, 0), 0)),
                  pl.BlockSpec((1, tq, 4 * KV_W), lambda b, i: (b, i, 0)),
                  pl.BlockSpec((1, 128, 4 * KV_W), lambda b, i: (b, jnp.minimum((i + 1) * r, nblk - 1), 0)),
                  pl.BlockSpec((1, lc, 4 * KV_W), lambda b, i: (b, 0, 0))],
        out_specs=pl.BlockSpec((1, tq, ATTN_W), lambda b, i: (b, i, 0)),
        out_shape=jax.ShapeDtypeStruct((nb, seq, ATTN_W), BF16),
        scratch_shapes=[pltpu.VMEM((tq + 256, 4 * KV_W), BF16)],
        compiler_params=_cparams(2),
        name="attention_window",
    )(sink, q, kv, kv, kv, kvc)


def _attn_ctx_body(sink_ref, q_ref, kvc_ref, o_ref, *, lc):
    kvc = kvc_ref[0]
    sink_cols = _sink_cols(sink_ref)
    for j in range(lc // 128):
        qblk = q_ref[0, j * 128:(j + 1) * 128, :].astype(F32)
        o_ref[0, j * 128:(j + 1) * 128, :] = _attend_block(qblk, [kvc], [None], sink_cols)


def _attn_ctx_call(sink, qc, kvc):
    nb, lc, _ = qc.shape
    return pl.pallas_call(
        functools.partial(_attn_ctx_body, lc=lc),
        grid=(nb,),
        in_specs=[pl.BlockSpec(memory_space=pltpu.SMEM),
                  pl.BlockSpec((1, lc, ATTN_W), lambda b: (b, 0, 0)),
                  pl.BlockSpec((1, lc, 4 * KV_W), lambda b: (b, 0, 0))],
        out_specs=pl.BlockSpec((1, lc, ATTN_W), lambda b: (b, 0, 0)),
        out_shape=jax.ShapeDtypeStruct((nb, lc, ATTN_W), BF16),
        compiler_params=_cparams(1),
        name="attention_ctx",
    )(sink, qc, kvc)


def _four_body(ua_ref, cb_ref, sb_ref, ca_ref, sa_ref, o_ref, *, nb, scale):
    i = pl.program_id(0)
    ca = ca_ref[pl.ds(i, 1), :]
    sa = sa_ref[pl.ds(i, 1), :]
    cb = cb_ref[...]
    sb = sb_ref[...]
    ct = (ca * cb - sa * sb).astype(BF16)
    st = (sa * cb + ca * sb).astype(BF16)
    for b in range(nb):
        ua = ua_ref[b, :, 0:FOUR_W]
        ub = ua_ref[b, :, FOUR_W:2 * FOUR_W]
        r = (jnp.dot(ct, ua, preferred_element_type=F32) - jnp.dot(st, ub, preferred_element_type=F32))
        o_ref[b] = (r * scale).astype(BF16)


def _four_call(uaub, cb, sb, ca, sa):
    nb, seq, _ = uaub.shape
    tm = cb.shape[0]
    one = pl.Buffered(1)
    return pl.pallas_call(
        functools.partial(_four_body, nb=nb, scale=float(seq) ** -0.5),
        grid=(seq // tm,),
        in_specs=[pl.BlockSpec((nb, seq, 2 * FOUR_W), lambda i: (0, 0, 0), pipeline_mode=one),
                  pl.BlockSpec((tm, seq), lambda i: (0, 0), pipeline_mode=one),
                  pl.BlockSpec((tm, seq), lambda i: (0, 0), pipeline_mode=one),
                  pl.BlockSpec((seq // tm, seq), lambda i: (0, 0), pipeline_mode=one),
                  pl.BlockSpec((seq // tm, seq), lambda i: (0, 0), pipeline_mode=one)],
        out_specs=pl.BlockSpec((nb, tm, FOUR_W), lambda i: (0, i, 0)),
        out_shape=jax.ShapeDtypeStruct((nb, seq, FOUR_W), BF16),
        compiler_params=_cparams(1),
        name="fourier_dft",
    )(uaub, cb, sb, ca, sa)


def _dft_tables(seq, tm):
    n = np.arange(seq)[None, :]

    def tab(rows):
        ang = ((rows[:, None] * n) % seq) * (2.0 * np.pi / seq)
        return jnp.asarray(np.cos(ang), F32), jnp.asarray(np.sin(ang), F32)

    cb, sb = tab(np.arange(tm))
    ca, sa = tab(np.arange(seq // tm) * tm)
    return cb, sb, ca, sa


def _four1_body(x_ref, g_ref, yr_ref, yi_ref):
    g = g_ref[...].astype(BF16)
    n = FOUR_R1 * 16
    w = FOUR_W
    for aa in range(x_ref.shape[2]):
        x = x_ref[0, :, aa].reshape(n, 2 * w)
        p = jnp.dot(g, x, preferred_element_type=F32)
        yr = p[0:n, 0:w] - p[n:2 * n, w:2 * w]
        yi = -(p[0:n, w:2 * w] + p[n:2 * n, 0:w])
        yr_ref[0, :, aa] = yr.astype(BF16).reshape(FOUR_R1, 16, w)
        yi_ref[0, :, aa] = yi.astype(BF16).reshape(FOUR_R1, 16, w)


def _four1_call(uaub, g):
    nb, seq, _ = uaub.shape
    r2 = seq // FOUR_R1
    na = r2 // 16
    x = uaub.reshape(nb, FOUR_R1, na, 16, 2 * FOUR_W)
    out = jax.ShapeDtypeStruct((nb, FOUR_R1, na, 16, FOUR_W), BF16)
    ac = FOUR1_A_CHUNK
    return pl.pallas_call(
        _four1_body,
        grid=(nb, na // ac),
        in_specs=[pl.BlockSpec((1, FOUR_R1, ac, 16, 2 * FOUR_W), lambda b, j: (b, 0, j, 0, 0)),
                  pl.BlockSpec(g.shape, lambda b, j: (0, 0))],
        out_specs=[pl.BlockSpec((1, FOUR_R1, ac, 16, FOUR_W), lambda b, j: (b, 0, j, 0, 0))] * 2,
        out_shape=[out, out],
        compiler_params=_cparams(2),
        name="fourier_stage1",
    )(x, g)


def _four2_body(yr_ref, yi_ref, c_ref, s_ref, ca_ref, sa_ref, o_ref, obuf, sem, *, nb, scale):
    k1 = pl.program_id(0)
    nk = pl.num_programs(0)
    ca = ca_ref[pl.ds(k1, 1), :]
    sa = sa_ref[pl.ds(k1, 1), :]
    c = c_ref[...]
    s = s_ref[...]
    gc = (c * ca - s * sa).astype(BF16)
    gs = (s * ca + c * sa).astype(BF16)

    def out_copy(slot, b, kk):
        return pltpu.make_async_copy(obuf.at[slot, b], o_ref.at[b, :, kk, :], sem.at[slot])

    for slot in range(2):
        @pl.when(k1 % 2 == slot)
        def _():
            @pl.when(k1 >= 2)
            def _():
                for b in range(nb):
                    out_copy(slot, b, 0).wait()

            for b in range(nb):
                acc = (jnp.dot(gc, yr_ref[b, 0], preferred_element_type=F32)
                       + jnp.dot(gs, yi_ref[b, 0], preferred_element_type=F32))
                obuf[slot, b] = acc * scale
            for b in range(nb):
                out_copy(slot, b, k1).start()

    @pl.when(k1 == nk - 1)
    def _():
        for slot in range(2):
            for b in range(nb):
                out_copy(slot, b, 0).wait()


def _four2_call(yr, yi, c, s, ca, sa):
    nb = yr.shape[0]
    r2 = c.shape[0]
    seq = FOUR_R1 * r2
    yr4 = yr.reshape(nb, FOUR_R1, r2, FOUR_W)
    yi4 = yi.reshape(nb, FOUR_R1, r2, FOUR_W)
    full = lambda shape: pl.BlockSpec(shape, lambda k: (0,) * len(shape))
    out = pl.pallas_call(
        functools.partial(_four2_body, nb=nb, scale=float(seq) ** -0.5),
        grid=(FOUR_R1,),
        in_specs=[pl.BlockSpec((nb, 1, r2, FOUR_W), lambda k: (0, k, 0, 0)),
                  pl.BlockSpec((nb, 1, r2, FOUR_W), lambda k: (0, k, 0, 0)),
                  full((r2, r2)), full((r2, r2)), full((FOUR_R1, r2)), full((FOUR_R1, r2))],
        out_specs=pl.BlockSpec(memory_space=pl.ANY),
        out_shape=jax.ShapeDtypeStruct((nb, r2, FOUR_R1, FOUR_W), F32),
        scratch_shapes=[pltpu.VMEM((2, nb, r2, FOUR_W), F32), pltpu.SemaphoreType.DMA((2,))],
        compiler_params=_cparams(1),
        name="fourier_stage2",
    )(yr4, yi4, c, s, ca, sa)
    return out.reshape(nb, seq, FOUR_W)


def _four2_tables(seq):
    r1 = FOUR_R1
    r2 = seq // r1
    assert r1 * r2 == seq and r2 % 16 == 0
    k = np.arange(r1)
    ang1 = ((k[:, None] * k[None, :]) % r1) * (2.0 * np.pi / r1)
    eye = np.eye(16)
    g = np.concatenate([np.kron(np.cos(ang1), eye), np.kron(np.sin(ang1), eye)], axis=0)
    m = np.arange(r2)
    ang2 = ((m[:, None] * m[None, :]) % r2) * (2.0 * np.pi / r2)
    alpha = ((k[:, None] * m[None, :]) % seq) * (2.0 * np.pi / seq)
    f = lambda t: jnp.asarray(t, F32)
    return f(g), (f(np.cos(ang2)), f(np.sin(ang2)), f(np.cos(alpha)), f(np.sin(alpha)))


def _route(lt, tri_ref, cnt_ref, tm):
    rowi = lax.broadcasted_iota(I32, (32, tm), 0)
    big = jnp.int32(999)

    def first_argmax(vals):
        mx = jnp.max(vals, axis=0, keepdims=True)
        return mx, jnp.min(jnp.where(vals == mx, rowi, big), axis=0, keepdims=True)

    is_grp = rowi < 4
    mg, gi = first_argmax(jnp.where(is_grp, lt, NEG_INF))
    pg = 1.0 / jnp.sum(jnp.where(is_grp, jnp.exp(jnp.where(is_grp, lt, mg) - mg), 0.0), axis=0, keepdims=True)
    est = 4 + 4 * gi
    le = jnp.where((rowi >= est) & (rowi < est + 4), lt, NEG_INF)
    m1, i1 = first_argmax(le)
    m2, i2 = first_argmax(jnp.where(rowi == i1, NEG_INF, le))
    e2 = jnp.exp(m2 - m1)
    w1 = pg / (1.0 + e2)
    w2 = pg * e2 / (1.0 + e2)
    a1 = i1 - est
    a2 = i2 - est
    code = jnp.minimum(a1, a2) * 4 + jnp.maximum(a1, a2)
    pidx = jnp.where(code == 1, 0, jnp.where(code == 6, 1, jnp.where(code == 2, 2,
           jnp.where(code == 3, 3, jnp.where(code == 7, 4, 5)))))
    slot_a = jnp.where(pidx == 0, 0, jnp.where(pidx <= 2, 2, 3))
    slot_b = jnp.where(pidx <= 1, 1, jnp.where(pidx <= 3, 0, jnp.where(pidx == 4, 1, 2)))
    wa = jnp.where(a1 == slot_a, w1, w2)
    wb = jnp.where(a1 == slot_b, w1, w2)
    bin_ = gi * 6 + pidx

    onehot = rowi == bin_
    pref = jnp.dot(onehot.astype(BF16), tri_ref[...], preferred_element_type=F32)
    carry = cnt_ref[:, 0:1]
    rank = jnp.sum(jnp.where(onehot, pref - 1.0 + carry, 0.0), axis=0, keepdims=True)
    cnt_ref[...] = jnp.broadcast_to(carry + pref[:, tm - 1:tm], cnt_ref.shape)
    return jnp.concatenate([bin_.astype(F32), rank, wa, wb, jnp.zeros((128 - 4, tm), F32)], axis=0)


def _out_body(x_ref, pup_ref, pum_ref, pun_ref, at_ref, fo_ref, wo_ref, pw_ref, ps_ref, band_ref, icnt_ref,
              mod_ref, g2_ref, w2_ref, br_ref, tri_ref, cin_ref,
              xo_ref, rows_ref, cnt_ref, *, tm, per_batch):
    b = pl.program_id(0)
    i = pl.program_id(1)
    nt = pl.num_programs(1)
    row = b if per_batch else CTX_ROW
    g1 = mod_ref[pl.ds(row, 1), pl.ds(2 * D, D)]
    sh2 = mod_ref[pl.ds(row, 1), pl.ds(3 * D, D)]
    sc2 = mod_ref[pl.ds(row, 1), pl.ds(4 * D, D)]

    @pl.when((b == 0) & (i == 0))
    def _():
        cnt_ref[...] = cin_ref[...]

    um = pum_ref[0]
    zh = jnp.zeros((HALO, POOL_W), BF16)
    uext = jnp.concatenate([jnp.where(i > 0, pup_ref[0], zh), um, jnp.where(i < nt - 1, pun_ref[0], zh)], axis=0)
    grp = lax.broadcasted_iota(I32, (POOL_CHUNK, POOL_W), 1) >> 6
    chunks = []
    for c in range(tm // POOL_CHUNK):
        uc = uext[POOL_CHUNK * c:POOL_CHUNK * (c + 1) + 2 * HALO]
        pc = jnp.zeros((POOL_CHUNK, POOL_W), F32)
        for g in range(len(POOL_WINDOWS)):
            pc = jnp.where(grp == g, jnp.dot(band_ref[g], uc, preferred_element_type=F32), pc)
        chunks.append(pc)
    pooled = jnp.concatenate(chunks, axis=0)
    y = pooled * icnt_ref[...] - um.astype(F32)
    pool_out = jnp.dot(y.astype(BF16), pw_ref[...], preferred_element_type=F32) * ps_ref[...]

    cat = jnp.concatenate([pool_out.astype(BF16), at_ref[0], fo_ref[0].astype(BF16)], axis=1)
    xm = x_ref[0] + g1 * jnp.dot(cat, wo_ref[...], preferred_element_type=F32)
    xo_ref[0] = xm

    ms = jnp.mean(xm * xm, axis=-1, keepdims=True)
    h2 = (xm * lax.rsqrt(ms + EPS) * g2_ref[...]) * (1.0 + sc2) + sh2

    hh = h2.astype(BF16)
    hl = (h2 - hh.astype(F32)).astype(BF16)
    w2 = w2_ref[...]
    p2 = jnp.dot(hh, w2, preferred_element_type=F32)
    logits = (p2[:, 0:128] + p2[:, 128:256] + jnp.dot(hl, w2[:, 0:128], preferred_element_type=F32) + br_ref[...])
    meta = _route(logits.T[0:32, :], tri_ref, cnt_ref, tm).T

    rows_ref[:, 0:D] = h2
    rows_ref[:, D:ROW_W] = meta


def _out_call(x3, pu, attn, four, w_out, pool_wbd, pool_scale, bands, icnt, mod, g2, w2, br, tri, cnt_in, *,
              per_batch):
    nb, seq, _ = x3.shape
    tm = min(TM_OUT, seq)
    nt = seq // tm
    hb = tm // HALO
    full = lambda shape: pl.BlockSpec(shape, lambda b, i: (0,) * len(shape))
    in_specs = [pl.BlockSpec((1, tm, D), lambda b, i: (b, i, 0)),
                pl.BlockSpec((1, HALO, POOL_W), lambda b, i: (b, jnp.maximum(i * hb - 1, 0), 0)),
                pl.BlockSpec((1, tm, POOL_W), lambda b, i: (b, i, 0)),
                pl.BlockSpec((1, HALO, POOL_W), lambda b, i: (b, jnp.minimum((i + 1) * hb, seq // HALO - 1), 0)),
                pl.BlockSpec((1, tm, ATTN_W), lambda b, i: (b, i, 0)),
                pl.BlockSpec((1, tm, FOUR_W), lambda b, i: (b, i, 0)),
                full((D, D)), full((POOL_W, POOL_W)), full((1, POOL_W)),
                full((len(POOL_WINDOWS), POOL_CHUNK, POOL_CHUNK + 2 * HALO)),
                pl.BlockSpec((tm, POOL_W), lambda b, i: (i, 0)),
                full((8, 6 * D)), full((1, D)), full((D, 256)), full((1, 128)), full((tm, tm)), full((32, 128))]
    args = [x3, pu, pu, pu, attn, four, w_out, pool_wbd, pool_scale, bands, icnt, mod, g2, w2, br, tri, cnt_in]
    return pl.pallas_call(
        functools.partial(_out_body, tm=tm, per_batch=per_batch),
        grid=(nb, nt),
        in_specs=in_specs,
        out_specs=[pl.BlockSpec((1, tm, D), lambda b, i: (b, i, 0)),
                   pl.BlockSpec((tm, ROW_W), lambda b, i: (b * nt + i, 0)),
                   pl.BlockSpec((32, 128), lambda b, i: (0, 0))],
        out_shape=[jax.ShapeDtypeStruct((nb, seq, D), F32),
                   jax.ShapeDtypeStruct((nb * seq, ROW_W), F32),
                   jax.ShapeDtypeStruct((32, 128), F32)],
        compiler_params=_cparams(2),
        name="out_proj_router" if per_batch else "out_proj_router_ctx",
    )(*args)


def _pool_tables(seq):
    t = np.arange(POOL_CHUNK)[:, None]
    s = np.arange(POOL_CHUNK + 2 * HALO)[None, :] - HALO
    bands = np.stack([(s >= t - w // 2) & (s <= t + w // 2 - 1) for w in POOL_WINDOWS]).astype(np.float32)
    pos = np.arange(seq)
    icnt = np.stack([1.0 / (np.minimum(pos + w // 2 - 1, seq - 1) - np.maximum(pos - w // 2, 0) + 1)
                     for w in POOL_WINDOWS], axis=1)
    return jnp.asarray(bands, BF16), jnp.asarray(np.repeat(icnt, POOL_W // len(POOL_WINDOWS), axis=1), F32)


def _row_copies(tm, make_copy):
    for r in range(tm):
        make_copy(r).start(priority=r % 2)


def _row_waits(tm, make_copy):
    def drain(r, c):
        make_copy(0).wait()
        return c

    lax.fori_loop(0, tm, drain, 0, unroll=8)


def _zero_fill(ends_ref, nv_ref, xs_ref, zbuf, zsem, tm, n_out, wait):
    def piece(off, size):
        return pltpu.make_async_copy(zbuf.at[pl.ds(0, size)], xs_ref.at[pl.ds(off, size)], zsem)

    def run(cond, off, size):
        @pl.when(cond)
        def _():
            c = piece(off, size)
            c.wait() if wait else c.start()

    for b in range(N_BINS):
        off = ends_ref[b]
        pad = (tm - (off & (tm - 1))) & (tm - 1)
        for k in range(tm.bit_length() - 1):
            run(((pad >> k) & 1) == 1, off, 1 << k)
            off = off + (pad & (1 << k))

    def tail(t, c):
        c_ = piece(t * tm, tm)
        c_.wait() if wait else c_.start()
        return c

    lax.fori_loop(nv_ref[0], n_out, tail, 0)


def _scatter_body(dest_ref, ends_ref, nv_ref, *refs, tm, n_tiles, n_first, n_out):
    n_h = len(refs) - 3 - 3 * SCATTER_SLOTS
    h_refs, xs_ref = refs[:n_h], refs[n_h]
    bufs = refs[n_h + 1:n_h + 1 + SCATTER_SLOTS]
    lsems = refs[n_h + 1 + SCATTER_SLOTS:n_h + 1 + 2 * SCATTER_SLOTS]
    rsems = refs[n_h + 1 + 2 * SCATTER_SLOTS:n_h + 1 + 3 * SCATTER_SLOTS]
    zbuf, zsem = refs[-2:]
    zbuf[...] = jnp.zeros_like(zbuf)
    _zero_fill(ends_ref, nv_ref, xs_ref, zbuf, zsem, tm, n_out, wait=False)

    def lane_block_copy(h_ref, tt, slot, j):
        return pltpu.make_async_copy(h_ref.at[pl.ds(tt * tm, tm), pl.ds(128 * j, 128)],
                                     bufs[slot].at[:, j, :], lsems[slot])

    def load(t, slot):
        def start(h_ref, tt):
            for j in range(ROW_TILES):
                lane_block_copy(h_ref, tt, slot, j).start()

        if n_h == 1:
            start(h_refs[0], t)
        else:
            @pl.when(t < n_first)
            def _():
                start(h_refs[0], t)

            @pl.when(t >= n_first)
            def _():
                start(h_refs[1], t - n_first)

    def row_copy(slot, r, d):
        return pltpu.make_async_copy(bufs[slot].at[r], xs_ref.at[d], rsems[slot])

    load(0, 0)
    load(1, 1)

    def tile_step(t, slot):
        ahead = (slot + 2) % SCATTER_SLOTS
        for j in range(ROW_TILES):
            lane_block_copy(h_refs[0], 0, slot, j).wait()

        @pl.when(t >= 2)
        def _():
            _row_waits(tm, lambda r: row_copy(ahead, r, 0))

        @pl.when(t + 2 < n_tiles)
        def _():
            load(t + 2, ahead)

        _row_copies(tm, lambda r: row_copy(slot, r, dest_ref[t * tm + r]))

    def group(g, c):
        for slot in range(SCATTER_SLOTS):
            tile_step(g * SCATTER_SLOTS + slot, slot)
        return c

    n_groups = n_tiles // SCATTER_SLOTS
    lax.fori_loop(0, n_groups, group, 0)
    for t in range(n_groups * SCATTER_SLOTS, n_tiles):
        tile_step(jnp.int32(t), t % SCATTER_SLOTS)
    for t in (n_tiles - 2, n_tiles - 1):
        _row_waits(tm, lambda r: row_copy(t % SCATTER_SLOTS, r, 0))
    _zero_fill(ends_ref, nv_ref, xs_ref, zbuf, zsem, tm, n_out, wait=True)


def _scatter_call(dest, bin_ends, n_valid, row_sets):
    tm = TM_ROW
    assert tm == TM_MOE
    n_first = row_sets[0].shape[0] // tm
    n_rows = sum(r.shape[0] for r in row_sets)
    n_tiles = n_rows // tm
    n_out = n_tiles + N_BINS
    assert n_tiles >= SCATTER_SLOTS
    return pl.pallas_call(
        functools.partial(_scatter_body, tm=tm, n_tiles=n_tiles, n_first=n_first, n_out=n_out),
        grid_spec=pltpu.PrefetchScalarGridSpec(
            num_scalar_prefetch=3,
            grid=(1,),
            in_specs=[pl.BlockSpec(memory_space=pl.ANY)] * len(row_sets),
            out_specs=pl.BlockSpec(memory_space=pl.ANY),
            scratch_shapes=([pltpu.VMEM((tm, ROW_TILES, 128), F32)] * SCATTER_SLOTS
                            + [pltpu.SemaphoreType.DMA(())] * (2 * SCATTER_SLOTS)
                            + [pltpu.VMEM((tm, ROW_TILES, 128), F32), pltpu.SemaphoreType.DMA(())])),
        out_shape=jax.ShapeDtypeStruct((n_out * tm, ROW_TILES, 128), F32),
        compiler_params=_cparams(1),
        name="moe_scatter_rows",
    )(dest, bin_ends, n_valid, *row_sets)


def _moe_body(e_ref, first_ref, buf_ref, nxt_ref, more_ref, nv_ref, xs_ref, wg_hbm, wu_hbm, wd_hbm, ys_ref,
              xbuf0, xbuf1, gbuf, ubuf, dbuf, sem0, sem1, wsem, *, tm, layer):
    s = pl.program_id(0)
    nv = nv_ref[0]
    xbufs = (xbuf0, xbuf1)
    sems = (sem0, sem1)

    def lane_block_copy(t, slot, j):
        return pltpu.make_async_copy(xs_ref.at[pl.ds(t * tm, tm), j, :], xbufs[slot].at[:, pl.ds(128 * j, 128)],
                                     sems[slot])

    def fetch(t, slot):
        for j in range(ROW_TILES):
            lane_block_copy(t, slot, j).start()

    def weight_copies(x, e, b):
        return [pltpu.make_async_copy(w.at[layer, e], buf.at[x, b], wsem.at[x, b])
                for w, buf in ((wg_hbm, gbuf), (wu_hbm, ubuf), (wd_hbm, dbuf))]

    @pl.when(s == 0)
    def _():
        fetch(0, 0)
        for x in range(2):
            for c in weight_copies(x, e_ref[x, 0], 0):
                c.start()

    @pl.when(s >= nv)
    def _():
        ys_ref[...] = jnp.zeros_like(ys_ref)

    for slot in range(2):
        @pl.when((s < nv) & (s % 2 == slot))
        def _():
            @pl.when(s + 1 < nv)
            def _():
                fetch(s + 1, 1 - slot)

            for x in range(2):
                @pl.when(first_ref[x, s] == 1)
                def _():
                    b = buf_ref[x, s]
                    for c in weight_copies(x, 0, b):
                        c.wait()

                    @pl.when(more_ref[x, s] == 1)
                    def _():
                        for c in weight_copies(x, nxt_ref[x, s], 1 - b):
                            c.start()

            for j in range(ROW_TILES):
                lane_block_copy(0, slot, j).wait()
            xs = xbufs[slot]
            h = xs[:, 0:D].astype(BF16)
            meta = xs[:, D:ROW_W]

            def expert(x, gate):
                b = buf_ref[x, s]
                g = jnp.dot(h, gbuf[x, b].astype(BF16), preferred_element_type=F32)
                u = jnp.dot(h, ubuf[x, b].astype(BF16), preferred_element_type=F32)
                a = _silu(g) * u * gate
                return jnp.dot(a.astype(BF16), dbuf[x, b].astype(BF16), preferred_element_type=F32)

            ys_ref[...] = expert(0, meta[:, 2:3]) + expert(1, meta[:, 3:4])


def _moe_call(items, xs, w_gate, w_up, w_down, layer):
    tm = TM_MOE
    n_tiles = xs.shape[0] // tm
    e = jnp.stack(items[:2])
    pos = jnp.arange(n_tiles, dtype=I32)
    first = jnp.concatenate([jnp.ones((2, 1), I32), (e[:, 1:] != e[:, :-1]).astype(I32)], axis=1)
    buf = (jnp.cumsum(first, axis=1) - 1) % 2
    later = (pos[None, None, :] > pos[None, :, None]) & (first[:, None, :] == 1)
    nidx = jnp.min(jnp.where(later, pos[None, None, :], n_tiles), axis=2)
    more = (nidx < n_tiles).astype(I32)
    nxt = jnp.take_along_axis(e, jnp.minimum(nidx, n_tiles - 1), axis=1)
    any_spec = pl.BlockSpec(memory_space=pl.ANY)
    return pl.pallas_call(
        functools.partial(_moe_body, tm=tm, layer=layer),
        grid_spec=pltpu.PrefetchScalarGridSpec(
            num_scalar_prefetch=6,
            grid=(n_tiles,),
            in_specs=[any_spec, any_spec, any_spec, any_spec],
            out_specs=pl.BlockSpec((tm, D), lambda s, *_: (s, 0)),
            scratch_shapes=[pltpu.VMEM((tm, ROW_W), F32), pltpu.VMEM((tm, ROW_W), F32),
                            pltpu.VMEM((2, 2, D, D_EXPERT), F32), pltpu.VMEM((2, 2, D, D_EXPERT), F32),
                            pltpu.VMEM((2, 2, D_EXPERT, D), F32),
                            pltpu.SemaphoreType.DMA(()), pltpu.SemaphoreType.DMA(()),
                            pltpu.SemaphoreType.DMA((2, 2))]),
        out_shape=jax.ShapeDtypeStruct((xs.shape[0], D), F32),
        compiler_params=_cparams(1),
        name="moe_experts",
    )(e, first, buf.astype(I32), nxt, more, items[2], xs, w_gate, w_up, w_down)


def _gather_body(dest_ref, x_ref, mod_ref, ys_ref, o_ref, ybuf0, ybuf1, sem0, sem1, *,
                 tm, seq, per_batch, dest_off):
    i = pl.program_id(0)
    n_tiles = pl.num_programs(0)
    ybufs = (ybuf0, ybuf1)
    sems = (sem0, sem1)
    row = (i * tm) // seq if per_batch else CTX_ROW
    g2 = mod_ref[pl.ds(row, 1), pl.ds(5 * D, D)]

    def row_copy(slot, r, d):
        return pltpu.make_async_copy(ys_ref.at[pl.ds(d, 1)], ybufs[slot].at[pl.ds(r, 1)], sems[slot])

    def fetch(t, slot):
        base = dest_off + t * tm
        _row_copies(tm, lambda r: row_copy(slot, r, dest_ref[base + r]))

    @pl.when(i == 0)
    def _():
        fetch(0, 0)

    for slot in range(2):
        @pl.when(i % 2 == slot)
        def _():
            @pl.when(i + 1 < n_tiles)
            def _():
                fetch(i + 1, 1 - slot)

            _row_waits(tm, lambda r: row_copy(slot, r, 0))
            o_ref[...] = x_ref[...] + g2 * ybufs[slot][...]


def _gather_call(dest, x2, mod, ys, *, seq, per_batch, dest_off):
    n = x2.shape[0]
    tm = min(TM_GATHER, seq)
    return pl.pallas_call(
        functools.partial(_gather_body, tm=tm, seq=seq, per_batch=per_batch, dest_off=dest_off),
        grid_spec=pltpu.PrefetchScalarGridSpec(
            num_scalar_prefetch=1,
            grid=(n // tm,),
            in_specs=[pl.BlockSpec((tm, D), lambda i, *_: (i, 0)),
                      pl.BlockSpec((8, 6 * D), lambda i, *_: (0, 0)),
                      pl.BlockSpec(memory_space=pl.ANY)],
            out_specs=pl.BlockSpec((tm, D), lambda i, *_: (i, 0)),
            scratch_shapes=[pltpu.VMEM((tm, D), F32), pltpu.VMEM((tm, D), F32),
                            pltpu.SemaphoreType.DMA(()), pltpu.SemaphoreType.DMA(())]),
        out_shape=jax.ShapeDtypeStruct((n, D), F32),
        compiler_params=_cparams(1),
        name="moe_gather_rows" if per_batch else "moe_gather_rows_ctx",
    )(dest, x2, mod, ys)


def _routing_tables(bins, rank, cnt, n_rows):
    tm = TM_MOE
    counts = cnt[:N_BINS, 0].astype(I32)
    tiles = (counts + tm - 1) // tm
    tile_end = jnp.cumsum(tiles)
    starts = (tile_end - tiles) * tm
    ids = jnp.arange(N_BINS, dtype=I32)
    pick = lambda key, tab: jnp.sum(jnp.where(key[:, None] == ids[None, :], tab[None, :], 0), axis=1)
    dest = rank + pick(bins, starts)
    n_valid = tile_end[-1]
    n_tiles = n_rows // tm + N_BINS
    s = jnp.minimum(jnp.arange(n_tiles, dtype=I32), n_valid - 1)
    tbin = jnp.sum((s[:, None] >= tile_end[None, :]).astype(I32), axis=1)
    pidx = tbin % 6
    six = jnp.arange(6, dtype=I32)
    slot = lambda tab: jnp.sum(jnp.where(pidx[:, None] == six[None, :], jnp.asarray(tab, I32)[None, :], 0), axis=1)
    ea = 4 * (tbin // 6) + slot(PAIR_SLOT_A)
    eb = 4 * (tbin // 6) + slot(PAIR_SLOT_B)
    return dest, starts + counts, (ea, eb, n_valid.reshape(1))


def _block_diag(w):
    g, c, d = w.shape
    eye = jnp.asarray(np.eye(g), w.dtype)
    return (w[:, :, None, :] * eye[:, None, :, None]).reshape(g * c, g * d)


def _rope_tables(n_tokens):
    rows = n_tokens // GRID_W
    r = np.repeat(np.arange(rows), GRID_W).astype(np.float64)
    col = np.tile(np.arange(GRID_W), rows).astype(np.float64)
    half = HEAD_DIM // 2
    inv = 1.0 / (ROPE_BASE ** (np.arange(0, half, 2, dtype=np.float64) / half))
    ar = r[:, None] * inv
    ac = col[:, None] * inv
    ang = np.concatenate([ar, ar, ac, ac], axis=-1)
    sign = np.where((np.arange(HEAD_DIM) & 16) == 0, -1.0, 1.0)
    cos, sin = np.cos(ang), np.sin(ang) * sign
    return jnp.asarray(np.tile(cos, (1, 2)), F32), jnp.asarray(np.tile(sin, (1, 2)), F32)


def kernel(x, c, ctx, c_ctx, w_mod, b_mod, norm1_g, w_in, q_norm_g, k_norm_g, attn_sink, pool_w, pool_scale,
           four_w, w_out, norm2_g, w_grp, b_grp, w_rtr, b_rtr, w_gate, w_up, w_down):
    nb, seq, _ = x.shape
    lc = ctx.shape[1]
    depth = w_mod.shape[0]
    t_lat = nb * seq
    t_ctx = nb * lc

    cs = jnp.concatenate([c, c_ctx[None, :], jnp.zeros((8 - nb - 1, D), F32)], axis=0)
    m512 = jnp.asarray(np.kron(np.eye(N_HEADS), np.full((HEAD_DIM, HEAD_DIM), 1.0 / HEAD_DIM)), BF16)
    kk = np.arange(HEAD_DIM)
    ang64 = 2.0 * np.pi * ((kk[:, None] * kk[None, :]) % HEAD_DIM) / HEAD_DIM
    c64bd = jnp.asarray(np.kron(np.eye(4), np.cos(ang64)), F32)
    s64bd = jnp.asarray(np.kron(np.eye(4), np.sin(ang64)), F32)
    tri_of = lambda n: jnp.asarray(np.triu(np.ones((min(TM_OUT, n),) * 2)), BF16)
    cos2, sin2 = _rope_tables(seq)
    four_cs, four_tabs = _four2_tables(seq)
    tabs_ctx = _dft_tables(lc, min(TM_FOUR, lc))
    pool_lat = _pool_tables(seq)
    pool_ctx = _pool_tables(lc)
    w_in_b = w_in.astype(BF16)
    w_out_b = w_out.astype(BF16)
    wbd = jnp.stack([_block_diag(four_w[l]) for l in range(depth)])
    pool_wbd = jnp.stack([_block_diag(pool_w[l]) for l in range(depth)]).astype(BF16)
    wr = jnp.concatenate([w_grp, w_rtr, jnp.zeros((depth, D, 128 - 4 - N_EXPERTS), F32)], axis=2)
    wr_hi = wr.astype(BF16)
    w2 = jnp.concatenate([wr_hi, (wr - wr_hi.astype(F32)).astype(BF16)], axis=2)
    br = jnp.concatenate([b_grp, b_rtr, jnp.zeros((depth, 128 - 4 - N_EXPERTS), F32)], axis=1)

    mod_all = _mod_call(cs, w_mod, b_mod)
    ab_all = _ab_call(c64bd, s64bd, wbd)

    xc = ctx
    for l in range(depth):
        last = l == depth - 1
        mod = mod_all[l]
        g1 = norm1_g[l][None, :]
        g2 = norm2_g[l][None, :]
        qg = jnp.tile(q_norm_g[l], N_HEADS)[None, :]
        kg = jnp.tile(k_norm_g[l], KV_W // HEAD_DIM)[None, :]
        in_args = (mod, g1, w_in_b[l], m512, qg, kg, ab_all[l])

        puc, qc, kvc, uac = _in_call(xc, *in_args, None, None, rope=False, per_batch=False)
        pul, ql, kvl, ual = _in_call(x, *in_args, cos2, sin2, rope=True, per_batch=True)
        attn_l = _attn_lat_call(attn_sink[l], ql, kvl, kvc)
        four_l = _four2_call(*_four1_call(ual, four_cs), *four_tabs)

        proj = (w_out_b[l], pool_wbd[l], pool_scale[l][None, :])
        rout = (mod, g2, w2[l], br[l][None, :])
        cnt0 = jnp.zeros((32, 128), F32)
        x_mid, rows_l, cnt = _out_call(x, pul, attn_l, four_l, *proj, *pool_lat, *rout, tri_of(seq), cnt0,
                                       per_batch=True)
        row_sets = [rows_l]
        if not last:
            attn_c = _attn_ctx_call(attn_sink[l], qc, kvc)
            four_c = _four_call(uac, *tabs_ctx)
            xc_mid, rows_c, cnt = _out_call(xc, puc, attn_c, four_c, *proj, *pool_ctx, *rout, tri_of(lc), cnt,
                                            per_batch=False)
            row_sets.append(rows_c)

        route = [r[:, D:D + 2].astype(I32) for r in row_sets]
        bins = jnp.concatenate([r[:, 0] for r in route])
        rank = jnp.concatenate([r[:, 1] for r in route])
        dest, bin_ends, items = _routing_tables(bins, rank, cnt, bins.shape[0])
        xs = _scatter_call(dest, bin_ends, items[2], row_sets)
        ys = _moe_call(items, xs, w_gate, w_up, w_down, l)
        x = _gather_call(dest, x_mid.reshape(t_lat, D), mod, ys, seq=seq, per_batch=True,
                         dest_off=0).reshape(nb, seq, D)
        if not last:
            xc = _gather_call(dest, xc_mid.reshape(t_ctx, D), mod, ys, seq=lc, per_batch=False,
                              dest_off=t_lat).reshape(nb, lc, D)
    return x
```
